```python
import jax, jax.numpy as jnp
from jax import lax
import numpy as np

D_MODEL = 1024
BATCH = 8
SEQ = 2048
DEPTH = 4

N_MEM = 256
EPS = 1e-6
D_A = D_MODEL
CONV_A_WIDTH = 3
FOX_HEADS = 16
FOX_HEAD_DIM = D_MODEL // FOX_HEADS
D_B = FOX_HEADS * FOX_HEAD_DIM
Q_BLOCK = 128
D_C = D_MODEL
CONV_C_WIDTH = 31
N_BRANCH = 3
X_HEADS = 4
X_HEAD_DIM = 128
D_X = X_HEADS * X_HEAD_DIM
D_FF = -(-8 * D_MODEL // (3 * 256)) * 256
IN_SIZES = (D_A, D_A, D_A, D_B, D_B, D_B, FOX_HEADS, D_C, D_C, D_MODEL, D_MODEL, D_MODEL)
D_IN_TOTAL = sum(IN_SIZES)

kernel_name = "hybrid_gated_conv_fox_conformer_decoder"


def rms_norm(x, g):
    xf = x.astype(jnp.float32)
    y = xf * lax.rsqrt(jnp.mean(xf * xf, axis=-1, keepdims=True) + EPS)
    return (y * g.astype(jnp.float32)).astype(x.dtype)


def layer_norm(x, g, b):
    xf = x.astype(jnp.float32)
    mu = jnp.mean(xf, axis=-1, keepdims=True)
    xc = xf - mu
    y = xc * lax.rsqrt(jnp.mean(xc * xc, axis=-1, keepdims=True) + EPS)
    return (y * g.astype(jnp.float32) + b.astype(jnp.float32)).astype(x.dtype)


def causal_depthwise_conv(x, w):
    K, C = w.shape
    return lax.conv_general_dilated(
        x, w[:, None, :].astype(x.dtype), window_strides=(1,), padding=[(K - 1, 0)],
        dimension_numbers=("NWC", "WIO", "NWC"), feature_group_count=C)


def split_columns(z):
    offs = np.cumsum(np.array(IN_SIZES))[:-1]
    return jnp.split(z, [int(o) for o in offs], axis=-1)


def forgetting_attention(q, k, v, log_f):
    B, T, H, Dh = q.shape
    c = jnp.cumsum(log_f, axis=1).transpose(0, 2, 1)
    scale = Dh ** -0.5
    outs = []
    for i in range(T // Q_BLOCK):
        q0 = i * Q_BLOCK
        k_end = q0 + Q_BLOCK
        s = jnp.einsum("bqhd,bkhd->bhqk", q[:, q0:k_end], k[:, :k_end]).astype(jnp.float32) * scale
        decay = c[:, :, q0:k_end, None] - c[:, :, None, :k_end]
        causal = (q0 + jnp.arange(Q_BLOCK))[:, None] >= jnp.arange(k_end)[None, :]
        s = jnp.where(causal, s + decay, -jnp.inf)
        p = jax.nn.softmax(s, axis=-1)
        outs.append(jnp.einsum("bhqk,bkhd->bqhd", p.astype(v.dtype), v[:, :k_end]))
    return jnp.concatenate(outs, axis=1)


def hybrid_mixer(h, w_in, b_gate, b_forget, conv_a, w_out_a, w_out_b, conv_c, conv_c_bias,
                 ln_c_gain, ln_c_bias, w_out_c, w_o):
    B, T, _ = h.shape
    z = jnp.einsum("btd,de->bte", h, w_in)
    (a_b, a_c, a_u, q, k, v, f_logit, c_val, c_gate, g_a, g_b, g_c) = split_columns(z)
    y_a = a_b * causal_depthwise_conv(a_c * a_u, conv_a)
    y_a = jnp.einsum("btc,cd->btd", y_a, w_out_a)
    log_f = jax.nn.log_sigmoid(f_logit.astype(jnp.float32) + b_forget.astype(jnp.float32))
    o = forgetting_attention(q.reshape(B, T, FOX_HEADS, FOX_HEAD_DIM),
                             k.reshape(B, T, FOX_HEADS, FOX_HEAD_DIM),
                             v.reshape(B, T, FOX_HEADS, FOX_HEAD_DIM), log_f)
    y_b = jnp.einsum("btc,cd->btd", o.reshape(B, T, D_B), w_out_b)
    u = c_val * jax.nn.sigmoid(c_gate)
    u = causal_depthwise_conv(u, conv_c) + conv_c_bias.astype(u.dtype)
    u = jax.nn.silu(layer_norm(u, ln_c_gain, ln_c_bias))
    y_c = jnp.einsum("btc,cd->btd", u, w_out_c)
    ga, gb, gc = jnp.split(jax.nn.sigmoid(jnp.concatenate([g_a, g_b, g_c], axis=-1) + b_gate), N_BRANCH, axis=-1)
    merged = ga * y_a + gb * y_b + gc * y_c
    return jnp.einsum("btd,de->bte", merged, w_o)


def memory_cross_attention(h, mem_n, w_xq, w_xkv, w_xo):
    B, T, _ = h.shape
    q = jnp.einsum("btd,de->bte", h, w_xq).reshape(B, T, X_HEADS, X_HEAD_DIM)
    kv = jnp.einsum("bmd,de->bme", mem_n, w_xkv)
    k, v = jnp.split(kv, 2, axis=-1)
    k = k.reshape(B, -1, X_HEADS, X_HEAD_DIM)
    v = v.reshape(B, -1, X_HEADS, X_HEAD_DIM)
    s = jnp.einsum("bthd,bmhd->bhtm", q, k).astype(jnp.float32) * (X_HEAD_DIM ** -0.5)
    p = jax.nn.softmax(s, axis=-1)
    o = jnp.einsum("bhtm,bmhd->bthd", p.astype(v.dtype), v).reshape(B, T, D_X)
    return jnp.einsum("btc,cd->btd", o, w_xo)


def swiglu_ffn(h, w_gate_up, w_down):
    g, u = jnp.split(jnp.einsum("btd,df->btf", h, w_gate_up), 2, axis=-1)
    return jnp.einsum("btf,fd->btd", jax.nn.silu(g) * u, w_down)


def _fwd_setup_inputs(seed: int = 0) -> dict:
    key = jax.random.key(seed)
    ks = jax.random.split(key, 24)
    f32 = jnp.float32

    def w(k, shape, fan_in):
        return jax.random.normal(k, shape, f32) * (fan_in ** -0.5)

    def gain(k, shape):
        return 1.0 + 0.02 * jax.random.normal(k, shape, f32)

    def small(k, shape, s=0.02):
        return s * jax.random.normal(k, shape, f32)

    L, D = DEPTH, D_MODEL
    return {
        "x": jax.random.normal(ks[0], (BATCH, SEQ, D), f32),
        "mem": jax.random.normal(ks[1], (BATCH, N_MEM, D), f32),
        "mix_norm": gain(ks[2], (L, D)),
        "w_in": w(ks[3], (L, D, D_IN_TOTAL), D),
        "b_gate": small(ks[4], (L, N_BRANCH * D)),
        "b_forget": 2.0 + 0.5 * jax.random.normal(ks[5], (L, FOX_HEADS), f32),
        "conv_a": w(ks[6], (L, CONV_A_WIDTH, D_A), CONV_A_WIDTH),
        "w_out_a": w(ks[7], (L, D_A, D), D_A),
        "w_out_b": w(ks[8], (L, D_B, D), D_B),
        "conv_c": w(ks[9], (L, CONV_C_WIDTH, D_C), CONV_C_WIDTH),
        "conv_c_bias": small(ks[10], (L, D_C)),
        "ln_c_gain": gain(ks[11], (L, D_C)),
        "ln_c_bias": small(ks[12], (L, D_C)),
        "w_out_c": w(ks[13], (L, D_C, D), D_C),
        "w_o": w(ks[14], (L, D, D), D),
        "xattn_norm": gain(ks[15], (L, D)),
        "mem_norm": gain(ks[16], (D,)),
        "w_xq": w(ks[17], (L, D, D_X), D),
        "w_xkv": w(ks[18], (L, D, 2 * D_X), D),
        "w_xo": w(ks[19], (L, D_X, D), D_X),
        "ffn_norm": gain(ks[20], (L, D)),
        "w_gate_up": w(ks[21], (L, D, 2 * D_FF), D),
        "w_down": w(ks[22], (L, D_FF, D), D_FF),
        "final_norm": gain(ks[23], (D,)),
    }


def _fwd_reference(x, mem, mix_norm, w_in, b_gate, b_forget, conv_a, w_out_a, w_out_b, conv_c,
              conv_c_bias, ln_c_gain, ln_c_bias, w_out_c, w_o, xattn_norm, mem_norm, w_xq,
              w_xkv, w_xo, ffn_norm, w_gate_up, w_down, final_norm):
    mem_n = rms_norm(mem, mem_norm)
    for l in range(DEPTH):
        x = x + hybrid_mixer(rms_norm(x, mix_norm[l]), w_in[l], b_gate[l], b_forget[l], conv_a[l],
                             w_out_a[l], w_out_b[l], conv_c[l], conv_c_bias[l], ln_c_gain[l],
                             ln_c_bias[l], w_out_c[l], w_o[l])
        x = x + memory_cross_attention(rms_norm(x, xattn_norm[l]), mem_n, w_xq[l], w_xkv[l], w_xo[l])
        x = x + swiglu_ffn(rms_norm(x, ffn_norm[l]), w_gate_up[l], w_down[l])
    return rms_norm(x, final_norm)


import jax as _jax
import jax.numpy as _jnp

TWIN_FORMAT = 'train_step'
FWD_PARAMS = ['x', 'mem', 'mix_norm', 'w_in', 'b_gate', 'b_forget', 'conv_a', 'w_out_a', 'w_out_b', 'conv_c', 'conv_c_bias', 'ln_c_gain', 'ln_c_bias', 'w_out_c', 'w_o', 'xattn_norm', 'mem_norm', 'w_xq', 'w_xkv', 'w_xo', 'ffn_norm', 'w_gate_up', 'w_down', 'final_norm']
TWIN_WEIGHTS = ['mix_norm', 'w_in', 'b_gate', 'b_forget', 'conv_a', 'w_out_a', 'w_out_b', 'conv_c', 'conv_c_bias', 'ln_c_gain', 'ln_c_bias', 'w_out_c', 'w_o', 'xattn_norm', 'mem_norm', 'w_xq', 'w_xkv', 'w_xo', 'ffn_norm', 'w_gate_up', 'w_down', 'final_norm']
TWIN_DIFF_INPUT = 'x'
TWIN_INPUTS = ['x', 'mem', 'mix_norm', 'w_in', 'b_gate', 'b_forget', 'conv_a', 'w_out_a', 'w_out_b', 'conv_c', 'conv_c_bias', 'ln_c_gain', 'ln_c_bias', 'w_out_c', 'w_o', 'xattn_norm', 'mem_norm', 'w_xq', 'w_xkv', 'w_xo', 'ffn_norm', 'w_gate_up', 'w_down', 'final_norm', 'loss_target', 'm_mix_norm', 'm_w_in', 'm_b_gate', 'm_b_forget', 'm_conv_a', 'm_w_out_a', 'm_w_out_b', 'm_conv_c', 'm_conv_c_bias', 'm_ln_c_gain', 'm_ln_c_bias', 'm_w_out_c', 'm_w_o', 'm_xattn_norm', 'm_mem_norm', 'm_w_xq', 'm_w_xkv', 'm_w_xo', 'm_ffn_norm', 'm_w_gate_up', 'm_w_down', 'm_final_norm', 'v_mix_norm', 'v_w_in', 'v_b_gate', 'v_b_forget', 'v_conv_a', 'v_w_out_a', 'v_w_out_b', 'v_conv_c', 'v_conv_c_bias', 'v_ln_c_gain', 'v_ln_c_bias', 'v_w_out_c', 'v_w_o', 'v_xattn_norm', 'v_mem_norm', 'v_w_xq', 'v_w_xkv', 'v_w_xo', 'v_ffn_norm', 'v_w_gate_up', 'v_w_down', 'v_final_norm']
TWIN_OUTPUTS = ['loss', 'grad_x', 'grad_mix_norm', 'grad_w_in', 'grad_b_gate', 'grad_b_forget', 'grad_conv_a', 'grad_w_out_a', 'grad_w_out_b', 'grad_conv_c', 'grad_conv_c_bias', 'grad_ln_c_gain', 'grad_ln_c_bias', 'grad_w_out_c', 'grad_w_o', 'grad_xattn_norm', 'grad_mem_norm', 'grad_w_xq', 'grad_w_xkv', 'grad_w_xo', 'grad_ffn_norm', 'grad_w_gate_up', 'grad_w_down', 'grad_final_norm', 'delta_mix_norm', 'delta_w_in', 'delta_b_gate', 'delta_b_forget', 'delta_conv_a', 'delta_w_out_a', 'delta_w_out_b', 'delta_conv_c', 'delta_conv_c_bias', 'delta_ln_c_gain', 'delta_ln_c_bias', 'delta_w_out_c', 'delta_w_o', 'delta_xattn_norm', 'delta_mem_norm', 'delta_w_xq', 'delta_w_xkv', 'delta_w_xo', 'delta_ffn_norm', 'delta_w_gate_up', 'delta_w_down', 'delta_final_norm', 'new_m_mix_norm', 'new_m_w_in', 'new_m_b_gate', 'new_m_b_forget', 'new_m_conv_a', 'new_m_w_out_a', 'new_m_w_out_b', 'new_m_conv_c', 'new_m_conv_c_bias', 'new_m_ln_c_gain', 'new_m_ln_c_bias', 'new_m_w_out_c', 'new_m_w_o', 'new_m_xattn_norm', 'new_m_mem_norm', 'new_m_w_xq', 'new_m_w_xkv', 'new_m_w_xo', 'new_m_ffn_norm', 'new_m_w_gate_up', 'new_m_w_down', 'new_m_final_norm', 'new_v_mix_norm', 'new_v_w_in', 'new_v_b_gate', 'new_v_b_forget', 'new_v_conv_a', 'new_v_w_out_a', 'new_v_w_out_b', 'new_v_conv_c', 'new_v_conv_c_bias', 'new_v_ln_c_gain', 'new_v_ln_c_bias', 'new_v_w_out_c', 'new_v_w_o', 'new_v_xattn_norm', 'new_v_mem_norm', 'new_v_w_xq', 'new_v_w_xkv', 'new_v_w_xo', 'new_v_ffn_norm', 'new_v_w_gate_up', 'new_v_w_down', 'new_v_final_norm']
TWIN_LEAF_KINDS = {'loss': 'loss', 'grad_x': 'grad_x', 'grad_mix_norm': 'grad_w', 'grad_w_in': 'grad_w', 'grad_b_gate': 'grad_w', 'grad_b_forget': 'grad_w', 'grad_conv_a': 'grad_w', 'grad_w_out_a': 'grad_w', 'grad_w_out_b': 'grad_w', 'grad_conv_c': 'grad_w', 'grad_conv_c_bias': 'grad_w', 'grad_ln_c_gain': 'grad_w', 'grad_ln_c_bias': 'grad_w', 'grad_w_out_c': 'grad_w', 'grad_w_o': 'grad_w', 'grad_xattn_norm': 'grad_w', 'grad_mem_norm': 'grad_w', 'grad_w_xq': 'grad_w', 'grad_w_xkv': 'grad_w', 'grad_w_xo': 'grad_w', 'grad_ffn_norm': 'grad_w', 'grad_w_gate_up': 'grad_w', 'grad_w_down': 'grad_w', 'grad_final_norm': 'grad_w', 'delta_mix_norm': 'delta_w', 'delta_w_in': 'delta_w', 'delta_b_gate': 'delta_w', 'delta_b_forget': 'delta_w', 'delta_conv_a': 'delta_w', 'delta_w_out_a': 'delta_w', 'delta_w_out_b': 'delta_w', 'delta_conv_c': 'delta_w', 'delta_conv_c_bias': 'delta_w', 'delta_ln_c_gain': 'delta_w', 'delta_ln_c_bias': 'delta_w', 'delta_w_out_c': 'delta_w', 'delta_w_o': 'delta_w', 'delta_xattn_norm': 'delta_w', 'delta_mem_norm': 'delta_w', 'delta_w_xq': 'delta_w', 'delta_w_xkv': 'delta_w', 'delta_w_xo': 'delta_w', 'delta_ffn_norm': 'delta_w', 'delta_w_gate_up': 'delta_w', 'delta_w_down': 'delta_w', 'delta_final_norm': 'delta_w', 'new_m_mix_norm': 'new_m', 'new_m_w_in': 'new_m', 'new_m_b_gate': 'new_m', 'new_m_b_forget': 'new_m', 'new_m_conv_a': 'new_m', 'new_m_w_out_a': 'new_m', 'new_m_w_out_b': 'new_m', 'new_m_conv_c': 'new_m', 'new_m_conv_c_bias': 'new_m', 'new_m_ln_c_gain': 'new_m', 'new_m_ln_c_bias': 'new_m', 'new_m_w_out_c': 'new_m', 'new_m_w_o': 'new_m', 'new_m_xattn_norm': 'new_m', 'new_m_mem_norm': 'new_m', 'new_m_w_xq': 'new_m', 'new_m_w_xkv': 'new_m', 'new_m_w_xo': 'new_m', 'new_m_ffn_norm': 'new_m', 'new_m_w_gate_up': 'new_m', 'new_m_w_down': 'new_m', 'new_m_final_norm': 'new_m', 'new_v_mix_norm': 'new_v', 'new_v_w_in': 'new_v', 'new_v_b_gate': 'new_v', 'new_v_b_forget': 'new_v', 'new_v_conv_a': 'new_v', 'new_v_w_out_a': 'new_v', 'new_v_w_out_b': 'new_v', 'new_v_conv_c': 'new_v', 'new_v_conv_c_bias': 'new_v', 'new_v_ln_c_gain': 'new_v', 'new_v_ln_c_bias': 'new_v', 'new_v_w_out_c': 'new_v', 'new_v_w_o': 'new_v', 'new_v_xattn_norm': 'new_v', 'new_v_mem_norm': 'new_v', 'new_v_w_xq': 'new_v', 'new_v_w_xkv': 'new_v', 'new_v_w_xo': 'new_v', 'new_v_ffn_norm': 'new_v', 'new_v_w_gate_up': 'new_v', 'new_v_w_down': 'new_v', 'new_v_final_norm': 'new_v'}


def _forward(args):
    return _fwd_reference(*[args[k] for k in FWD_PARAMS])


def _output_shape():
    out = _jax.eval_shape(lambda: _forward(_fwd_setup_inputs(0)))
    return out.shape, out.dtype

N_MICROBATCH = 1
ADAM_LR = 0.001
ADAM_B1 = 0.9
ADAM_B2 = 0.999
ADAM_EPS = 1e-08
ADAM_WD = 0.01
ADAM_STEP = 10
PER_EXAMPLE_BATCH_AXIS = {'x': 0, 'mem': 0, 'loss_target': 0}
SHARED_INPUTS = []
_WEIGHT_DTYPES = {'mix_norm': _jnp.float32, 'w_in': _jnp.float32, 'b_gate': _jnp.float32, 'b_forget': _jnp.float32, 'conv_a': _jnp.float32, 'w_out_a': _jnp.float32, 'w_out_b': _jnp.float32, 'conv_c': _jnp.float32, 'conv_c_bias': _jnp.float32, 'ln_c_gain': _jnp.float32, 'ln_c_bias': _jnp.float32, 'w_out_c': _jnp.float32, 'w_o': _jnp.float32, 'xattn_norm': _jnp.float32, 'mem_norm': _jnp.float32, 'w_xq': _jnp.float32, 'w_xkv': _jnp.float32, 'w_xo': _jnp.float32, 'ffn_norm': _jnp.float32, 'w_gate_up': _jnp.float32, 'w_down': _jnp.float32, 'final_norm': _jnp.float32}
MOMENT_SCALE = {'mix_norm': 1.366912e-01, 'w_in': 4.049102e-02, 'b_gate': 1.893497e-02, 'b_forget': 1.194123e-01, 'conv_a': 6.887579e-02, 'w_out_a': 6.721564e-02, 'w_out_b': 2.840241e-02, 'conv_c': 4.145013e-02, 'conv_c_bias': 9.063826e-02, 'ln_c_gain': 5.073917e-02, 'ln_c_bias': 4.343163e-02, 'w_out_c': 4.048456e-02, 'w_o': 8.281039e-02, 'xattn_norm': 1.290506e-02, 'mem_norm': 3.876032e-02, 'w_xq': 1.839456e-02, 'w_xkv': 1.859596e-02, 'w_xo': 1.324098e-02, 'ffn_norm': 8.994199e-02, 'w_gate_up': 3.823499e-02, 'w_down': 6.232605e-02, 'final_norm': 1.597756e+01}


def _to_microbatches(a, axis):
    t = _jnp.moveaxis(a, axis, 0)
    t = t.reshape((N_MICROBATCH, t.shape[0] // N_MICROBATCH) + t.shape[1:])
    return _jnp.moveaxis(t, 1, axis + 1)


def setup_inputs(seed: int = 0) -> dict:
    inp = _fwd_setup_inputs(seed)
    key = _jax.random.fold_in(_jax.random.key(seed), 7919)
    shape, _ = _output_shape()
    out = dict(inp)
    out["loss_target"] = _jax.random.normal(_jax.random.fold_in(key, 0), shape, _jnp.float32)
    for i, name in enumerate(TWIN_WEIGHTS):
        w = inp[name].astype(_jnp.float32)
        if MOMENT_SCALE is None:
            s = _jnp.sqrt(_jnp.mean(_jnp.square(w)) + 1e-30)
        else:
            s = MOMENT_SCALE[name]
        km, kv = _jax.random.split(_jax.random.fold_in(key, i + 1))
        out[name] = w
        out["m_" + name] = s * _jax.random.normal(km, w.shape, _jnp.float32)
        out["v_" + name] = (s * s) * _jax.random.uniform(kv, w.shape, _jnp.float32, 0.5, 1.5)
    if N_MICROBATCH > 1:
        for name, axis in PER_EXAMPLE_BATCH_AXIS.items():
            out[name] = _to_microbatches(out[name], axis)
    return {'x': out['x'], 'mem': out['mem'], 'mix_norm': out['mix_norm'], 'w_in': out['w_in'], 'b_gate': out['b_gate'], 'b_forget': out['b_forget'], 'conv_a': out['conv_a'], 'w_out_a': out['w_out_a'], 'w_out_b': out['w_out_b'], 'conv_c': out['conv_c'], 'conv_c_bias': out['conv_c_bias'], 'ln_c_gain': out['ln_c_gain'], 'ln_c_bias': out['ln_c_bias'], 'w_out_c': out['w_out_c'], 'w_o': out['w_o'], 'xattn_norm': out['xattn_norm'], 'mem_norm': out['mem_norm'], 'w_xq': out['w_xq'], 'w_xkv': out['w_xkv'], 'w_xo': out['w_xo'], 'ffn_norm': out['ffn_norm'], 'w_gate_up': out['w_gate_up'], 'w_down': out['w_down'], 'final_norm': out['final_norm'], 'loss_target': out['loss_target'], 'm_mix_norm': out['m_mix_norm'], 'm_w_in': out['m_w_in'], 'm_b_gate': out['m_b_gate'], 'm_b_forget': out['m_b_forget'], 'm_conv_a': out['m_conv_a'], 'm_w_out_a': out['m_w_out_a'], 'm_w_out_b': out['m_w_out_b'], 'm_conv_c': out['m_conv_c'], 'm_conv_c_bias': out['m_conv_c_bias'], 'm_ln_c_gain': out['m_ln_c_gain'], 'm_ln_c_bias': out['m_ln_c_bias'], 'm_w_out_c': out['m_w_out_c'], 'm_w_o': out['m_w_o'], 'm_xattn_norm': out['m_xattn_norm'], 'm_mem_norm': out['m_mem_norm'], 'm_w_xq': out['m_w_xq'], 'm_w_xkv': out['m_w_xkv'], 'm_w_xo': out['m_w_xo'], 'm_ffn_norm': out['m_ffn_norm'], 'm_w_gate_up': out['m_w_gate_up'], 'm_w_down': out['m_w_down'], 'm_final_norm': out['m_final_norm'], 'v_mix_norm': out['v_mix_norm'], 'v_w_in': out['v_w_in'], 'v_b_gate': out['v_b_gate'], 'v_b_forget': out['v_b_forget'], 'v_conv_a': out['v_conv_a'], 'v_w_out_a': out['v_w_out_a'], 'v_w_out_b': out['v_w_out_b'], 'v_conv_c': out['v_conv_c'], 'v_conv_c_bias': out['v_conv_c_bias'], 'v_ln_c_gain': out['v_ln_c_gain'], 'v_ln_c_bias': out['v_ln_c_bias'], 'v_w_out_c': out['v_w_out_c'], 'v_w_o': out['v_w_o'], 'v_xattn_norm': out['v_xattn_norm'], 'v_mem_norm': out['v_mem_norm'], 'v_w_xq': out['v_w_xq'], 'v_w_xkv': out['v_w_xkv'], 'v_w_xo': out['v_w_xo'], 'v_ffn_norm': out['v_ffn_norm'], 'v_w_gate_up': out['v_w_gate_up'], 'v_w_down': out['v_w_down'], 'v_final_norm': out['v_final_norm']}


def _loss(weights, diff, rest, loss_target):
    with _jax.named_scope("forward"):
        args = {**rest, TWIN_DIFF_INPUT: diff, **{k: w.astype(_WEIGHT_DTYPES[k]) for k, w in weights.items()}}
        y = _forward(args)
    with _jax.named_scope("loss_head"):
        err = _jnp.square(y.astype(_jnp.float32) - loss_target)
        return 0.5 * _jnp.sum(_jnp.mean(err, axis=-1)) if err.ndim else 0.5 * err


def _adamw(w, g, m, v):
    m = ADAM_B1 * m + (1.0 - ADAM_B1) * g
    v = ADAM_B2 * v + (1.0 - ADAM_B2) * _jnp.square(g)
    m_hat = m / (1.0 - ADAM_B1 ** ADAM_STEP)
    v_hat = v / (1.0 - ADAM_B2 ** ADAM_STEP)
    delta = -ADAM_LR * (m_hat / (_jnp.sqrt(v_hat) + ADAM_EPS) + ADAM_WD * w)
    return delta, m, v


def reference(x, mem, mix_norm, w_in, b_gate, b_forget, conv_a, w_out_a, w_out_b, conv_c, conv_c_bias, ln_c_gain, ln_c_bias, w_out_c, w_o, xattn_norm, mem_norm, w_xq, w_xkv, w_xo, ffn_norm, w_gate_up, w_down, final_norm, loss_target, m_mix_norm, m_w_in, m_b_gate, m_b_forget, m_conv_a, m_w_out_a, m_w_out_b, m_conv_c, m_conv_c_bias, m_ln_c_gain, m_ln_c_bias, m_w_out_c, m_w_o, m_xattn_norm, m_mem_norm, m_w_xq, m_w_xkv, m_w_xo, m_ffn_norm, m_w_gate_up, m_w_down, m_final_norm, v_mix_norm, v_w_in, v_b_gate, v_b_forget, v_conv_a, v_w_out_a, v_w_out_b, v_conv_c, v_conv_c_bias, v_ln_c_gain, v_ln_c_bias, v_w_out_c, v_w_o, v_xattn_norm, v_mem_norm, v_w_xq, v_w_xkv, v_w_xo, v_ffn_norm, v_w_gate_up, v_w_down, v_final_norm):
    given = dict(x=x, mem=mem, mix_norm=mix_norm, w_in=w_in, b_gate=b_gate, b_forget=b_forget, conv_a=conv_a, w_out_a=w_out_a, w_out_b=w_out_b, conv_c=conv_c, conv_c_bias=conv_c_bias, ln_c_gain=ln_c_gain, ln_c_bias=ln_c_bias, w_out_c=w_out_c, w_o=w_o, xattn_norm=xattn_norm, mem_norm=mem_norm, w_xq=w_xq, w_xkv=w_xkv, w_xo=w_xo, ffn_norm=ffn_norm, w_gate_up=w_gate_up, w_down=w_down, final_norm=final_norm, loss_target=loss_target, m_mix_norm=m_mix_norm, m_w_in=m_w_in, m_b_gate=m_b_gate, m_b_forget=m_b_forget, m_conv_a=m_conv_a, m_w_out_a=m_w_out_a, m_w_out_b=m_w_out_b, m_conv_c=m_conv_c, m_conv_c_bias=m_conv_c_bias, m_ln_c_gain=m_ln_c_gain, m_ln_c_bias=m_ln_c_bias, m_w_out_c=m_w_out_c, m_w_o=m_w_o, m_xattn_norm=m_xattn_norm, m_mem_norm=m_mem_norm, m_w_xq=m_w_xq, m_w_xkv=m_w_xkv, m_w_xo=m_w_xo, m_ffn_norm=m_ffn_norm, m_w_gate_up=m_w_gate_up, m_w_down=m_w_down, m_final_norm=m_final_norm, v_mix_norm=v_mix_norm, v_w_in=v_w_in, v_b_gate=v_b_gate, v_b_forget=v_b_forget, v_conv_a=v_conv_a, v_w_out_a=v_w_out_a, v_w_out_b=v_w_out_b, v_conv_c=v_conv_c, v_conv_c_bias=v_conv_c_bias, v_ln_c_gain=v_ln_c_gain, v_ln_c_bias=v_ln_c_bias, v_w_out_c=v_w_out_c, v_w_o=v_w_o, v_xattn_norm=v_xattn_norm, v_mem_norm=v_mem_norm, v_w_xq=v_w_xq, v_w_xkv=v_w_xkv, v_w_xo=v_w_xo, v_ffn_norm=v_ffn_norm, v_w_gate_up=v_w_gate_up, v_w_down=v_w_down, v_final_norm=v_final_norm)
    weights = {n: given[n] for n in TWIN_WEIGHTS}
    shared = {n: given[n] for n in SHARED_INPUTS}
    per_example = {n: given[n] for n in ['x', 'mem']}
    grad_fn = _jax.value_and_grad(_loss, argnums=(0, 1))

    def one_microbatch(ex, loss_target):
        ex = dict(ex)
        diff = ex.pop(TWIN_DIFF_INPUT)
        return grad_fn(weights, diff, {**shared, **ex}, loss_target)

    if N_MICROBATCH == 1:
        loss, (grad_w, grad_x) = one_microbatch(per_example, given["loss_target"])
    else:
        def body(carry, xs):
            loss_sum, grad_sum = carry
            l_k, (gw_k, gx_k) = one_microbatch(xs[0], xs[1])
            with _jax.named_scope("update"):
                return (loss_sum + l_k, _jax.tree.map(_jnp.add, grad_sum, gw_k)), gx_k

        init = (_jnp.zeros((), _jnp.float32), _jax.tree.map(_jnp.zeros_like, weights))
        (loss, grad_w), grad_x = _jax.lax.scan(body, init, (per_example, given["loss_target"]))
    with _jax.named_scope("update"):
        delta_w, new_m, new_v = {}, {}, {}
        for n in TWIN_WEIGHTS:
            delta_w[n], new_m[n], new_v[n] = _adamw(weights[n], grad_w[n], given["m_" + n], given["v_" + n])
    return (loss, grad_x, *[grad_w[n] for n in TWIN_WEIGHTS], *[delta_w[n] for n in TWIN_WEIGHTS],
            *[new_m[n] for n in TWIN_WEIGHTS], *[new_v[n] for n in TWIN_WEIGHTS])
```

```python
import functools

import jax
import jax.numpy as jnp
from jax import lax
from jax.experimental import pallas as pl
from jax.experimental.pallas import tpu as pltpu

F32 = jnp.float32
BF16 = jnp.bfloat16
MESH = pl.DeviceIdType.MESH

N_DEV = 8
EPS = 1e-6
FOX_HEADS = 16
X_HEADS = 4
LANES = 128
HALO_A = 16
HALO_C = 32
CONV_A_W = 3
CONV_C_W = 31
VMEM_LIMIT = 56 << 20

ADAM_LR = 0.001
ADAM_B1 = 0.9
ADAM_B2 = 0.999
ADAM_EPS = 1e-08
ADAM_WD = 0.01
ADAM_STEP = 10

WEIGHTS = ['mix_norm', 'w_in', 'b_gate', 'b_forget', 'conv_a', 'w_out_a', 'w_out_b', 'conv_c', 'conv_c_bias',
           'ln_c_gain', 'ln_c_bias', 'w_out_c', 'w_o', 'xattn_norm', 'mem_norm', 'w_xq', 'w_xkv', 'w_xo',
           'ffn_norm', 'w_gate_up', 'w_down', 'final_norm']
SHARDED = ['w_in', 'conv_a', 'w_out_a', 'w_out_b', 'conv_c', 'w_out_c', 'w_o', 'w_xq', 'w_xkv', 'w_xo',
           'w_gate_up', 'w_down']
ROW_SHARDED = ['w_out_a', 'w_out_b', 'w_out_c', 'w_o', 'w_xq', 'w_xkv', 'w_down']
SMALL = [w for w in WEIGHTS if w not in SHARDED]
ARG_NAMES = (['x', 'mem'] + WEIGHTS + ['loss_target'] + ['m_' + w for w in WEIGHTS] + ['v_' + w for w in WEIGHTS])


def _pick(n, cands=(1024, 1408, 512, 256, 128)):
    for c in cands:
        if n % c == 0:
            return c
    return n


def _call(body, *, name, out_shape, grid=(), in_specs=None, out_specs=None, scratch=(), sem=None, aliases=None):
    params = dict(vmem_limit_bytes=VMEM_LIMIT)
    if sem is not None:
        params['dimension_semantics'] = sem
    kw = {}
    if in_specs is not None:
        kw['in_specs'] = in_specs
    if out_specs is not None:
        kw['out_specs'] = out_specs
    return pl.pallas_call(body, name=name, out_shape=out_shape, grid=grid, scratch_shapes=list(scratch),
                          input_output_aliases=aliases or {}, compiler_params=pltpu.CompilerParams(**params),
                          interpret=False, **kw)


def _sds(shape, dtype):
    return jax.ShapeDtypeStruct(tuple(shape), dtype)


def _sigmoid(x):
    return 1.0 / (1.0 + jnp.exp(-x))


_DN = {'nn': (((1,), (0,)), ((), ())), 'nt': (((1,), (1,)), ((), ())), 'tn': (((0,), (0,)), ((), ()))}


def _mm(a, b, *, mode, name, out_dtype=BF16, lb=None, add=None, out=None, lo=None):
    if mode == 'tn':
        K, M = a.shape
    else:
        M, K = a.shape
    b2 = b.shape[-2:]
    N = b2[0] if mode == 'nt' else b2[1]
    tm, tn = _pick(M), _pick(N)
    tk = K if K <= 2048 else _pick(K)
    ni, nj, nk = M // tm, N // tn, K // tk
    a_bytes, b_bytes = M * K * a.dtype.itemsize, K * N * b.dtype.itemsize
    n_outer = (b_bytes + nj * a_bytes) < (a_bytes + ni * b_bytes)
    if n_outer:
        grid = (nj, ni, nk)
        ij = lambda g0, g1: (g1, g0)
    else:
        grid = (ni, nj, nk)
        ij = lambda g0, g1: (g0, g1)

    def a_map(g0, g1, k):
        i, _ = ij(g0, g1)
        return (k, i) if mode == 'tn' else (i, k)

    def b_map(g0, g1, k):
        _, j = ij(g0, g1)
        kj = (j, k) if mode == 'nt' else (k, j)
        return kj if lb is None else (lb,) + kj

    def o_map(g0, g1, k):
        i, j = ij(g0, g1)
        return (i, j) if lo is None else (lo, i, j)

    a_blk = (tk, tm) if mode == 'tn' else (tm, tk)
    b_blk = (tn, tk) if mode == 'nt' else (tk, tn)
    in_specs = [pl.BlockSpec(a_blk, a_map), pl.BlockSpec(b_blk if lb is None else (None,) + b_blk, b_map)]
    args = [a, b]
    has_add = add is not None
    if has_add:
        in_specs.append(pl.BlockSpec((tm, tn), lambda g0, g1, k: ij(g0, g1)))
        args.append(add)
    aliases = {}
    if lo is None:
        out_shape = _sds((M, N), out_dtype)
        out_spec = pl.BlockSpec((tm, tn), o_map)
    else:
        if isinstance(out, int):
            out_shape = _sds((out, M, N), out_dtype)
        else:
            out_shape = _sds(out.shape, out.dtype)
            in_specs.append(pl.BlockSpec(memory_space=pl.ANY))
            aliases = {len(args): 0}
            args.append(out)
        out_spec = pl.BlockSpec((None, tm, tn), o_map)
    dn = _DN[mode]
    n_in = len(args)

    def body(*refs):
        a_ref, b_ref = refs[0], refs[1]
        add_ref = refs[2] if has_add else None
        o_ref = refs[n_in]
        part = lax.dot_general(a_ref[...].astype(BF16), b_ref[...].astype(BF16), dn, preferred_element_type=F32)

        def finish(r):
            if has_add:
                r = r + add_ref[...]
            o_ref[...] = r.astype(o_ref.dtype)

        if nk == 1:
            finish(part)
        else:
            acc = refs[n_in + 1]
            k = pl.program_id(2)

            @pl.when(k == 0)
            def _():
                acc[...] = part

            @pl.when(k > 0)
            def _():
                acc[...] += part

            @pl.when(k == nk - 1)
            def _():
                finish(acc[...])

    scratch = [pltpu.VMEM((tm, tn), F32)] if nk > 1 else []
    return _call(body, name=name, out_shape=out_shape, grid=grid, in_specs=in_specs, out_specs=out_spec,
                 scratch=scratch, sem=('parallel', 'parallel', 'arbitrary'), aliases=aliases)(*args)


def _rms_fwd(x, g, name):
    R, D = x.shape
    tb = _pick(R, (512, 256, 128))

    def body(x_ref, g_ref, o_ref):
        xv = x_ref[...]
        r = lax.rsqrt(jnp.mean(xv * xv, axis=-1, keepdims=True) + EPS)
        o_ref[...] = (xv * r * g_ref[...]).astype(o_ref.dtype)

    row = pl.BlockSpec((tb, D), lambda i: (i, 0))
    vec = pl.BlockSpec((1, D), lambda i: (0, 0))
    return _call(body, name=name, out_shape=_sds((R, D), BF16), grid=(R // tb,), in_specs=[row, vec],
                 out_specs=row, sem=('parallel',))(x, g)


def _rms_bwd(x, g, dh, dres, name):
    R, D = x.shape
    tb = _pick(R, (512, 256, 128))
    has_res = dres is not None

    def body(*refs):
        x_ref, g_ref, dh_ref = refs[:3]
        dx_ref, dg_ref = refs[-2:]
        xv = x_ref[...]
        dhv = dh_ref[...].astype(F32)
        r = lax.rsqrt(jnp.mean(xv * xv, axis=-1, keepdims=True) + EPS)
        u = dhv * g_ref[...]
        m = jnp.mean(u * xv, axis=-1, keepdims=True)
        dx = r * u - xv * (r * r * r * m)
        if has_res:
            dx = dx + refs[3][...]
        dx_ref[...] = dx

        @pl.when(pl.program_id(0) == 0)
        def _():
            dg_ref[...] = jnp.zeros_like(dg_ref)

        dg_ref[...] += jnp.sum(dhv * xv * r, axis=0, keepdims=True)

    row = pl.BlockSpec((tb, D), lambda i: (i, 0))
    vec = pl.BlockSpec((1, D), lambda i: (0, 0))
    args = [x, g, dh] + ([dres] if has_res else [])
    return _call(body, name=name, out_shape=(_sds((R, D), F32), _sds((1, D), F32)), grid=(R // tb,),
                 in_specs=[row, vec, row] + ([row] if has_res else []), out_specs=(row, vec),
                 sem=('arbitrary',))(*args)


def _final(x, g, tgt):
    R, D = x.shape
    tb = _pick(R, (512, 256, 128))

    def body(x_ref, g_ref, t_ref, dx_ref, dg_ref, loss_ref):
        xv = x_ref[...]
        gv = g_ref[...]
        r = lax.rsqrt(jnp.mean(xv * xv, axis=-1, keepdims=True) + EPS)
        e = xv * r * gv - t_ref[...]
        row_loss = jnp.mean(e * e, axis=-1, keepdims=True)
        blk_loss = 0.5 * jnp.sum(row_loss, axis=0, keepdims=True)
        dy = e * (1.0 / D)
        u = dy * gv
        m = jnp.mean(u * xv, axis=-1, keepdims=True)
        dx_ref[...] = r * u - xv * (r * r * r * m)

        @pl.when(pl.program_id(0) == 0)
        def _():
            dg_ref[...] = jnp.zeros_like(dg_ref)
            loss_ref[...] = jnp.zeros_like(loss_ref)

        dg_ref[...] += jnp.sum(dy * xv * r, axis=0, keepdims=True)
        loss_ref[...] += jnp.broadcast_to(blk_loss, loss_ref.shape)

    row = pl.BlockSpec((tb, D), lambda i: (i, 0))
    vec = pl.BlockSpec((1, D), lambda i: (0, 0))
    one = pl.BlockSpec((1, LANES), lambda i: (0, 0))
    return _call(body, name='final_loss', out_shape=(_sds((R, D), F32), _sds((1, D), F32), _sds((1, LANES), F32)),
                 grid=(R // tb,), in_specs=[row, vec, row], out_specs=(row, vec, one), sem=('arbitrary',))(x, g, tgt)


def _col(tb, w, cb):
    return pl.BlockSpec((tb, w), lambda i: (i, cb))


def _prev(tb, h, w, cb):
    return pl.BlockSpec((h, w), lambda i: (jnp.maximum(i * (tb // h) - 1, 0), cb))


def _next(tb, h, w, cb, rows):
    return pl.BlockSpec((h, w), lambda i: (jnp.minimum((i + 1) * (tb // h), rows // h - 1), cb))


def _full(shape):
    return pl.BlockSpec(shape, lambda i: (0,) * len(shape))


def _mixa_fwd(z, w):
    T = z.shape[0]
    D = w.shape[1]
    tb = _pick(T, (256, 128))
    H = HALO_A

    def body(ab, ac, au, acp, aup, w_ref, o_ref, ext):
        first = pl.program_id(0) == 0
        ext[0:H, :] = jnp.where(first, 0.0, acp[...].astype(F32) * aup[...].astype(F32))
        ext[H:H + tb, :] = ac[...].astype(F32) * au[...].astype(F32)
        cp = jnp.zeros((tb, D), F32)
        for k in range(CONV_A_W):
            off = H - (CONV_A_W - 1) + k
            cp = cp + ext[off:off + tb, :] * w_ref[k:k + 1, :]
        o_ref[...] = (ab[...].astype(F32) * cp).astype(o_ref.dtype)

    return _call(body, name='mixa_fwd', out_shape=_sds((T, D), BF16), grid=(T // tb,),
                 in_specs=[_col(tb, D, 0), _col(tb, D, 1), _col(tb, D, 2), _prev(tb, H, D, 1), _prev(tb, H, D, 2),
                           _full((8, D))],
                 out_specs=_col(tb, D, 0), scratch=[pltpu.VMEM((H + tb, D), F32)], sem=('parallel',))(z, z, z, z, z, w)


def _mixa_bwd(z, w, dy):
    T = z.shape[0]
    D = w.shape[1]
    tb = _pick(T, (256, 128))
    H = HALO_A
    nb = T // tb

    def body(ab, ac, au, acp, aup, abn, dy_ref, dyn, w_ref, dz_ref, dw_ref, pext, dext):
        i = pl.program_id(0)
        a_b, a_c, a_u = ab[...].astype(F32), ac[...].astype(F32), au[...].astype(F32)
        pext[0:H, :] = jnp.where(i == 0, 0.0, acp[...].astype(F32) * aup[...].astype(F32))
        pext[H:H + tb, :] = a_c * a_u
        dyv = dy_ref[...].astype(F32)
        dcp = dyv * a_b
        dext[0:tb, :] = dcp
        dext[tb:tb + H, :] = jnp.where(i == nb - 1, 0.0, dyn[...].astype(F32) * abn[...].astype(F32))

        @pl.when(i == 0)
        def _():
            dw_ref[...] = jnp.zeros_like(dw_ref)

        cp = jnp.zeros((tb, D), F32)
        dp = jnp.zeros((tb, D), F32)
        for k in range(CONV_A_W):
            off = H - (CONV_A_W - 1) + k
            wk = w_ref[k:k + 1, :]
            pk = pext[off:off + tb, :]
            cp = cp + pk * wk
            dp = dp + dext[CONV_A_W - 1 - k:CONV_A_W - 1 - k + tb, :] * wk
            dw_ref[k:k + 1, :] += jnp.sum(dcp * pk, axis=0, keepdims=True)
        dz_ref[:, 0:D] = (dyv * cp).astype(dz_ref.dtype)
        dz_ref[:, D:2 * D] = (dp * a_u).astype(dz_ref.dtype)
        dz_ref[:, 2 * D:3 * D] = (dp * a_c).astype(dz_ref.dtype)

    return _call(body, name='mixa_bwd', out_shape=(_sds((T, 3 * D), BF16), _sds((8, D), F32)), grid=(nb,),
                 in_specs=[_col(tb, D, 0), _col(tb, D, 1), _col(tb, D, 2), _prev(tb, H, D, 1), _prev(tb, H, D, 2),
                           _next(tb, H, D, 0, T), _col(tb, D, 0), _next(tb, H, D, 0, T), _full((8, D))],
                 out_specs=(_col(tb, 3 * D, 0), _full((8, D))),
                 scratch=[pltpu.VMEM((H + tb, D), F32), pltpu.VMEM((tb + H, D), F32)],
                 sem=('arbitrary',))(z, z, z, z, z, z, dy, dy, w)


def _split3(v):
    hi = v.astype(BF16)
    r1 = v - hi.astype(F32)
    mid = r1.astype(BF16)
    lo = (r1 - mid.astype(F32)).astype(BF16)
    return hi, mid, lo


def _tri_dot(tri, v):
    hi, mid, lo = _split3(v)
    d = lambda p: jnp.dot(tri, p, preferred_element_type=F32)
    return d(hi) + d(mid) + d(lo)


def _fox_c(zf, bf):
    T = zf.shape[0]
    tb = LANES

    def body(zf_ref, b_ref, c_ref, carry):
        @pl.when(pl.program_id(0) == 0)
        def _():
            carry[...] = jnp.zeros_like(carry)

        xv = zf_ref[...] + b_ref[...]
        lf = jnp.minimum(xv, 0.0) - jnp.log(1.0 + jnp.exp(-jnp.abs(xv)))
        r = lax.broadcasted_iota(jnp.int32, (tb, tb), 0)
        c = lax.broadcasted_iota(jnp.int32, (tb, tb), 1)
        tri = (r >= c).astype(BF16)
        cs = _tri_dot(tri, lf) + carry[...]
        c_ref[...] = cs
        carry[...] = cs[tb - 1:tb, :]

    blk = pl.BlockSpec((tb, LANES), lambda i: (i, 0))
    return _call(body, name='fox_cumsum', out_shape=_sds((T, LANES), F32), grid=(T // tb,),
                 in_specs=[blk, _full((1, LANES))], out_specs=blk, scratch=[pltpu.VMEM((1, LANES), F32)],
                 sem=('arbitrary',))(zf, bf)


def _fox_c_bwd(dc, zf, bf):
    T = zf.shape[0]
    tb = LANES
    nb = T // tb

    def body(dc_ref, zf_ref, b_ref, dz_ref, db_ref, carry):
        @pl.when(pl.program_id(0) == 0)
        def _():
            carry[...] = jnp.zeros_like(carry)
            db_ref[...] = jnp.zeros_like(db_ref)

        r = lax.broadcasted_iota(jnp.int32, (tb, tb), 0)
        c = lax.broadcasted_iota(jnp.int32, (tb, tb), 1)
        tri = (c >= r).astype(BF16)
        dlf = _tri_dot(tri, dc_ref[...]) + carry[...]
        carry[...] = dlf[0:1, :]
        xv = zf_ref[...] + b_ref[...]
        lane = lax.broadcasted_iota(jnp.int32, (tb, LANES), 1)
        dz = jnp.where(lane < FOX_HEADS, dlf * _sigmoid(-xv), 0.0)
        dz_ref[...] = dz
        db_ref[...] += jnp.sum(dz, axis=0, keepdims=True)

    blk = pl.BlockSpec((tb, LANES), lambda i: (nb - 1 - i, 0))
    return _call(body, name='fox_cumsum_bwd', out_shape=(_sds((T, LANES), F32), _sds((1, LANES), F32)), grid=(nb,),
                 in_specs=[blk, blk, _full((1, LANES))], out_specs=(blk, _full((1, LANES))),
                 scratch=[pltpu.VMEM((1, LANES), F32)], sem=('arbitrary',))(dc, zf, bf)


def _head_cols(c_ref, ct_ref, h, T):
    lane = lax.broadcasted_iota(jnp.int32, (T, LANES), 1)
    cq = jnp.sum(jnp.where(lane == h, c_ref[...], 0.0), axis=1, keepdims=True)
    sub = lax.broadcasted_iota(jnp.int32, (FOX_HEADS, T), 0)
    ck = jnp.sum(jnp.where(sub == h, ct_ref[...], 0.0), axis=0, keepdims=True)
    return cq, ck


def _fox_scores(qs, ks, cq, ck, q0, ke, scale):
    tq = ke - q0
    s = lax.dot_general(qs, ks, _DN['nt'], preferred_element_type=F32) * scale
    s = s + (cq[q0:ke, :] - ck[:, 0:ke])
    r = lax.broadcasted_iota(jnp.int32, (tq, ke), 0) + q0
    c = lax.broadcasted_iota(jnp.int32, (tq, ke), 1)
    return jnp.where(r >= c, s, -jnp.inf)


def _fox_fwd(z, c, ct):
    T = z.shape[0]
    D = z.shape[1] // 11
    dh = D // FOX_HEADS
    hp = LANES // dh
    ncb = D // LANES
    tq = _pick(T, (256, 128))
    scale = dh ** -0.5

    def body(q_ref, k_ref, v_ref, c_ref, ct_ref, o_ref, lse_ref):
        j = pl.program_id(0)
        lse_ref[...] = jnp.zeros_like(lse_ref)
        for hh in range(hp):
            cq, ck = _head_cols(c_ref, ct_ref, j * hp + hh, T)
            hs = slice(hh * dh, (hh + 1) * dh)
            for q0 in range(0, T, tq):
                ke = q0 + tq
                s = _fox_scores(q_ref[q0:ke, hs], k_ref[0:ke, hs], cq, ck, q0, ke, scale)
                m = jnp.max(s, axis=-1, keepdims=True)
                p = jnp.exp(s - m)
                l = jnp.sum(p, axis=-1, keepdims=True)
                o = jnp.dot(p.astype(BF16), v_ref[0:ke, hs], preferred_element_type=F32)
                o_ref[q0:ke, hs] = (o * (1.0 / l)).astype(o_ref.dtype)
                lse_ref[q0:ke, hh:hh + 1] = m + jnp.log(l)

    blk = lambda cb0: pl.BlockSpec((T, LANES), lambda j: (0, cb0 + j))
    return _call(body, name='fox_fwd', out_shape=(_sds((T, D), BF16), _sds((ncb, T, LANES), F32)), grid=(ncb,),
                 in_specs=[blk(3 * ncb), blk(4 * ncb), blk(5 * ncb), _full((T, LANES)), _full((FOX_HEADS, T))],
                 out_specs=(blk(0), pl.BlockSpec((None, T, LANES), lambda j: (j, 0, 0))),
                 sem=('parallel',))(z, z, z, c, ct)


def _fox_bwd(z, c, ct, o, lse, do):
    T = z.shape[0]
    D = o.shape[1]
    dh = D // FOX_HEADS
    hp = LANES // dh
    ncb = D // LANES
    tq = _pick(T, (256, 128))
    scale = dh ** -0.5

    def body(q_ref, k_ref, v_ref, c_ref, ct_ref, o_ref, lse_ref, do_ref, dq_ref, dk_ref, dv_ref, dcq_ref, dck_ref,
             dk_acc, dv_acc):
        j = pl.program_id(0)
        dk_acc[...] = jnp.zeros_like(dk_acc)
        dv_acc[...] = jnp.zeros_like(dv_acc)
        dcq_ref[...] = jnp.zeros_like(dcq_ref)
        dck_ref[...] = jnp.zeros_like(dck_ref)
        for hh in range(hp):
            cq, ck = _head_cols(c_ref, ct_ref, j * hp + hh, T)
            hs = slice(hh * dh, (hh + 1) * dh)
            for q0 in range(0, T, tq):
                ke = q0 + tq
                qs, ks, vs = q_ref[q0:ke, hs], k_ref[0:ke, hs], v_ref[0:ke, hs]
                dos = do_ref[q0:ke, hs]
                s = _fox_scores(qs, ks, cq, ck, q0, ke, scale)
                p = jnp.exp(s - lse_ref[q0:ke, hh:hh + 1])
                dp = lax.dot_general(dos, vs, _DN['nt'], preferred_element_type=F32)
                delta = jnp.sum(dos.astype(F32) * o_ref[q0:ke, hs].astype(F32), axis=-1, keepdims=True)
                ds = p * (dp - delta)
                dsb = ds.astype(BF16)
                dq = jnp.dot(dsb, ks, preferred_element_type=F32) * scale
                dq_ref[q0:ke, hs] = dq.astype(dq_ref.dtype)
                dk_acc[0:ke, hs] += lax.dot_general(dsb, qs, _DN['tn'], preferred_element_type=F32) * scale
                dv_acc[0:ke, hs] += lax.dot_general(p.astype(BF16), dos, _DN['tn'], preferred_element_type=F32)
                dcq_ref[q0:ke, hh:hh + 1] = jnp.sum(ds, axis=-1, keepdims=True)
                dck_ref[hh:hh + 1, 0:ke] -= jnp.sum(ds, axis=0, keepdims=True)
        dk_ref[...] = dk_acc[...].astype(dk_ref.dtype)
        dv_ref[...] = dv_acc[...].astype(dv_ref.dtype)

    blk = lambda cb0: pl.BlockSpec((T, LANES), lambda j: (0, cb0 + j))
    pair = pl.BlockSpec((None, T, LANES), lambda j: (j, 0, 0))
    grad = _sds((T, D), BF16)
    return _call(body, name='fox_bwd',
                 out_shape=(grad, grad, grad, _sds((ncb, T, LANES), F32), _sds((ncb, 8, T), F32)), grid=(ncb,),
                 in_specs=[blk(3 * ncb), blk(4 * ncb), blk(5 * ncb), _full((T, LANES)), _full((FOX_HEADS, T)), blk(0),
                           pair, blk(0)],
                 out_specs=(blk(0), blk(0), blk(0), pair, pl.BlockSpec((None, 8, T), lambda j: (j, 0, 0))),
                 scratch=[pltpu.VMEM((T, LANES), F32), pltpu.VMEM((T, LANES), F32)],
                 sem=('parallel',))(z, z, z, c, ct, o, lse, do)


def _ln_parts(u1, gain, bias):
    mu = jnp.mean(u1, axis=-1, keepdims=True)
    xc = u1 - mu
    rstd = lax.rsqrt(jnp.mean(xc * xc, axis=-1, keepdims=True) + EPS)
    xhat = xc * rstd
    return xhat, rstd, xhat * gain + bias


def _mixc_fwd(z, w, cb, gain, bias):
    T = z.shape[0]
    D = w.shape[1]
    tb = _pick(T, (256, 128))
    H = HALO_C

    def body(cv, cg, cvp, cgp, w_ref, cb_ref, g_ref, b_ref, o_ref, u1_ref, ext):
        first = pl.program_id(0) == 0
        ext[0:H, :] = jnp.where(first, 0.0, cvp[...].astype(F32) * _sigmoid(cgp[...].astype(F32)))
        ext[H:H + tb, :] = cv[...].astype(F32) * _sigmoid(cg[...].astype(F32))
        u1 = jnp.zeros((tb, D), F32)
        for k in range(CONV_C_W):
            off = H - (CONV_C_W - 1) + k
            u1 = u1 + ext[off:off + tb, :] * w_ref[k:k + 1, :]
        u1 = u1 + cb_ref[...]
        u1_ref[...] = u1
        _, _, u2 = _ln_parts(u1, g_ref[...], b_ref[...])
        o_ref[...] = (u2 * _sigmoid(u2)).astype(o_ref.dtype)

    vec = _full((1, D))
    row = _col(tb, D, 0)
    return _call(body, name='mixc_fwd', out_shape=(_sds((T, D), BF16), _sds((T, D), F32)), grid=(T // tb,),
                 in_specs=[_col(tb, D, 6), _col(tb, D, 7), _prev(tb, H, D, 6), _prev(tb, H, D, 7), _full((32, D)), vec,
                           vec, vec],
                 out_specs=(row, row), scratch=[pltpu.VMEM((H + tb, D), F32)],
                 sem=('parallel',))(z, z, z, z, w, cb, gain, bias)


def _mixc_bwd(z, w, gain, bias, u1, du3):
    T = z.shape[0]
    D = w.shape[1]
    tb = _pick(T, (256, 128))
    H = HALO_C
    nb = T // tb

    def du1_of(u1v, du3v, gv, bv):
        xhat, rstd, u2 = _ln_parts(u1v, gv, bv)
        sg = _sigmoid(u2)
        du2 = du3v * (sg * (1.0 + u2 * (1.0 - sg)))
        dxh = du2 * gv
        m1 = jnp.mean(dxh, axis=-1, keepdims=True)
        m2 = jnp.mean(dxh * xhat, axis=-1, keepdims=True)
        return rstd * (dxh - m1 - xhat * m2), du2, xhat

    def body(cv, cg, cvp, cgp, w_ref, g_ref, b_ref, u1_ref, u1n, du3_ref, du3n, dz_ref, dw_ref, dcb_ref, dg_ref,
             db_ref, uext, dext):
        i = pl.program_id(0)
        gv, bv = g_ref[...], b_ref[...]
        c_val = cv[...].astype(F32)
        sg = _sigmoid(cg[...].astype(F32))
        uext[0:H, :] = jnp.where(i == 0, 0.0, cvp[...].astype(F32) * _sigmoid(cgp[...].astype(F32)))
        uext[H:H + tb, :] = c_val * sg
        du1, du2, xhat = du1_of(u1_ref[...], du3_ref[...].astype(F32), gv, bv)
        du1n, _, _ = du1_of(u1n[...], du3n[...].astype(F32), gv, bv)
        dext[0:tb, :] = du1
        dext[tb:tb + H, :] = jnp.where(i == nb - 1, 0.0, du1n)

        @pl.when(i == 0)
        def _():
            dw_ref[...] = jnp.zeros_like(dw_ref)
            dcb_ref[...] = jnp.zeros_like(dcb_ref)
            dg_ref[...] = jnp.zeros_like(dg_ref)
            db_ref[...] = jnp.zeros_like(db_ref)

        dcb_ref[...] += jnp.sum(du1, axis=0, keepdims=True)
        dg_ref[...] += jnp.sum(du2 * xhat, axis=0, keepdims=True)
        db_ref[...] += jnp.sum(du2, axis=0, keepdims=True)
        du0 = jnp.zeros((tb, D), F32)
        for k in range(CONV_C_W):
            off = H - (CONV_C_W - 1) + k
            du0 = du0 + dext[CONV_C_W - 1 - k:CONV_C_W - 1 - k + tb, :] * w_ref[k:k + 1, :]
            dw_ref[k:k + 1, :] += jnp.sum(du1 * uext[off:off + tb, :], axis=0, keepdims=True)
        dz_ref[:, 0:D] = (du0 * sg).astype(dz_ref.dtype)
        dz_ref[:, D:2 * D] = (du0 * c_val * sg * (1.0 - sg)).astype(dz_ref.dtype)

    vec = _full((1, D))
    row = _col(tb, D, 0)
    nxt = _next(tb, H, D, 0, T)
    return _call(body, name='mixc_bwd',
                 out_shape=(_sds((T, 2 * D), BF16), _sds((32, D), F32), _sds((1, D), F32), _sds((1, D), F32),
                            _sds((1, D), F32)), grid=(nb,),
                 in_specs=[_col(tb, D, 6), _col(tb, D, 7), _prev(tb, H, D, 6), _prev(tb, H, D, 7), _full((32, D)), vec,
                           vec, row, nxt, row, nxt],
                 out_specs=(_col(tb, 2 * D, 0), _full((32, D)), vec, vec, vec),
                 scratch=[pltpu.VMEM((H + tb, D), F32), pltpu.VMEM((tb + H, D), F32)],
                 sem=('arbitrary',))(z, z, z, z, w, gain, bias, u1, u1, du3, du3)


def _gate_fwd(z, bg, ya, yb, yc):
    T = z.shape[0]
    D = ya.shape[1]
    tb = _pick(T, (256, 128))

    def body(ga, gb, gc, bg_ref, ya_ref, yb_ref, yc_ref, o_ref):
        acc = jnp.zeros((tb, D), F32)
        for n, (g, y) in enumerate(((ga, ya_ref), (gb, yb_ref), (gc, yc_ref))):
            acc = acc + _sigmoid(g[...].astype(F32) + bg_ref[:, n * D:(n + 1) * D]) * y[...].astype(F32)
        o_ref[...] = acc.astype(o_ref.dtype)

    row = _col(tb, D, 0)
    return _call(body, name='gate_fwd', out_shape=_sds((T, D), BF16), grid=(T // tb,),
                 in_specs=[_col(tb, D, 8), _col(tb, D, 9), _col(tb, D, 10), _full((1, 3 * D)), row, row, row],
                 out_specs=row, sem=('parallel',))(z, z, z, bg, ya, yb, yc)


def _gate_bwd(z, bg, ya, yb, yc, dm):
    T = z.shape[0]
    D = ya.shape[1]
    tb = _pick(T, (256, 128))

    def body(ga, gb, gc, bg_ref, ya_ref, yb_ref, yc_ref, dm_ref, dya, dyb, dyc, dg_ref, dbg_ref):
        @pl.when(pl.program_id(0) == 0)
        def _():
            dbg_ref[...] = jnp.zeros_like(dbg_ref)

        dmv = dm_ref[...]
        for n, (g, y, dy) in enumerate(((ga, ya_ref, dya), (gb, yb_ref, dyb), (gc, yc_ref, dyc))):
            cols = slice(n * D, (n + 1) * D)
            sg = _sigmoid(g[...].astype(F32) + bg_ref[:, cols])
            dy[...] = (dmv * sg).astype(dy.dtype)
            dg = dmv * y[...].astype(F32) * sg * (1.0 - sg)
            dg_ref[:, cols] = dg.astype(dg_ref.dtype)
            dbg_ref[:, cols] += jnp.sum(dg, axis=0, keepdims=True)

    row = _col(tb, D, 0)
    act = _sds((T, D), BF16)
    return _call(body, name='gate_bwd', out_shape=(act, act, act, _sds((T, 3 * D), BF16), _sds((1, 3 * D), F32)),
                 grid=(T // tb,),
                 in_specs=[_col(tb, D, 8), _col(tb, D, 9), _col(tb, D, 10), _full((1, 3 * D)), row, row, row, row],
                 out_specs=(row, row, row, _col(tb, 3 * D, 0), _full((1, 3 * D))),
                 sem=('arbitrary',))(z, z, z, bg, ya, yb, yc, dm)


def _xattn_probs(qs, ks, scale):
    s = lax.dot_general(qs, ks, _DN['nt'], preferred_element_type=F32) * scale
    p = jnp.exp(s - jnp.max(s, axis=-1, keepdims=True))
    return p * (1.0 / jnp.sum(p, axis=-1, keepdims=True))


def _xattn_fwd(q, kv):
    T, DX = q.shape
    M = kv.shape[0]
    dh = DX // X_HEADS
    tb = _pick(T, (512, 256, 128))
    scale = dh ** -0.5

    def body(q_ref, kv_ref, o_ref):
        for h in range(X_HEADS):
            hs = slice(h * dh, (h + 1) * dh)
            p = _xattn_probs(q_ref[:, hs], kv_ref[:, hs], scale)
            o_ref[:, hs] = jnp.dot(p.astype(BF16), kv_ref[:, DX + h * dh:DX + (h + 1) * dh],
                                   preferred_element_type=F32).astype(o_ref.dtype)

    row = _col(tb, DX, 0)
    return _call(body, name='xattn_fwd', out_shape=_sds((T, DX), BF16), grid=(T // tb,),
                 in_specs=[row, _full((M, 2 * DX))], out_specs=row, sem=('parallel',))(q, kv)


def _xattn_bwd(q, kv, do):
    T, DX = q.shape
    M = kv.shape[0]
    dh = DX // X_HEADS
    tb = _pick(T, (512, 256, 128))
    scale = dh ** -0.5

    def body(q_ref, kv_ref, do_ref, dq_ref, dkv_ref):
        @pl.when(pl.program_id(0) == 0)
        def _():
            dkv_ref[...] = jnp.zeros_like(dkv_ref)

        for h in range(X_HEADS):
            hs = slice(h * dh, (h + 1) * dh)
            vs_cols = slice(DX + h * dh, DX + (h + 1) * dh)
            qs, ks, vs = q_ref[:, hs], kv_ref[:, hs], kv_ref[:, vs_cols]
            dos = do_ref[:, hs].astype(BF16)
            p = _xattn_probs(qs, ks, scale)
            dp = lax.dot_general(dos, vs, _DN['nt'], preferred_element_type=F32)
            ds = p * (dp - jnp.sum(p * dp, axis=-1, keepdims=True))
            dsb = ds.astype(BF16)
            dq_ref[:, hs] = (jnp.dot(dsb, ks, preferred_element_type=F32) * scale).astype(dq_ref.dtype)
            dkv_ref[:, hs] += lax.dot_general(dsb, qs, _DN['tn'], preferred_element_type=F32) * scale
            dkv_ref[:, vs_cols] += lax.dot_general(p.astype(BF16), dos, _DN['tn'], preferred_element_type=F32)

    row = _col(tb, DX, 0)
    return _call(body, name='xattn_bwd', out_shape=(_sds((T, DX), BF16), _sds((M, 2 * DX), F32)), grid=(T // tb,),
                 in_specs=[row, _full((M, 2 * DX)), row], out_specs=(row, _full((M, 2 * DX))),
                 sem=('arbitrary',))(q, kv, do)


def _swiglu_fwd(gu):
    T, F2 = gu.shape
    F = F2 // 2
    tb = _pick(T, (256, 128))

    def body(g_ref, u_ref, o_ref):
        g = g_ref[...].astype(F32)
        o_ref[...] = (g * _sigmoid(g) * u_ref[...].astype(F32)).astype(o_ref.dtype)

    return _call(body, name='swiglu_fwd', out_shape=_sds((T, F), BF16), grid=(T // tb,),
                 in_specs=[_col(tb, F, 0), _col(tb, F, 1)], out_specs=_col(tb, F, 0), sem=('parallel',))(gu, gu)


def _swiglu_bwd(gu, da):
    T, F2 = gu.shape
    F = F2 // 2
    tb = _pick(T, (256, 128))

    def body(g_ref, u_ref, da_ref, o_ref):
        g = g_ref[...].astype(F32)
        u = u_ref[...].astype(F32)
        dav = da_ref[...].astype(F32)
        sg = _sigmoid(g)
        o_ref[:, 0:F] = (dav * u * (sg * (1.0 + g * (1.0 - sg)))).astype(o_ref.dtype)
        o_ref[:, F:F2] = (dav * g * sg).astype(o_ref.dtype)

    return _call(body, name='swiglu_bwd', out_shape=_sds((T, F2), BF16), grid=(T // tb,),
                 in_specs=[_col(tb, F, 0), _col(tb, F, 1), _col(tb, F, 0)], out_specs=_col(tb, F2, 0),
                 sem=('parallel',))(gu, gu, da)


def _place():
    x, y, c = lax.axis_index('x'), lax.axis_index('y'), lax.axis_index('c')
    return x, y, c


def _allgather(arrs, lead, name):
    n = len(arrs)

    def out_shape(a, ld):
        return (N_DEV,) + a.shape if ld else (a.shape[0], N_DEV) + a.shape[1:]

    def body(*refs):
        ins, outs = refs[:n], refs[n:2 * n]
        send_sems, recv_sems, local_sems = refs[2 * n:]
        x, y, c = _place()
        me, sibling = (x, y, c), (x, y, 1 - c)
        chips = [(1 - x, y), (x, 1 - y), (1 - x, 1 - y)]

        def slot(a, dev):
            idx = 4 * dev[0] + 2 * dev[1] + dev[2]
            return outs[a].at[idx] if lead[a] else outs[a].at[:, idx]

        def copy(a, k, block, to, src=None):
            return pltpu.make_async_remote_copy(
                src_ref=slot(a, block) if src is None else src, dst_ref=slot(a, block),
                send_sem=send_sems.at[a * 7 + k], recv_sem=recv_sems.at[a * 7 + k], device_id=to, device_id_type=MESH)

        mine = [pltpu.make_async_copy(ins[a], slot(a, me), local_sems.at[a]) for a in range(n)]
        for cp in mine:
            cp.start()
        first = []
        for a in range(n):
            first.append(copy(a, 0, me, sibling, src=ins[a]))
            first += [copy(a, 1 + j, me, (*chip, c), src=ins[a]) for j, chip in enumerate(chips)]
        for cp in first:
            cp.start()
        passed = []
        for a in range(n):
            for j, chip in enumerate(chips):
                copy(a, 1 + j, (*chip, c), me).wait_recv()
                fwd = copy(a, 4 + j, (*chip, c), sibling)
                fwd.start()
                passed.append(fwd)
        for a in range(n):
            copy(a, 0, sibling, me).wait_recv()
            for j, chip in enumerate(chips):
                copy(a, 4 + j, (*chip, 1 - c), me).wait_recv()
        for cp in first + passed:
            cp.wait_send()
        for cp in mine:
            cp.wait()

    any_spec = pl.BlockSpec(memory_space=pl.ANY)
    return _call(body, name=name, out_shape=tuple(_sds(out_shape(a, ld), a.dtype) for a, ld in zip(arrs, lead)),
                 in_specs=[any_spec] * n, out_specs=tuple([any_spec] * n),
                 scratch=[pltpu.SemaphoreType.DMA((7 * n,)), pltpu.SemaphoreType.DMA((7 * n,)),
                          pltpu.SemaphoreType.DMA((n,))])(*arrs)


def _alltoall(arrs, lead, name):
    n = len(arrs)

    def out_shape(a, ld):
        return (N_DEV,) + (a.shape[1:] if ld else a.shape[:1] + a.shape[2:])

    def body(*refs):
        ins, outs = refs[:n], refs[n:2 * n]
        send_sems, recv_sems, local_sems = refs[2 * n:]
        x, y, c = _place()
        me = 4 * x + 2 * y + c

        def chunk(a, idx):
            return ins[a].at[idx] if lead[a] else ins[a].at[:, idx]

        def copy(a, r):
            px, py, pc = x ^ (r >> 2), y ^ ((r >> 1) & 1), c ^ (r & 1)
            return pltpu.make_async_remote_copy(
                src_ref=chunk(a, 4 * px + 2 * py + pc), dst_ref=outs[a].at[me],
                send_sem=send_sems.at[a * 7 + r - 1], recv_sem=recv_sems.at[a * 7 + r - 1],
                device_id=(px, py, pc), device_id_type=MESH)

        mine = [pltpu.make_async_copy(chunk(a, me), outs[a].at[me], local_sems.at[a]) for a in range(n)]
        for cp in mine:
            cp.start()
        order = [1, 4, 2, 5, 3, 6, 7]
        sent = [copy(a, r) for a in range(n) for r in order]
        for cp in sent:
            cp.start()
        for cp in sent:
            cp.wait_recv()
        for cp in sent:
            cp.wait_send()
        for cp in mine:
            cp.wait()

    any_spec = pl.BlockSpec(memory_space=pl.ANY)
    return _call(body, name=name, out_shape=tuple(_sds(out_shape(a, ld), a.dtype) for a, ld in zip(arrs, lead)),
                 in_specs=[any_spec] * n, out_specs=tuple([any_spec] * n),
                 scratch=[pltpu.SemaphoreType.DMA((7 * n,)), pltpu.SemaphoreType.DMA((7 * n,)),
                          pltpu.SemaphoreType.DMA((n,))])(*arrs)


def _adamw(parts, w, m, v, name):
    R, C = w.shape
    tb = _pick(R, (128, 64, 32, 16, 8))
    c1 = 1.0 - ADAM_B1 ** ADAM_STEP
    c2 = 1.0 - ADAM_B2 ** ADAM_STEP

    def body(p_ref, w_ref, m_ref, v_ref, g_out, d_out, m_out, v_out):
        g = p_ref[0].astype(F32)
        for s in range(1, N_DEV):
            g = g + p_ref[s].astype(F32)
        mn = ADAM_B1 * m_ref[...] + (1.0 - ADAM_B1) * g
        vn = ADAM_B2 * v_ref[...] + (1.0 - ADAM_B2) * (g * g)
        m_hat = mn / c1
        v_hat = vn / c2
        g_out[...] = g
        d_out[...] = -ADAM_LR * (m_hat / (jnp.sqrt(v_hat) + ADAM_EPS) + ADAM_WD * w_ref[...])
        m_out[...] = mn
        v_out[...] = vn

    row = pl.BlockSpec((tb, C), lambda i: (i, 0))
    o = _sds((R, C), F32)
    return _call(body, name=name, out_shape=(o, o, o, o), grid=(R // tb,),
                 in_specs=[pl.BlockSpec((N_DEV, tb, C), lambda i: (0, i, 0)), row, row, row],
                 out_specs=(row, row, row, row), sem=('parallel',))(parts, w, m, v)


def _cols_to_stack(g):
    return jnp.transpose(g, (1, 2, 0, 3)).reshape(g.shape[1], g.shape[2], N_DEV * g.shape[3])


def _stack_to_cols(w):
    L, K, N = w.shape
    return jnp.transpose(w.reshape(L, K, N_DEV, N // N_DEV), (2, 0, 1, 3))


def kernel(x, mem, mix_norm, w_in, b_gate, b_forget, conv_a, w_out_a, w_out_b, conv_c, conv_c_bias, ln_c_gain, ln_c_bias, w_out_c, w_o, xattn_norm, mem_norm, w_xq, w_xkv, w_xo, ffn_norm, w_gate_up, w_down, final_norm, loss_target, m_mix_norm, m_w_in, m_b_gate, m_b_forget, m_conv_a, m_w_out_a, m_w_out_b, m_conv_c, m_conv_c_bias, m_ln_c_gain, m_ln_c_bias, m_w_out_c, m_w_o, m_xattn_norm, m_mem_norm, m_w_xq, m_w_xkv, m_w_xo, m_ffn_norm, m_w_gate_up, m_w_down, m_final_norm, v_mix_norm, v_w_in, v_b_gate, v_b_forget, v_conv_a, v_w_out_a, v_w_out_b, v_conv_c, v_conv_c_bias, v_ln_c_gain, v_ln_c_bias, v_w_out_c, v_w_o, v_xattn_norm, v_mem_norm, v_w_xq, v_w_xkv, v_w_xo, v_ffn_norm, v_w_gate_up, v_w_down, v_final_norm):
    P = dict(zip(ARG_NAMES, (x, mem, mix_norm, w_in, b_gate, b_forget, conv_a, w_out_a, w_out_b, conv_c, conv_c_bias, ln_c_gain, ln_c_bias, w_out_c, w_o, xattn_norm, mem_norm, w_xq, w_xkv, w_xo, ffn_norm, w_gate_up, w_down, final_norm, loss_target, m_mix_norm, m_w_in, m_b_gate, m_b_forget, m_conv_a, m_w_out_a, m_w_out_b, m_conv_c, m_conv_c_bias, m_ln_c_gain, m_ln_c_bias, m_w_out_c, m_w_o, m_xattn_norm, m_mem_norm, m_w_xq, m_w_xkv, m_w_xo, m_ffn_norm, m_w_gate_up, m_w_down, m_final_norm, v_mix_norm, v_w_in, v_b_gate, v_b_forget, v_conv_a, v_w_out_a, v_w_out_b, v_conv_c, v_conv_c_bias, v_ln_c_gain, v_ln_c_bias, v_w_out_c, v_w_o, v_xattn_norm, v_mem_norm, v_w_xq, v_w_xkv, v_w_xo, v_ffn_norm, v_w_gate_up, v_w_down, v_final_norm)))
    L, D = mix_norm.shape
    T = x.shape[1]
    F = w_down.shape[1] * N_DEV
    DX = w_xq.shape[2]
    n_in = w_in.shape[2] * N_DEV
    f_off = 6 * D

    conv_names = ('conv_a', 'conv_c')
    shard = {w: (P[w] if w in conv_names else P[w].astype(BF16)) for w in SHARDED}
    gathered = dict(zip(SHARDED, _allgather([shard[w] for w in SHARDED], [w not in ROW_SHARDED for w in SHARDED],
                                            'gather_weights')))
    W = {w: gathered[w].reshape(L, -1, gathered[w].shape[-1]) for w in ROW_SHARDED}
    w_in_full = _cols_to_stack(gathered['w_in'])
    W['w_main'] = jnp.concatenate([w_in_full[..., :f_off], w_in_full[..., f_off + FOX_HEADS:]], axis=-1)
    W['w_f'] = jnp.pad(w_in_full[..., f_off:f_off + FOX_HEADS], ((0, 0), (0, 0), (0, LANES - FOX_HEADS)))
    W['w_gate_up'] = _cols_to_stack(gathered['w_gate_up'])
    W['w_xo'] = _cols_to_stack(gathered['w_xo'])
    conv_a_full = jnp.pad(_cols_to_stack(gathered['conv_a']), ((0, 0), (0, 8 - CONV_A_W), (0, 0)))
    conv_c_full = jnp.pad(_cols_to_stack(gathered['conv_c']), ((0, 0), (0, 32 - CONV_C_W), (0, 0)))
    b_forget_pad = jnp.pad(b_forget, ((0, 0), (0, LANES - FOX_HEADS)))

    row = lambda a, l: a[l][None, :]

    mem_n = _rms_fwd(mem[0], mem_norm[None, :], 'rms_mem')
    xs = x[0]
    saved = []
    for l in range(L):
        s = {'x0': xs}
        s['h1'] = _rms_fwd(xs, row(mix_norm, l), 'rms_mix')
        s['z'] = z = _mm(s['h1'], W['w_main'], mode='nn', lb=l, name='mm_in')
        s['zf'] = _mm(s['h1'], W['w_f'], mode='nn', lb=l, out_dtype=F32, name='mm_in_f')
        s['ya_pre'] = _mixa_fwd(z, conv_a_full[l])
        s['ya'] = _mm(s['ya_pre'], W['w_out_a'], mode='nn', lb=l, name='mm_out_a')
        s['c'] = _fox_c(s['zf'], row(b_forget_pad, l))
        s['ct'] = jnp.transpose(s['c'][:, :FOX_HEADS])
        s['o'], s['lse'] = _fox_fwd(z, s['c'], s['ct'])
        s['yb'] = _mm(s['o'], W['w_out_b'], mode='nn', lb=l, name='mm_out_b')
        s['u3'], s['u1'] = _mixc_fwd(z, conv_c_full[l], row(conv_c_bias, l), row(ln_c_gain, l), row(ln_c_bias, l))
        s['yc'] = _mm(s['u3'], W['w_out_c'], mode='nn', lb=l, name='mm_out_c')
        s['merged'] = _gate_fwd(z, row(b_gate, l), s['ya'], s['yb'], s['yc'])
        xs = _mm(s['merged'], W['w_o'], mode='nn', lb=l, add=xs, out_dtype=F32, name='mm_o')
        s['x1'] = xs
        s['h2'] = _rms_fwd(xs, row(xattn_norm, l), 'rms_xattn')
        s['qx'] = _mm(s['h2'], W['w_xq'], mode='nn', lb=l, name='mm_xq')
        s['kv'] = _mm(mem_n, W['w_xkv'], mode='nn', lb=l, name='mm_xkv')
        s['ox'] = _xattn_fwd(s['qx'], s['kv'])
        xs = _mm(s['ox'], W['w_xo'], mode='nn', lb=l, add=xs, out_dtype=F32, name='mm_xo')
        s['x2'] = xs
        s['h3'] = _rms_fwd(xs, row(ffn_norm, l), 'rms_ffn')
        s['gu'] = _mm(s['h3'], W['w_gate_up'], mode='nn', lb=l, name='mm_gate_up')
        s['act'] = _swiglu_fwd(s['gu'])
        xs = _mm(s['act'], W['w_down'], mode='nn', lb=l, add=xs, out_dtype=F32, name='mm_down')
        saved.append(s)

    dx, d_final_norm, loss_part = _final(xs, final_norm[None, :], loss_target[0])

    GW = {}
    GS = {w: [None] * L for w in SMALL}
    G_conv_a, G_conv_c = [None] * L, [None] * L
    d_mem_n = None

    def wgrad(key, a, b, l):
        GW[key] = _mm(a, b, mode='tn', name='mmg_' + key, out=GW.get(key, L), lo=l)

    for l in reversed(range(L)):
        s = saved[l]
        z = s['z']
        d_act = _mm(dx, W['w_down'], mode='nt', lb=l, name='mmb_down')
        wgrad('w_down', s['act'], dx, l)
        dgu = _swiglu_bwd(s['gu'], d_act)
        dh3 = _mm(dgu, W['w_gate_up'], mode='nt', lb=l, out_dtype=F32, name='mmb_gate_up')
        wgrad('w_gate_up', s['h3'], dgu, l)
        dx, GS['ffn_norm'][l] = _rms_bwd(s['x2'], row(ffn_norm, l), dh3, dx, 'rms_ffn_bwd')
        d_ox = _mm(dx, W['w_xo'], mode='nt', lb=l, out_dtype=F32, name='mmb_xo')
        wgrad('w_xo', s['ox'], dx, l)
        dqx, dkv = _xattn_bwd(s['qx'], s['kv'], d_ox)
        dh2 = _mm(dqx, W['w_xq'], mode='nt', lb=l, out_dtype=F32, name='mmb_xq')
        wgrad('w_xq', s['h2'], dqx, l)
        wgrad('w_xkv', mem_n, dkv, l)
        d_mem_n = _mm(dkv, W['w_xkv'], mode='nt', lb=l, add=d_mem_n, out_dtype=F32, name='mmb_xkv')
        dx, GS['xattn_norm'][l] = _rms_bwd(s['x1'], row(xattn_norm, l), dh2, dx, 'rms_xattn_bwd')
        dm = _mm(dx, W['w_o'], mode='nt', lb=l, out_dtype=F32, name='mmb_o')
        wgrad('w_o', s['merged'], dx, l)
        dya, dyb, dyc, dz_g, GS['b_gate'][l] = _gate_bwd(z, row(b_gate, l), s['ya'], s['yb'], s['yc'], dm)
        du3 = _mm(dyc, W['w_out_c'], mode='nt', lb=l, out_dtype=F32, name='mmb_out_c')
        wgrad('w_out_c', s['u3'], dyc, l)
        dz_c, G_conv_c[l], GS['conv_c_bias'][l], GS['ln_c_gain'][l], GS['ln_c_bias'][l] = _mixc_bwd(
            z, conv_c_full[l], row(ln_c_gain, l), row(ln_c_bias, l), s['u1'], du3)
        do = _mm(dyb, W['w_out_b'], mode='nt', lb=l, name='mmb_out_b')
        wgrad('w_out_b', s['o'], dyb, l)
        dq, dk, dv, dcq, dck = _fox_bwd(z, s['c'], s['ct'], s['o'], s['lse'], do)
        hp = dcq.shape[0]
        dc = (jnp.transpose(dcq[:, :, :FOX_HEADS // hp], (1, 0, 2)).reshape(T, FOX_HEADS)
              + jnp.transpose(dck[:, :FOX_HEADS // hp, :], (2, 0, 1)).reshape(T, FOX_HEADS))
        dzf, db_f = _fox_c_bwd(jnp.pad(dc, ((0, 0), (0, LANES - FOX_HEADS))), s['zf'], row(b_forget_pad, l))
        GS['b_forget'][l] = db_f[:, :FOX_HEADS]
        dya_pre = _mm(dya, W['w_out_a'], mode='nt', lb=l, out_dtype=F32, name='mmb_out_a')
        wgrad('w_out_a', s['ya_pre'], dya, l)
        dz_a, G_conv_a[l] = _mixa_bwd(z, conv_a_full[l], dya_pre)
        dz = jnp.concatenate([dz_a, dq, dk, dv, dz_c, dz_g], axis=1)
        dh1 = _mm(dzf, W['w_f'], mode='nt', lb=l, out_dtype=F32, name='mmb_in_f')
        dh1 = _mm(dz, W['w_main'], mode='nt', lb=l, add=dh1, out_dtype=F32, name='mmb_in')
        wgrad('w_main', s['h1'], dz, l)
        wgrad('w_f', s['h1'], dzf, l)
        dx, GS['mix_norm'][l] = _rms_bwd(s['x0'], row(mix_norm, l), dh1, dx, 'rms_mix_bwd')

    _, d_mem_norm = _rms_bwd(mem[0], mem_norm[None, :], d_mem_n, None, 'rms_mem_bwd')

    g_w_in = jnp.concatenate([GW['w_main'][..., :f_off], GW['w_f'][..., :FOX_HEADS], GW['w_main'][..., f_off:]], axis=-1)
    send = {
        'w_in': _stack_to_cols(g_w_in),
        'w_gate_up': _stack_to_cols(GW['w_gate_up']),
        'w_xo': _stack_to_cols(GW['w_xo']),
        'conv_a': _stack_to_cols(jnp.stack(G_conv_a)[:, :CONV_A_W]),
        'conv_c': _stack_to_cols(jnp.stack(G_conv_c)[:, :CONV_C_W]),
    }
    for w in ROW_SHARDED:
        g = GW[w]
        send[w] = g.reshape(L, N_DEV, g.shape[1] // N_DEV, g.shape[2])
    recv = dict(zip(SHARDED, _alltoall([send[w] for w in SHARDED], [w not in ROW_SHARDED for w in SHARDED],
                                       'exchange_grads')))

    small_parts = {w: jnp.concatenate(GS[w], axis=0) for w in SMALL if w not in ('mem_norm', 'final_norm')}
    small_parts['mem_norm'] = d_mem_norm
    small_parts['final_norm'] = d_final_norm
    sizes = [P[w].size for w in SMALL]
    n_small = sum(sizes) + LANES
    n_rows = -(-n_small // (8 * LANES)) * 8
    pad = n_rows * LANES - n_small

    def pack(parts, tail):
        flat = jnp.concatenate([p.reshape(-1) for p in parts] + [tail, jnp.zeros((pad,), F32)])
        return flat.reshape(n_rows, LANES)

    zeros_tail = jnp.zeros((LANES,), F32)
    g_small = pack([small_parts[w] for w in SMALL], loss_part.reshape(-1))
    (all_small,) = _allgather([g_small], [True], 'gather_small')
    sm = _adamw(all_small, pack([P[w] for w in SMALL], zeros_tail), pack([P['m_' + w] for w in SMALL], zeros_tail),
                pack([P['v_' + w] for w in SMALL], zeros_tail), 'adamw_small')
    loss = sm[0].reshape(-1)[sum(sizes)]

    results = {}
    for w in SHARDED:
        shp = P[w].shape
        r2 = (-1, shp[-1])
        parts = recv[w].reshape((N_DEV,) + P[w].reshape(r2).shape)
        outs = _adamw(parts, P[w].reshape(r2), P['m_' + w].reshape(r2), P['v_' + w].reshape(r2), 'adamw_' + w)
        results[w] = [o.reshape(shp) for o in outs]
    off = 0
    for w, n in zip(SMALL, sizes):
        results[w] = [o.reshape(-1)[off:off + n].reshape(P[w].shape) for o in sm]
        off += n

    out = [loss, dx[None]]
    for k in range(4):
        out += [results[w][k] for w in WEIGHTS]
    return tuple(out)
```

```python
import functools

import jax
import jax.numpy as jnp
from jax import lax
from jax.experimental import pallas as pl
from jax.experimental.pallas import tpu as pltpu

F32 = jnp.float32
BF16 = jnp.bfloat16
MESH = pl.DeviceIdType.MESH

N_DEV = 8
EPS = 1e-6
FOX_HEADS = 16
X_HEADS = 4
LANES = 128
ROW_TILE = 16
HALO_A = 16
HALO_C = 32
CONV_A_W = 3
CONV_C_W = 31
VMEM_LIMIT = 56 << 20

ADAM_LR = 0.001
ADAM_B1 = 0.9
ADAM_B2 = 0.999
ADAM_EPS = 1e-08
ADAM_WD = 0.01
ADAM_STEP = 10

WEIGHTS = ['mix_norm', 'w_in', 'b_gate', 'b_forget', 'conv_a', 'w_out_a', 'w_out_b', 'conv_c', 'conv_c_bias',
           'ln_c_gain', 'ln_c_bias', 'w_out_c', 'w_o', 'xattn_norm', 'mem_norm', 'w_xq', 'w_xkv', 'w_xo',
           'ffn_norm', 'w_gate_up', 'w_down', 'final_norm']
SHARDED = ['w_in', 'conv_a', 'w_out_a', 'w_out_b', 'conv_c', 'w_out_c', 'w_o', 'w_xq', 'w_xkv', 'w_xo',
           'w_gate_up', 'w_down']
ROW_SHARDED = ['w_out_a', 'w_out_b', 'w_out_c', 'w_o', 'w_xq', 'w_xkv', 'w_down']
SMALL = [w for w in WEIGHTS if w not in SHARDED]
ARG_NAMES = (['x', 'mem'] + WEIGHTS + ['loss_target'] + ['m_' + w for w in WEIGHTS] + ['v_' + w for w in WEIGHTS])


def _pick(n, cands=(1024, 1408, 512, 256, 128)):
    for c in cands:
        if n % c == 0:
            return c
    return n


def _call(body, *, name, out_shape, grid=(), in_specs=None, out_specs=None, scratch=(), sem=None, aliases=None,
          deps=()):
    params = dict(vmem_limit_bytes=VMEM_LIMIT)
    if sem is not None:
        params['dimension_semantics'] = sem
    n_in, n_dep = len(in_specs), len(deps)

    def body_without_deps(*refs):
        body(*refs[:n_in], *refs[n_in + n_dep:])

    call = pl.pallas_call(body_without_deps, name=name, out_shape=out_shape, grid=grid, scratch_shapes=list(scratch),
                          in_specs=list(in_specs) + [pl.BlockSpec(memory_space=pl.ANY)] * n_dep, out_specs=out_specs,
                          input_output_aliases=aliases or {}, compiler_params=pltpu.CompilerParams(**params),
                          interpret=False)
    return lambda *args: call(*args, *deps)


def _sds(shape, dtype):
    return jax.ShapeDtypeStruct(tuple(shape), dtype)


def _sigmoid(x):
    return 1.0 / (1.0 + jnp.exp(-x))


_DN = {'nn': (((1,), (0,)), ((), ())), 'nt': (((1,), (1,)), ((), ())), 'tn': (((0,), (0,)), ((), ()))}


def _mm(a, b, *, mode, name, out_dtype=BF16, add=None, b_rows=None, out=None, o_off=0, deps=()):
    if mode == 'tn':
        K, M = a.shape
    else:
        M, K = a.shape
    b_off, b_n = (0, b.shape[0]) if b_rows is None else b_rows
    N = b_n if mode == 'nt' else b.shape[1]
    assert (b.shape[1] if mode == 'nt' else b_n) == K
    tm, tn = _pick(M), _pick(N)
    tk = K if K <= 2048 else _pick(K)
    ni, nj, nk = M // tm, N // tn, K // tk
    a_bytes, b_bytes = M * K * a.dtype.itemsize, K * N * b.dtype.itemsize
    n_outer = (b_bytes + nj * a_bytes) < (a_bytes + ni * b_bytes)
    if n_outer:
        grid = (nj, ni, nk)
        ij = lambda g0, g1: (g1, g0)
    else:
        grid = (ni, nj, nk)
        ij = lambda g0, g1: (g0, g1)

    def a_map(g0, g1, k):
        i, _ = ij(g0, g1)
        return (k, i) if mode == 'tn' else (i, k)

    def elems(blk):
        return tuple(pl.Element(s) for s in blk)

    def at(rows, cols):
        return pl.multiple_of(rows, ROW_TILE), pl.multiple_of(cols, LANES)

    a_blk = (tk, tm) if mode == 'tn' else (tm, tk)
    b_blk = (tn, tk) if mode == 'nt' else (tk, tn)
    if b_rows is None:
        def b_map(g0, g1, k):
            _, j = ij(g0, g1)
            return (j, k) if mode == 'nt' else (k, j)
        b_spec = pl.BlockSpec(b_blk, b_map)
    else:
        def b_map(g0, g1, k):
            _, j = ij(g0, g1)
            return at(b_off + j * tn, k * tk) if mode == 'nt' else at(b_off + k * tk, j * tn)
        b_spec = pl.BlockSpec(elems(b_blk), b_map)
    in_specs = [pl.BlockSpec(a_blk, a_map), b_spec]
    args = [a, b]
    has_add = add is not None
    if has_add:
        in_specs.append(pl.BlockSpec((tm, tn), lambda g0, g1, k: ij(g0, g1)))
        args.append(add)
    aliases = {}
    if out is None:
        out_shape = _sds((M, N), out_dtype)
        out_spec = pl.BlockSpec((tm, tn), lambda g0, g1, k: ij(g0, g1))
    else:
        if isinstance(out, int):
            out_shape = _sds((out, N), out_dtype)
        else:
            out_shape = _sds(out.shape, out.dtype)
            in_specs.append(pl.BlockSpec(memory_space=pl.ANY))
            aliases = {len(args): 0}
            args.append(out)

        def o_map(g0, g1, k):
            i, j = ij(g0, g1)
            return at(o_off + i * tm, j * tn)
        out_spec = pl.BlockSpec(elems((tm, tn)), o_map)
    dn = _DN[mode]
    n_in = len(args)

    def body(*refs):
        a_ref, b_ref = refs[0], refs[1]
        add_ref = refs[2] if has_add else None
        o_ref = refs[n_in]
        part = lax.dot_general(a_ref[...].astype(BF16), b_ref[...].astype(BF16), dn, preferred_element_type=F32)

        def finish(r):
            if has_add:
                r = r + add_ref[...]
            o_ref[...] = r.astype(o_ref.dtype)

        if nk == 1:
            finish(part)
        else:
            acc = refs[n_in + 1]
            k = pl.program_id(2)

            @pl.when(k == 0)
            def _():
                acc[...] = part

            @pl.when(k > 0)
            def _():
                acc[...] += part

            @pl.when(k == nk - 1)
            def _():
                finish(acc[...])

    scratch = [pltpu.VMEM((tm, tn), F32)] if nk > 1 else []
    return _call(body, name=name, out_shape=out_shape, grid=grid, in_specs=in_specs, out_specs=out_spec,
                 scratch=scratch, sem=('parallel', 'parallel', 'arbitrary'), aliases=aliases, deps=deps)(*args)


def _rms_fwd(x, g, name, deps=()):
    R, D = x.shape
    tb = _pick(R, (512, 256, 128))

    def body(x_ref, g_ref, o_ref):
        xv = x_ref[...]
        r = lax.rsqrt(jnp.mean(xv * xv, axis=-1, keepdims=True) + EPS)
        o_ref[...] = (xv * r * g_ref[...]).astype(o_ref.dtype)

    row = pl.BlockSpec((tb, D), lambda i: (i, 0))
    vec = pl.BlockSpec((1, D), lambda i: (0, 0))
    return _call(body, name=name, out_shape=_sds((R, D), BF16), grid=(R // tb,), in_specs=[row, vec],
                 out_specs=row, sem=('parallel',), deps=deps)(x, g)


def _rms_bwd(x, g, dh, dres, name):
    R, D = x.shape
    tb = _pick(R, (512, 256, 128))
    has_res = dres is not None

    def body(*refs):
        x_ref, g_ref, dh_ref = refs[:3]
        dx_ref, dg_ref = refs[-2:]
        xv = x_ref[...]
        dhv = dh_ref[...].astype(F32)
        r = lax.rsqrt(jnp.mean(xv * xv, axis=-1, keepdims=True) + EPS)
        u = dhv * g_ref[...]
        m = jnp.mean(u * xv, axis=-1, keepdims=True)
        dx = r * u - xv * (r * r * r * m)
        if has_res:
            dx = dx + refs[3][...]
        dx_ref[...] = dx

        @pl.when(pl.program_id(0) == 0)
        def _():
            dg_ref[...] = jnp.zeros_like(dg_ref)

        dg_ref[...] += jnp.sum(dhv * xv * r, axis=0, keepdims=True)

    row = pl.BlockSpec((tb, D), lambda i: (i, 0))
    vec = pl.BlockSpec((1, D), lambda i: (0, 0))
    args = [x, g, dh] + ([dres] if has_res else [])
    return _call(body, name=name, out_shape=(_sds((R, D), F32), _sds((1, D), F32)), grid=(R // tb,),
                 in_specs=[row, vec, row] + ([row] if has_res else []), out_specs=(row, vec),
                 sem=('arbitrary',))(*args)


def _final(x, g, tgt):
    R, D = x.shape
    tb = _pick(R, (512, 256, 128))

    def body(x_ref, g_ref, t_ref, dx_ref, dg_ref, loss_ref):
        xv = x_ref[...]
        gv = g_ref[...]
        r = lax.rsqrt(jnp.mean(xv * xv, axis=-1, keepdims=True) + EPS)
        e = xv * r * gv - t_ref[...]
        row_loss = jnp.mean(e * e, axis=-1, keepdims=True)
        blk_loss = 0.5 * jnp.sum(row_loss, axis=0, keepdims=True)
        dy = e * (1.0 / D)
        u = dy * gv
        m = jnp.mean(u * xv, axis=-1, keepdims=True)
        dx_ref[...] = r * u - xv * (r * r * r * m)

        @pl.when(pl.program_id(0) == 0)
        def _():
            dg_ref[...] = jnp.zeros_like(dg_ref)
            loss_ref[...] = jnp.zeros_like(loss_ref)

        dg_ref[...] += jnp.sum(dy * xv * r, axis=0, keepdims=True)
        loss_ref[...] += jnp.broadcast_to(blk_loss, loss_ref.shape)

    row = pl.BlockSpec((tb, D), lambda i: (i, 0))
    vec = pl.BlockSpec((1, D), lambda i: (0, 0))
    one = pl.BlockSpec((1, LANES), lambda i: (0, 0))
    return _call(body, name='final_loss', out_shape=(_sds((R, D), F32), _sds((1, D), F32), _sds((1, LANES), F32)),
                 grid=(R // tb,), in_specs=[row, vec, row], out_specs=(row, vec, one), sem=('arbitrary',))(x, g, tgt)


def _col(tb, w, cb):
    return pl.BlockSpec((tb, w), lambda i: (i, cb))


def _prev(tb, h, w, cb):
    return pl.BlockSpec((h, w), lambda i: (jnp.maximum(i * (tb // h) - 1, 0), cb))


def _next(tb, h, w, cb, rows):
    return pl.BlockSpec((h, w), lambda i: (jnp.minimum((i + 1) * (tb // h), rows // h - 1), cb))


def _full(shape):
    return pl.BlockSpec(shape, lambda i: (0,) * len(shape))


def _mixa_fwd(z, w):
    T = z.shape[0]
    D = w.shape[1]
    tb = _pick(T, (256, 128))
    H = HALO_A

    def body(ab, ac, au, acp, aup, w_ref, o_ref, ext):
        first = pl.program_id(0) == 0
        ext[0:H, :] = jnp.where(first, 0.0, acp[...].astype(F32) * aup[...].astype(F32))
        ext[H:H + tb, :] = ac[...].astype(F32) * au[...].astype(F32)
        cp = jnp.zeros((tb, D), F32)
        for k in range(CONV_A_W):
            off = H - (CONV_A_W - 1) + k
            cp = cp + ext[off:off + tb, :] * w_ref[k:k + 1, :]
        o_ref[...] = (ab[...].astype(F32) * cp).astype(o_ref.dtype)

    return _call(body, name='mixa_fwd', out_shape=_sds((T, D), BF16), grid=(T // tb,),
                 in_specs=[_col(tb, D, 0), _col(tb, D, 1), _col(tb, D, 2), _prev(tb, H, D, 1), _prev(tb, H, D, 2),
                           _full((8, D))],
                 out_specs=_col(tb, D, 0), scratch=[pltpu.VMEM((H + tb, D), F32)], sem=('parallel',))(z, z, z, z, z, w)


def _mixa_bwd(z, w, dy):
    T = z.shape[0]
    D = w.shape[1]
    tb = _pick(T, (256, 128))
    H = HALO_A
    nb = T // tb

    def body(ab, ac, au, acp, aup, abn, dy_ref, dyn, w_ref, dz_ref, dw_ref, pext, dext):
        i = pl.program_id(0)
        a_b, a_c, a_u = ab[...].astype(F32), ac[...].astype(F32), au[...].astype(F32)
        pext[0:H, :] = jnp.where(i == 0, 0.0, acp[...].astype(F32) * aup[...].astype(F32))
        pext[H:H + tb, :] = a_c * a_u
        dyv = dy_ref[...].astype(F32)
        dcp = dyv * a_b
        dext[0:tb, :] = dcp
        dext[tb:tb + H, :] = jnp.where(i == nb - 1, 0.0, dyn[...].astype(F32) * abn[...].astype(F32))

        @pl.when(i == 0)
        def _():
            dw_ref[...] = jnp.zeros_like(dw_ref)

        cp = jnp.zeros((tb, D), F32)
        dp = jnp.zeros((tb, D), F32)
        for k in range(CONV_A_W):
            off = H - (CONV_A_W - 1) + k
            wk = w_ref[k:k + 1, :]
            pk = pext[off:off + tb, :]
            cp = cp + pk * wk
            dp = dp + dext[CONV_A_W - 1 - k:CONV_A_W - 1 - k + tb, :] * wk
            dw_ref[k:k + 1, :] += jnp.sum(dcp * pk, axis=0, keepdims=True)
        dz_ref[:, 0:D] = (dyv * cp).astype(dz_ref.dtype)
        dz_ref[:, D:2 * D] = (dp * a_u).astype(dz_ref.dtype)
        dz_ref[:, 2 * D:3 * D] = (dp * a_c).astype(dz_ref.dtype)

    return _call(body, name='mixa_bwd', out_shape=(_sds((T, 3 * D), BF16), _sds((8, D), F32)), grid=(nb,),
                 in_specs=[_col(tb, D, 0), _col(tb, D, 1), _col(tb, D, 2), _prev(tb, H, D, 1), _prev(tb, H, D, 2),
                           _next(tb, H, D, 0, T), _col(tb, D, 0), _next(tb, H, D, 0, T), _full((8, D))],
                 out_specs=(_col(tb, 3 * D, 0), _full((8, D))),
                 scratch=[pltpu.VMEM((H + tb, D), F32), pltpu.VMEM((tb + H, D), F32)],
                 sem=('arbitrary',))(z, z, z, z, z, z, dy, dy, w)


def _split3(v):
    hi = v.astype(BF16)
    r1 = v - hi.astype(F32)
    mid = r1.astype(BF16)
    lo = (r1 - mid.astype(F32)).astype(BF16)
    return hi, mid, lo


def _tri_dot(tri, v):
    hi, mid, lo = _split3(v)
    d = lambda p: jnp.dot(tri, p, preferred_element_type=F32)
    return d(hi) + d(mid) + d(lo)


def _fox_c(zf, bf):
    T, H = zf.shape
    tb = LANES

    def body(zf_ref, b_ref, c_ref, carry):
        @pl.when(pl.program_id(0) == 0)
        def _():
            carry[...] = jnp.zeros_like(carry)

        xv = zf_ref[...] + b_ref[...]
        lf = jnp.minimum(xv, 0.0) - jnp.log(1.0 + jnp.exp(-jnp.abs(xv)))
        r = lax.broadcasted_iota(jnp.int32, (tb, tb), 0)
        c = lax.broadcasted_iota(jnp.int32, (tb, tb), 1)
        tri = (r >= c).astype(BF16)
        cs = _tri_dot(tri, lf) + carry[...]
        c_ref[...] = cs
        carry[...] = cs[tb - 1:tb, :]

    blk = pl.BlockSpec((tb, H), lambda i: (i, 0))
    return _call(body, name='fox_cumsum', out_shape=_sds((T, H), F32), grid=(T // tb,),
                 in_specs=[blk, _full((1, H))], out_specs=blk, scratch=[pltpu.VMEM((1, H), F32)],
                 sem=('arbitrary',))(zf, bf)


def _fox_c_bwd(dc, zf, bf):
    T, H = zf.shape
    tb = LANES
    nb = T // tb

    def body(dc_ref, zf_ref, b_ref, dz_ref, db_ref, carry):
        @pl.when(pl.program_id(0) == 0)
        def _():
            carry[...] = jnp.zeros_like(carry)
            db_ref[...] = jnp.zeros_like(db_ref)

        r = lax.broadcasted_iota(jnp.int32, (tb, tb), 0)
        c = lax.broadcasted_iota(jnp.int32, (tb, tb), 1)
        tri = (c >= r).astype(BF16)
        dlf = _tri_dot(tri, dc_ref[...]) + carry[...]
        carry[...] = dlf[0:1, :]
        xv = zf_ref[...] + b_ref[...]
        dz = dlf * _sigmoid(-xv)
        dz_ref[...] = dz
        db_ref[...] += jnp.sum(dz, axis=0, keepdims=True)

    blk = pl.BlockSpec((tb, H), lambda i: (nb - 1 - i, 0))
    return _call(body, name='fox_cumsum_bwd', out_shape=(_sds((T, H), F32), _sds((1, H), F32)), grid=(nb,),
                 in_specs=[blk, blk, _full((1, H))], out_specs=(blk, _full((1, H))),
                 scratch=[pltpu.VMEM((1, H), F32)], sem=('arbitrary',))(dc, zf, bf)


def _head_cols(c_ref, ct_ref, h, T):
    lane = lax.broadcasted_iota(jnp.int32, (T, FOX_HEADS), 1)
    cq = jnp.sum(jnp.where(lane == h, c_ref[...], 0.0), axis=1, keepdims=True)
    sub = lax.broadcasted_iota(jnp.int32, (FOX_HEADS, T), 0)
    ck = jnp.sum(jnp.where(sub == h, ct_ref[...], 0.0), axis=0, keepdims=True)
    return cq, ck


def _fox_scores(qs, ks, cq, ck, q0, ke, scale):
    tq = ke - q0
    s = lax.dot_general(qs, ks, _DN['nt'], preferred_element_type=F32) * scale
    s = s + (cq[q0:ke, :] - ck[:, 0:ke])
    r = lax.broadcasted_iota(jnp.int32, (tq, ke), 0) + q0
    c = lax.broadcasted_iota(jnp.int32, (tq, ke), 1)
    return jnp.where(r >= c, s, -jnp.inf)


def _fox_fwd(z, c, ct):
    T = z.shape[0]
    D = z.shape[1] // 6
    dh = D // FOX_HEADS
    hp = LANES // dh
    ncb = D // LANES
    tq = _pick(T, (256, 128))
    scale = dh ** -0.5

    def body(q_ref, k_ref, v_ref, c_ref, ct_ref, o_ref, lse_ref):
        j = pl.program_id(0)
        lse_ref[...] = jnp.zeros_like(lse_ref)
        for hh in range(hp):
            cq, ck = _head_cols(c_ref, ct_ref, j * hp + hh, T)
            hs = slice(hh * dh, (hh + 1) * dh)
            for q0 in range(0, T, tq):
                ke = q0 + tq
                s = _fox_scores(q_ref[q0:ke, hs], k_ref[0:ke, hs], cq, ck, q0, ke, scale)
                m = jnp.max(s, axis=-1, keepdims=True)
                p = jnp.exp(s - m)
                l = jnp.sum(p, axis=-1, keepdims=True)
                o = jnp.dot(p.astype(BF16), v_ref[0:ke, hs], preferred_element_type=F32)
                o_ref[q0:ke, hs] = (o * (1.0 / l)).astype(o_ref.dtype)
                lse_ref[q0:ke, hh:hh + 1] = m + jnp.log(l)

    blk = lambda cb0: pl.BlockSpec((T, LANES), lambda j: (0, cb0 + j))
    return _call(body, name='fox_fwd', out_shape=(_sds((T, D), BF16), _sds((ncb, T, LANES), F32)), grid=(ncb,),
                 in_specs=[blk(3 * ncb), blk(4 * ncb), blk(5 * ncb), _full((T, FOX_HEADS)), _full((FOX_HEADS, T))],
                 out_specs=(blk(0), pl.BlockSpec((None, T, LANES), lambda j: (j, 0, 0))),
                 sem=('parallel',))(z, z, z, c, ct)


def _fox_bwd(z, c, ct, o, lse, do):
    T = z.shape[0]
    D = o.shape[1]
    dh = D // FOX_HEADS
    hp = LANES // dh
    ncb = D // LANES
    tq = _pick(T, (256, 128))
    scale = dh ** -0.5

    def body(q_ref, k_ref, v_ref, c_ref, ct_ref, o_ref, lse_ref, do_ref, dq_ref, dk_ref, dv_ref, dcq_ref, dck_ref,
             dk_acc, dv_acc):
        j = pl.program_id(0)
        dk_acc[...] = jnp.zeros_like(dk_acc)
        dv_acc[...] = jnp.zeros_like(dv_acc)
        dcq_ref[...] = jnp.zeros_like(dcq_ref)
        dck_ref[...] = jnp.zeros_like(dck_ref)
        for hh in range(hp):
            cq, ck = _head_cols(c_ref, ct_ref, j * hp + hh, T)
            hs = slice(hh * dh, (hh + 1) * dh)
            for q0 in range(0, T, tq):
                ke = q0 + tq
                qs, ks, vs = q_ref[q0:ke, hs], k_ref[0:ke, hs], v_ref[0:ke, hs]
                dos = do_ref[q0:ke, hs]
                s = _fox_scores(qs, ks, cq, ck, q0, ke, scale)
                p = jnp.exp(s - lse_ref[q0:ke, hh:hh + 1])
                dp = lax.dot_general(dos, vs, _DN['nt'], preferred_element_type=F32)
                delta = jnp.sum(dos.astype(F32) * o_ref[q0:ke, hs].astype(F32), axis=-1, keepdims=True)
                ds = p * (dp - delta)
                dsb = ds.astype(BF16)
                dq = jnp.dot(dsb, ks, preferred_element_type=F32) * scale
                dq_ref[q0:ke, hs] = dq.astype(dq_ref.dtype)
                dk_acc[0:ke, hs] += lax.dot_general(dsb, qs, _DN['tn'], preferred_element_type=F32) * scale
                dv_acc[0:ke, hs] += lax.dot_general(p.astype(BF16), dos, _DN['tn'], preferred_element_type=F32)
                dcq_ref[q0:ke, hh:hh + 1] = jnp.sum(ds, axis=-1, keepdims=True)
                dck_ref[hh:hh + 1, 0:ke] -= jnp.sum(ds, axis=0, keepdims=True)
        dk_ref[...] = dk_acc[...].astype(dk_ref.dtype)
        dv_ref[...] = dv_acc[...].astype(dv_ref.dtype)

    blk = lambda cb0: pl.BlockSpec((T, LANES), lambda j: (0, cb0 + j))
    pair = pl.BlockSpec((None, T, LANES), lambda j: (j, 0, 0))
    grad = _sds((T, D), BF16)
    return _call(body, name='fox_bwd',
                 out_shape=(grad, grad, grad, _sds((ncb, T, LANES), F32), _sds((ncb, 8, T), F32)), grid=(ncb,),
                 in_specs=[blk(3 * ncb), blk(4 * ncb), blk(5 * ncb), _full((T, FOX_HEADS)), _full((FOX_HEADS, T)), blk(0),
                           pair, blk(0)],
                 out_specs=(blk(0), blk(0), blk(0), pair, pl.BlockSpec((None, 8, T), lambda j: (j, 0, 0))),
                 scratch=[pltpu.VMEM((T, LANES), F32), pltpu.VMEM((T, LANES), F32)],
                 sem=('parallel',))(z, z, z, c, ct, o, lse, do)


def _ln_parts(u1, gain, bias):
    mu = jnp.mean(u1, axis=-1, keepdims=True)
    xc = u1 - mu
    rstd = lax.rsqrt(jnp.mean(xc * xc, axis=-1, keepdims=True) + EPS)
    xhat = xc * rstd
    return xhat, rstd, xhat * gain + bias


def _mixc_fwd(z, w, cb, gain, bias):
    T = z.shape[0]
    D = w.shape[1]
    tb = _pick(T, (256, 128))
    H = HALO_C

    def body(cv, cg, cvp, cgp, w_ref, cb_ref, g_ref, b_ref, o_ref, u1_ref, ext):
        first = pl.program_id(0) == 0
        ext[0:H, :] = jnp.where(first, 0.0, cvp[...].astype(F32) * _sigmoid(cgp[...].astype(F32)))
        ext[H:H + tb, :] = cv[...].astype(F32) * _sigmoid(cg[...].astype(F32))
        u1 = jnp.zeros((tb, D), F32)
        for k in range(CONV_C_W):
            off = H - (CONV_C_W - 1) + k
            u1 = u1 + ext[off:off + tb, :] * w_ref[k:k + 1, :]
        u1 = u1 + cb_ref[...]
        u1_ref[...] = u1
        _, _, u2 = _ln_parts(u1, g_ref[...], b_ref[...])
        o_ref[...] = (u2 * _sigmoid(u2)).astype(o_ref.dtype)

    vec = _full((1, D))
    row = _col(tb, D, 0)
    return _call(body, name='mixc_fwd', out_shape=(_sds((T, D), BF16), _sds((T, D), F32)), grid=(T // tb,),
                 in_specs=[_col(tb, D, 0), _col(tb, D, 1), _prev(tb, H, D, 0), _prev(tb, H, D, 1), _full((32, D)), vec,
                           vec, vec],
                 out_specs=(row, row), scratch=[pltpu.VMEM((H + tb, D), F32)],
                 sem=('parallel',))(z, z, z, z, w, cb, gain, bias)


def _mixc_bwd(z, w, gain, bias, u1, du3):
    T = z.shape[0]
    D = w.shape[1]
    tb = _pick(T, (256, 128))
    H = HALO_C
    nb = T // tb

    def du1_of(u1v, du3v, gv, bv):
        xhat, rstd, u2 = _ln_parts(u1v, gv, bv)
        sg = _sigmoid(u2)
        du2 = du3v * (sg * (1.0 + u2 * (1.0 - sg)))
        dxh = du2 * gv
        m1 = jnp.mean(dxh, axis=-1, keepdims=True)
        m2 = jnp.mean(dxh * xhat, axis=-1, keepdims=True)
        return rstd * (dxh - m1 - xhat * m2), du2, xhat

    def body(cv, cg, cvp, cgp, w_ref, g_ref, b_ref, u1_ref, u1n, du3_ref, du3n, dz_ref, dw_ref, dcb_ref, dg_ref,
             db_ref, uext, dext):
        i = pl.program_id(0)
        gv, bv = g_ref[...], b_ref[...]
        c_val = cv[...].astype(F32)
        sg = _sigmoid(cg[...].astype(F32))
        uext[0:H, :] = jnp.where(i == 0, 0.0, cvp[...].astype(F32) * _sigmoid(cgp[...].astype(F32)))
        uext[H:H + tb, :] = c_val * sg
        du1, du2, xhat = du1_of(u1_ref[...], du3_ref[...].astype(F32), gv, bv)
        du1n, _, _ = du1_of(u1n[...], du3n[...].astype(F32), gv, bv)
        dext[0:tb, :] = du1
        dext[tb:tb + H, :] = jnp.where(i == nb - 1, 0.0, du1n)

        @pl.when(i == 0)
        def _():
            dw_ref[...] = jnp.zeros_like(dw_ref)
            dcb_ref[...] = jnp.zeros_like(dcb_ref)
            dg_ref[...] = jnp.zeros_like(dg_ref)
            db_ref[...] = jnp.zeros_like(db_ref)

        dcb_ref[...] += jnp.sum(du1, axis=0, keepdims=True)
        dg_ref[...] += jnp.sum(du2 * xhat, axis=0, keepdims=True)
        db_ref[...] += jnp.sum(du2, axis=0, keepdims=True)
        du0 = jnp.zeros((tb, D), F32)
        for k in range(CONV_C_W):
            off = H - (CONV_C_W - 1) + k
            du0 = du0 + dext[CONV_C_W - 1 - k:CONV_C_W - 1 - k + tb, :] * w_ref[k:k + 1, :]
            dw_ref[k:k + 1, :] += jnp.sum(du1 * uext[off:off + tb, :], axis=0, keepdims=True)
        dz_ref[:, 0:D] = (du0 * sg).astype(dz_ref.dtype)
        dz_ref[:, D:2 * D] = (du0 * c_val * sg * (1.0 - sg)).astype(dz_ref.dtype)

    vec = _full((1, D))
    row = _col(tb, D, 0)
    nxt = _next(tb, H, D, 0, T)
    return _call(body, name='mixc_bwd',
                 out_shape=(_sds((T, 2 * D), BF16), _sds((32, D), F32), _sds((1, D), F32), _sds((1, D), F32),
                            _sds((1, D), F32)), grid=(nb,),
                 in_specs=[_col(tb, D, 0), _col(tb, D, 1), _prev(tb, H, D, 0), _prev(tb, H, D, 1), _full((32, D)), vec,
                           vec, row, nxt, row, nxt],
                 out_specs=(_col(tb, 2 * D, 0), _full((32, D)), vec, vec, vec),
                 scratch=[pltpu.VMEM((H + tb, D), F32), pltpu.VMEM((tb + H, D), F32)],
                 sem=('arbitrary',))(z, z, z, z, w, gain, bias, u1, u1, du3, du3)


def _gate_fwd(z, bg, ya, yb, yc):
    T = z.shape[0]
    D = ya.shape[1]
    tb = _pick(T, (256, 128))

    def body(ga, gb, gc, bg_ref, ya_ref, yb_ref, yc_ref, o_ref):
        acc = jnp.zeros((tb, D), F32)
        for n, (g, y) in enumerate(((ga, ya_ref), (gb, yb_ref), (gc, yc_ref))):
            acc = acc + _sigmoid(g[...].astype(F32) + bg_ref[:, n * D:(n + 1) * D]) * y[...].astype(F32)
        o_ref[...] = acc.astype(o_ref.dtype)

    row = _col(tb, D, 0)
    return _call(body, name='gate_fwd', out_shape=_sds((T, D), BF16), grid=(T // tb,),
                 in_specs=[_col(tb, D, 2), _col(tb, D, 3), _col(tb, D, 4), _full((1, 3 * D)), row, row, row],
                 out_specs=row, sem=('parallel',))(z, z, z, bg, ya, yb, yc)


def _gate_bwd(z, bg, ya, yb, yc, dm):
    T = z.shape[0]
    D = ya.shape[1]
    tb = _pick(T, (256, 128))

    def body(ga, gb, gc, bg_ref, ya_ref, yb_ref, yc_ref, dm_ref, dya, dyb, dyc, dg_ref, dbg_ref):
        @pl.when(pl.program_id(0) == 0)
        def _():
            dbg_ref[...] = jnp.zeros_like(dbg_ref)

        dmv = dm_ref[...]
        for n, (g, y, dy) in enumerate(((ga, ya_ref, dya), (gb, yb_ref, dyb), (gc, yc_ref, dyc))):
            cols = slice(n * D, (n + 1) * D)
            sg = _sigmoid(g[...].astype(F32) + bg_ref[:, cols])
            dy[...] = (dmv * sg).astype(dy.dtype)
            dg = dmv * y[...].astype(F32) * sg * (1.0 - sg)
            dg_ref[:, cols] = dg.astype(dg_ref.dtype)
            dbg_ref[:, cols] += jnp.sum(dg, axis=0, keepdims=True)

    row = _col(tb, D, 0)
    act = _sds((T, D), BF16)
    return _call(body, name='gate_bwd', out_shape=(act, act, act, _sds((T, 3 * D), BF16), _sds((1, 3 * D), F32)),
                 grid=(T // tb,),
                 in_specs=[_col(tb, D, 2), _col(tb, D, 3), _col(tb, D, 4), _full((1, 3 * D)), row, row, row, row],
                 out_specs=(row, row, row, _col(tb, 3 * D, 0), _full((1, 3 * D))),
                 sem=('arbitrary',))(z, z, z, bg, ya, yb, yc, dm)


def _xattn_probs(qs, ks, scale):
    s = lax.dot_general(qs, ks, _DN['nt'], preferred_element_type=F32) * scale
    p = jnp.exp(s - jnp.max(s, axis=-1, keepdims=True))
    return p * (1.0 / jnp.sum(p, axis=-1, keepdims=True))


def _xattn_fwd(q, kv):
    T, DX = q.shape
    M = kv.shape[0]
    dh = DX // X_HEADS
    tb = _pick(T, (512, 256, 128))
    scale = dh ** -0.5

    def body(q_ref, kv_ref, o_ref):
        for h in range(X_HEADS):
            hs = slice(h * dh, (h + 1) * dh)
            p = _xattn_probs(q_ref[:, hs], kv_ref[:, hs], scale)
            o_ref[:, hs] = jnp.dot(p.astype(BF16), kv_ref[:, DX + h * dh:DX + (h + 1) * dh],
                                   preferred_element_type=F32).astype(o_ref.dtype)

    row = _col(tb, DX, 0)
    return _call(body, name='xattn_fwd', out_shape=_sds((T, DX), BF16), grid=(T // tb,),
                 in_specs=[row, _full((M, 2 * DX))], out_specs=row, sem=('parallel',))(q, kv)


def _xattn_bwd(q, kv, do):
    T, DX = q.shape
    M = kv.shape[0]
    dh = DX // X_HEADS
    tb = _pick(T, (512, 256, 128))
    scale = dh ** -0.5

    def body(q_ref, kv_ref, do_ref, dq_ref, dkv_ref):
        @pl.when(pl.program_id(0) == 0)
        def _():
            dkv_ref[...] = jnp.zeros_like(dkv_ref)

        for h in range(X_HEADS):
            hs = slice(h * dh, (h + 1) * dh)
            vs_cols = slice(DX + h * dh, DX + (h + 1) * dh)
            qs, ks, vs = q_ref[:, hs], kv_ref[:, hs], kv_ref[:, vs_cols]
            dos = do_ref[:, hs].astype(BF16)
            p = _xattn_probs(qs, ks, scale)
            dp = lax.dot_general(dos, vs, _DN['nt'], preferred_element_type=F32)
            ds = p * (dp - jnp.sum(p * dp, axis=-1, keepdims=True))
            dsb = ds.astype(BF16)
            dq_ref[:, hs] = (jnp.dot(dsb, ks, preferred_element_type=F32) * scale).astype(dq_ref.dtype)
            dkv_ref[:, hs] += lax.dot_general(dsb, qs, _DN['tn'], preferred_element_type=F32) * scale
            dkv_ref[:, vs_cols] += lax.dot_general(p.astype(BF16), dos, _DN['tn'], preferred_element_type=F32)

    row = _col(tb, DX, 0)
    return _call(body, name='xattn_bwd', out_shape=(_sds((T, DX), BF16), _sds((M, 2 * DX), F32)), grid=(T // tb,),
                 in_specs=[row, _full((M, 2 * DX)), row], out_specs=(row, _full((M, 2 * DX))),
                 sem=('arbitrary',))(q, kv, do)


def _swiglu_fwd(gu):
    T, F2 = gu.shape
    F = F2 // 2
    tb = _pick(T, (256, 128))

    def body(g_ref, u_ref, o_ref):
        g = g_ref[...].astype(F32)
        o_ref[...] = (g * _sigmoid(g) * u_ref[...].astype(F32)).astype(o_ref.dtype)

    return _call(body, name='swiglu_fwd', out_shape=_sds((T, F), BF16), grid=(T // tb,),
                 in_specs=[_col(tb, F, 0), _col(tb, F, 1)], out_specs=_col(tb, F, 0), sem=('parallel',))(gu, gu)


def _swiglu_bwd(gu, da):
    T, F2 = gu.shape
    F = F2 // 2
    tb = _pick(T, (256, 128))

    def body(g_ref, u_ref, da_ref, o_ref):
        g = g_ref[...].astype(F32)
        u = u_ref[...].astype(F32)
        dav = da_ref[...].astype(F32)
        sg = _sigmoid(g)
        o_ref[:, 0:F] = (dav * u * (sg * (1.0 + g * (1.0 - sg)))).astype(o_ref.dtype)
        o_ref[:, F:F2] = (dav * g * sg).astype(o_ref.dtype)

    return _call(body, name='swiglu_bwd', out_shape=_sds((T, F2), BF16), grid=(T // tb,),
                 in_specs=[_col(tb, F, 0), _col(tb, F, 1), _col(tb, F, 0)], out_specs=_col(tb, F2, 0),
                 sem=('parallel',))(gu, gu, da)


def _place():
    x, y, c = lax.axis_index('x'), lax.axis_index('y'), lax.axis_index('c')
    return x, y, c


def _allgather(arrs, lead, name):
    n = len(arrs)

    def out_shape(a, ld):
        return (N_DEV,) + a.shape if ld else (a.shape[0], N_DEV) + a.shape[1:]

    def body(*refs):
        ins, outs = refs[:n], refs[n:2 * n]
        send_sems, recv_sems, local_sems = refs[2 * n:]
        x, y, c = _place()
        me, sibling = (x, y, c), (x, y, 1 - c)
        chips = [(1 - x, y), (x, 1 - y), (1 - x, 1 - y)]

        def slot(a, dev):
            idx = 4 * dev[0] + 2 * dev[1] + dev[2]
            return outs[a].at[idx] if lead[a] else outs[a].at[:, idx]

        def copy(a, k, block, to, src=None):
            return pltpu.make_async_remote_copy(
                src_ref=slot(a, block) if src is None else src, dst_ref=slot(a, block),
                send_sem=send_sems.at[a * 7 + k], recv_sem=recv_sems.at[a * 7 + k], device_id=to, device_id_type=MESH)

        mine = [pltpu.make_async_copy(ins[a], slot(a, me), local_sems.at[a]) for a in range(n)]
        for cp in mine:
            cp.start()
        first = []
        for a in range(n):
            first.append(copy(a, 0, me, sibling, src=ins[a]))
            first += [copy(a, 1 + j, me, (*chip, c), src=ins[a]) for j, chip in enumerate(chips)]
        for cp in first:
            cp.start()
        passed = []
        for a in range(n):
            for j, chip in enumerate(chips):
                copy(a, 1 + j, (*chip, c), me).wait_recv()
                fwd = copy(a, 4 + j, (*chip, c), sibling)
                fwd.start()
                passed.append(fwd)
        for a in range(n):
            copy(a, 0, sibling, me).wait_recv()
            for j, chip in enumerate(chips):
                copy(a, 4 + j, (*chip, 1 - c), me).wait_recv()
        for cp in first + passed:
            cp.wait_send()
        for cp in mine:
            cp.wait()

    any_spec = pl.BlockSpec(memory_space=pl.ANY)
    return _call(body, name=name, out_shape=tuple(_sds(out_shape(a, ld), a.dtype) for a, ld in zip(arrs, lead)),
                 in_specs=[any_spec] * n, out_specs=tuple([any_spec] * n),
                 scratch=[pltpu.SemaphoreType.DMA((7 * n,)), pltpu.SemaphoreType.DMA((7 * n,)),
                          pltpu.SemaphoreType.DMA((n,))])(*arrs)


_PEER_ORDER = (1, 4, 2, 5, 3, 6, 7)


def _peers():
    x, y, c = _place()
    peers = []
    for r in _PEER_ORDER:
        px, py, pc = x ^ (r >> 2), y ^ ((r >> 1) & 1), c ^ (r & 1)
        peers.append((r - 1, (px, py, pc), 4 * px + 2 * py + pc))
    return 4 * x + 2 * y + c, peers


def _plan_copies(plan, srcs, lands, send_sems, recv_sems):
    me, peers = _peers()
    copies = []
    for k, dev, peer in peers:
        for si, li, kind, args in plan:
            src, land = srcs[si], lands[li]
            if kind == 'rows':
                s0, n, stride, d0 = args
                src, dst = src.at[pl.ds(s0, n)], land.at[pl.ds(pl.multiple_of(stride * me + d0, ROW_TILE), n)]
            elif kind == 'slot':
                src, dst = (src if args is None else src.at[pl.ds(args[0], args[1])]), land.at[me]
            else:
                stride, n = args
                src, dst = src.at[pl.ds(pl.multiple_of(stride * peer, ROW_TILE), n)], land.at[me]
            copies.append(pltpu.make_async_remote_copy(src_ref=src, dst_ref=dst, send_sem=send_sems.at[k],
                                                       recv_sem=recv_sems.at[k], device_id=dev, device_id_type=MESH))
    return copies


_HBM = pl.BlockSpec(memory_space=pltpu.HBM)
_SEM = pl.BlockSpec(memory_space=pltpu.SEMAPHORE)


def _xfer_start(srcs, lands, plan, name):
    ns, nb = len(srcs), len(srcs) + len(lands)

    def body(*refs):
        send_sems, recv_sems, token = refs[nb], refs[nb + 1], refs[-1]
        for cp in _plan_copies(plan, refs[:ns], refs[ns:nb], send_sems, recv_sems):
            cp.start()
        token[...] = jnp.zeros_like(token)

    bufs = [pltpu.with_memory_space_constraint(v, pltpu.HBM) for v in (*srcs, *lands)]
    res = pl.pallas_call(
        body, name=name,
        out_shape=(pltpu.SemaphoreType.DMA((N_DEV - 1,)), pltpu.SemaphoreType.DMA((N_DEV - 1,)),
                   *[pltpu.HBM(v.shape, v.dtype) for v in bufs], _sds((8, LANES), F32)),
        in_specs=[_HBM] * nb, out_specs=(_SEM, _SEM, *[_HBM] * nb, pl.BlockSpec(memory_space=pltpu.VMEM)),
        input_output_aliases={i: 2 + i for i in range(nb)},
        compiler_params=pltpu.CompilerParams(has_side_effects=pltpu.SideEffectType.DATAFLOW_SIDE_EFFECTING),
        interpret=False)(*bufs)
    return res[0], res[1], list(res[2:2 + ns]), list(res[2 + ns:2 + nb]), res[-1]


def _xfer_wait(send_sems, recv_sems, srcs, lands, plan, after, name):
    ns, nb = len(srcs), len(srcs) + len(lands)

    def body(*refs):
        copies = _plan_copies(plan, refs[:ns], refs[ns:nb], refs[nb], refs[nb + 1])
        for cp in copies:
            cp.wait_send()
        for cp in copies:
            cp.wait_recv()

    res = pl.pallas_call(
        body, name=name, out_shape=tuple(pltpu.HBM(v.shape, v.dtype) for v in (*srcs, *lands)),
        in_specs=[_HBM] * nb + [_SEM, _SEM] + [pl.BlockSpec(memory_space=pl.ANY)] * len(after),
        out_specs=tuple([_HBM] * nb), input_output_aliases={i: i for i in range(nb)},
        compiler_params=pltpu.CompilerParams(has_side_effects=pltpu.SideEffectType.DATAFLOW_SIDE_EFFECTING),
        interpret=False)(*srcs, *lands, send_sems, recv_sems, *after)
    return list(res[ns:])


def _merge_edges(wt, edges, tiles_per_dev):
    D = wt.shape[1]

    def body(w_ref, e_ref, o_ref):
        s = pl.program_id(0)
        o_ref[...] = jnp.where(s == 0, e_ref[...], w_ref[...] + e_ref[...])

    tile = pl.BlockSpec((ROW_TILE, D), lambda s: (s * tiles_per_dev, 0))
    return _call(body, name='merge_edges', out_shape=_sds(wt.shape, wt.dtype), grid=(N_DEV,),
                 in_specs=[tile, pl.BlockSpec((None, ROW_TILE, D), lambda s: (s, 0, 0))], out_specs=tile,
                 sem=('arbitrary',), aliases={0: 0})(wt, edges)


def _adam_update(g, w, m, v):
    c1 = 1.0 - ADAM_B1 ** ADAM_STEP
    c2 = 1.0 - ADAM_B2 ** ADAM_STEP
    mn = ADAM_B1 * m + (1.0 - ADAM_B1) * g
    vn = ADAM_B2 * v + (1.0 - ADAM_B2) * (g * g)
    m_hat = mn / c1
    v_hat = vn / c2
    return -ADAM_LR * (m_hat / (jnp.sqrt(v_hat) + ADAM_EPS) + ADAM_WD * w), mn, vn


def _adamw(parts, w, m, v, name):
    n_parts = parts.shape[0]
    R, C = w.shape
    tb = _pick(R, (128, 64, 32, 16, 8))

    def body(p_ref, w_ref, m_ref, v_ref, g_out, d_out, m_out, v_out):
        g = p_ref[0].astype(F32)
        for s in range(1, n_parts):
            g = g + p_ref[s].astype(F32)
        g_out[...] = g
        d_out[...], m_out[...], v_out[...] = _adam_update(g, w_ref[...], m_ref[...], v_ref[...])

    row = pl.BlockSpec((tb, C), lambda i: (i, 0))
    o = _sds((R, C), F32)
    return _call(body, name=name, out_shape=(o, o, o, o), grid=(R // tb,),
                 in_specs=[pl.BlockSpec((n_parts, tb, C), lambda i: (0, i, 0)), row, row, row],
                 out_specs=(row, row, row, row), sem=('parallel',))(parts, w, m, v)


def _adamw_layer(parts, w, m, v, layer, prev, shift, name):
    L, n, C = w.shape
    n_p = parts.shape[1]
    cb = C if N_DEV * n_p * C * parts.dtype.itemsize <= (6 << 20) else 2 * LANES
    n_prev = 0 if prev is None else 4

    def body(sh_ref, p_ref, w_ref, m_ref, v_ref, *rest):
        g_out, d_out, m_out, v_out = rest[n_prev:n_prev + 4]
        g = p_ref[0].astype(F32)
        for s in range(1, N_DEV):
            g = g + p_ref[s].astype(F32)
        if n_p != n:
            rolled = rest[-1]
            rolled[...] = pltpu.roll(g, n_p - sh_ref[0], 0)
            g = rolled[0:n, :]
        g_out[...] = g
        d_out[...], m_out[...], v_out[...] = _adam_update(g, w_ref[...], m_ref[...], v_ref[...])

    lay = pl.BlockSpec((None, n, cb), lambda j, sh: (layer, 0, j))
    stack = _sds((L, n, C), F32)
    grid_spec = pltpu.PrefetchScalarGridSpec(
        num_scalar_prefetch=1, grid=(C // cb,),
        in_specs=[pl.BlockSpec((N_DEV, n_p, cb), lambda j, sh: (0, 0, j)), lay, lay, lay]
        + [pl.BlockSpec(memory_space=pl.ANY)] * n_prev,
        out_specs=(lay, lay, lay, lay), scratch_shapes=[pltpu.VMEM((n_p, cb), F32)] if n_p != n else [])
    return pl.pallas_call(body, name=name, grid_spec=grid_spec, out_shape=(stack, stack, stack, stack),
                          input_output_aliases={5 + k: k for k in range(n_prev)},
                          compiler_params=pltpu.CompilerParams(dimension_semantics=('parallel',),
                                                               vmem_limit_bytes=VMEM_LIMIT),
                          interpret=False)(shift, parts, w, m, v, *(prev or ()))


def kernel(x, mem, mix_norm, w_in, b_gate, b_forget, conv_a, w_out_a, w_out_b, conv_c, conv_c_bias, ln_c_gain, ln_c_bias, w_out_c, w_o, xattn_norm, mem_norm, w_xq, w_xkv, w_xo, ffn_norm, w_gate_up, w_down, final_norm, loss_target, m_mix_norm, m_w_in, m_b_gate, m_b_forget, m_conv_a, m_w_out_a, m_w_out_b, m_conv_c, m_conv_c_bias, m_ln_c_gain, m_ln_c_bias, m_w_out_c, m_w_o, m_xattn_norm, m_mem_norm, m_w_xq, m_w_xkv, m_w_xo, m_ffn_norm, m_w_gate_up, m_w_down, m_final_norm, v_mix_norm, v_w_in, v_b_gate, v_b_forget, v_conv_a, v_w_out_a, v_w_out_b, v_conv_c, v_conv_c_bias, v_ln_c_gain, v_ln_c_bias, v_w_out_c, v_w_o, v_xattn_norm, v_mem_norm, v_w_xq, v_w_xkv, v_w_xo, v_ffn_norm, v_w_gate_up, v_w_down, v_final_norm):
    P = dict(zip(ARG_NAMES, (x, mem, mix_norm, w_in, b_gate, b_forget, conv_a, w_out_a, w_out_b, conv_c, conv_c_bias, ln_c_gain, ln_c_bias, w_out_c, w_o, xattn_norm, mem_norm, w_xq, w_xkv, w_xo, ffn_norm, w_gate_up, w_down, final_norm, loss_target, m_mix_norm, m_w_in, m_b_gate, m_b_forget, m_conv_a, m_w_out_a, m_w_out_b, m_conv_c, m_conv_c_bias, m_ln_c_gain, m_ln_c_bias, m_w_out_c, m_w_o, m_xattn_norm, m_mem_norm, m_w_xq, m_w_xkv, m_w_xo, m_ffn_norm, m_w_gate_up, m_w_down, m_final_norm, v_mix_norm, v_w_in, v_b_gate, v_b_forget, v_conv_a, v_w_out_a, v_w_out_b, v_conv_c, v_conv_c_bias, v_ln_c_gain, v_ln_c_bias, v_w_out_c, v_w_o, v_xattn_norm, v_mem_norm, v_w_xq, v_w_xkv, v_w_xo, v_ffn_norm, v_w_gate_up, v_w_down, v_final_norm)))
    return _step(P)


_T_VIEW = ('w_in', 'w_gate_up', 'w_xo')
_BIG = [w for w in SHARDED if w not in ('conv_a', 'conv_c')]
_OTHER = [w for w in _BIG if w != 'w_in']
_CONVS = (('conv_a', CONV_A_W, 8), ('conv_c', CONV_C_W, 32))


def _view(name, a):
    return jnp.transpose(a, (0, 2, 1)) if name in _T_VIEW else a


def _step(P):
    L, D = P['mix_norm'].shape
    T = P['x'].shape[1]
    n_in = P['w_in'].shape[2]
    stride = n_in // ROW_TILE * ROW_TILE
    rem = n_in - stride
    win = stride + ROW_TILE
    assert rem * N_DEV == ROW_TILE
    r_in = n_in * N_DEV
    f_off = 6 * D
    hi_off = f_off + FOX_HEADS
    x_i, y_i, c_i = _place()
    me = 4 * x_i + 2 * y_i + c_i
    V = {}
    for w in _BIG:
        for k in (w, 'm_' + w, 'v_' + w):
            V[k] = _view(w, P[k])
    n_own = {w: V[w].shape[1] for w in _OTHER}
    shift = jnp.reshape(rem * me, (1,)).astype(jnp.int32)
    no_shift = jnp.zeros((1,), jnp.int32)
    row = lambda a, l: a[l][None, :]
    dus = lax.dynamic_update_slice

    src_bf = {w: V[w].astype(BF16) for w in _OTHER}
    p_in = dus(jnp.zeros((L, win, D), BF16), V['w_in'].astype(BF16), (0, rem * me, 0))

    def gather_start(l):
        srcs = [p_in[l]] + [src_bf[w][l] for w in _OTHER] + [P[w][l] for w, _, _ in _CONVS]
        lands = [dus(lax.empty((r_in, D), BF16), p_in[l, ROW_TILE:], (stride * me + ROW_TILE, 0)),
                 dus(lax.empty((N_DEV, ROW_TILE, D), BF16), p_in[l, :ROW_TILE][None], (me, 0, 0))]
        plan = [(0, 0, 'rows', (ROW_TILE, stride, stride, ROW_TILE)), (0, 1, 'slot', (0, ROW_TILE))]
        for i, w in enumerate(_OTHER):
            n, s = n_own[w], srcs[1 + i]
            lands.append(dus(lax.empty((N_DEV * n, s.shape[1]), BF16), s, (n * me, 0)))
            plan.append((1 + i, 2 + i, 'rows', (0, n, n, 0)))
        for j in range(len(_CONVS)):
            i = 1 + len(_OTHER) + j
            lands.append(dus(lax.empty((N_DEV,) + srcs[i].shape, F32), srcs[i][None], (me, 0, 0)))
            plan.append((i, i + 1, 'slot', None))
        return (plan,) + _xfer_start(srcs, lands, plan, f'gather_start_{l}')

    def gather_wait(g, after, l):
        plan, send_sems, recv_sems, srcs, lands, _ = g
        lands = _xfer_wait(send_sems, recv_sems, srcs, lands, plan, after, f'gather_wait_{l}')
        W = {'w_in': _merge_edges(lands[0], lands[1], stride // ROW_TILE)}
        for i, w in enumerate(_OTHER):
            W[w] = lands[2 + i]
        for j, (w, taps, padded) in enumerate(_CONVS):
            full = jnp.transpose(lands[2 + len(_OTHER) + j], (1, 0, 2)).reshape(taps, D)
            W[w] = jnp.pad(full, ((0, padded - taps), (0, 0)))
        return W

    gathers = [gather_start(l) for l in range(L)]
    tokens = [g[-1] for g in gathers]

    mem_x = P['mem'][0]
    mem_n = _rms_fwd(mem_x, P['mem_norm'][None, :], 'rms_mem')
    xs = P['x'][0]
    saved = []
    for l in range(L):
        W = gather_wait(gathers[l], [xs] + (tokens if l == 0 else []), l)
        s = {'x0': xs, 'W': W}
        wt = W['w_in']
        s['h1'] = _rms_fwd(xs, row(P['mix_norm'], l), 'rms_mix')
        s['z1'] = z1 = _mm(s['h1'], wt, mode='nt', b_rows=(0, f_off), name='mm_in_lo')
        s['z2'] = z2 = _mm(s['h1'], wt, mode='nt', b_rows=(hi_off, r_in - hi_off), name='mm_in_hi')
        s['zf'] = _mm(s['h1'], wt, mode='nt', b_rows=(f_off, FOX_HEADS), out_dtype=F32, name='mm_in_f')
        s['ya_pre'] = _mixa_fwd(z1, W['conv_a'])
        s['ya'] = _mm(s['ya_pre'], W['w_out_a'], mode='nn', name='mm_out_a')
        s['c'] = _fox_c(s['zf'], row(P['b_forget'], l))
        s['ct'] = jnp.transpose(s['c'])
        s['o'], s['lse'] = _fox_fwd(z1, s['c'], s['ct'])
        s['yb'] = _mm(s['o'], W['w_out_b'], mode='nn', name='mm_out_b')
        s['u3'], s['u1'] = _mixc_fwd(z2, W['conv_c'], row(P['conv_c_bias'], l), row(P['ln_c_gain'], l),
                                     row(P['ln_c_bias'], l))
        s['yc'] = _mm(s['u3'], W['w_out_c'], mode='nn', name='mm_out_c')
        s['merged'] = _gate_fwd(z2, row(P['b_gate'], l), s['ya'], s['yb'], s['yc'])
        xs = _mm(s['merged'], W['w_o'], mode='nn', add=xs, out_dtype=F32, name='mm_o')
        s['x1'] = xs
        s['h2'] = _rms_fwd(xs, row(P['xattn_norm'], l), 'rms_xattn')
        s['qx'] = _mm(s['h2'], W['w_xq'], mode='nn', name='mm_xq')
        s['kv'] = _mm(mem_n, W['w_xkv'], mode='nn', name='mm_xkv')
        s['ox'] = _xattn_fwd(s['qx'], s['kv'])
        xs = _mm(s['ox'], W['w_xo'], mode='nt', add=xs, out_dtype=F32, name='mm_xo')
        s['x2'] = xs
        s['h3'] = _rms_fwd(xs, row(P['ffn_norm'], l), 'rms_ffn')
        s['gu'] = _mm(s['h3'], W['w_gate_up'], mode='nt', name='mm_gate_up')
        s['act'] = _swiglu_fwd(s['gu'])
        xs = _mm(s['act'], W['w_down'], mode='nn', add=xs, out_dtype=F32, name='mm_down')
        saved.append(s)

    dx, d_final_norm, loss_part = _final(xs, P['final_norm'][None, :], P['loss_target'][0])

    GS = {w: [None] * L for w in SMALL}
    G_conv = {'conv_a': [None] * L, 'conv_c': [None] * L}
    d_mem_n = None
    results = {w: None for w in _BIG}

    def exchange_start(G, l):
        srcs = [G[w] for w in _BIG]
        plan, lands = [], []
        for i, w in enumerate(_BIG):
            step, n = (stride, win) if w == 'w_in' else (n_own[w], n_own[w])
            own = lax.dynamic_slice(srcs[i], (step * me, 0), (n, srcs[i].shape[1]))
            lands.append(dus(lax.empty((N_DEV, n, srcs[i].shape[1]), BF16), own[None], (me, 0, 0)))
            plan.append((i, i, 'chunk', (step, n)))
        return (plan,) + _xfer_start(srcs, lands, plan, f'exchange_start_{l}')

    def exchange_finish(e, after, l):
        plan, send_sems, recv_sems, srcs, lands, _ = e
        lands = _xfer_wait(send_sems, recv_sems, srcs, lands, plan, after, f'exchange_wait_{l}')
        for i, w in enumerate(_BIG):
            results[w] = _adamw_layer(lands[i], V[w], V['m_' + w], V['v_' + w], l, results[w],
                                      shift if w == 'w_in' else no_shift, 'adamw_' + w)

    pending = None
    for l in reversed(range(L)):
        s = saved[l]
        W, z1, z2 = s['W'], s['z1'], s['z2']
        wt = W['w_in']
        G = {}
        d_act = _mm(dx, W['w_down'], mode='nt', name='mmb_down', deps=[pending[0][-1]] if pending else ())
        G['w_down'] = _mm(s['act'], dx, mode='tn', name='mmg_w_down')
        dgu = _swiglu_bwd(s['gu'], d_act)
        dh3 = _mm(dgu, W['w_gate_up'], mode='nn', out_dtype=F32, name='mmb_gate_up')
        G['w_gate_up'] = _mm(dgu, s['h3'], mode='tn', name='mmg_w_gate_up')
        dx, GS['ffn_norm'][l] = _rms_bwd(s['x2'], row(P['ffn_norm'], l), dh3, dx, 'rms_ffn_bwd')
        d_ox = _mm(dx, W['w_xo'], mode='nn', out_dtype=F32, name='mmb_xo')
        G['w_xo'] = _mm(dx, s['ox'], mode='tn', name='mmg_w_xo')
        dqx, dkv = _xattn_bwd(s['qx'], s['kv'], d_ox)
        dh2 = _mm(dqx, W['w_xq'], mode='nt', out_dtype=F32, name='mmb_xq')
        G['w_xq'] = _mm(s['h2'], dqx, mode='tn', name='mmg_w_xq')
        G['w_xkv'] = _mm(mem_n, dkv, mode='tn', name='mmg_w_xkv')
        d_mem_n = _mm(dkv, W['w_xkv'], mode='nt', add=d_mem_n, out_dtype=F32, name='mmb_xkv')
        dx, GS['xattn_norm'][l] = _rms_bwd(s['x1'], row(P['xattn_norm'], l), dh2, dx, 'rms_xattn_bwd')
        dm = _mm(dx, W['w_o'], mode='nt', out_dtype=F32, name='mmb_o')
        G['w_o'] = _mm(s['merged'], dx, mode='tn', name='mmg_w_o')
        dya, dyb, dyc, dz_g, GS['b_gate'][l] = _gate_bwd(z2, row(P['b_gate'], l), s['ya'], s['yb'], s['yc'], dm)
        du3 = _mm(dyc, W['w_out_c'], mode='nt', out_dtype=F32, name='mmb_out_c')
        G['w_out_c'] = _mm(s['u3'], dyc, mode='tn', name='mmg_w_out_c')
        dz_c, G_conv['conv_c'][l], GS['conv_c_bias'][l], GS['ln_c_gain'][l], GS['ln_c_bias'][l] = _mixc_bwd(
            z2, W['conv_c'], row(P['ln_c_gain'], l), row(P['ln_c_bias'], l), s['u1'], du3)
        do = _mm(dyb, W['w_out_b'], mode='nt', name='mmb_out_b')
        G['w_out_b'] = _mm(s['o'], dyb, mode='tn', name='mmg_w_out_b')
        dq, dk, dv, dcq, dck = _fox_bwd(z1, s['c'], s['ct'], s['o'], s['lse'], do)
        hp = FOX_HEADS // dcq.shape[0]
        dc = (jnp.transpose(dcq[:, :, :hp], (1, 0, 2)).reshape(T, FOX_HEADS)
              + jnp.transpose(dck[:, :hp, :], (2, 0, 1)).reshape(T, FOX_HEADS))
        dzf, GS['b_forget'][l] = _fox_c_bwd(dc, s['zf'], row(P['b_forget'], l))
        dya_pre = _mm(dya, W['w_out_a'], mode='nt', out_dtype=F32, name='mmb_out_a')
        G['w_out_a'] = _mm(s['ya_pre'], dya, mode='tn', name='mmg_w_out_a')
        dz_a, G_conv['conv_a'][l] = _mixa_bwd(z1, W['conv_a'], dya_pre)
        segs = [(dzf, f_off), (dz_a, 0), (dq, 3 * D), (dk, 4 * D), (dv, 5 * D), (dz_c, hi_off), (dz_g, hi_off + 2 * D)]
        dh1, g_in = None, r_in
        for dz, off in segs:
            dh1 = _mm(dz, wt, mode='nn', b_rows=(off, dz.shape[1]), add=dh1, out_dtype=F32, name='mmb_in')
            g_in = _mm(dz, s['h1'], mode='tn', out=g_in, o_off=off, name='mmg_w_in')
        G['w_in'] = g_in
        dx, GS['mix_norm'][l] = _rms_bwd(s['x0'], row(P['mix_norm'], l), dh1, dx, 'rms_mix_bwd')
        started = exchange_start(G, l)
        if pending is not None:
            exchange_finish(pending[0], [dx], pending[1])
        pending = (started, l)

    _, d_mem_norm = _rms_bwd(mem_x, P['mem_norm'][None, :], d_mem_n, None, 'rms_mem_bwd')

    small_parts = {w: jnp.concatenate(GS[w], axis=0) for w in SMALL if w not in ('mem_norm', 'final_norm')}
    small_parts['mem_norm'] = d_mem_norm
    small_parts['final_norm'] = d_final_norm
    conv_parts = [jnp.stack(G_conv[w])[:, :taps] for w, taps, _ in _CONVS]
    sizes = [P[w].size for w in SMALL]
    conv_sizes = [c.size for c in conv_parts]
    n_small = sum(sizes) + sum(conv_sizes) + LANES
    n_rows = -(-n_small // (8 * LANES)) * 8

    def pack(parts):
        flat = jnp.concatenate([p.reshape(-1) for p in parts])
        return jnp.pad(flat, (0, n_rows * LANES - flat.size)).reshape(n_rows, LANES)

    g_small = pack([small_parts[w] for w in SMALL] + conv_parts + [loss_part])
    (all_small,) = _allgather([g_small], [True], 'gather_small')
    sm = _adamw(all_small, pack([P[w] for w in SMALL]), pack([P['m_' + w] for w in SMALL]),
                pack([P['v_' + w] for w in SMALL]), 'adamw_small')
    g_all = sm[0].reshape(-1)
    loss = g_all[sum(sizes) + sum(conv_sizes)]
    final = {}
    off = 0
    for w, n in zip(SMALL, sizes):
        final[w] = [o.reshape(-1)[off:off + n].reshape(P[w].shape) for o in sm]
        off += n
    for (w, taps, _), c, n in zip(_CONVS, conv_parts, conv_sizes):
        n_col = P[w].shape[2]
        g_own = lax.dynamic_slice(g_all[off:off + n].reshape(c.shape), (0, 0, n_col * me), (L, taps, n_col))
        flat = lambda a: a.reshape(L * taps, n_col)
        outs = _adamw(flat(g_own)[None], flat(P[w]), flat(P['m_' + w]), flat(P['v_' + w]), 'adamw_' + w)
        final[w] = [o.reshape(P[w].shape) for o in outs]
        off += n

    exchange_finish(pending[0], [sm[0]], pending[1])
    for w in _BIG:
        final[w] = [_view(w, o) for o in results[w]]

    out = [loss, dx[None]]
    for k in range(4):
        out += [final[w][k] for w in WEIGHTS]
    return tuple(out)
```

```python
import functools

import jax
import jax.numpy as jnp
from jax import lax
from jax.experimental import pallas as pl
from jax.experimental.pallas import tpu as pltpu

F32 = jnp.float32
BF16 = jnp.bfloat16
MESH = pl.DeviceIdType.MESH

N_DEV = 8
EPS = 1e-6
FOX_HEADS = 16
X_HEADS = 4
LANES = 128
ROW_TILE = 16
HALO_A = 16
HALO_C = 32
CONV_A_W = 3
CONV_C_W = 31
VMEM_LIMIT = 56 << 20

ADAM_LR = 0.001
ADAM_B1 = 0.9
ADAM_B2 = 0.999
ADAM_EPS = 1e-08
ADAM_WD = 0.01
ADAM_STEP = 10

WEIGHTS = ['mix_norm', 'w_in', 'b_gate', 'b_forget', 'conv_a', 'w_out_a', 'w_out_b', 'conv_c', 'conv_c_bias',
           'ln_c_gain', 'ln_c_bias', 'w_out_c', 'w_o', 'xattn_norm', 'mem_norm', 'w_xq', 'w_xkv', 'w_xo',
           'ffn_norm', 'w_gate_up', 'w_down', 'final_norm']
SHARDED = ['w_in', 'conv_a', 'w_out_a', 'w_out_b', 'conv_c', 'w_out_c', 'w_o', 'w_xq', 'w_xkv', 'w_xo',
           'w_gate_up', 'w_down']
ROW_SHARDED = ['w_out_a', 'w_out_b', 'w_out_c', 'w_o', 'w_xq', 'w_xkv', 'w_down']
SMALL = [w for w in WEIGHTS if w not in SHARDED]
ARG_NAMES = (['x', 'mem'] + WEIGHTS + ['loss_target'] + ['m_' + w for w in WEIGHTS] + ['v_' + w for w in WEIGHTS])


def _pick(n, cands=(1024, 1408, 512, 256, 128)):
    for c in cands:
        if n % c == 0:
            return c
    return n


def _call(body, *, name, out_shape, grid=(), in_specs=None, out_specs=None, scratch=(), sem=None, aliases=None,
          deps=()):
    params = dict(vmem_limit_bytes=VMEM_LIMIT)
    if sem is not None:
        params['dimension_semantics'] = sem
    n_in, n_dep = len(in_specs), len(deps)

    def body_without_deps(*refs):
        body(*refs[:n_in], *refs[n_in + n_dep:])

    call = pl.pallas_call(body_without_deps, name=name, out_shape=out_shape, grid=grid, scratch_shapes=list(scratch),
                          in_specs=list(in_specs) + [pl.BlockSpec(memory_space=pl.ANY)] * n_dep, out_specs=out_specs,
                          input_output_aliases=aliases or {}, compiler_params=pltpu.CompilerParams(**params),
                          interpret=False)
    return lambda *args: call(*args, *deps)


def _sds(shape, dtype):
    return jax.ShapeDtypeStruct(tuple(shape), dtype)


def _sigmoid(x):
    return 1.0 / (1.0 + jnp.exp(-x))


_DN = {'nn': (((1,), (0,)), ((), ())), 'nt': (((1,), (1,)), ((), ())), 'tn': (((0,), (0,)), ((), ()))}


def _mm(a, b, *, mode, name, out_dtype=BF16, add=None, b_rows=None, out=None, o_off=0, deps=()):
    if mode == 'tn':
        K, M = a.shape
    else:
        M, K = a.shape
    b_off, b_n = (0, b.shape[0]) if b_rows is None else b_rows
    N = b_n if mode == 'nt' else b.shape[1]
    assert (b.shape[1] if mode == 'nt' else b_n) == K
    tm, tn = _pick(M), _pick(N)
    tk = K if K <= 2048 else _pick(K)
    ni, nj, nk = M // tm, N // tn, K // tk
    a_bytes, b_bytes = M * K * a.dtype.itemsize, K * N * b.dtype.itemsize
    n_outer = (b_bytes + nj * a_bytes) < (a_bytes + ni * b_bytes)
    if n_outer:
        grid = (nj, ni, nk)
        ij = lambda g0, g1: (g1, g0)
    else:
        grid = (ni, nj, nk)
        ij = lambda g0, g1: (g0, g1)

    def a_map(g0, g1, k):
        i, _ = ij(g0, g1)
        return (k, i) if mode == 'tn' else (i, k)

    def elems(blk):
        return tuple(pl.Element(s) for s in blk)

    def at(rows, cols):
        return pl.multiple_of(rows, ROW_TILE), pl.multiple_of(cols, LANES)

    a_blk = (tk, tm) if mode == 'tn' else (tm, tk)
    b_blk = (tn, tk) if mode == 'nt' else (tk, tn)
    if b_rows is None:
        def b_map(g0, g1, k):
            _, j = ij(g0, g1)
            return (j, k) if mode == 'nt' else (k, j)
        b_spec = pl.BlockSpec(b_blk, b_map)
    else:
        def b_map(g0, g1, k):
            _, j = ij(g0, g1)
            return at(b_off + j * tn, k * tk) if mode == 'nt' else at(b_off + k * tk, j * tn)
        b_spec = pl.BlockSpec(elems(b_blk), b_map)
    in_specs = [pl.BlockSpec(a_blk, a_map), b_spec]
    args = [a, b]
    has_add = add is not None
    if has_add:
        in_specs.append(pl.BlockSpec((tm, tn), lambda g0, g1, k: ij(g0, g1)))
        args.append(add)
    aliases = {}
    if out is None:
        out_shape = _sds((M, N), out_dtype)
        out_spec = pl.BlockSpec((tm, tn), lambda g0, g1, k: ij(g0, g1))
    else:
        if isinstance(out, int):
            out_shape = _sds((out, N), out_dtype)
        else:
            out_shape = _sds(out.shape, out.dtype)
            in_specs.append(pl.BlockSpec(memory_space=pl.ANY))
            aliases = {len(args): 0}
            args.append(out)

        def o_map(g0, g1, k):
            i, j = ij(g0, g1)
            return at(o_off + i * tm, j * tn)
        out_spec = pl.BlockSpec(elems((tm, tn)), o_map)
    dn = _DN[mode]
    n_in = len(args)

    def body(*refs):
        a_ref, b_ref = refs[0], refs[1]
        add_ref = refs[2] if has_add else None
        o_ref = refs[n_in]
        part = lax.dot_general(a_ref[...].astype(BF16), b_ref[...].astype(BF16), dn, preferred_element_type=F32)

        def finish(r):
            if has_add:
                r = r + add_ref[...]
            o_ref[...] = r.astype(o_ref.dtype)

        if nk == 1:
            finish(part)
        else:
            acc = refs[n_in + 1]
            k = pl.program_id(2)

            @pl.when(k == 0)
            def _():
                acc[...] = part

            @pl.when(k > 0)
            def _():
                acc[...] += part

            @pl.when(k == nk - 1)
            def _():
                finish(acc[...])

    scratch = [pltpu.VMEM((tm, tn), F32)] if nk > 1 else []
    return _call(body, name=name, out_shape=out_shape, grid=grid, in_specs=in_specs, out_specs=out_spec,
                 scratch=scratch, sem=('parallel', 'parallel', 'arbitrary'), aliases=aliases, deps=deps)(*args)


def _rms_fwd(x, g, name, deps=()):
    R, D = x.shape
    tb = _pick(R, (512, 256, 128))

    def body(x_ref, g_ref, o_ref):
        xv = x_ref[...]
        r = lax.rsqrt(jnp.mean(xv * xv, axis=-1, keepdims=True) + EPS)
        o_ref[...] = (xv * r * g_ref[...]).astype(o_ref.dtype)

    row = pl.BlockSpec((tb, D), lambda i: (i, 0))
    vec = pl.BlockSpec((1, D), lambda i: (0, 0))
    return _call(body, name=name, out_shape=_sds((R, D), BF16), grid=(R // tb,), in_specs=[row, vec],
                 out_specs=row, sem=('parallel',), deps=deps)(x, g)


def _rms_bwd(x, g, dh, dres, name, deps=()):
    R, D = x.shape
    tb = _pick(R, (512, 256, 128))
    has_res = dres is not None

    def body(*refs):
        x_ref, g_ref, dh_ref = refs[:3]
        dx_ref, dg_ref = refs[-2:]
        xv = x_ref[...]
        dhv = dh_ref[...].astype(F32)
        r = lax.rsqrt(jnp.mean(xv * xv, axis=-1, keepdims=True) + EPS)
        u = dhv * g_ref[...]
        m = jnp.mean(u * xv, axis=-1, keepdims=True)
        dx = r * u - xv * (r * r * r * m)
        if has_res:
            dx = dx + refs[3][...]
        dx_ref[...] = dx

        @pl.when(pl.program_id(0) == 0)
        def _():
            dg_ref[...] = jnp.zeros_like(dg_ref)

        dg_ref[...] += jnp.sum(dhv * xv * r, axis=0, keepdims=True)

    row = pl.BlockSpec((tb, D), lambda i: (i, 0))
    vec = pl.BlockSpec((1, D), lambda i: (0, 0))
    args = [x, g, dh] + ([dres] if has_res else [])
    return _call(body, name=name, out_shape=(_sds((R, D), F32), _sds((1, D), F32)), grid=(R // tb,),
                 in_specs=[row, vec, row] + ([row] if has_res else []), out_specs=(row, vec),
                 sem=('arbitrary',), deps=deps)(*args)


def _final(x, g, tgt):
    R, D = x.shape
    tb = _pick(R, (512, 256, 128))

    def body(x_ref, g_ref, t_ref, dx_ref, dg_ref, loss_ref):
        xv = x_ref[...]
        gv = g_ref[...]
        r = lax.rsqrt(jnp.mean(xv * xv, axis=-1, keepdims=True) + EPS)
        e = xv * r * gv - t_ref[...]
        row_loss = jnp.mean(e * e, axis=-1, keepdims=True)
        blk_loss = 0.5 * jnp.sum(row_loss, axis=0, keepdims=True)
        dy = e * (1.0 / D)
        u = dy * gv
        m = jnp.mean(u * xv, axis=-1, keepdims=True)
        dx_ref[...] = r * u - xv * (r * r * r * m)

        @pl.when(pl.program_id(0) == 0)
        def _():
            dg_ref[...] = jnp.zeros_like(dg_ref)
            loss_ref[...] = jnp.zeros_like(loss_ref)

        dg_ref[...] += jnp.sum(dy * xv * r, axis=0, keepdims=True)
        loss_ref[...] += jnp.broadcast_to(blk_loss, loss_ref.shape)

    row = pl.BlockSpec((tb, D), lambda i: (i, 0))
    vec = pl.BlockSpec((1, D), lambda i: (0, 0))
    one = pl.BlockSpec((1, LANES), lambda i: (0, 0))
    return _call(body, name='final_loss', out_shape=(_sds((R, D), F32), _sds((1, D), F32), _sds((1, LANES), F32)),
                 grid=(R // tb,), in_specs=[row, vec, row], out_specs=(row, vec, one), sem=('arbitrary',))(x, g, tgt)


def _col(tb, w, cb):
    return pl.BlockSpec((tb, w), lambda i: (i, cb))


def _prev(tb, h, w, cb):
    return pl.BlockSpec((h, w), lambda i: (jnp.maximum(i * (tb // h) - 1, 0), cb))


def _next(tb, h, w, cb, rows):
    return pl.BlockSpec((h, w), lambda i: (jnp.minimum((i + 1) * (tb // h), rows // h - 1), cb))


def _full(shape):
    return pl.BlockSpec(shape, lambda i: (0,) * len(shape))


def _mixa_fwd(z, w):
    T = z.shape[0]
    D = w.shape[1]
    tb = _pick(T, (256, 128))
    H = HALO_A

    def body(ab, ac, au, acp, aup, w_ref, o_ref, ext):
        first = pl.program_id(0) == 0
        ext[0:H, :] = jnp.where(first, 0.0, acp[...].astype(F32) * aup[...].astype(F32))
        ext[H:H + tb, :] = ac[...].astype(F32) * au[...].astype(F32)
        cp = jnp.zeros((tb, D), F32)
        for k in range(CONV_A_W):
            off = H - (CONV_A_W - 1) + k
            cp = cp + ext[off:off + tb, :] * w_ref[k:k + 1, :]
        o_ref[...] = (ab[...].astype(F32) * cp).astype(o_ref.dtype)

    return _call(body, name='mixa_fwd', out_shape=_sds((T, D), BF16), grid=(T // tb,),
                 in_specs=[_col(tb, D, 0), _col(tb, D, 1), _col(tb, D, 2), _prev(tb, H, D, 1), _prev(tb, H, D, 2),
                           _full((8, D))],
                 out_specs=_col(tb, D, 0), scratch=[pltpu.VMEM((H + tb, D), F32)], sem=('parallel',))(z, z, z, z, z, w)


def _mixa_bwd(z, w, dy):
    T = z.shape[0]
    D = w.shape[1]
    tb = _pick(T, (256, 128))
    H = HALO_A
    nb = T // tb

    def body(ab, ac, au, acp, aup, abn, dy_ref, dyn, w_ref, dz_ref, dw_ref, pext, dext):
        i = pl.program_id(0)
        a_b, a_c, a_u = ab[...].astype(F32), ac[...].astype(F32), au[...].astype(F32)
        pext[0:H, :] = jnp.where(i == 0, 0.0, acp[...].astype(F32) * aup[...].astype(F32))
        pext[H:H + tb, :] = a_c * a_u
        dyv = dy_ref[...].astype(F32)
        dcp = dyv * a_b
        dext[0:tb, :] = dcp
        dext[tb:tb + H, :] = jnp.where(i == nb - 1, 0.0, dyn[...].astype(F32) * abn[...].astype(F32))

        @pl.when(i == 0)
        def _():
            dw_ref[...] = jnp.zeros_like(dw_ref)

        cp = jnp.zeros((tb, D), F32)
        dp = jnp.zeros((tb, D), F32)
        for k in range(CONV_A_W):
            off = H - (CONV_A_W - 1) + k
            wk = w_ref[k:k + 1, :]
            pk = pext[off:off + tb, :]
            cp = cp + pk * wk
            dp = dp + dext[CONV_A_W - 1 - k:CONV_A_W - 1 - k + tb, :] * wk
            dw_ref[k:k + 1, :] += jnp.sum(dcp * pk, axis=0, keepdims=True)
        dz_ref[:, 0:D] = (dyv * cp).astype(dz_ref.dtype)
        dz_ref[:, D:2 * D] = (dp * a_u).astype(dz_ref.dtype)
        dz_ref[:, 2 * D:3 * D] = (dp * a_c).astype(dz_ref.dtype)

    return _call(body, name='mixa_bwd', out_shape=(_sds((T, 3 * D), BF16), _sds((8, D), F32)), grid=(nb,),
                 in_specs=[_col(tb, D, 0), _col(tb, D, 1), _col(tb, D, 2), _prev(tb, H, D, 1), _prev(tb, H, D, 2),
                           _next(tb, H, D, 0, T), _col(tb, D, 0), _next(tb, H, D, 0, T), _full((8, D))],
                 out_specs=(_col(tb, 3 * D, 0), _full((8, D))),
                 scratch=[pltpu.VMEM((H + tb, D), F32), pltpu.VMEM((tb + H, D), F32)],
                 sem=('arbitrary',))(z, z, z, z, z, z, dy, dy, w)


def _split3(v):
    hi = v.astype(BF16)
    r1 = v - hi.astype(F32)
    mid = r1.astype(BF16)
    lo = (r1 - mid.astype(F32)).astype(BF16)
    return hi, mid, lo


def _tri_dot(tri, v):
    hi, mid, lo = _split3(v)
    d = lambda p: jnp.dot(tri, p, preferred_element_type=F32)
    return d(hi) + d(mid) + d(lo)


def _fox_c(zf, bf):
    T, H = zf.shape
    tb = LANES

    def body(zf_ref, b_ref, c_ref, carry):
        @pl.when(pl.program_id(0) == 0)
        def _():
            carry[...] = jnp.zeros_like(carry)

        xv = zf_ref[...] + b_ref[...]
        lf = jnp.minimum(xv, 0.0) - jnp.log(1.0 + jnp.exp(-jnp.abs(xv)))
        r = lax.broadcasted_iota(jnp.int32, (tb, tb), 0)
        c = lax.broadcasted_iota(jnp.int32, (tb, tb), 1)
        tri = (r >= c).astype(BF16)
        cs = _tri_dot(tri, lf) + carry[...]
        c_ref[...] = cs
        carry[...] = cs[tb - 1:tb, :]

    blk = pl.BlockSpec((tb, H), lambda i: (i, 0))
    return _call(body, name='fox_cumsum', out_shape=_sds((T, H), F32), grid=(T // tb,),
                 in_specs=[blk, _full((1, H))], out_specs=blk, scratch=[pltpu.VMEM((1, H), F32)],
                 sem=('arbitrary',))(zf, bf)


def _fox_c_bwd(dc, zf, bf):
    T, H = zf.shape
    tb = LANES
    nb = T // tb

    def body(dc_ref, zf_ref, b_ref, dz_ref, db_ref, carry):
        @pl.when(pl.program_id(0) == 0)
        def _():
            carry[...] = jnp.zeros_like(carry)
            db_ref[...] = jnp.zeros_like(db_ref)

        r = lax.broadcasted_iota(jnp.int32, (tb, tb), 0)
        c = lax.broadcasted_iota(jnp.int32, (tb, tb), 1)
        tri = (c >= r).astype(BF16)
        dlf = _tri_dot(tri, dc_ref[...]) + carry[...]
        carry[...] = dlf[0:1, :]
        xv = zf_ref[...] + b_ref[...]
        dz = dlf * _sigmoid(-xv)
        dz_ref[...] = dz
        db_ref[...] += jnp.sum(dz, axis=0, keepdims=True)

    blk = pl.BlockSpec((tb, H), lambda i: (nb - 1 - i, 0))
    return _call(body, name='fox_cumsum_bwd', out_shape=(_sds((T, H), F32), _sds((1, H), F32)), grid=(nb,),
                 in_specs=[blk, blk, _full((1, H))], out_specs=(blk, _full((1, H))),
                 scratch=[pltpu.VMEM((1, H), F32)], sem=('arbitrary',))(dc, zf, bf)


def _head_cols(c_ref, ct_ref, h, T):
    lane = lax.broadcasted_iota(jnp.int32, (T, FOX_HEADS), 1)
    cq = jnp.sum(jnp.where(lane == h, c_ref[...], 0.0), axis=1, keepdims=True)
    sub = lax.broadcasted_iota(jnp.int32, (FOX_HEADS, T), 0)
    ck = jnp.sum(jnp.where(sub == h, ct_ref[...], 0.0), axis=0, keepdims=True)
    return cq, ck


def _fox_scores(qs, ks, cq, ck, q0, ke, scale):
    tq = ke - q0
    s = lax.dot_general(qs, ks, _DN['nt'], preferred_element_type=F32) * scale
    s = s + (cq[q0:ke, :] - ck[:, 0:ke])
    r = lax.broadcasted_iota(jnp.int32, (tq, ke), 0) + q0
    c = lax.broadcasted_iota(jnp.int32, (tq, ke), 1)
    return jnp.where(r >= c, s, -jnp.inf)


def _fox_fwd(z, c, ct):
    T = z.shape[0]
    D = z.shape[1] // 6
    dh = D // FOX_HEADS
    hp = LANES // dh
    ncb = D // LANES
    tq = _pick(T, (256, 128))
    scale = dh ** -0.5

    def body(q_ref, k_ref, v_ref, c_ref, ct_ref, o_ref, lse_ref):
        j = pl.program_id(0)
        lse_ref[...] = jnp.zeros_like(lse_ref)
        for hh in range(hp):
            cq, ck = _head_cols(c_ref, ct_ref, j * hp + hh, T)
            hs = slice(hh * dh, (hh + 1) * dh)
            for q0 in range(0, T, tq):
                ke = q0 + tq
                s = _fox_scores(q_ref[q0:ke, hs], k_ref[0:ke, hs], cq, ck, q0, ke, scale)
                m = jnp.max(s, axis=-1, keepdims=True)
                p = jnp.exp(s - m)
                l = jnp.sum(p, axis=-1, keepdims=True)
                o = jnp.dot(p.astype(BF16), v_ref[0:ke, hs], preferred_element_type=F32)
                o_ref[q0:ke, hs] = (o * (1.0 / l)).astype(o_ref.dtype)
                lse_ref[q0:ke, hh:hh + 1] = m + jnp.log(l)

    blk = lambda cb0: pl.BlockSpec((T, LANES), lambda j: (0, cb0 + j))
    return _call(body, name='fox_fwd', out_shape=(_sds((T, D), BF16), _sds((ncb, T, LANES), F32)), grid=(ncb,),
                 in_specs=[blk(3 * ncb), blk(4 * ncb), blk(5 * ncb), _full((T, FOX_HEADS)), _full((FOX_HEADS, T))],
                 out_specs=(blk(0), pl.BlockSpec((None, T, LANES), lambda j: (j, 0, 0))),
                 sem=('parallel',))(z, z, z, c, ct)


def _fox_bwd(z, c, ct, o, lse, do):
    T = z.shape[0]
    D = o.shape[1]
    dh = D // FOX_HEADS
    hp = LANES // dh
    ncb = D // LANES
    tq = _pick(T, (256, 128))
    scale = dh ** -0.5

    def body(q_ref, k_ref, v_ref, c_ref, ct_ref, o_ref, lse_ref, do_ref, dq_ref, dk_ref, dv_ref, dcq_ref, dck_ref,
             dk_acc, dv_acc):
        j = pl.program_id(0)
        dk_acc[...] = jnp.zeros_like(dk_acc)
        dv_acc[...] = jnp.zeros_like(dv_acc)
        dcq_ref[...] = jnp.zeros_like(dcq_ref)
        dck_ref[...] = jnp.zeros_like(dck_ref)
        for hh in range(hp):
            cq, ck = _head_cols(c_ref, ct_ref, j * hp + hh, T)
            hs = slice(hh * dh, (hh + 1) * dh)
            for q0 in range(0, T, tq):
                ke = q0 + tq
                qs, ks, vs = q_ref[q0:ke, hs], k_ref[0:ke, hs], v_ref[0:ke, hs]
                dos = do_ref[q0:ke, hs]
                s = _fox_scores(qs, ks, cq, ck, q0, ke, scale)
                p = jnp.exp(s - lse_ref[q0:ke, hh:hh + 1])
                dp = lax.dot_general(dos, vs, _DN['nt'], preferred_element_type=F32)
                delta = jnp.sum(dos.astype(F32) * o_ref[q0:ke, hs].astype(F32), axis=-1, keepdims=True)
                ds = p * (dp - delta)
                dsb = ds.astype(BF16)
                dq = jnp.dot(dsb, ks, preferred_element_type=F32) * scale
                dq_ref[q0:ke, hs] = dq.astype(dq_ref.dtype)
                dk_acc[0:ke, hs] += lax.dot_general(dsb, qs, _DN['tn'], preferred_element_type=F32) * scale
                dv_acc[0:ke, hs] += lax.dot_general(p.astype(BF16), dos, _DN['tn'], preferred_element_type=F32)
                dcq_ref[q0:ke, hh:hh + 1] = jnp.sum(ds, axis=-1, keepdims=True)
                dck_ref[hh:hh + 1, 0:ke] -= jnp.sum(ds, axis=0, keepdims=True)
        dk_ref[...] = dk_acc[...].astype(dk_ref.dtype)
        dv_ref[...] = dv_acc[...].astype(dv_ref.dtype)

    blk = lambda cb0: pl.BlockSpec((T, LANES), lambda j: (0, cb0 + j))
    pair = pl.BlockSpec((None, T, LANES), lambda j: (j, 0, 0))
    grad = _sds((T, D), BF16)
    return _call(body, name='fox_bwd',
                 out_shape=(grad, grad, grad, _sds((ncb, T, LANES), F32), _sds((ncb, 8, T), F32)), grid=(ncb,),
                 in_specs=[blk(3 * ncb), blk(4 * ncb), blk(5 * ncb), _full((T, FOX_HEADS)), _full((FOX_HEADS, T)), blk(0),
                           pair, blk(0)],
                 out_specs=(blk(0), blk(0), blk(0), pair, pl.BlockSpec((None, 8, T), lambda j: (j, 0, 0))),
                 scratch=[pltpu.VMEM((T, LANES), F32), pltpu.VMEM((T, LANES), F32)],
                 sem=('parallel',))(z, z, z, c, ct, o, lse, do)


def _ln_parts(u1, gain, bias):
    mu = jnp.mean(u1, axis=-1, keepdims=True)
    xc = u1 - mu
    rstd = lax.rsqrt(jnp.mean(xc * xc, axis=-1, keepdims=True) + EPS)
    xhat = xc * rstd
    return xhat, rstd, xhat * gain + bias


def _mixc_fwd(z, w, cb, gain, bias):
    T = z.shape[0]
    D = w.shape[1]
    tb = _pick(T, (256, 128))
    H = HALO_C

    def body(cv, cg, cvp, cgp, w_ref, cb_ref, g_ref, b_ref, o_ref, u1_ref, ext):
        first = pl.program_id(0) == 0
        ext[0:H, :] = jnp.where(first, 0.0, cvp[...].astype(F32) * _sigmoid(cgp[...].astype(F32)))
        ext[H:H + tb, :] = cv[...].astype(F32) * _sigmoid(cg[...].astype(F32))
        u1 = jnp.zeros((tb, D), F32)
        for k in range(CONV_C_W):
            off = H - (CONV_C_W - 1) + k
            u1 = u1 + ext[off:off + tb, :] * w_ref[k:k + 1, :]
        u1 = u1 + cb_ref[...]
        u1_ref[...] = u1
        _, _, u2 = _ln_parts(u1, g_ref[...], b_ref[...])
        o_ref[...] = (u2 * _sigmoid(u2)).astype(o_ref.dtype)

    vec = _full((1, D))
    row = _col(tb, D, 0)
    return _call(body, name='mixc_fwd', out_shape=(_sds((T, D), BF16), _sds((T, D), F32)), grid=(T // tb,),
                 in_specs=[_col(tb, D, 0), _col(tb, D, 1), _prev(tb, H, D, 0), _prev(tb, H, D, 1), _full((32, D)), vec,
                           vec, vec],
                 out_specs=(row, row), scratch=[pltpu.VMEM((H + tb, D), F32)],
                 sem=('parallel',))(z, z, z, z, w, cb, gain, bias)


def _mixc_bwd(z, w, gain, bias, u1, du3):
    T = z.shape[0]
    D = w.shape[1]
    tb = _pick(T, (256, 128))
    H = HALO_C
    nb = T // tb

    def du1_of(u1v, du3v, gv, bv):
        xhat, rstd, u2 = _ln_parts(u1v, gv, bv)
        sg = _sigmoid(u2)
        du2 = du3v * (sg * (1.0 + u2 * (1.0 - sg)))
        dxh = du2 * gv
        m1 = jnp.mean(dxh, axis=-1, keepdims=True)
        m2 = jnp.mean(dxh * xhat, axis=-1, keepdims=True)
        return rstd * (dxh - m1 - xhat * m2), du2, xhat

    def body(cv, cg, cvp, cgp, w_ref, g_ref, b_ref, u1_ref, u1n, du3_ref, du3n, dz_ref, dw_ref, dcb_ref, dg_ref,
             db_ref, uext, dext):
        i = pl.program_id(0)
        gv, bv = g_ref[...], b_ref[...]
        c_val = cv[...].astype(F32)
        sg = _sigmoid(cg[...].astype(F32))
        uext[0:H, :] = jnp.where(i == 0, 0.0, cvp[...].astype(F32) * _sigmoid(cgp[...].astype(F32)))
        uext[H:H + tb, :] = c_val * sg
        du1, du2, xhat = du1_of(u1_ref[...], du3_ref[...].astype(F32), gv, bv)
        du1n, _, _ = du1_of(u1n[...], du3n[...].astype(F32), gv, bv)
        dext[0:tb, :] = du1
        dext[tb:tb + H, :] = jnp.where(i == nb - 1, 0.0, du1n)

        @pl.when(i == 0)
        def _():
            dw_ref[...] = jnp.zeros_like(dw_ref)
            dcb_ref[...] = jnp.zeros_like(dcb_ref)
            dg_ref[...] = jnp.zeros_like(dg_ref)
            db_ref[...] = jnp.zeros_like(db_ref)

        dcb_ref[...] += jnp.sum(du1, axis=0, keepdims=True)
        dg_ref[...] += jnp.sum(du2 * xhat, axis=0, keepdims=True)
        db_ref[...] += jnp.sum(du2, axis=0, keepdims=True)
        du0 = jnp.zeros((tb, D), F32)
        for k in range(CONV_C_W):
            off = H - (CONV_C_W - 1) + k
            du0 = du0 + dext[CONV_C_W - 1 - k:CONV_C_W - 1 - k + tb, :] * w_ref[k:k + 1, :]
            dw_ref[k:k + 1, :] += jnp.sum(du1 * uext[off:off + tb, :], axis=0, keepdims=True)
        dz_ref[:, 0:D] = (du0 * sg).astype(dz_ref.dtype)
        dz_ref[:, D:2 * D] = (du0 * c_val * sg * (1.0 - sg)).astype(dz_ref.dtype)

    vec = _full((1, D))
    row = _col(tb, D, 0)
    nxt = _next(tb, H, D, 0, T)
    return _call(body, name='mixc_bwd',
                 out_shape=(_sds((T, 2 * D), BF16), _sds((32, D), F32), _sds((1, D), F32), _sds((1, D), F32),
                            _sds((1, D), F32)), grid=(nb,),
                 in_specs=[_col(tb, D, 0), _col(tb, D, 1), _prev(tb, H, D, 0), _prev(tb, H, D, 1), _full((32, D)), vec,
                           vec, row, nxt, row, nxt],
                 out_specs=(_col(tb, 2 * D, 0), _full((32, D)), vec, vec, vec),
                 scratch=[pltpu.VMEM((H + tb, D), F32), pltpu.VMEM((tb + H, D), F32)],
                 sem=('arbitrary',))(z, z, z, z, w, gain, bias, u1, u1, du3, du3)


def _gate_fwd(z, bg, ya, yb, yc):
    T = z.shape[0]
    D = ya.shape[1]
    tb = _pick(T, (256, 128))

    def body(ga, gb, gc, bg_ref, ya_ref, yb_ref, yc_ref, o_ref):
        acc = jnp.zeros((tb, D), F32)
        for n, (g, y) in enumerate(((ga, ya_ref), (gb, yb_ref), (gc, yc_ref))):
            acc = acc + _sigmoid(g[...].astype(F32) + bg_ref[:, n * D:(n + 1) * D]) * y[...].astype(F32)
        o_ref[...] = acc.astype(o_ref.dtype)

    row = _col(tb, D, 0)
    return _call(body, name='gate_fwd', out_shape=_sds((T, D), BF16), grid=(T // tb,),
                 in_specs=[_col(tb, D, 2), _col(tb, D, 3), _col(tb, D, 4), _full((1, 3 * D)), row, row, row],
                 out_specs=row, sem=('parallel',))(z, z, z, bg, ya, yb, yc)


def _gate_bwd(z, bg, ya, yb, yc, dm):
    T = z.shape[0]
    D = ya.shape[1]
    tb = _pick(T, (256, 128))

    def body(ga, gb, gc, bg_ref, ya_ref, yb_ref, yc_ref, dm_ref, dya, dyb, dyc, dg_ref, dbg_ref):
        @pl.when(pl.program_id(0) == 0)
        def _():
            dbg_ref[...] = jnp.zeros_like(dbg_ref)

        dmv = dm_ref[...]
        for n, (g, y, dy) in enumerate(((ga, ya_ref, dya), (gb, yb_ref, dyb), (gc, yc_ref, dyc))):
            cols = slice(n * D, (n + 1) * D)
            sg = _sigmoid(g[...].astype(F32) + bg_ref[:, cols])
            dy[...] = (dmv * sg).astype(dy.dtype)
            dg = dmv * y[...].astype(F32) * sg * (1.0 - sg)
            dg_ref[:, cols] = dg.astype(dg_ref.dtype)
            dbg_ref[:, cols] += jnp.sum(dg, axis=0, keepdims=True)

    row = _col(tb, D, 0)
    act = _sds((T, D), BF16)
    return _call(body, name='gate_bwd', out_shape=(act, act, act, _sds((T, 3 * D), BF16), _sds((1, 3 * D), F32)),
                 grid=(T // tb,),
                 in_specs=[_col(tb, D, 2), _col(tb, D, 3), _col(tb, D, 4), _full((1, 3 * D)), row, row, row, row],
                 out_specs=(row, row, row, _col(tb, 3 * D, 0), _full((1, 3 * D))),
                 sem=('arbitrary',))(z, z, z, bg, ya, yb, yc, dm)


def _xattn_probs(qs, ks, scale):
    s = lax.dot_general(qs, ks, _DN['nt'], preferred_element_type=F32) * scale
    p = jnp.exp(s - jnp.max(s, axis=-1, keepdims=True))
    return p * (1.0 / jnp.sum(p, axis=-1, keepdims=True))


def _xattn_fwd(q, kv):
    T, DX = q.shape
    M = kv.shape[0]
    dh = DX // X_HEADS
    tb = _pick(T, (512, 256, 128))
    scale = dh ** -0.5

    def body(q_ref, kv_ref, o_ref):
        for h in range(X_HEADS):
            hs = slice(h * dh, (h + 1) * dh)
            p = _xattn_probs(q_ref[:, hs], kv_ref[:, hs], scale)
            o_ref[:, hs] = jnp.dot(p.astype(BF16), kv_ref[:, DX + h * dh:DX + (h + 1) * dh],
                                   preferred_element_type=F32).astype(o_ref.dtype)

    row = _col(tb, DX, 0)
    return _call(body, name='xattn_fwd', out_shape=_sds((T, DX), BF16), grid=(T // tb,),
                 in_specs=[row, _full((M, 2 * DX))], out_specs=row, sem=('parallel',))(q, kv)


def _xattn_bwd(q, kv, do):
    T, DX = q.shape
    M = kv.shape[0]
    dh = DX // X_HEADS
    tb = _pick(T, (512, 256, 128))
    scale = dh ** -0.5

    def body(q_ref, kv_ref, do_ref, dq_ref, dkv_ref):
        @pl.when(pl.program_id(0) == 0)
        def _():
            dkv_ref[...] = jnp.zeros_like(dkv_ref)

        for h in range(X_HEADS):
            hs = slice(h * dh, (h + 1) * dh)
            vs_cols = slice(DX + h * dh, DX + (h + 1) * dh)
            qs, ks, vs = q_ref[:, hs], kv_ref[:, hs], kv_ref[:, vs_cols]
            dos = do_ref[:, hs].astype(BF16)
            p = _xattn_probs(qs, ks, scale)
            dp = lax.dot_general(dos, vs, _DN['nt'], preferred_element_type=F32)
            ds = p * (dp - jnp.sum(p * dp, axis=-1, keepdims=True))
            dsb = ds.astype(BF16)
            dq_ref[:, hs] = (jnp.dot(dsb, ks, preferred_element_type=F32) * scale).astype(dq_ref.dtype)
            dkv_ref[:, hs] += lax.dot_general(dsb, qs, _DN['tn'], preferred_element_type=F32) * scale
            dkv_ref[:, vs_cols] += lax.dot_general(p.astype(BF16), dos, _DN['tn'], preferred_element_type=F32)

    row = _col(tb, DX, 0)
    return _call(body, name='xattn_bwd', out_shape=(_sds((T, DX), BF16), _sds((M, 2 * DX), F32)), grid=(T // tb,),
                 in_specs=[row, _full((M, 2 * DX)), row], out_specs=(row, _full((M, 2 * DX))),
                 sem=('arbitrary',))(q, kv, do)


def _swiglu_fwd(gu):
    T, F2 = gu.shape
    F = F2 // 2
    tb = _pick(T, (256, 128))

    def body(g_ref, u_ref, o_ref):
        g = g_ref[...].astype(F32)
        o_ref[...] = (g * _sigmoid(g) * u_ref[...].astype(F32)).astype(o_ref.dtype)

    return _call(body, name='swiglu_fwd', out_shape=_sds((T, F), BF16), grid=(T // tb,),
                 in_specs=[_col(tb, F, 0), _col(tb, F, 1)], out_specs=_col(tb, F, 0), sem=('parallel',))(gu, gu)


def _swiglu_bwd(gu, da):
    T, F2 = gu.shape
    F = F2 // 2
    tb = _pick(T, (256, 128))

    def body(g_ref, u_ref, da_ref, o_ref):
        g = g_ref[...].astype(F32)
        u = u_ref[...].astype(F32)
        dav = da_ref[...].astype(F32)
        sg = _sigmoid(g)
        o_ref[:, 0:F] = (dav * u * (sg * (1.0 + g * (1.0 - sg)))).astype(o_ref.dtype)
        o_ref[:, F:F2] = (dav * g * sg).astype(o_ref.dtype)

    return _call(body, name='swiglu_bwd', out_shape=_sds((T, F2), BF16), grid=(T // tb,),
                 in_specs=[_col(tb, F, 0), _col(tb, F, 1), _col(tb, F, 0)], out_specs=_col(tb, F2, 0),
                 sem=('parallel',))(gu, gu, da)


def _place():
    x, y, c = lax.axis_index('x'), lax.axis_index('y'), lax.axis_index('c')
    return x, y, c


def _allgather(arrs, lead, name):
    n = len(arrs)

    def out_shape(a, ld):
        return (N_DEV,) + a.shape if ld else (a.shape[0], N_DEV) + a.shape[1:]

    def body(*refs):
        ins, outs = refs[:n], refs[n:2 * n]
        send_sems, recv_sems, local_sems = refs[2 * n:]
        x, y, c = _place()
        me, sibling = (x, y, c), (x, y, 1 - c)
        chips = [(1 - x, y), (x, 1 - y), (1 - x, 1 - y)]

        def slot(a, dev):
            idx = 4 * dev[0] + 2 * dev[1] + dev[2]
            return outs[a].at[idx] if lead[a] else outs[a].at[:, idx]

        def copy(a, k, block, to, src=None):
            return pltpu.make_async_remote_copy(
                src_ref=slot(a, block) if src is None else src, dst_ref=slot(a, block),
                send_sem=send_sems.at[a * 7 + k], recv_sem=recv_sems.at[a * 7 + k], device_id=to, device_id_type=MESH)

        mine = [pltpu.make_async_copy(ins[a], slot(a, me), local_sems.at[a]) for a in range(n)]
        for cp in mine:
            cp.start()
        first = []
        for a in range(n):
            first.append(copy(a, 0, me, sibling, src=ins[a]))
            first += [copy(a, 1 + j, me, (*chip, c), src=ins[a]) for j, chip in enumerate(chips)]
        for cp in first:
            cp.start()
        passed = []
        for a in range(n):
            for j, chip in enumerate(chips):
                copy(a, 1 + j, (*chip, c), me).wait_recv()
                fwd = copy(a, 4 + j, (*chip, c), sibling)
                fwd.start()
                passed.append(fwd)
        for a in range(n):
            copy(a, 0, sibling, me).wait_recv()
            for j, chip in enumerate(chips):
                copy(a, 4 + j, (*chip, 1 - c), me).wait_recv()
        for cp in first + passed:
            cp.wait_send()
        for cp in mine:
            cp.wait()

    any_spec = pl.BlockSpec(memory_space=pl.ANY)
    return _call(body, name=name, out_shape=tuple(_sds(out_shape(a, ld), a.dtype) for a, ld in zip(arrs, lead)),
                 in_specs=[any_spec] * n, out_specs=tuple([any_spec] * n),
                 scratch=[pltpu.SemaphoreType.DMA((7 * n,)), pltpu.SemaphoreType.DMA((7 * n,)),
                          pltpu.SemaphoreType.DMA((n,))])(*arrs)


_PEER_ORDER = (1, 4, 2, 5, 3, 6, 7)
_HBM = pl.BlockSpec(memory_space=pltpu.HBM)
_SEM = pl.BlockSpec(memory_space=pltpu.SEMAPHORE)


def _my_index():
    x, y, c = _place()
    return 4 * x + 2 * y + c


def _remote(src, dst, send_sems, recv_sems, k, dev):
    return pltpu.make_async_remote_copy(src_ref=src, dst_ref=dst, send_sem=send_sems.at[k], recv_sem=recv_sems.at[k],
                                        device_id=dev, device_id_type=MESH)


def _own_part(src, kind, args):
    return src if args is None else src.at[pl.ds(args[0], args[1])]


def _block(land, kind, args, owner):
    if kind == 'rows':
        _, n, stride, d0 = args
        return land.at[pl.ds(pl.multiple_of(stride * owner + d0, ROW_TILE), n)]
    return land.at[owner]


def _chunk(src, args, dest):
    stride, n = args
    return src.at[pl.ds(pl.multiple_of(stride * dest, ROW_TILE), n)]


def _scatter_copies(plan, srcs, lands, send_sems, recv_sems):
    x, y, c = _place()
    me = 4 * x + 2 * y + c
    copies = []
    for r in _PEER_ORDER:
        px, py, pc = x ^ (r >> 2), y ^ ((r >> 1) & 1), c ^ (r & 1)
        for si, li, _, args in plan:
            copies.append(_remote(_chunk(srcs[si], args, 4 * px + 2 * py + pc), lands[li].at[me], send_sems, recv_sems,
                                  r - 1, (px, py, pc)))
    return copies


def _gather_copies(plan, srcs, lands, send_sems, recv_sems, passing):
    x, y, c = _place()
    me = 4 * x + 2 * y + c
    sibling = (x, y, 1 - c)
    chips = [(1 - x, y), (x, 1 - y), (1 - x, 1 - y)]
    copies = []
    if passing:
        for k, chip in enumerate(chips):
            for _, li, kind, args in plan:
                blk = _block(lands[li], kind, args, 4 * chip[0] + 2 * chip[1] + c)
                copies.append((k, _remote(blk, blk, send_sems, recv_sems, k, sibling)))
    else:
        for k, dev in enumerate([sibling] + [(*chip, c) for chip in chips]):
            for si, li, kind, args in plan:
                dst = _block(lands[li], kind, args, me)
                src = dst if srcs is None else _own_part(srcs[si], kind, args)
                copies.append((k, _remote(src, dst, send_sems, recv_sems, k, dev)))
    return copies


def _sem_call(name, bufs, sems_in, new_sems, after, action):
    nb, ni, nn = len(bufs), len(sems_in), len(new_sems)

    def body(*refs):
        action(refs[:nb], refs[nb:nb + ni], refs[-1 - nb - nn:-1 - nb])
        refs[-1][...] = jnp.zeros_like(refs[-1])

    bufs = [pltpu.with_memory_space_constraint(v, pltpu.HBM) for v in bufs]
    res = pl.pallas_call(
        body, name=name,
        out_shape=(*[pltpu.SemaphoreType.DMA((n,)) for n in new_sems], *[pltpu.HBM(v.shape, v.dtype) for v in bufs],
                   _sds((8, LANES), F32)),
        in_specs=[_HBM] * nb + [_SEM] * ni + [pl.BlockSpec(memory_space=pl.ANY)] * len(after),
        out_specs=(*[_SEM] * nn, *[_HBM] * nb, pl.BlockSpec(memory_space=pltpu.VMEM)),
        input_output_aliases={i: nn + i for i in range(nb)},
        compiler_params=pltpu.CompilerParams(has_side_effects=pltpu.SideEffectType.DATAFLOW_SIDE_EFFECTING),
        interpret=False)(*bufs, *sems_in, *after)
    return list(res[:nn]), list(res[nn:nn + nb]), res[-1]


def _place_own(srcs, plan, lands_like, name):
    ns, nl = len(srcs), len(lands_like)

    def body(*refs):
        src_r, land_r, sems = refs[:ns], refs[ns:ns + nl], refs[-1]
        me = _my_index()
        copies = []
        for i, (si, li, kind, args) in enumerate(plan):
            if kind == 'chunk':
                src, dst = _chunk(src_r[si], args, me), land_r[li].at[me]
            else:
                src, dst = _own_part(src_r[si], kind, args), _block(land_r[li], kind, args, me)
            copies.append(pltpu.make_async_copy(src, dst, sems.at[i]))
        for cp in copies:
            cp.start()
        for cp in copies:
            cp.wait()

    anywhere = pl.BlockSpec(memory_space=pl.ANY)
    return list(_call(body, name=name, out_shape=tuple(lands_like), in_specs=[anywhere] * ns,
                      out_specs=tuple([anywhere] * nl), scratch=[pltpu.SemaphoreType.DMA((len(plan),))])(*srcs))


def _scatter_start(srcs, lands, plan, name):
    ns = len(srcs)

    def action(bufs, _, new):
        for cp in _scatter_copies(plan, bufs[:ns], bufs[ns:], new[0], new[1]):
            cp.start()

    sems, bufs, token = _sem_call(name, [*srcs, *lands], [], [N_DEV - 1, N_DEV - 1], [], action)
    return sems, bufs[:ns], bufs[ns:], token


def _scatter_wait(sems, srcs, lands, plan, after, name):
    ns = len(srcs)

    def action(bufs, old, _):
        copies = _scatter_copies(plan, bufs[:ns], bufs[ns:], old[0], old[1])
        for cp in copies:
            cp.wait_send()
        for cp in copies:
            cp.wait_recv()

    return _sem_call(name, [*srcs, *lands], sems, [], after, action)[1][ns:]


def _gather_start(srcs, lands, plan, name):
    ns = len(srcs)

    def action(bufs, _, new):
        for _, cp in _gather_copies(plan, bufs[:ns], bufs[ns:], new[0], new[1], False):
            cp.start()

    sems, bufs, token = _sem_call(name, [*srcs, *lands], [], [4, 4], [], action)
    return sems, bufs[:ns], bufs[ns:], token


def _gather_pass(recv_sems, lands, plan, after, name):
    def action(bufs, old, new):
        for _, cp in _gather_copies(plan, None, bufs, old[0], old[0], False):
            cp.wait_recv()
        for _, cp in _gather_copies(plan, None, bufs, new[0], new[1], True):
            cp.start()

    return _sem_call(name, lands, [recv_sems], [3, 3], after, action)


def _gather_wait(send_sems, pass_sems, srcs, lands, plan, after, name):
    ns = len(srcs)

    def action(bufs, old, _):
        for _, cp in _gather_copies(plan, bufs[:ns], bufs[ns:], old[0], old[0], False):
            cp.wait_send()
        passed = _gather_copies(plan, None, bufs[ns:], old[1], old[2], True)
        for _, cp in passed:
            cp.wait_send()
        for _, cp in passed:
            cp.wait_recv()

    return _sem_call(name, [*srcs, *lands], [send_sems, *pass_sems], [], after, action)[1][ns:]


def _merge_edges(wt, edges, tiles_per_dev):
    D = wt.shape[1]

    def body(w_ref, e_ref, o_ref):
        s = pl.program_id(0)
        o_ref[...] = jnp.where(s == 0, e_ref[...], w_ref[...] + e_ref[...])

    tile = pl.BlockSpec((ROW_TILE, D), lambda s: (s * tiles_per_dev, 0))
    return _call(body, name='merge_edges', out_shape=_sds(wt.shape, wt.dtype), grid=(N_DEV,),
                 in_specs=[tile, pl.BlockSpec((None, ROW_TILE, D), lambda s: (s, 0, 0))], out_specs=tile,
                 sem=('arbitrary',), aliases={0: 0})(wt, edges)


def _adam_update(g, w, m, v):
    c1 = 1.0 - ADAM_B1 ** ADAM_STEP
    c2 = 1.0 - ADAM_B2 ** ADAM_STEP
    mn = ADAM_B1 * m + (1.0 - ADAM_B1) * g
    vn = ADAM_B2 * v + (1.0 - ADAM_B2) * (g * g)
    m_hat = mn / c1
    v_hat = vn / c2
    return -ADAM_LR * (m_hat / (jnp.sqrt(v_hat) + ADAM_EPS) + ADAM_WD * w), mn, vn


def _adamw(parts, w, m, v, name):
    n_parts = parts.shape[0]
    R, C = w.shape
    tb = _pick(R, (128, 64, 32, 16, 8))

    def body(p_ref, w_ref, m_ref, v_ref, g_out, d_out, m_out, v_out):
        g = p_ref[0].astype(F32)
        for s in range(1, n_parts):
            g = g + p_ref[s].astype(F32)
        g_out[...] = g
        d_out[...], m_out[...], v_out[...] = _adam_update(g, w_ref[...], m_ref[...], v_ref[...])

    row = pl.BlockSpec((tb, C), lambda i: (i, 0))
    o = _sds((R, C), F32)
    return _call(body, name=name, out_shape=(o, o, o, o), grid=(R // tb,),
                 in_specs=[pl.BlockSpec((n_parts, tb, C), lambda i: (0, i, 0)), row, row, row],
                 out_specs=(row, row, row, row), sem=('parallel',))(parts, w, m, v)


def _adamw_layer(parts, w, m, v, layer, prev, shift, name):
    L, n, C = w.shape
    n_p = parts.shape[1]
    cb = C if N_DEV * n_p * C * parts.dtype.itemsize <= (6 << 20) else 2 * LANES
    n_prev = 0 if prev is None else 4

    def body(sh_ref, p_ref, w_ref, m_ref, v_ref, *rest):
        g_out, d_out, m_out, v_out = rest[n_prev:n_prev + 4]
        g = p_ref[0].astype(F32)
        for s in range(1, N_DEV):
            g = g + p_ref[s].astype(F32)
        if n_p != n:
            rolled = rest[-1]
            rolled[...] = pltpu.roll(g, n_p - sh_ref[0], 0)
            g = rolled[0:n, :]
        g_out[...] = g
        d_out[...], m_out[...], v_out[...] = _adam_update(g, w_ref[...], m_ref[...], v_ref[...])

    lay = pl.BlockSpec((None, n, cb), lambda j, sh: (layer, 0, j))
    stack = _sds((L, n, C), F32)
    grid_spec = pltpu.PrefetchScalarGridSpec(
        num_scalar_prefetch=1, grid=(C // cb,),
        in_specs=[pl.BlockSpec((N_DEV, n_p, cb), lambda j, sh: (0, 0, j)), lay, lay, lay]
        + [pl.BlockSpec(memory_space=pl.ANY)] * n_prev,
        out_specs=(lay, lay, lay, lay), scratch_shapes=[pltpu.VMEM((n_p, cb), F32)] if n_p != n else [])
    return pl.pallas_call(body, name=name, grid_spec=grid_spec, out_shape=(stack, stack, stack, stack),
                          input_output_aliases={5 + k: k for k in range(n_prev)},
                          compiler_params=pltpu.CompilerParams(dimension_semantics=('parallel',),
                                                               vmem_limit_bytes=VMEM_LIMIT),
                          interpret=False)(shift, parts, w, m, v, *(prev or ()))


def kernel(x, mem, mix_norm, w_in, b_gate, b_forget, conv_a, w_out_a, w_out_b, conv_c, conv_c_bias, ln_c_gain, ln_c_bias, w_out_c, w_o, xattn_norm, mem_norm, w_xq, w_xkv, w_xo, ffn_norm, w_gate_up, w_down, final_norm, loss_target, m_mix_norm, m_w_in, m_b_gate, m_b_forget, m_conv_a, m_w_out_a, m_w_out_b, m_conv_c, m_conv_c_bias, m_ln_c_gain, m_ln_c_bias, m_w_out_c, m_w_o, m_xattn_norm, m_mem_norm, m_w_xq, m_w_xkv, m_w_xo, m_ffn_norm, m_w_gate_up, m_w_down, m_final_norm, v_mix_norm, v_w_in, v_b_gate, v_b_forget, v_conv_a, v_w_out_a, v_w_out_b, v_conv_c, v_conv_c_bias, v_ln_c_gain, v_ln_c_bias, v_w_out_c, v_w_o, v_xattn_norm, v_mem_norm, v_w_xq, v_w_xkv, v_w_xo, v_ffn_norm, v_w_gate_up, v_w_down, v_final_norm):
    P = dict(zip(ARG_NAMES, (x, mem, mix_norm, w_in, b_gate, b_forget, conv_a, w_out_a, w_out_b, conv_c, conv_c_bias, ln_c_gain, ln_c_bias, w_out_c, w_o, xattn_norm, mem_norm, w_xq, w_xkv, w_xo, ffn_norm, w_gate_up, w_down, final_norm, loss_target, m_mix_norm, m_w_in, m_b_gate, m_b_forget, m_conv_a, m_w_out_a, m_w_out_b, m_conv_c, m_conv_c_bias, m_ln_c_gain, m_ln_c_bias, m_w_out_c, m_w_o, m_xattn_norm, m_mem_norm, m_w_xq, m_w_xkv, m_w_xo, m_ffn_norm, m_w_gate_up, m_w_down, m_final_norm, v_mix_norm, v_w_in, v_b_gate, v_b_forget, v_conv_a, v_w_out_a, v_w_out_b, v_conv_c, v_conv_c_bias, v_ln_c_gain, v_ln_c_bias, v_w_out_c, v_w_o, v_xattn_norm, v_mem_norm, v_w_xq, v_w_xkv, v_w_xo, v_ffn_norm, v_w_gate_up, v_w_down, v_final_norm)))
    return _step(P)


_T_VIEW = ('w_in', 'w_gate_up', 'w_xo')
_BIG = [w for w in SHARDED if w not in ('conv_a', 'conv_c')]
_OTHER = [w for w in _BIG if w != 'w_in']
_CONVS = (('conv_a', CONV_A_W, 8), ('conv_c', CONV_C_W, 32))


def _view(name, a):
    return jnp.transpose(a, (0, 2, 1)) if name in _T_VIEW else a


def _step(P):
    L, D = P['mix_norm'].shape
    T = P['x'].shape[1]
    n_in = P['w_in'].shape[2]
    stride = n_in // ROW_TILE * ROW_TILE
    rem = n_in - stride
    win = stride + ROW_TILE
    assert rem * N_DEV == ROW_TILE
    r_in = n_in * N_DEV
    f_off = 6 * D
    hi_off = f_off + FOX_HEADS
    x_i, y_i, c_i = _place()
    me = 4 * x_i + 2 * y_i + c_i
    V = {}
    for w in _BIG:
        for k in (w, 'm_' + w, 'v_' + w):
            V[k] = _view(w, P[k])
    n_own = {w: V[w].shape[1] for w in _OTHER}
    shift = jnp.reshape(rem * me, (1,)).astype(jnp.int32)
    no_shift = jnp.zeros((1,), jnp.int32)
    row = lambda a, l: a[l][None, :]
    dus = lax.dynamic_update_slice

    def gather_start(l):
        p_in = dus(jnp.zeros((win, D), BF16), V['w_in'][l].astype(BF16), (rem * me, 0))
        srcs = [p_in] + [V[w][l].astype(BF16) for w in _OTHER] + [P[w][l] for w, _, _ in _CONVS]
        lands = [_sds((r_in, D), BF16), _sds((N_DEV, ROW_TILE, D), BF16)]
        plan = [(0, 0, 'rows', (ROW_TILE, stride, stride, ROW_TILE)), (0, 1, 'slot', (0, ROW_TILE))]
        for i, w in enumerate(_OTHER):
            n, s = n_own[w], srcs[1 + i]
            lands.append(_sds((N_DEV * n, s.shape[1]), BF16))
            plan.append((1 + i, 2 + i, 'rows', (0, n, n, 0)))
        for j in range(len(_CONVS)):
            i = 1 + len(_OTHER) + j
            lands.append(_sds((N_DEV,) + srcs[i].shape, F32))
            plan.append((i, i + 1, 'slot', None))
        lands = _place_own(srcs, plan, lands, 'gather_own')
        sems, srcs, lands, token = _gather_start(srcs, lands, plan, f'gather_start_{l}')
        return dict(plan=plan, send=sems[0], recv=sems[1], srcs=srcs, lands=lands, token=token)

    def gather_pass(g, after, l):
        g['pass'], g['lands'], g['token'] = _gather_pass(g['recv'], g['lands'], g['plan'], after, f'gather_pass_{l}')

    def gather_wait(g, after, l):
        lands = _gather_wait(g['send'], g['pass'], g['srcs'], g['lands'], g['plan'], after, f'gather_wait_{l}')
        W = {'w_in': _merge_edges(lands[0], lands[1], stride // ROW_TILE)}
        for i, w in enumerate(_OTHER):
            W[w] = lands[2 + i]
        for j, (w, taps, padded) in enumerate(_CONVS):
            full = jnp.transpose(lands[2 + len(_OTHER) + j], (1, 0, 2)).reshape(taps, D)
            W[w] = jnp.pad(full, ((0, padded - taps), (0, 0)))
        return W

    gathers = [gather_start(l) for l in range(L)]
    gather_pass(gathers[0], [g['token'] for g in gathers], 0)

    mem_x = P['mem'][0]
    mem_n = _rms_fwd(mem_x, P['mem_norm'][None, :], 'rms_mem')
    xs = P['x'][0]
    saved = []
    for l in range(L):
        W = gather_wait(gathers[l], [xs], l)
        s = {'x0': xs, 'W': W}
        wt = W['w_in']
        s['h1'] = _rms_fwd(xs, row(P['mix_norm'], l), 'rms_mix')
        s['z1'] = z1 = _mm(s['h1'], wt, mode='nt', b_rows=(0, f_off), name='mm_in_lo')
        s['z2'] = z2 = _mm(s['h1'], wt, mode='nt', b_rows=(hi_off, r_in - hi_off), name='mm_in_hi')
        s['zf'] = _mm(s['h1'], wt, mode='nt', b_rows=(f_off, FOX_HEADS), out_dtype=F32, name='mm_in_f')
        s['ya_pre'] = _mixa_fwd(z1, W['conv_a'])
        s['ya'] = _mm(s['ya_pre'], W['w_out_a'], mode='nn', name='mm_out_a')
        s['c'] = _fox_c(s['zf'], row(P['b_forget'], l))
        s['ct'] = jnp.transpose(s['c'])
        s['o'], s['lse'] = _fox_fwd(z1, s['c'], s['ct'])
        s['yb'] = _mm(s['o'], W['w_out_b'], mode='nn', name='mm_out_b')
        s['u3'], s['u1'] = _mixc_fwd(z2, W['conv_c'], row(P['conv_c_bias'], l), row(P['ln_c_gain'], l),
                                     row(P['ln_c_bias'], l))
        s['yc'] = _mm(s['u3'], W['w_out_c'], mode='nn', name='mm_out_c')
        s['merged'] = _gate_fwd(z2, row(P['b_gate'], l), s['ya'], s['yb'], s['yc'])
        xs = _mm(s['merged'], W['w_o'], mode='nn', add=xs, out_dtype=F32, name='mm_o')
        s['x1'] = xs
        deps = []
        if l + 1 < L:
            gather_pass(gathers[l + 1], [xs], l + 1)
            deps = [gathers[l + 1]['token']]
        s['h2'] = _rms_fwd(xs, row(P['xattn_norm'], l), 'rms_xattn', deps=deps)
        s['qx'] = _mm(s['h2'], W['w_xq'], mode='nn', name='mm_xq')
        s['kv'] = _mm(mem_n, W['w_xkv'], mode='nn', name='mm_xkv')
        s['ox'] = _xattn_fwd(s['qx'], s['kv'])
        xs = _mm(s['ox'], W['w_xo'], mode='nt', add=xs, out_dtype=F32, name='mm_xo')
        s['x2'] = xs
        s['h3'] = _rms_fwd(xs, row(P['ffn_norm'], l), 'rms_ffn')
        s['gu'] = _mm(s['h3'], W['w_gate_up'], mode='nt', name='mm_gate_up')
        s['act'] = _swiglu_fwd(s['gu'])
        xs = _mm(s['act'], W['w_down'], mode='nn', add=xs, out_dtype=F32, name='mm_down')
        saved.append(s)

    dx, d_final_norm, loss_part = _final(xs, P['final_norm'][None, :], P['loss_target'][0])

    GS = {w: [None] * L for w in SMALL}
    G_conv = {'conv_a': [None] * L, 'conv_c': [None] * L}
    d_mem_n = None
    results = {w: None for w in _BIG}

    def exchange_start(G, l):
        srcs = [G[w] for w in _BIG]
        plan, lands = [], []
        for i, w in enumerate(_BIG):
            step, n = (stride, win) if w == 'w_in' else (n_own[w], n_own[w])
            lands.append(_sds((N_DEV, n, srcs[i].shape[1]), BF16))
            plan.append((i, i, 'chunk', (step, n)))
        lands = _place_own(srcs, plan, lands, 'exchange_own')
        sems, srcs, lands, token = _scatter_start(srcs, lands, plan, f'exchange_start_{l}')
        return (plan, sems, srcs, lands, token)

    def exchange_finish(e, after, l):
        plan, sems, srcs, lands, _ = e
        lands = _scatter_wait(sems, srcs, lands, plan, after, f'exchange_wait_{l}')
        for i, w in enumerate(_BIG):
            results[w] = _adamw_layer(lands[i], V[w], V['m_' + w], V['v_' + w], l, results[w],
                                      shift if w == 'w_in' else no_shift, 'adamw_' + w)

    pending = None
    for l in reversed(range(L)):
        s = saved[l]
        W, z1, z2 = s['W'], s['z1'], s['z2']
        wt = W['w_in']
        G = {}
        d_act = _mm(dx, W['w_down'], mode='nt', name='mmb_down', deps=[pending[0][-1]] if pending else ())
        G['w_down'] = _mm(s['act'], dx, mode='tn', name='mmg_w_down')
        dgu = _swiglu_bwd(s['gu'], d_act)
        dh3 = _mm(dgu, W['w_gate_up'], mode='nn', out_dtype=F32, name='mmb_gate_up')
        G['w_gate_up'] = _mm(dgu, s['h3'], mode='tn', name='mmg_w_gate_up')
        dx, GS['ffn_norm'][l] = _rms_bwd(s['x2'], row(P['ffn_norm'], l), dh3, dx, 'rms_ffn_bwd')
        d_ox = _mm(dx, W['w_xo'], mode='nn', out_dtype=F32, name='mmb_xo')
        G['w_xo'] = _mm(dx, s['ox'], mode='tn', name='mmg_w_xo')
        dqx, dkv = _xattn_bwd(s['qx'], s['kv'], d_ox)
        dh2 = _mm(dqx, W['w_xq'], mode='nt', out_dtype=F32, name='mmb_xq')
        G['w_xq'] = _mm(s['h2'], dqx, mode='tn', name='mmg_w_xq')
        G['w_xkv'] = _mm(mem_n, dkv, mode='tn', name='mmg_w_xkv')
        d_mem_n = _mm(dkv, W['w_xkv'], mode='nt', add=d_mem_n, out_dtype=F32, name='mmb_xkv')
        dx, GS['xattn_norm'][l] = _rms_bwd(s['x1'], row(P['xattn_norm'], l), dh2, dx, 'rms_xattn_bwd')
        dm = _mm(dx, W['w_o'], mode='nt', out_dtype=F32, name='mmb_o')
        G['w_o'] = _mm(s['merged'], dx, mode='tn', name='mmg_w_o')
        dya, dyb, dyc, dz_g, GS['b_gate'][l] = _gate_bwd(z2, row(P['b_gate'], l), s['ya'], s['yb'], s['yc'], dm)
        du3 = _mm(dyc, W['w_out_c'], mode='nt', out_dtype=F32, name='mmb_out_c')
        G['w_out_c'] = _mm(s['u3'], dyc, mode='tn', name='mmg_w_out_c')
        dz_c, G_conv['conv_c'][l], GS['conv_c_bias'][l], GS['ln_c_gain'][l], GS['ln_c_bias'][l] = _mixc_bwd(
            z2, W['conv_c'], row(P['ln_c_gain'], l), row(P['ln_c_bias'], l), s['u1'], du3)
        do = _mm(dyb, W['w_out_b'], mode='nt', name='mmb_out_b')
        G['w_out_b'] = _mm(s['o'], dyb, mode='tn', name='mmg_w_out_b')
        dq, dk, dv, dcq, dck = _fox_bwd(z1, s['c'], s['ct'], s['o'], s['lse'], do)
        hp = FOX_HEADS // dcq.shape[0]
        dc = (jnp.transpose(dcq[:, :, :hp], (1, 0, 2)).reshape(T, FOX_HEADS)
              + jnp.transpose(dck[:, :hp, :], (2, 0, 1)).reshape(T, FOX_HEADS))
        dzf, GS['b_forget'][l] = _fox_c_bwd(dc, s['zf'], row(P['b_forget'], l))
        dya_pre = _mm(dya, W['w_out_a'], mode='nt', out_dtype=F32, name='mmb_out_a')
        G['w_out_a'] = _mm(s['ya_pre'], dya, mode='tn', name='mmg_w_out_a')
        dz_a, G_conv['conv_a'][l] = _mixa_bwd(z1, W['conv_a'], dya_pre)
        segs = [(dzf, f_off), (dz_a, 0), (dq, 3 * D), (dk, 4 * D), (dv, 5 * D), (dz_c, hi_off), (dz_g, hi_off + 2 * D)]
        dh1, g_in = None, r_in
        for dz, off in segs:
            dh1 = _mm(dz, wt, mode='nn', b_rows=(off, dz.shape[1]), add=dh1, out_dtype=F32, name='mmb_in')
            g_in = _mm(dz, s['h1'], mode='tn', out=g_in, o_off=off, name='mmg_w_in')
        G['w_in'] = g_in
        dx, GS['mix_norm'][l] = _rms_bwd(s['x0'], row(P['mix_norm'], l), dh1, dx, 'rms_mix_bwd')
        started = exchange_start(G, l)
        if pending is not None:
            exchange_finish(pending[0], [dx], pending[1])
        pending = (started, l)

    _, d_mem_norm = _rms_bwd(mem_x, P['mem_norm'][None, :], d_mem_n, None, 'rms_mem_bwd', deps=[pending[0][-1]])

    small_parts = {w: jnp.concatenate(GS[w], axis=0) for w in SMALL if w not in ('mem_norm', 'final_norm')}
    small_parts['mem_norm'] = d_mem_norm
    small_parts['final_norm'] = d_final_norm
    conv_parts = [jnp.stack(G_conv[w])[:, :taps] for w, taps, _ in _CONVS]
    sizes = [P[w].size for w in SMALL]
    conv_sizes = [c.size for c in conv_parts]
    n_small = sum(sizes) + sum(conv_sizes) + LANES
    n_rows = -(-n_small // (8 * LANES)) * 8

    def pack(parts):
        flat = jnp.concatenate([p.reshape(-1) for p in parts])
        return jnp.pad(flat, (0, n_rows * LANES - flat.size)).reshape(n_rows, LANES)

    g_small = pack([small_parts[w] for w in SMALL] + conv_parts + [loss_part])
    (all_small,) = _allgather([g_small], [True], 'gather_small')
    sm = _adamw(all_small, pack([P[w] for w in SMALL]), pack([P['m_' + w] for w in SMALL]),
                pack([P['v_' + w] for w in SMALL]), 'adamw_small')
    g_all = sm[0].reshape(-1)
    loss = g_all[sum(sizes) + sum(conv_sizes)]
    final = {}
    off = 0
    for w, n in zip(SMALL, sizes):
        final[w] = [o.reshape(-1)[off:off + n].reshape(P[w].shape) for o in sm]
        off += n
    for (w, taps, _), c, n in zip(_CONVS, conv_parts, conv_sizes):
        n_col = P[w].shape[2]
        g_own = lax.dynamic_slice(g_all[off:off + n].reshape(c.shape), (0, 0, n_col * me), (L, taps, n_col))
        flat = lambda a: a.reshape(L * taps, n_col)
        outs = _adamw(flat(g_own)[None], flat(P[w]), flat(P['m_' + w]), flat(P['v_' + w]), 'adamw_' + w)
        final[w] = [o.reshape(P[w].shape) for o in outs]
        off += n

    exchange_finish(pending[0], [sm[0]], pending[1])
    for w in _BIG:
        final[w] = [_view(w, o) for o in results[w]]

    out = [loss, dx[None]]
    for k in range(4):
        out += [final[w][k] for w in WEIGHTS]
    return tuple(out)
```

```python
import functools

import jax
import jax.numpy as jnp
from jax import lax
from jax.experimental import pallas as pl
from jax.experimental.pallas import tpu as pltpu

F32 = jnp.float32
BF16 = jnp.bfloat16
MESH = pl.DeviceIdType.MESH

N_DEV = 8
EPS = 1e-6
FOX_HEADS = 16
X_HEADS = 4
LANES = 128
ROW_TILE = 16
HALO_A = 16
HALO_C = 32
CONV_A_W = 3
CONV_C_W = 31
VMEM_LIMIT = 56 << 20

ADAM_LR = 0.001
ADAM_B1 = 0.9
ADAM_B2 = 0.999
ADAM_EPS = 1e-08
ADAM_WD = 0.01
ADAM_STEP = 10

WEIGHTS = ['mix_norm', 'w_in', 'b_gate', 'b_forget', 'conv_a', 'w_out_a', 'w_out_b', 'conv_c', 'conv_c_bias',
           'ln_c_gain', 'ln_c_bias', 'w_out_c', 'w_o', 'xattn_norm', 'mem_norm', 'w_xq', 'w_xkv', 'w_xo',
           'ffn_norm', 'w_gate_up', 'w_down', 'final_norm']
SHARDED = ['w_in', 'conv_a', 'w_out_a', 'w_out_b', 'conv_c', 'w_out_c', 'w_o', 'w_xq', 'w_xkv', 'w_xo',
           'w_gate_up', 'w_down']
ROW_SHARDED = ['w_out_a', 'w_out_b', 'w_out_c', 'w_o', 'w_xq', 'w_xkv', 'w_down']
SMALL = [w for w in WEIGHTS if w not in SHARDED]
ARG_NAMES = (['x', 'mem'] + WEIGHTS + ['loss_target'] + ['m_' + w for w in WEIGHTS] + ['v_' + w for w in WEIGHTS])


def _pick(n, cands=(1024, 1408, 512, 256, 128)):
    for c in cands:
        if n % c == 0:
            return c
    return n


def _call(body, *, name, out_shape, grid=(), in_specs=None, out_specs=None, scratch=(), sem=None, aliases=None,
          deps=()):
    params = dict(vmem_limit_bytes=VMEM_LIMIT)
    if sem is not None:
        params['dimension_semantics'] = sem
    n_in, n_dep = len(in_specs), len(deps)

    def body_without_deps(*refs):
        body(*refs[:n_in], *refs[n_in + n_dep:])

    call = pl.pallas_call(body_without_deps, name=name, out_shape=out_shape, grid=grid, scratch_shapes=list(scratch),
                          in_specs=list(in_specs) + [pl.BlockSpec(memory_space=pl.ANY)] * n_dep, out_specs=out_specs,
                          input_output_aliases=aliases or {}, compiler_params=pltpu.CompilerParams(**params),
                          interpret=False)
    return lambda *args: call(*args, *deps)


def _sds(shape, dtype):
    return jax.ShapeDtypeStruct(tuple(shape), dtype)


def _sigmoid(x):
    return 1.0 / (1.0 + jnp.exp(-x))


_DN = {'nn': (((1,), (0,)), ((), ())), 'nt': (((1,), (1,)), ((), ())), 'tn': (((0,), (0,)), ((), ()))}


def _mm(a, b, *, mode, name, out_dtype=BF16, add=None, b_rows=None, out=None, o_off=0, deps=()):
    if mode == 'tn':
        K, M = a.shape
    else:
        M, K = a.shape
    b_off, b_n = (0, b.shape[0]) if b_rows is None else b_rows
    N = b_n if mode == 'nt' else b.shape[1]
    assert (b.shape[1] if mode == 'nt' else b_n) == K
    tm, tn = _pick(M), _pick(N)
    tk = K if K <= 2048 else _pick(K)
    ni, nj, nk = M // tm, N // tn, K // tk
    a_bytes, b_bytes = M * K * a.dtype.itemsize, K * N * b.dtype.itemsize
    n_outer = (b_bytes + nj * a_bytes) < (a_bytes + ni * b_bytes)
    if n_outer:
        grid = (nj, ni, nk)
        ij = lambda g0, g1: (g1, g0)
    else:
        grid = (ni, nj, nk)
        ij = lambda g0, g1: (g0, g1)

    def a_map(g0, g1, k):
        i, _ = ij(g0, g1)
        return (k, i) if mode == 'tn' else (i, k)

    def elems(blk):
        return tuple(pl.Element(s) for s in blk)

    def at(rows, cols):
        return pl.multiple_of(rows, ROW_TILE), pl.multiple_of(cols, LANES)

    a_blk = (tk, tm) if mode == 'tn' else (tm, tk)
    b_blk = (tn, tk) if mode == 'nt' else (tk, tn)
    if b_rows is None:
        def b_map(g0, g1, k):
            _, j = ij(g0, g1)
            return (j, k) if mode == 'nt' else (k, j)
        b_spec = pl.BlockSpec(b_blk, b_map)
    else:
        def b_map(g0, g1, k):
            _, j = ij(g0, g1)
            return at(b_off + j * tn, k * tk) if mode == 'nt' else at(b_off + k * tk, j * tn)
        b_spec = pl.BlockSpec(elems(b_blk), b_map)
    in_specs = [pl.BlockSpec(a_blk, a_map), b_spec]
    args = [a, b]
    has_add = add is not None
    if has_add:
        in_specs.append(pl.BlockSpec((tm, tn), lambda g0, g1, k: ij(g0, g1)))
        args.append(add)
    aliases = {}
    if out is None:
        out_shape = _sds((M, N), out_dtype)
        out_spec = pl.BlockSpec((tm, tn), lambda g0, g1, k: ij(g0, g1))
    else:
        if isinstance(out, int):
            out_shape = _sds((out, N), out_dtype)
        else:
            out_shape = _sds(out.shape, out.dtype)
            in_specs.append(pl.BlockSpec(memory_space=pl.ANY))
            aliases = {len(args): 0}
            args.append(out)

        def o_map(g0, g1, k):
            i, j = ij(g0, g1)
            return at(o_off + i * tm, j * tn)
        out_spec = pl.BlockSpec(elems((tm, tn)), o_map)
    dn = _DN[mode]
    n_in = len(args)

    def body(*refs):
        a_ref, b_ref = refs[0], refs[1]
        add_ref = refs[2] if has_add else None
        o_ref = refs[n_in]
        part = lax.dot_general(a_ref[...].astype(BF16), b_ref[...].astype(BF16), dn, preferred_element_type=F32)

        def finish(r):
            if has_add:
                r = r + add_ref[...]
            o_ref[...] = r.astype(o_ref.dtype)

        if nk == 1:
            finish(part)
        else:
            acc = refs[n_in + 1]
            k = pl.program_id(2)

            @pl.when(k == 0)
            def _():
                acc[...] = part

            @pl.when(k > 0)
            def _():
                acc[...] += part

            @pl.when(k == nk - 1)
            def _():
                finish(acc[...])

    scratch = [pltpu.VMEM((tm, tn), F32)] if nk > 1 else []
    return _call(body, name=name, out_shape=out_shape, grid=grid, in_specs=in_specs, out_specs=out_spec,
                 scratch=scratch, sem=('parallel', 'parallel', 'arbitrary'), aliases=aliases, deps=deps)(*args)


def _rms_fwd(x, g, name, deps=()):
    R, D = x.shape
    tb = _pick(R, (512, 256, 128))

    def body(x_ref, g_ref, o_ref):
        xv = x_ref[...]
        r = lax.rsqrt(jnp.mean(xv * xv, axis=-1, keepdims=True) + EPS)
        o_ref[...] = (xv * r * g_ref[...]).astype(o_ref.dtype)

    row = pl.BlockSpec((tb, D), lambda i: (i, 0))
    vec = pl.BlockSpec((1, D), lambda i: (0, 0))
    return _call(body, name=name, out_shape=_sds((R, D), BF16), grid=(R // tb,), in_specs=[row, vec],
                 out_specs=row, sem=('parallel',), deps=deps)(x, g)


def _rms_bwd(x, g, dh, dres, name, deps=()):
    R, D = x.shape
    tb = _pick(R, (512, 256, 128))
    has_res = dres is not None

    def body(*refs):
        x_ref, g_ref, dh_ref = refs[:3]
        dx_ref, dg_ref = refs[-2:]
        xv = x_ref[...]
        dhv = dh_ref[...].astype(F32)
        r = lax.rsqrt(jnp.mean(xv * xv, axis=-1, keepdims=True) + EPS)
        u = dhv * g_ref[...]
        m = jnp.mean(u * xv, axis=-1, keepdims=True)
        dx = r * u - xv * (r * r * r * m)
        if has_res:
            dx = dx + refs[3][...]
        dx_ref[...] = dx

        @pl.when(pl.program_id(0) == 0)
        def _():
            dg_ref[...] = jnp.zeros_like(dg_ref)

        dg_ref[...] += jnp.sum(dhv * xv * r, axis=0, keepdims=True)

    row = pl.BlockSpec((tb, D), lambda i: (i, 0))
    vec = pl.BlockSpec((1, D), lambda i: (0, 0))
    args = [x, g, dh] + ([dres] if has_res else [])
    return _call(body, name=name, out_shape=(_sds((R, D), F32), _sds((1, D), F32)), grid=(R // tb,),
                 in_specs=[row, vec, row] + ([row] if has_res else []), out_specs=(row, vec),
                 sem=('arbitrary',), deps=deps)(*args)


def _final(x, g, tgt):
    R, D = x.shape
    tb = _pick(R, (512, 256, 128))

    def body(x_ref, g_ref, t_ref, dx_ref, dg_ref, loss_ref):
        xv = x_ref[...]
        gv = g_ref[...]
        r = lax.rsqrt(jnp.mean(xv * xv, axis=-1, keepdims=True) + EPS)
        e = xv * r * gv - t_ref[...]
        row_loss = jnp.mean(e * e, axis=-1, keepdims=True)
        blk_loss = 0.5 * jnp.sum(row_loss, axis=0, keepdims=True)
        dy = e * (1.0 / D)
        u = dy * gv
        m = jnp.mean(u * xv, axis=-1, keepdims=True)
        dx_ref[...] = r * u - xv * (r * r * r * m)

        @pl.when(pl.program_id(0) == 0)
        def _():
            dg_ref[...] = jnp.zeros_like(dg_ref)
            loss_ref[...] = jnp.zeros_like(loss_ref)

        dg_ref[...] += jnp.sum(dy * xv * r, axis=0, keepdims=True)
        loss_ref[...] += jnp.broadcast_to(blk_loss, loss_ref.shape)

    row = pl.BlockSpec((tb, D), lambda i: (i, 0))
    vec = pl.BlockSpec((1, D), lambda i: (0, 0))
    one = pl.BlockSpec((1, LANES), lambda i: (0, 0))
    return _call(body, name='final_loss', out_shape=(_sds((R, D), F32), _sds((1, D), F32), _sds((1, LANES), F32)),
                 grid=(R // tb,), in_specs=[row, vec, row], out_specs=(row, vec, one), sem=('arbitrary',))(x, g, tgt)


def _col(tb, w, cb):
    return pl.BlockSpec((tb, w), lambda i: (i, cb))


def _prev(tb, h, w, cb):
    return pl.BlockSpec((h, w), lambda i: (jnp.maximum(i * (tb // h) - 1, 0), cb))


def _next(tb, h, w, cb, rows):
    return pl.BlockSpec((h, w), lambda i: (jnp.minimum((i + 1) * (tb // h), rows // h - 1), cb))


def _full(shape):
    return pl.BlockSpec(shape, lambda i: (0,) * len(shape))


def _mixa_fwd(z, w):
    T = z.shape[0]
    D = w.shape[1]
    tb = _pick(T, (256, 128))
    H = HALO_A

    def body(ab, ac, au, acp, aup, w_ref, o_ref, ext):
        first = pl.program_id(0) == 0
        ext[0:H, :] = jnp.where(first, 0.0, acp[...].astype(F32) * aup[...].astype(F32))
        ext[H:H + tb, :] = ac[...].astype(F32) * au[...].astype(F32)
        cp = jnp.zeros((tb, D), F32)
        for k in range(CONV_A_W):
            off = H - (CONV_A_W - 1) + k
            cp = cp + ext[off:off + tb, :] * w_ref[k:k + 1, :]
        o_ref[...] = (ab[...].astype(F32) * cp).astype(o_ref.dtype)

    return _call(body, name='mixa_fwd', out_shape=_sds((T, D), BF16), grid=(T // tb,),
                 in_specs=[_col(tb, D, 0), _col(tb, D, 1), _col(tb, D, 2), _prev(tb, H, D, 1), _prev(tb, H, D, 2),
                           _full((8, D))],
                 out_specs=_col(tb, D, 0), scratch=[pltpu.VMEM((H + tb, D), F32)], sem=('parallel',))(z, z, z, z, z, w)


def _mixa_bwd(z, w, dy):
    T = z.shape[0]
    D = w.shape[1]
    tb = _pick(T, (256, 128))
    H = HALO_A
    nb = T // tb

    def body(ab, ac, au, acp, aup, abn, dy_ref, dyn, w_ref, dz_ref, dw_ref, pext, dext):
        i = pl.program_id(0)
        a_b, a_c, a_u = ab[...].astype(F32), ac[...].astype(F32), au[...].astype(F32)
        pext[0:H, :] = jnp.where(i == 0, 0.0, acp[...].astype(F32) * aup[...].astype(F32))
        pext[H:H + tb, :] = a_c * a_u
        dyv = dy_ref[...].astype(F32)
        dcp = dyv * a_b
        dext[0:tb, :] = dcp
        dext[tb:tb + H, :] = jnp.where(i == nb - 1, 0.0, dyn[...].astype(F32) * abn[...].astype(F32))

        @pl.when(i == 0)
        def _():
            dw_ref[...] = jnp.zeros_like(dw_ref)

        cp = jnp.zeros((tb, D), F32)
        dp = jnp.zeros((tb, D), F32)
        for k in range(CONV_A_W):
            off = H - (CONV_A_W - 1) + k
            wk = w_ref[k:k + 1, :]
            pk = pext[off:off + tb, :]
            cp = cp + pk * wk
            dp = dp + dext[CONV_A_W - 1 - k:CONV_A_W - 1 - k + tb, :] * wk
            dw_ref[k:k + 1, :] += jnp.sum(dcp * pk, axis=0, keepdims=True)
        dz_ref[:, 0:D] = (dyv * cp).astype(dz_ref.dtype)
        dz_ref[:, D:2 * D] = (dp * a_u).astype(dz_ref.dtype)
        dz_ref[:, 2 * D:3 * D] = (dp * a_c).astype(dz_ref.dtype)

    return _call(body, name='mixa_bwd', out_shape=(_sds((T, 3 * D), BF16), _sds((8, D), F32)), grid=(nb,),
                 in_specs=[_col(tb, D, 0), _col(tb, D, 1), _col(tb, D, 2), _prev(tb, H, D, 1), _prev(tb, H, D, 2),
                           _next(tb, H, D, 0, T), _col(tb, D, 0), _next(tb, H, D, 0, T), _full((8, D))],
                 out_specs=(_col(tb, 3 * D, 0), _full((8, D))),
                 scratch=[pltpu.VMEM((H + tb, D), F32), pltpu.VMEM((tb + H, D), F32)],
                 sem=('arbitrary',))(z, z, z, z, z, z, dy, dy, w)


def _split3(v):
    hi = v.astype(BF16)
    r1 = v - hi.astype(F32)
    mid = r1.astype(BF16)
    lo = (r1 - mid.astype(F32)).astype(BF16)
    return hi, mid, lo


def _tri_dot(tri, v):
    hi, mid, lo = _split3(v)
    d = lambda p: jnp.dot(tri, p, preferred_element_type=F32)
    return d(hi) + d(mid) + d(lo)


def _fox_c(zf, bf):
    T, H = zf.shape
    tb = LANES

    def body(zf_ref, b_ref, c_ref, carry):
        @pl.when(pl.program_id(0) == 0)
        def _():
            carry[...] = jnp.zeros_like(carry)

        xv = zf_ref[...] + b_ref[...]
        lf = jnp.minimum(xv, 0.0) - jnp.log(1.0 + jnp.exp(-jnp.abs(xv)))
        r = lax.broadcasted_iota(jnp.int32, (tb, tb), 0)
        c = lax.broadcasted_iota(jnp.int32, (tb, tb), 1)
        tri = (r >= c).astype(BF16)
        cs = _tri_dot(tri, lf) + carry[...]
        c_ref[...] = cs
        carry[...] = cs[tb - 1:tb, :]

    blk = pl.BlockSpec((tb, H), lambda i: (i, 0))
    return _call(body, name='fox_cumsum', out_shape=_sds((T, H), F32), grid=(T // tb,),
                 in_specs=[blk, _full((1, H))], out_specs=blk, scratch=[pltpu.VMEM((1, H), F32)],
                 sem=('arbitrary',))(zf, bf)


def _fox_c_bwd(dc, zf, bf):
    T, H = zf.shape
    tb = LANES
    nb = T // tb

    def body(dc_ref, zf_ref, b_ref, dz_ref, db_ref, carry):
        @pl.when(pl.program_id(0) == 0)
        def _():
            carry[...] = jnp.zeros_like(carry)
            db_ref[...] = jnp.zeros_like(db_ref)

        r = lax.broadcasted_iota(jnp.int32, (tb, tb), 0)
        c = lax.broadcasted_iota(jnp.int32, (tb, tb), 1)
        tri = (c >= r).astype(BF16)
        dlf = _tri_dot(tri, dc_ref[...]) + carry[...]
        carry[...] = dlf[0:1, :]
        xv = zf_ref[...] + b_ref[...]
        dz = dlf * _sigmoid(-xv)
        dz_ref[...] = dz
        db_ref[...] += jnp.sum(dz, axis=0, keepdims=True)

    blk = pl.BlockSpec((tb, H), lambda i: (nb - 1 - i, 0))
    return _call(body, name='fox_cumsum_bwd', out_shape=(_sds((T, H), F32), _sds((1, H), F32)), grid=(nb,),
                 in_specs=[blk, blk, _full((1, H))], out_specs=(blk, _full((1, H))),
                 scratch=[pltpu.VMEM((1, H), F32)], sem=('arbitrary',))(dc, zf, bf)


def _head_cols(c_ref, ct_ref, h, T):
    lane = lax.broadcasted_iota(jnp.int32, (T, FOX_HEADS), 1)
    cq = jnp.sum(jnp.where(lane == h, c_ref[...], 0.0), axis=1, keepdims=True)
    sub = lax.broadcasted_iota(jnp.int32, (FOX_HEADS, T), 0)
    ck = jnp.sum(jnp.where(sub == h, ct_ref[...], 0.0), axis=0, keepdims=True)
    return cq, ck


def _fox_scores(qs, ks, cq, ck, q0, ke, scale):
    tq = ke - q0
    s = lax.dot_general(qs, ks, _DN['nt'], preferred_element_type=F32) * scale
    s = s + (cq[q0:ke, :] - ck[:, 0:ke])
    r = lax.broadcasted_iota(jnp.int32, (tq, ke), 0) + q0
    c = lax.broadcasted_iota(jnp.int32, (tq, ke), 1)
    return jnp.where(r >= c, s, -jnp.inf)


def _fox_fwd(z, c, ct):
    T = z.shape[0]
    D = z.shape[1] // 6
    dh = D // FOX_HEADS
    hp = LANES // dh
    ncb = D // LANES
    tq = _pick(T, (256, 128))
    scale = dh ** -0.5

    def body(q_ref, k_ref, v_ref, c_ref, ct_ref, o_ref, lse_ref):
        j = pl.program_id(0)
        lse_ref[...] = jnp.zeros_like(lse_ref)
        for hh in range(hp):
            cq, ck = _head_cols(c_ref, ct_ref, j * hp + hh, T)
            hs = slice(hh * dh, (hh + 1) * dh)
            for q0 in range(0, T, tq):
                ke = q0 + tq
                s = _fox_scores(q_ref[q0:ke, hs], k_ref[0:ke, hs], cq, ck, q0, ke, scale)
                m = jnp.max(s, axis=-1, keepdims=True)
                p = jnp.exp(s - m)
                l = jnp.sum(p, axis=-1, keepdims=True)
                o = jnp.dot(p.astype(BF16), v_ref[0:ke, hs], preferred_element_type=F32)
                o_ref[q0:ke, hs] = (o * (1.0 / l)).astype(o_ref.dtype)
                lse_ref[q0:ke, hh:hh + 1] = m + jnp.log(l)

    blk = lambda cb0: pl.BlockSpec((T, LANES), lambda j: (0, cb0 + j))
    return _call(body, name='fox_fwd', out_shape=(_sds((T, D), BF16), _sds((ncb, T, LANES), F32)), grid=(ncb,),
                 in_specs=[blk(3 * ncb), blk(4 * ncb), blk(5 * ncb), _full((T, FOX_HEADS)), _full((FOX_HEADS, T))],
                 out_specs=(blk(0), pl.BlockSpec((None, T, LANES), lambda j: (j, 0, 0))),
                 sem=('parallel',))(z, z, z, c, ct)


def _fox_bwd(z, c, ct, o, lse, do):
    T = z.shape[0]
    D = o.shape[1]
    dh = D // FOX_HEADS
    hp = LANES // dh
    ncb = D // LANES
    tq = _pick(T, (256, 128))
    scale = dh ** -0.5

    def body(q_ref, k_ref, v_ref, c_ref, ct_ref, o_ref, lse_ref, do_ref, dq_ref, dk_ref, dv_ref, dcq_ref, dck_ref,
             dk_acc, dv_acc):
        j = pl.program_id(0)
        dk_acc[...] = jnp.zeros_like(dk_acc)
        dv_acc[...] = jnp.zeros_like(dv_acc)
        dcq_ref[...] = jnp.zeros_like(dcq_ref)
        dck_ref[...] = jnp.zeros_like(dck_ref)
        for hh in range(hp):
            cq, ck = _head_cols(c_ref, ct_ref, j * hp + hh, T)
            hs = slice(hh * dh, (hh + 1) * dh)
            for q0 in range(0, T, tq):
                ke = q0 + tq
                qs, ks, vs = q_ref[q0:ke, hs], k_ref[0:ke, hs], v_ref[0:ke, hs]
                dos = do_ref[q0:ke, hs]
                s = _fox_scores(qs, ks, cq, ck, q0, ke, scale)
                p = jnp.exp(s - lse_ref[q0:ke, hh:hh + 1])
                dp = lax.dot_general(dos, vs, _DN['nt'], preferred_element_type=F32)
                delta = jnp.sum(dos.astype(F32) * o_ref[q0:ke, hs].astype(F32), axis=-1, keepdims=True)
                ds = p * (dp - delta)
                dsb = ds.astype(BF16)
                dq = jnp.dot(dsb, ks, preferred_element_type=F32) * scale
                dq_ref[q0:ke, hs] = dq.astype(dq_ref.dtype)
                dk_acc[0:ke, hs] += lax.dot_general(dsb, qs, _DN['tn'], preferred_element_type=F32) * scale
                dv_acc[0:ke, hs] += lax.dot_general(p.astype(BF16), dos, _DN['tn'], preferred_element_type=F32)
                dcq_ref[q0:ke, hh:hh + 1] = jnp.sum(ds, axis=-1, keepdims=True)
                dck_ref[hh:hh + 1, 0:ke] -= jnp.sum(ds, axis=0, keepdims=True)
        dk_ref[...] = dk_acc[...].astype(dk_ref.dtype)
        dv_ref[...] = dv_acc[...].astype(dv_ref.dtype)

    blk = lambda cb0: pl.BlockSpec((T, LANES), lambda j: (0, cb0 + j))
    pair = pl.BlockSpec((None, T, LANES), lambda j: (j, 0, 0))
    grad = _sds((T, D), BF16)
    return _call(body, name='fox_bwd',
                 out_shape=(grad, grad, grad, _sds((ncb, T, LANES), F32), _sds((ncb, 8, T), F32)), grid=(ncb,),
                 in_specs=[blk(3 * ncb), blk(4 * ncb), blk(5 * ncb), _full((T, FOX_HEADS)), _full((FOX_HEADS, T)), blk(0),
                           pair, blk(0)],
                 out_specs=(blk(0), blk(0), blk(0), pair, pl.BlockSpec((None, 8, T), lambda j: (j, 0, 0))),
                 scratch=[pltpu.VMEM((T, LANES), F32), pltpu.VMEM((T, LANES), F32)],
                 sem=('parallel',))(z, z, z, c, ct, o, lse, do)


def _ln_parts(u1, gain, bias):
    mu = jnp.mean(u1, axis=-1, keepdims=True)
    xc = u1 - mu
    rstd = lax.rsqrt(jnp.mean(xc * xc, axis=-1, keepdims=True) + EPS)
    xhat = xc * rstd
    return xhat, rstd, xhat * gain + bias


def _mixc_fwd(z, w, cb, gain, bias):
    T = z.shape[0]
    D = w.shape[1]
    tb = _pick(T, (256, 128))
    H = HALO_C

    def body(cv, cg, cvp, cgp, w_ref, cb_ref, g_ref, b_ref, o_ref, u1_ref, ext):
        first = pl.program_id(0) == 0
        ext[0:H, :] = jnp.where(first, 0.0, cvp[...].astype(F32) * _sigmoid(cgp[...].astype(F32)))
        ext[H:H + tb, :] = cv[...].astype(F32) * _sigmoid(cg[...].astype(F32))
        u1 = jnp.zeros((tb, D), F32)
        for k in range(CONV_C_W):
            off = H - (CONV_C_W - 1) + k
            u1 = u1 + ext[off:off + tb, :] * w_ref[k:k + 1, :]
        u1 = u1 + cb_ref[...]
        u1_ref[...] = u1
        _, _, u2 = _ln_parts(u1, g_ref[...], b_ref[...])
        o_ref[...] = (u2 * _sigmoid(u2)).astype(o_ref.dtype)

    vec = _full((1, D))
    row = _col(tb, D, 0)
    return _call(body, name='mixc_fwd', out_shape=(_sds((T, D), BF16), _sds((T, D), F32)), grid=(T // tb,),
                 in_specs=[_col(tb, D, 0), _col(tb, D, 1), _prev(tb, H, D, 0), _prev(tb, H, D, 1), _full((32, D)), vec,
                           vec, vec],
                 out_specs=(row, row), scratch=[pltpu.VMEM((H + tb, D), F32)],
                 sem=('parallel',))(z, z, z, z, w, cb, gain, bias)


def _mixc_bwd(z, w, gain, bias, u1, du3):
    T = z.shape[0]
    D = w.shape[1]
    tb = _pick(T, (256, 128))
    H = HALO_C
    nb = T // tb

    def du1_of(u1v, du3v, gv, bv):
        xhat, rstd, u2 = _ln_parts(u1v, gv, bv)
        sg = _sigmoid(u2)
        du2 = du3v * (sg * (1.0 + u2 * (1.0 - sg)))
        dxh = du2 * gv
        m1 = jnp.mean(dxh, axis=-1, keepdims=True)
        m2 = jnp.mean(dxh * xhat, axis=-1, keepdims=True)
        return rstd * (dxh - m1 - xhat * m2), du2, xhat

    def body(cv, cg, cvp, cgp, w_ref, g_ref, b_ref, u1_ref, u1n, du3_ref, du3n, dz_ref, dw_ref, dcb_ref, dg_ref,
             db_ref, uext, dext):
        i = pl.program_id(0)
        gv, bv = g_ref[...], b_ref[...]
        c_val = cv[...].astype(F32)
        sg = _sigmoid(cg[...].astype(F32))
        uext[0:H, :] = jnp.where(i == 0, 0.0, cvp[...].astype(F32) * _sigmoid(cgp[...].astype(F32)))
        uext[H:H + tb, :] = c_val * sg
        du1, du2, xhat = du1_of(u1_ref[...], du3_ref[...].astype(F32), gv, bv)
        du1n, _, _ = du1_of(u1n[...], du3n[...].astype(F32), gv, bv)
        dext[0:tb, :] = du1
        dext[tb:tb + H, :] = jnp.where(i == nb - 1, 0.0, du1n)

        @pl.when(i == 0)
        def _():
            dw_ref[...] = jnp.zeros_like(dw_ref)
            dcb_ref[...] = jnp.zeros_like(dcb_ref)
            dg_ref[...] = jnp.zeros_like(dg_ref)
            db_ref[...] = jnp.zeros_like(db_ref)

        dcb_ref[...] += jnp.sum(du1, axis=0, keepdims=True)
        dg_ref[...] += jnp.sum(du2 * xhat, axis=0, keepdims=True)
        db_ref[...] += jnp.sum(du2, axis=0, keepdims=True)
        du0 = jnp.zeros((tb, D), F32)
        for k in range(CONV_C_W):
            off = H - (CONV_C_W - 1) + k
            du0 = du0 + dext[CONV_C_W - 1 - k:CONV_C_W - 1 - k + tb, :] * w_ref[k:k + 1, :]
            dw_ref[k:k + 1, :] += jnp.sum(du1 * uext[off:off + tb, :], axis=0, keepdims=True)
        dz_ref[:, 0:D] = (du0 * sg).astype(dz_ref.dtype)
        dz_ref[:, D:2 * D] = (du0 * c_val * sg * (1.0 - sg)).astype(dz_ref.dtype)

    vec = _full((1, D))
    row = _col(tb, D, 0)
    nxt = _next(tb, H, D, 0, T)
    return _call(body, name='mixc_bwd',
                 out_shape=(_sds((T, 2 * D), BF16), _sds((32, D), F32), _sds((1, D), F32), _sds((1, D), F32),
                            _sds((1, D), F32)), grid=(nb,),
                 in_specs=[_col(tb, D, 0), _col(tb, D, 1), _prev(tb, H, D, 0), _prev(tb, H, D, 1), _full((32, D)), vec,
                           vec, row, nxt, row, nxt],
                 out_specs=(_col(tb, 2 * D, 0), _full((32, D)), vec, vec, vec),
                 scratch=[pltpu.VMEM((H + tb, D), F32), pltpu.VMEM((tb + H, D), F32)],
                 sem=('arbitrary',))(z, z, z, z, w, gain, bias, u1, u1, du3, du3)


def _gate_fwd(z, bg, ya, yb, yc):
    T = z.shape[0]
    D = ya.shape[1]
    tb = _pick(T, (256, 128))

    def body(ga, gb, gc, bg_ref, ya_ref, yb_ref, yc_ref, o_ref):
        acc = jnp.zeros((tb, D), F32)
        for n, (g, y) in enumerate(((ga, ya_ref), (gb, yb_ref), (gc, yc_ref))):
            acc = acc + _sigmoid(g[...].astype(F32) + bg_ref[:, n * D:(n + 1) * D]) * y[...].astype(F32)
        o_ref[...] = acc.astype(o_ref.dtype)

    row = _col(tb, D, 0)
    return _call(body, name='gate_fwd', out_shape=_sds((T, D), BF16), grid=(T // tb,),
                 in_specs=[_col(tb, D, 2), _col(tb, D, 3), _col(tb, D, 4), _full((1, 3 * D)), row, row, row],
                 out_specs=row, sem=('parallel',))(z, z, z, bg, ya, yb, yc)


def _gate_bwd(z, bg, ya, yb, yc, dm):
    T = z.shape[0]
    D = ya.shape[1]
    tb = _pick(T, (256, 128))

    def body(ga, gb, gc, bg_ref, ya_ref, yb_ref, yc_ref, dm_ref, dya, dyb, dyc, dg_ref, dbg_ref):
        @pl.when(pl.program_id(0) == 0)
        def _():
            dbg_ref[...] = jnp.zeros_like(dbg_ref)

        dmv = dm_ref[...]
        for n, (g, y, dy) in enumerate(((ga, ya_ref, dya), (gb, yb_ref, dyb), (gc, yc_ref, dyc))):
            cols = slice(n * D, (n + 1) * D)
            sg = _sigmoid(g[...].astype(F32) + bg_ref[:, cols])
            dy[...] = (dmv * sg).astype(dy.dtype)
            dg = dmv * y[...].astype(F32) * sg * (1.0 - sg)
            dg_ref[:, cols] = dg.astype(dg_ref.dtype)
            dbg_ref[:, cols] += jnp.sum(dg, axis=0, keepdims=True)

    row = _col(tb, D, 0)
    act = _sds((T, D), BF16)
    return _call(body, name='gate_bwd', out_shape=(act, act, act, _sds((T, 3 * D), BF16), _sds((1, 3 * D), F32)),
                 grid=(T // tb,),
                 in_specs=[_col(tb, D, 2), _col(tb, D, 3), _col(tb, D, 4), _full((1, 3 * D)), row, row, row, row],
                 out_specs=(row, row, row, _col(tb, 3 * D, 0), _full((1, 3 * D))),
                 sem=('arbitrary',))(z, z, z, bg, ya, yb, yc, dm)


def _xattn_probs(qs, ks, scale):
    s = lax.dot_general(qs, ks, _DN['nt'], preferred_element_type=F32) * scale
    p = jnp.exp(s - jnp.max(s, axis=-1, keepdims=True))
    return p * (1.0 / jnp.sum(p, axis=-1, keepdims=True))


def _xattn_fwd(q, kv):
    T, DX = q.shape
    M = kv.shape[0]
    dh = DX // X_HEADS
    tb = _pick(T, (512, 256, 128))
    scale = dh ** -0.5

    def body(q_ref, kv_ref, o_ref):
        for h in range(X_HEADS):
            hs = slice(h * dh, (h + 1) * dh)
            p = _xattn_probs(q_ref[:, hs], kv_ref[:, hs], scale)
            o_ref[:, hs] = jnp.dot(p.astype(BF16), kv_ref[:, DX + h * dh:DX + (h + 1) * dh],
                                   preferred_element_type=F32).astype(o_ref.dtype)

    row = _col(tb, DX, 0)
    return _call(body, name='xattn_fwd', out_shape=_sds((T, DX), BF16), grid=(T // tb,),
                 in_specs=[row, _full((M, 2 * DX))], out_specs=row, sem=('parallel',))(q, kv)


def _xattn_bwd(q, kv, do):
    T, DX = q.shape
    M = kv.shape[0]
    dh = DX // X_HEADS
    tb = _pick(T, (512, 256, 128))
    scale = dh ** -0.5

    def body(q_ref, kv_ref, do_ref, dq_ref, dkv_ref):
        @pl.when(pl.program_id(0) == 0)
        def _():
            dkv_ref[...] = jnp.zeros_like(dkv_ref)

        for h in range(X_HEADS):
            hs = slice(h * dh, (h + 1) * dh)
            vs_cols = slice(DX + h * dh, DX + (h + 1) * dh)
            qs, ks, vs = q_ref[:, hs], kv_ref[:, hs], kv_ref[:, vs_cols]
            dos = do_ref[:, hs].astype(BF16)
            p = _xattn_probs(qs, ks, scale)
            dp = lax.dot_general(dos, vs, _DN['nt'], preferred_element_type=F32)
            ds = p * (dp - jnp.sum(p * dp, axis=-1, keepdims=True))
            dsb = ds.astype(BF16)
            dq_ref[:, hs] = (jnp.dot(dsb, ks, preferred_element_type=F32) * scale).astype(dq_ref.dtype)
            dkv_ref[:, hs] += lax.dot_general(dsb, qs, _DN['tn'], preferred_element_type=F32) * scale
            dkv_ref[:, vs_cols] += lax.dot_general(p.astype(BF16), dos, _DN['tn'], preferred_element_type=F32)

    row = _col(tb, DX, 0)
    return _call(body, name='xattn_bwd', out_shape=(_sds((T, DX), BF16), _sds((M, 2 * DX), F32)), grid=(T // tb,),
                 in_specs=[row, _full((M, 2 * DX)), row], out_specs=(row, _full((M, 2 * DX))),
                 sem=('arbitrary',))(q, kv, do)


def _swiglu_fwd(gu):
    T, F2 = gu.shape
    F = F2 // 2
    tb = _pick(T, (256, 128))

    def body(g_ref, u_ref, o_ref):
        g = g_ref[...].astype(F32)
        o_ref[...] = (g * _sigmoid(g) * u_ref[...].astype(F32)).astype(o_ref.dtype)

    return _call(body, name='swiglu_fwd', out_shape=_sds((T, F), BF16), grid=(T // tb,),
                 in_specs=[_col(tb, F, 0), _col(tb, F, 1)], out_specs=_col(tb, F, 0), sem=('parallel',))(gu, gu)


def _swiglu_bwd(gu, da):
    T, F2 = gu.shape
    F = F2 // 2
    tb = _pick(T, (256, 128))

    def body(g_ref, u_ref, da_ref, o_ref):
        g = g_ref[...].astype(F32)
        u = u_ref[...].astype(F32)
        dav = da_ref[...].astype(F32)
        sg = _sigmoid(g)
        o_ref[:, 0:F] = (dav * u * (sg * (1.0 + g * (1.0 - sg)))).astype(o_ref.dtype)
        o_ref[:, F:F2] = (dav * g * sg).astype(o_ref.dtype)

    return _call(body, name='swiglu_bwd', out_shape=_sds((T, F2), BF16), grid=(T // tb,),
                 in_specs=[_col(tb, F, 0), _col(tb, F, 1), _col(tb, F, 0)], out_specs=_col(tb, F2, 0),
                 sem=('parallel',))(gu, gu, da)


def _place():
    x, y, c = lax.axis_index('x'), lax.axis_index('y'), lax.axis_index('c')
    return x, y, c


def _allgather(arrs, lead, name):
    n = len(arrs)

    def out_shape(a, ld):
        return (N_DEV,) + a.shape if ld else (a.shape[0], N_DEV) + a.shape[1:]

    def body(*refs):
        ins, outs = refs[:n], refs[n:2 * n]
        send_sems, recv_sems, local_sems = refs[2 * n:]
        x, y, c = _place()
        me, sibling = (x, y, c), (x, y, 1 - c)
        chips = [(1 - x, y), (x, 1 - y), (1 - x, 1 - y)]

        def slot(a, dev):
            idx = 4 * dev[0] + 2 * dev[1] + dev[2]
            return outs[a].at[idx] if lead[a] else outs[a].at[:, idx]

        def copy(a, k, block, to, src=None):
            return pltpu.make_async_remote_copy(
                src_ref=slot(a, block) if src is None else src, dst_ref=slot(a, block),
                send_sem=send_sems.at[a * 7 + k], recv_sem=recv_sems.at[a * 7 + k], device_id=to, device_id_type=MESH)

        mine = [pltpu.make_async_copy(ins[a], slot(a, me), local_sems.at[a]) for a in range(n)]
        for cp in mine:
            cp.start()
        first = []
        for a in range(n):
            first.append(copy(a, 0, me, sibling, src=ins[a]))
            first += [copy(a, 1 + j, me, (*chip, c), src=ins[a]) for j, chip in enumerate(chips)]
        for cp in first:
            cp.start()
        passed = []
        for a in range(n):
            for j, chip in enumerate(chips):
                copy(a, 1 + j, (*chip, c), me).wait_recv()
                fwd = copy(a, 4 + j, (*chip, c), sibling)
                fwd.start()
                passed.append(fwd)
        for a in range(n):
            copy(a, 0, sibling, me).wait_recv()
            for j, chip in enumerate(chips):
                copy(a, 4 + j, (*chip, 1 - c), me).wait_recv()
        for cp in first + passed:
            cp.wait_send()
        for cp in mine:
            cp.wait()

    any_spec = pl.BlockSpec(memory_space=pl.ANY)
    return _call(body, name=name, out_shape=tuple(_sds(out_shape(a, ld), a.dtype) for a, ld in zip(arrs, lead)),
                 in_specs=[any_spec] * n, out_specs=tuple([any_spec] * n),
                 scratch=[pltpu.SemaphoreType.DMA((7 * n,)), pltpu.SemaphoreType.DMA((7 * n,)),
                          pltpu.SemaphoreType.DMA((n,))])(*arrs)


_PEER_ORDER = (1, 4, 2, 5, 3, 6, 7)
_HBM = pl.BlockSpec(memory_space=pltpu.HBM)
_SEM = pl.BlockSpec(memory_space=pltpu.SEMAPHORE)


def _my_index():
    x, y, c = _place()
    return 4 * x + 2 * y + c


def _remote(src, dst, send_sems, recv_sems, k, dev):
    return pltpu.make_async_remote_copy(src_ref=src, dst_ref=dst, send_sem=send_sems.at[k], recv_sem=recv_sems.at[k],
                                        device_id=dev, device_id_type=MESH)


def _own_part(src, kind, args):
    return src if args is None else src.at[pl.ds(args[0], args[1])]


def _block(land, kind, args, owner):
    if kind == 'rows':
        _, n, stride, d0 = args
        return land.at[pl.ds(pl.multiple_of(stride * owner + d0, ROW_TILE), n)]
    return land.at[owner]


def _chunk(src, args, dest):
    stride, n = args
    return src.at[pl.ds(pl.multiple_of(stride * dest, ROW_TILE), n)]


def _scatter_copies(plan, srcs, lands, send_sems, recv_sems):
    x, y, c = _place()
    me = 4 * x + 2 * y + c
    copies = []
    for r in _PEER_ORDER:
        px, py, pc = x ^ (r >> 2), y ^ ((r >> 1) & 1), c ^ (r & 1)
        for si, li, _, args in plan:
            copies.append(_remote(_chunk(srcs[si], args, 4 * px + 2 * py + pc), lands[li].at[me], send_sems, recv_sems,
                                  r - 1, (px, py, pc)))
    return copies


def _gather_copies(plan, srcs, lands, send_sems, recv_sems, passing):
    x, y, c = _place()
    me = 4 * x + 2 * y + c
    sibling = (x, y, 1 - c)
    chips = [(1 - x, y), (x, 1 - y), (1 - x, 1 - y)]
    copies = []
    if passing:
        for k, chip in enumerate(chips):
            for _, li, kind, args in plan:
                blk = _block(lands[li], kind, args, 4 * chip[0] + 2 * chip[1] + c)
                copies.append((k, _remote(blk, blk, send_sems, recv_sems, k, sibling)))
    else:
        for k, dev in enumerate([sibling] + [(*chip, c) for chip in chips]):
            for si, li, kind, args in plan:
                dst = _block(lands[li], kind, args, me)
                src = dst if srcs is None else _own_part(srcs[si], kind, args)
                copies.append((k, _remote(src, dst, send_sems, recv_sems, k, dev)))
    return copies


def _sem_call(name, bufs, sems_in, new_sems, after, action):
    nb, ni, nn = len(bufs), len(sems_in), len(new_sems)

    def body(*refs):
        action(refs[:nb], refs[nb:nb + ni], refs[-1 - nb - nn:-1 - nb])
        refs[-1][...] = jnp.zeros_like(refs[-1])

    bufs = [pltpu.with_memory_space_constraint(v, pltpu.HBM) for v in bufs]
    res = pl.pallas_call(
        body, name=name,
        out_shape=(*[pltpu.SemaphoreType.DMA((n,)) for n in new_sems], *[pltpu.HBM(v.shape, v.dtype) for v in bufs],
                   _sds((8, LANES), F32)),
        in_specs=[_HBM] * nb + [_SEM] * ni + [pl.BlockSpec(memory_space=pl.ANY)] * len(after),
        out_specs=(*[_SEM] * nn, *[_HBM] * nb, pl.BlockSpec(memory_space=pltpu.VMEM)),
        input_output_aliases={i: nn + i for i in range(nb)},
        compiler_params=pltpu.CompilerParams(has_side_effects=pltpu.SideEffectType.DATAFLOW_SIDE_EFFECTING),
        interpret=False)(*bufs, *sems_in, *after)
    return list(res[:nn]), list(res[nn:nn + nb]), res[-1]


def _place_own(srcs, plan, lands_like, name):
    ns, nl, ne = len(srcs), len(lands_like), len(plan)

    def part_shape(entry):
        si, _, kind, args = entry
        rows = args[1] if (kind == 'chunk' or args is not None) else None
        return srcs[si].shape if rows is None else (rows,) + srcs[si].shape[1:]

    def body(*refs):
        src_r, land_r = refs[:ns], refs[ns:ns + nl]
        stage, sems_in, sems_out = refs[ns + nl:ns + nl + ne], refs[-2], refs[-1]
        me = _my_index()
        loads, stores = [], []
        for i, (si, li, kind, args) in enumerate(plan):
            if kind == 'chunk':
                src, dst = _chunk(src_r[si], args, me), land_r[li].at[me]
            else:
                src, dst = _own_part(src_r[si], kind, args), _block(land_r[li], kind, args, me)
            loads.append(pltpu.make_async_copy(src, stage[i], sems_in.at[i]))
            stores.append(pltpu.make_async_copy(stage[i], dst, sems_out.at[i]))
        for cp in loads:
            cp.start()
        for load, store in zip(loads, stores):
            load.wait()
            store.start()
        for cp in stores:
            cp.wait()

    anywhere = pl.BlockSpec(memory_space=pl.ANY)
    scratch = [pltpu.VMEM(part_shape(e), srcs[e[0]].dtype) for e in plan]
    scratch += [pltpu.SemaphoreType.DMA((ne,)), pltpu.SemaphoreType.DMA((ne,))]
    return list(_call(body, name=name, out_shape=tuple(lands_like), in_specs=[anywhere] * ns,
                      out_specs=tuple([anywhere] * nl), scratch=scratch)(*srcs))


def _scatter_start(srcs, lands, plan, after, name):
    ns = len(srcs)

    def action(bufs, _, new):
        for cp in _scatter_copies(plan, bufs[:ns], bufs[ns:], new[0], new[1]):
            cp.start()

    sems, bufs, token = _sem_call(name, [*srcs, *lands], [], [N_DEV - 1, N_DEV - 1], after, action)
    return sems, bufs[:ns], bufs[ns:], token


def _scatter_wait(sems, srcs, lands, plan, after, name):
    ns = len(srcs)

    def action(bufs, old, _):
        copies = _scatter_copies(plan, bufs[:ns], bufs[ns:], old[0], old[1])
        for cp in copies:
            cp.wait_send()
        for cp in copies:
            cp.wait_recv()

    return _sem_call(name, [*srcs, *lands], sems, [], after, action)[1][ns:]


def _gather_start(srcs, lands, plan, name):
    ns = len(srcs)

    def action(bufs, _, new):
        for _, cp in _gather_copies(plan, bufs[:ns], bufs[ns:], new[0], new[1], False):
            cp.start()

    sems, bufs, token = _sem_call(name, [*srcs, *lands], [], [4, 4], [], action)
    return sems, bufs[:ns], bufs[ns:], token


def _gather_pass(recv_sems, lands, plan, after, name):
    def action(bufs, old, new):
        for _, cp in _gather_copies(plan, None, bufs, old[0], old[0], False):
            cp.wait_recv()
        for _, cp in _gather_copies(plan, None, bufs, new[0], new[1], True):
            cp.start()

    return _sem_call(name, lands, [recv_sems], [3, 3], after, action)


def _gather_wait(send_sems, pass_sems, srcs, lands, plan, after, name):
    ns = len(srcs)

    def action(bufs, old, _):
        for _, cp in _gather_copies(plan, bufs[:ns], bufs[ns:], old[0], old[0], False):
            cp.wait_send()
        passed = _gather_copies(plan, None, bufs[ns:], old[1], old[2], True)
        for _, cp in passed:
            cp.wait_send()
        for _, cp in passed:
            cp.wait_recv()

    return _sem_call(name, [*srcs, *lands], [send_sems, *pass_sems], [], after, action)[1][ns:]


def _merge_edges(wt, edges, tiles_per_dev):
    D = wt.shape[1]

    def body(w_ref, e_ref, o_ref):
        s = pl.program_id(0)
        o_ref[...] = jnp.where(s == 0, e_ref[...], w_ref[...] + e_ref[...])

    tile = pl.BlockSpec((ROW_TILE, D), lambda s: (s * tiles_per_dev, 0))
    return _call(body, name='merge_edges', out_shape=_sds(wt.shape, wt.dtype), grid=(N_DEV,),
                 in_specs=[tile, pl.BlockSpec((None, ROW_TILE, D), lambda s: (s, 0, 0))], out_specs=tile,
                 sem=('arbitrary',), aliases={0: 0})(wt, edges)


def _adam_update(g, w, m, v):
    c1 = 1.0 - ADAM_B1 ** ADAM_STEP
    c2 = 1.0 - ADAM_B2 ** ADAM_STEP
    mn = ADAM_B1 * m + (1.0 - ADAM_B1) * g
    vn = ADAM_B2 * v + (1.0 - ADAM_B2) * (g * g)
    m_hat = mn / c1
    v_hat = vn / c2
    return -ADAM_LR * (m_hat / (jnp.sqrt(v_hat) + ADAM_EPS) + ADAM_WD * w), mn, vn


def _adamw(parts, w, m, v, name):
    n_parts = parts.shape[0]
    R, C = w.shape
    tb = _pick(R, (128, 64, 32, 16, 8))

    def body(p_ref, w_ref, m_ref, v_ref, g_out, d_out, m_out, v_out):
        g = p_ref[0].astype(F32)
        for s in range(1, n_parts):
            g = g + p_ref[s].astype(F32)
        g_out[...] = g
        d_out[...], m_out[...], v_out[...] = _adam_update(g, w_ref[...], m_ref[...], v_ref[...])

    row = pl.BlockSpec((tb, C), lambda i: (i, 0))
    o = _sds((R, C), F32)
    return _call(body, name=name, out_shape=(o, o, o, o), grid=(R // tb,),
                 in_specs=[pl.BlockSpec((n_parts, tb, C), lambda i: (0, i, 0)), row, row, row],
                 out_specs=(row, row, row, row), sem=('parallel',))(parts, w, m, v)


def _adamw_layer(parts, w, m, v, layer, prev, shift, name):
    L, n, C = w.shape
    n_p = parts.shape[1]
    cb = C if N_DEV * n_p * C * parts.dtype.itemsize <= (6 << 20) else 2 * LANES
    n_prev = 0 if prev is None else 4

    def body(sh_ref, p_ref, w_ref, m_ref, v_ref, *rest):
        g_out, d_out, m_out, v_out = rest[n_prev:n_prev + 4]
        g = p_ref[0].astype(F32)
        for s in range(1, N_DEV):
            g = g + p_ref[s].astype(F32)
        if n_p != n:
            rolled = rest[-1]
            rolled[...] = pltpu.roll(g, n_p - sh_ref[0], 0)
            g = rolled[0:n, :]
        g_out[...] = g
        d_out[...], m_out[...], v_out[...] = _adam_update(g, w_ref[...], m_ref[...], v_ref[...])

    lay = pl.BlockSpec((None, n, cb), lambda j, sh: (layer, 0, j))
    stack = _sds((L, n, C), F32)
    grid_spec = pltpu.PrefetchScalarGridSpec(
        num_scalar_prefetch=1, grid=(C // cb,),
        in_specs=[pl.BlockSpec((N_DEV, n_p, cb), lambda j, sh: (0, 0, j)), lay, lay, lay]
        + [pl.BlockSpec(memory_space=pl.ANY)] * n_prev,
        out_specs=(lay, lay, lay, lay), scratch_shapes=[pltpu.VMEM((n_p, cb), F32)] if n_p != n else [])
    return pl.pallas_call(body, name=name, grid_spec=grid_spec, out_shape=(stack, stack, stack, stack),
                          input_output_aliases={5 + k: k for k in range(n_prev)},
                          compiler_params=pltpu.CompilerParams(dimension_semantics=('parallel',),
                                                               vmem_limit_bytes=VMEM_LIMIT),
                          interpret=False)(shift, parts, w, m, v, *(prev or ()))


def kernel(x, mem, mix_norm, w_in, b_gate, b_forget, conv_a, w_out_a, w_out_b, conv_c, conv_c_bias, ln_c_gain, ln_c_bias, w_out_c, w_o, xattn_norm, mem_norm, w_xq, w_xkv, w_xo, ffn_norm, w_gate_up, w_down, final_norm, loss_target, m_mix_norm, m_w_in, m_b_gate, m_b_forget, m_conv_a, m_w_out_a, m_w_out_b, m_conv_c, m_conv_c_bias, m_ln_c_gain, m_ln_c_bias, m_w_out_c, m_w_o, m_xattn_norm, m_mem_norm, m_w_xq, m_w_xkv, m_w_xo, m_ffn_norm, m_w_gate_up, m_w_down, m_final_norm, v_mix_norm, v_w_in, v_b_gate, v_b_forget, v_conv_a, v_w_out_a, v_w_out_b, v_conv_c, v_conv_c_bias, v_ln_c_gain, v_ln_c_bias, v_w_out_c, v_w_o, v_xattn_norm, v_mem_norm, v_w_xq, v_w_xkv, v_w_xo, v_ffn_norm, v_w_gate_up, v_w_down, v_final_norm):
    P = dict(zip(ARG_NAMES, (x, mem, mix_norm, w_in, b_gate, b_forget, conv_a, w_out_a, w_out_b, conv_c, conv_c_bias, ln_c_gain, ln_c_bias, w_out_c, w_o, xattn_norm, mem_norm, w_xq, w_xkv, w_xo, ffn_norm, w_gate_up, w_down, final_norm, loss_target, m_mix_norm, m_w_in, m_b_gate, m_b_forget, m_conv_a, m_w_out_a, m_w_out_b, m_conv_c, m_conv_c_bias, m_ln_c_gain, m_ln_c_bias, m_w_out_c, m_w_o, m_xattn_norm, m_mem_norm, m_w_xq, m_w_xkv, m_w_xo, m_ffn_norm, m_w_gate_up, m_w_down, m_final_norm, v_mix_norm, v_w_in, v_b_gate, v_b_forget, v_conv_a, v_w_out_a, v_w_out_b, v_conv_c, v_conv_c_bias, v_ln_c_gain, v_ln_c_bias, v_w_out_c, v_w_o, v_xattn_norm, v_mem_norm, v_w_xq, v_w_xkv, v_w_xo, v_ffn_norm, v_w_gate_up, v_w_down, v_final_norm)))
    return _step(P)


_T_VIEW = ('w_in', 'w_gate_up', 'w_xo')
_BIG = [w for w in SHARDED if w not in ('conv_a', 'conv_c')]
_OTHER = [w for w in _BIG if w != 'w_in']
_CONVS = (('conv_a', CONV_A_W, 8), ('conv_c', CONV_C_W, 32))
_EARLY = ['w_down', 'w_gate_up', 'w_xo', 'w_xq', 'w_xkv', 'w_o']
_LATE = ['w_out_c', 'w_out_b', 'w_out_a', 'w_in']


def _view(name, a):
    return jnp.transpose(a, (0, 2, 1)) if name in _T_VIEW else a


def _step(P):
    L, D = P['mix_norm'].shape
    T = P['x'].shape[1]
    n_in = P['w_in'].shape[2]
    stride = n_in // ROW_TILE * ROW_TILE
    rem = n_in - stride
    win = stride + ROW_TILE
    assert rem * N_DEV == ROW_TILE
    r_in = n_in * N_DEV
    f_off = 6 * D
    hi_off = f_off + FOX_HEADS
    x_i, y_i, c_i = _place()
    me = 4 * x_i + 2 * y_i + c_i
    V = {}
    for w in _BIG:
        for k in (w, 'm_' + w, 'v_' + w):
            V[k] = _view(w, P[k])
    n_own = {w: V[w].shape[1] for w in _OTHER}
    shift = jnp.reshape(rem * me, (1,)).astype(jnp.int32)
    no_shift = jnp.zeros((1,), jnp.int32)
    row = lambda a, l: a[l][None, :]
    dus = lax.dynamic_update_slice

    def gather_start(l):
        p_in = dus(jnp.zeros((win, D), BF16), V['w_in'][l].astype(BF16), (rem * me, 0))
        srcs = [p_in] + [V[w][l].astype(BF16) for w in _OTHER] + [P[w][l] for w, _, _ in _CONVS]
        lands = [_sds((r_in, D), BF16), _sds((N_DEV, ROW_TILE, D), BF16)]
        plan = [(0, 0, 'rows', (ROW_TILE, stride, stride, ROW_TILE)), (0, 1, 'slot', (0, ROW_TILE))]
        for i, w in enumerate(_OTHER):
            n, s = n_own[w], srcs[1 + i]
            lands.append(_sds((N_DEV * n, s.shape[1]), BF16))
            plan.append((1 + i, 2 + i, 'rows', (0, n, n, 0)))
        for j in range(len(_CONVS)):
            i = 1 + len(_OTHER) + j
            lands.append(_sds((N_DEV,) + srcs[i].shape, F32))
            plan.append((i, i + 1, 'slot', None))
        lands = _place_own(srcs, plan, lands, 'gather_own')
        sems, srcs, lands, token = _gather_start(srcs, lands, plan, f'gather_start_{l}')
        return dict(plan=plan, send=sems[0], recv=sems[1], srcs=srcs, lands=lands, token=token)

    def gather_pass(g, after, l):
        g['pass'], g['lands'], g['token'] = _gather_pass(g['recv'], g['lands'], g['plan'], after, f'gather_pass_{l}')

    def gather_wait(g, after, l):
        lands = _gather_wait(g['send'], g['pass'], g['srcs'], g['lands'], g['plan'], after, f'gather_wait_{l}')
        W = {'w_in': _merge_edges(lands[0], lands[1], stride // ROW_TILE)}
        for i, w in enumerate(_OTHER):
            W[w] = lands[2 + i]
        for j, (w, taps, padded) in enumerate(_CONVS):
            full = jnp.transpose(lands[2 + len(_OTHER) + j], (1, 0, 2)).reshape(taps, D)
            W[w] = jnp.pad(full, ((0, padded - taps), (0, 0)))
        return W

    gathers = [gather_start(l) for l in range(L)]
    gather_pass(gathers[0], [g['token'] for g in gathers], 0)

    mem_x = P['mem'][0]
    mem_n = _rms_fwd(mem_x, P['mem_norm'][None, :], 'rms_mem')
    xs = P['x'][0]
    saved = []
    for l in range(L):
        W = gather_wait(gathers[l], [xs], l)
        s = {'x0': xs, 'W': W}
        wt = W['w_in']
        s['h1'] = _rms_fwd(xs, row(P['mix_norm'], l), 'rms_mix')
        s['z1'] = z1 = _mm(s['h1'], wt, mode='nt', b_rows=(0, f_off), name='mm_in_lo')
        s['z2'] = z2 = _mm(s['h1'], wt, mode='nt', b_rows=(hi_off, r_in - hi_off), name='mm_in_hi')
        s['zf'] = _mm(s['h1'], wt, mode='nt', b_rows=(f_off, FOX_HEADS), out_dtype=F32, name='mm_in_f')
        s['ya_pre'] = _mixa_fwd(z1, W['conv_a'])
        s['ya'] = _mm(s['ya_pre'], W['w_out_a'], mode='nn', name='mm_out_a')
        s['c'] = _fox_c(s['zf'], row(P['b_forget'], l))
        s['ct'] = jnp.transpose(s['c'])
        s['o'], s['lse'] = _fox_fwd(z1, s['c'], s['ct'])
        s['yb'] = _mm(s['o'], W['w_out_b'], mode='nn', name='mm_out_b')
        s['u3'], s['u1'] = _mixc_fwd(z2, W['conv_c'], row(P['conv_c_bias'], l), row(P['ln_c_gain'], l),
                                     row(P['ln_c_bias'], l))
        s['yc'] = _mm(s['u3'], W['w_out_c'], mode='nn', name='mm_out_c')
        s['merged'] = _gate_fwd(z2, row(P['b_gate'], l), s['ya'], s['yb'], s['yc'])
        xs = _mm(s['merged'], W['w_o'], mode='nn', add=xs, out_dtype=F32, name='mm_o')
        s['x1'] = xs
        deps = []
        if l + 1 < L:
            gather_pass(gathers[l + 1], [xs], l + 1)
            deps = [gathers[l + 1]['token']]
        s['h2'] = _rms_fwd(xs, row(P['xattn_norm'], l), 'rms_xattn', deps=deps)
        s['qx'] = _mm(s['h2'], W['w_xq'], mode='nn', name='mm_xq')
        s['kv'] = _mm(mem_n, W['w_xkv'], mode='nn', name='mm_xkv')
        s['ox'] = _xattn_fwd(s['qx'], s['kv'])
        xs = _mm(s['ox'], W['w_xo'], mode='nt', add=xs, out_dtype=F32, name='mm_xo')
        s['x2'] = xs
        s['h3'] = _rms_fwd(xs, row(P['ffn_norm'], l), 'rms_ffn')
        s['gu'] = _mm(s['h3'], W['w_gate_up'], mode='nt', name='mm_gate_up')
        s['act'] = _swiglu_fwd(s['gu'])
        xs = _mm(s['act'], W['w_down'], mode='nn', add=xs, out_dtype=F32, name='mm_down')
        saved.append(s)

    dx, d_final_norm, loss_part = _final(xs, P['final_norm'][None, :], P['loss_target'][0])

    GS = {w: [None] * L for w in SMALL}
    G_conv = {'conv_a': [None] * L, 'conv_c': [None] * L}
    d_mem_n = None
    results = {w: None for w in _BIG}

    def exchange_start(G, names, l, tag, after):
        srcs = [G[w] for w in names]
        plan, lands = [], []
        for i, w in enumerate(names):
            step, n = (stride, win) if w == 'w_in' else (n_own[w], n_own[w])
            lands.append(_sds((N_DEV, n, srcs[i].shape[1]), BF16))
            plan.append((i, i, 'chunk', (step, n)))
        lands = _place_own(srcs, plan, lands, 'exchange_own')
        sems, srcs, lands, token = _scatter_start(srcs, lands, plan, after, f'exchange_start_{tag}{l}')
        return (names, l, tag, plan, sems, srcs, lands, token)

    def exchange_finish(e, after):
        names, l, tag, plan, sems, srcs, lands, _ = e
        lands = _scatter_wait(sems, srcs, lands, plan, after, f'exchange_wait_{tag}{l}')
        for i, w in enumerate(names):
            results[w] = _adamw_layer(lands[i], V[w], V['m_' + w], V['v_' + w], l, results[w],
                                      shift if w == 'w_in' else no_shift, 'adamw_' + w)

    pending = []
    for l in reversed(range(L)):
        s = saved[l]
        W, z1, z2 = s['W'], s['z1'], s['z2']
        wt = W['w_in']
        G = {}
        d_act = _mm(dx, W['w_down'], mode='nt', name='mmb_down', deps=[e[-1] for e in pending])
        G['w_down'] = _mm(s['act'], dx, mode='tn', name='mmg_w_down')
        dgu = _swiglu_bwd(s['gu'], d_act)
        dh3 = _mm(dgu, W['w_gate_up'], mode='nn', out_dtype=F32, name='mmb_gate_up')
        G['w_gate_up'] = _mm(dgu, s['h3'], mode='tn', name='mmg_w_gate_up')
        dx, GS['ffn_norm'][l] = _rms_bwd(s['x2'], row(P['ffn_norm'], l), dh3, dx, 'rms_ffn_bwd')
        d_ox = _mm(dx, W['w_xo'], mode='nn', out_dtype=F32, name='mmb_xo')
        G['w_xo'] = _mm(dx, s['ox'], mode='tn', name='mmg_w_xo')
        dqx, dkv = _xattn_bwd(s['qx'], s['kv'], d_ox)
        dh2 = _mm(dqx, W['w_xq'], mode='nt', out_dtype=F32, name='mmb_xq')
        G['w_xq'] = _mm(s['h2'], dqx, mode='tn', name='mmg_w_xq')
        G['w_xkv'] = _mm(mem_n, dkv, mode='tn', name='mmg_w_xkv')
        d_mem_n = _mm(dkv, W['w_xkv'], mode='nt', add=d_mem_n, out_dtype=F32, name='mmb_xkv')
        dx, GS['xattn_norm'][l] = _rms_bwd(s['x1'], row(P['xattn_norm'], l), dh2, dx, 'rms_xattn_bwd')
        dm = _mm(dx, W['w_o'], mode='nt', out_dtype=F32, name='mmb_o')
        G['w_o'] = _mm(s['merged'], dx, mode='tn', name='mmg_w_o')
        early = exchange_start(G, _EARLY, l, 'a', [])
        dya, dyb, dyc, dz_g, GS['b_gate'][l] = _gate_bwd(z2, row(P['b_gate'], l), s['ya'], s['yb'], s['yc'], dm)
        du3 = _mm(dyc, W['w_out_c'], mode='nt', out_dtype=F32, name='mmb_out_c', deps=[early[-1]])
        G['w_out_c'] = _mm(s['u3'], dyc, mode='tn', name='mmg_w_out_c')
        dz_c, G_conv['conv_c'][l], GS['conv_c_bias'][l], GS['ln_c_gain'][l], GS['ln_c_bias'][l] = _mixc_bwd(
            z2, W['conv_c'], row(P['ln_c_gain'], l), row(P['ln_c_bias'], l), s['u1'], du3)
        do = _mm(dyb, W['w_out_b'], mode='nt', name='mmb_out_b')
        G['w_out_b'] = _mm(s['o'], dyb, mode='tn', name='mmg_w_out_b')
        dq, dk, dv, dcq, dck = _fox_bwd(z1, s['c'], s['ct'], s['o'], s['lse'], do)
        hp = FOX_HEADS // dcq.shape[0]
        dc = (jnp.transpose(dcq[:, :, :hp], (1, 0, 2)).reshape(T, FOX_HEADS)
              + jnp.transpose(dck[:, :hp, :], (2, 0, 1)).reshape(T, FOX_HEADS))
        dzf, GS['b_forget'][l] = _fox_c_bwd(dc, s['zf'], row(P['b_forget'], l))
        dya_pre = _mm(dya, W['w_out_a'], mode='nt', out_dtype=F32, name='mmb_out_a')
        G['w_out_a'] = _mm(s['ya_pre'], dya, mode='tn', name='mmg_w_out_a')
        dz_a, G_conv['conv_a'][l] = _mixa_bwd(z1, W['conv_a'], dya_pre)
        segs = [(dzf, f_off), (dz_a, 0), (dq, 3 * D), (dk, 4 * D), (dv, 5 * D), (dz_c, hi_off), (dz_g, hi_off + 2 * D)]
        dh1, g_in = None, r_in
        for dz, off in segs:
            dh1 = _mm(dz, wt, mode='nn', b_rows=(off, dz.shape[1]), add=dh1, out_dtype=F32, name='mmb_in')
            g_in = _mm(dz, s['h1'], mode='tn', out=g_in, o_off=off, name='mmg_w_in')
        G['w_in'] = g_in
        dx, GS['mix_norm'][l] = _rms_bwd(s['x0'], row(P['mix_norm'], l), dh1, dx, 'rms_mix_bwd')
        for e in pending:
            exchange_finish(e, [dx])
        pending = [early]
        if l > 0:
            pending.append(exchange_start(G, _LATE, l, 'b', []))

    _, d_mem_norm = _rms_bwd(mem_x, P['mem_norm'][None, :], d_mem_n, None, 'rms_mem_bwd')

    small_parts = {w: jnp.concatenate(GS[w], axis=0) for w in SMALL if w not in ('mem_norm', 'final_norm')}
    small_parts['mem_norm'] = d_mem_norm
    small_parts['final_norm'] = d_final_norm
    conv_parts = [jnp.stack(G_conv[w])[:, :taps] for w, taps, _ in _CONVS]
    sizes = [P[w].size for w in SMALL]
    conv_sizes = [c.size for c in conv_parts]
    n_small = sum(sizes) + sum(conv_sizes) + LANES
    n_rows = -(-n_small // (8 * LANES)) * 8

    def pack(parts):
        flat = jnp.concatenate([p.reshape(-1) for p in parts])
        return jnp.pad(flat, (0, n_rows * LANES - flat.size)).reshape(n_rows, LANES)

    g_small = pack([small_parts[w] for w in SMALL] + conv_parts + [loss_part])
    (all_small,) = _allgather([g_small], [True], 'gather_small')
    pending.append(exchange_start(G, _LATE, 0, 'b', [all_small]))
    sm = _adamw(all_small, pack([P[w] for w in SMALL]), pack([P['m_' + w] for w in SMALL]),
                pack([P['v_' + w] for w in SMALL]), 'adamw_small')
    g_all = sm[0].reshape(-1)
    loss = g_all[sum(sizes) + sum(conv_sizes)]
    final = {}
    off = 0
    for w, n in zip(SMALL, sizes):
        final[w] = [o.reshape(-1)[off:off + n].reshape(P[w].shape) for o in sm]
        off += n
    for (w, taps, _), c, n in zip(_CONVS, conv_parts, conv_sizes):
        n_col = P[w].shape[2]
        g_own = lax.dynamic_slice(g_all[off:off + n].reshape(c.shape), (0, 0, n_col * me), (L, taps, n_col))
        flat = lambda a: a.reshape(L * taps, n_col)
        outs = _adamw(flat(g_own)[None], flat(P[w]), flat(P['m_' + w]), flat(P['v_' + w]), 'adamw_' + w)
        final[w] = [o.reshape(P[w].shape) for o in outs]
        off += n

    for e in pending:
        exchange_finish(e, [sm[0], final['conv_c'][0]])
    for w in _BIG:
        final[w] = [_view(w, o) for o in results[w]]

    out = [loss, dx[None]]
    for k in range(4):
        out += [final[w][k] for w in WEIGHTS]
    return tuple(out)
```

```python
import functools

import jax
import jax.numpy as jnp
from jax import lax
from jax.experimental import pallas as pl
from jax.experimental.pallas import tpu as pltpu

F32 = jnp.float32
BF16 = jnp.bfloat16
MESH = pl.DeviceIdType.MESH

N_DEV = 8
EPS = 1e-6
FOX_HEADS = 16
X_HEADS = 4
LANES = 128
ROW_TILE = 16
HALO_A = 16
HALO_C = 32
SUBLANES = 8
CONV_ROWS = 128
ROW_CHUNK = 32
CONV_A_W = 3
CONV_C_W = 31
VMEM_LIMIT = 56 << 20

ADAM_LR = 0.001
ADAM_B1 = 0.9
ADAM_B2 = 0.999
ADAM_EPS = 1e-08
ADAM_WD = 0.01
ADAM_STEP = 10

WEIGHTS = ['mix_norm', 'w_in', 'b_gate', 'b_forget', 'conv_a', 'w_out_a', 'w_out_b', 'conv_c', 'conv_c_bias',
           'ln_c_gain', 'ln_c_bias', 'w_out_c', 'w_o', 'xattn_norm', 'mem_norm', 'w_xq', 'w_xkv', 'w_xo',
           'ffn_norm', 'w_gate_up', 'w_down', 'final_norm']
SHARDED = ['w_in', 'conv_a', 'w_out_a', 'w_out_b', 'conv_c', 'w_out_c', 'w_o', 'w_xq', 'w_xkv', 'w_xo',
           'w_gate_up', 'w_down']
ROW_SHARDED = ['w_out_a', 'w_out_b', 'w_out_c', 'w_o', 'w_xq', 'w_xkv', 'w_down']
SMALL = [w for w in WEIGHTS if w not in SHARDED]
ARG_NAMES = (['x', 'mem'] + WEIGHTS + ['loss_target'] + ['m_' + w for w in WEIGHTS] + ['v_' + w for w in WEIGHTS])


def _pick(n, cands=(1024, 1408, 512, 256, 128)):
    for c in cands:
        if n % c == 0:
            return c
    return n


def _call(body, *, name, out_shape, grid=(), in_specs=None, out_specs=None, scratch=(), sem=None, aliases=None,
          deps=()):
    params = dict(vmem_limit_bytes=VMEM_LIMIT)
    if sem is not None:
        params['dimension_semantics'] = sem
    n_in, n_dep = len(in_specs), len(deps)

    def body_without_deps(*refs):
        body(*refs[:n_in], *refs[n_in + n_dep:])

    call = pl.pallas_call(body_without_deps, name=name, out_shape=out_shape, grid=grid, scratch_shapes=list(scratch),
                          in_specs=list(in_specs) + [pl.BlockSpec(memory_space=pl.ANY)] * n_dep, out_specs=out_specs,
                          input_output_aliases=aliases or {}, compiler_params=pltpu.CompilerParams(**params),
                          interpret=False)
    return lambda *args: call(*args, *deps)


def _sds(shape, dtype):
    return jax.ShapeDtypeStruct(tuple(shape), dtype)


def _sigmoid(x):
    return 1.0 / (1.0 + jnp.exp(-x))


_DN = {'nn': (((1,), (0,)), ((), ())), 'nt': (((1,), (1,)), ((), ())), 'tn': (((0,), (0,)), ((), ()))}


def _mm(a, b, *, mode, name, out_dtype=BF16, add=None, b_rows=None, out=None, o_off=0, deps=()):
    if mode == 'tn':
        K, M = a.shape
    else:
        M, K = a.shape
    b_off, b_n = (0, b.shape[0]) if b_rows is None else b_rows
    N = b_n if mode == 'nt' else b.shape[1]
    assert (b.shape[1] if mode == 'nt' else b_n) == K
    tm, tn = _pick(M), _pick(N)
    tk = K if K <= 2048 else _pick(K)
    ni, nj, nk = M // tm, N // tn, K // tk
    a_bytes, b_bytes = M * K * a.dtype.itemsize, K * N * b.dtype.itemsize
    n_outer = (b_bytes + nj * a_bytes) < (a_bytes + ni * b_bytes)
    if n_outer:
        grid = (nj, ni, nk)
        ij = lambda g0, g1: (g1, g0)
    else:
        grid = (ni, nj, nk)
        ij = lambda g0, g1: (g0, g1)

    def a_map(g0, g1, k):
        i, _ = ij(g0, g1)
        return (k, i) if mode == 'tn' else (i, k)

    def elems(blk):
        return tuple(pl.Element(s) for s in blk)

    def at(rows, cols):
        return pl.multiple_of(rows, ROW_TILE), pl.multiple_of(cols, LANES)

    a_blk = (tk, tm) if mode == 'tn' else (tm, tk)
    b_blk = (tn, tk) if mode == 'nt' else (tk, tn)
    if b_rows is None:
        def b_map(g0, g1, k):
            _, j = ij(g0, g1)
            return (j, k) if mode == 'nt' else (k, j)
        b_spec = pl.BlockSpec(b_blk, b_map)
    else:
        def b_map(g0, g1, k):
            _, j = ij(g0, g1)
            return at(b_off + j * tn, k * tk) if mode == 'nt' else at(b_off + k * tk, j * tn)
        b_spec = pl.BlockSpec(elems(b_blk), b_map)
    in_specs = [pl.BlockSpec(a_blk, a_map), b_spec]
    args = [a, b]
    has_add = add is not None
    if has_add:
        in_specs.append(pl.BlockSpec((tm, tn), lambda g0, g1, k: ij(g0, g1)))
        args.append(add)
    aliases = {}
    if out is None:
        out_shape = _sds((M, N), out_dtype)
        out_spec = pl.BlockSpec((tm, tn), lambda g0, g1, k: ij(g0, g1))
    else:
        if isinstance(out, int):
            out_shape = _sds((out, N), out_dtype)
        else:
            out_shape = _sds(out.shape, out.dtype)
            in_specs.append(pl.BlockSpec(memory_space=pl.ANY))
            aliases = {len(args): 0}
            args.append(out)

        def o_map(g0, g1, k):
            i, j = ij(g0, g1)
            return at(o_off + i * tm, j * tn)
        out_spec = pl.BlockSpec(elems((tm, tn)), o_map)
    dn = _DN[mode]
    n_in = len(args)

    def body(*refs):
        a_ref, b_ref = refs[0], refs[1]
        add_ref = refs[2] if has_add else None
        o_ref = refs[n_in]
        part = lax.dot_general(a_ref[...].astype(BF16), b_ref[...].astype(BF16), dn, preferred_element_type=F32)

        def finish(r):
            if has_add:
                r = r + add_ref[...]
            o_ref[...] = r.astype(o_ref.dtype)

        if nk == 1:
            finish(part)
        else:
            acc = refs[n_in + 1]
            k = pl.program_id(2)

            @pl.when(k == 0)
            def _():
                acc[...] = part

            @pl.when(k > 0)
            def _():
                acc[...] += part

            @pl.when(k == nk - 1)
            def _():
                finish(acc[...])

    scratch = [pltpu.VMEM((tm, tn), F32)] if nk > 1 else []
    return _call(body, name=name, out_shape=out_shape, grid=grid, in_specs=in_specs, out_specs=out_spec,
                 scratch=scratch, sem=('parallel', 'parallel', 'arbitrary'), aliases=aliases, deps=deps)(*args)


def _rms_fwd(x, g, name, deps=()):
    R, D = x.shape
    tb = _pick(R, (512, 256, 128))

    def body(x_ref, g_ref, o_ref):
        xv = x_ref[...]
        r = lax.rsqrt(jnp.mean(xv * xv, axis=-1, keepdims=True) + EPS)
        o_ref[...] = (xv * r * g_ref[...]).astype(o_ref.dtype)

    row = pl.BlockSpec((tb, D), lambda i: (i, 0))
    vec = pl.BlockSpec((1, D), lambda i: (0, 0))
    return _call(body, name=name, out_shape=_sds((R, D), BF16), grid=(R // tb,), in_specs=[row, vec],
                 out_specs=row, sem=('parallel',), deps=deps)(x, g)


def _rms_bwd(x, g, dh, dres, name, deps=()):
    R, D = x.shape
    tb = _pick(R, (512, 256, 128))
    has_res = dres is not None

    def body(*refs):
        x_ref, g_ref, dh_ref = refs[:3]
        dx_ref, dg_ref = refs[-2:]
        xv = x_ref[...]
        dhv = dh_ref[...].astype(F32)
        r = lax.rsqrt(jnp.mean(xv * xv, axis=-1, keepdims=True) + EPS)
        u = dhv * g_ref[...]
        m = jnp.mean(u * xv, axis=-1, keepdims=True)
        dx = r * u - xv * (r * r * r * m)
        if has_res:
            dx = dx + refs[3][...]
        dx_ref[...] = dx

        @pl.when(pl.program_id(0) == 0)
        def _():
            dg_ref[...] = jnp.zeros_like(dg_ref)

        dg_ref[...] += jnp.sum(dhv * xv * r, axis=0, keepdims=True)

    row = pl.BlockSpec((tb, D), lambda i: (i, 0))
    vec = pl.BlockSpec((1, D), lambda i: (0, 0))
    args = [x, g, dh] + ([dres] if has_res else [])
    return _call(body, name=name, out_shape=(_sds((R, D), F32), _sds((1, D), F32)), grid=(R // tb,),
                 in_specs=[row, vec, row] + ([row] if has_res else []), out_specs=(row, vec),
                 sem=('arbitrary',), deps=deps)(*args)


def _final(x, g, tgt):
    R, D = x.shape
    tb = _pick(R, (512, 256, 128))

    def body(x_ref, g_ref, t_ref, dx_ref, dg_ref, loss_ref):
        xv = x_ref[...]
        gv = g_ref[...]
        r = lax.rsqrt(jnp.mean(xv * xv, axis=-1, keepdims=True) + EPS)
        e = xv * r * gv - t_ref[...]
        row_loss = jnp.mean(e * e, axis=-1, keepdims=True)
        blk_loss = 0.5 * jnp.sum(row_loss, axis=0, keepdims=True)
        dy = e * (1.0 / D)
        u = dy * gv
        m = jnp.mean(u * xv, axis=-1, keepdims=True)
        dx_ref[...] = r * u - xv * (r * r * r * m)

        @pl.when(pl.program_id(0) == 0)
        def _():
            dg_ref[...] = jnp.zeros_like(dg_ref)
            loss_ref[...] = jnp.zeros_like(loss_ref)

        dg_ref[...] += jnp.sum(dy * xv * r, axis=0, keepdims=True)
        loss_ref[...] += jnp.broadcast_to(blk_loss, loss_ref.shape)

    row = pl.BlockSpec((tb, D), lambda i: (i, 0))
    vec = pl.BlockSpec((1, D), lambda i: (0, 0))
    one = pl.BlockSpec((1, LANES), lambda i: (0, 0))
    return _call(body, name='final_loss', out_shape=(_sds((R, D), F32), _sds((1, D), F32), _sds((1, LANES), F32)),
                 grid=(R // tb,), in_specs=[row, vec, row], out_specs=(row, vec, one), sem=('arbitrary',))(x, g, tgt)


def _col(tb, w, cb):
    return pl.BlockSpec((tb, w), lambda i: (i, cb))


def _prev(tb, h, w, cb):
    return pl.BlockSpec((h, w), lambda i: (jnp.maximum(i * (tb // h) - 1, 0), cb))


def _next(tb, h, w, cb, rows):
    return pl.BlockSpec((h, w), lambda i: (jnp.minimum((i + 1) * (tb // h), rows // h - 1), cb))


def _full(shape):
    return pl.BlockSpec(shape, lambda i: (0,) * len(shape))


def _mixa_fwd(z, w):
    T = z.shape[0]
    D = w.shape[1]
    tb = _pick(T, (256, 128))
    H = HALO_A

    def body(ab, ac, au, acp, aup, w_ref, o_ref, ext):
        first = pl.program_id(0) == 0
        ext[0:H, :] = jnp.where(first, 0.0, acp[...].astype(F32) * aup[...].astype(F32))
        ext[H:H + tb, :] = ac[...].astype(F32) * au[...].astype(F32)
        cp = jnp.zeros((tb, D), F32)
        for k in range(CONV_A_W):
            off = H - (CONV_A_W - 1) + k
            cp = cp + ext[off:off + tb, :] * w_ref[k:k + 1, :]
        o_ref[...] = (ab[...].astype(F32) * cp).astype(o_ref.dtype)

    return _call(body, name='mixa_fwd', out_shape=_sds((T, D), BF16), grid=(T // tb,),
                 in_specs=[_col(tb, D, 0), _col(tb, D, 1), _col(tb, D, 2), _prev(tb, H, D, 1), _prev(tb, H, D, 2),
                           _full((8, D))],
                 out_specs=_col(tb, D, 0), scratch=[pltpu.VMEM((H + tb, D), F32)], sem=('parallel',))(z, z, z, z, z, w)


def _mixa_bwd(z, w, dy):
    T = z.shape[0]
    D = w.shape[1]
    tb = _pick(T, (256, 128))
    H = HALO_A
    nb = T // tb

    def body(ab, ac, au, acp, aup, abn, dy_ref, dyn, w_ref, dz_ref, dw_ref, pext, dext):
        i = pl.program_id(0)
        a_b, a_c, a_u = ab[...].astype(F32), ac[...].astype(F32), au[...].astype(F32)
        pext[0:H, :] = jnp.where(i == 0, 0.0, acp[...].astype(F32) * aup[...].astype(F32))
        pext[H:H + tb, :] = a_c * a_u
        dyv = dy_ref[...].astype(F32)
        dcp = dyv * a_b
        dext[0:tb, :] = dcp
        dext[tb:tb + H, :] = jnp.where(i == nb - 1, 0.0, dyn[...].astype(F32) * abn[...].astype(F32))

        @pl.when(i == 0)
        def _():
            dw_ref[...] = jnp.zeros_like(dw_ref)

        cp = jnp.zeros((tb, D), F32)
        dp = jnp.zeros((tb, D), F32)
        for k in range(CONV_A_W):
            off = H - (CONV_A_W - 1) + k
            wk = w_ref[k:k + 1, :]
            pk = pext[off:off + tb, :]
            cp = cp + pk * wk
            dp = dp + dext[CONV_A_W - 1 - k:CONV_A_W - 1 - k + tb, :] * wk
            dw_ref[k:k + 1, :] += jnp.sum(dcp * pk, axis=0, keepdims=True)
        dz_ref[:, 0:D] = (dyv * cp).astype(dz_ref.dtype)
        dz_ref[:, D:2 * D] = (dp * a_u).astype(dz_ref.dtype)
        dz_ref[:, 2 * D:3 * D] = (dp * a_c).astype(dz_ref.dtype)

    return _call(body, name='mixa_bwd', out_shape=(_sds((T, 3 * D), BF16), _sds((8, D), F32)), grid=(nb,),
                 in_specs=[_col(tb, D, 0), _col(tb, D, 1), _col(tb, D, 2), _prev(tb, H, D, 1), _prev(tb, H, D, 2),
                           _next(tb, H, D, 0, T), _col(tb, D, 0), _next(tb, H, D, 0, T), _full((8, D))],
                 out_specs=(_col(tb, 3 * D, 0), _full((8, D))),
                 scratch=[pltpu.VMEM((H + tb, D), F32), pltpu.VMEM((tb + H, D), F32)],
                 sem=('arbitrary',))(z, z, z, z, z, z, dy, dy, w)


def _split3(v):
    hi = v.astype(BF16)
    r1 = v - hi.astype(F32)
    mid = r1.astype(BF16)
    lo = (r1 - mid.astype(F32)).astype(BF16)
    return hi, mid, lo


def _tri_dot(tri, v):
    hi, mid, lo = _split3(v)
    d = lambda p: jnp.dot(tri, p, preferred_element_type=F32)
    return d(hi) + d(mid) + d(lo)


def _fox_c(zf, bf):
    T, H = zf.shape
    tb = LANES

    def body(zf_ref, b_ref, c_ref, carry):
        @pl.when(pl.program_id(0) == 0)
        def _():
            carry[...] = jnp.zeros_like(carry)

        xv = zf_ref[...] + b_ref[...]
        lf = jnp.minimum(xv, 0.0) - jnp.log(1.0 + jnp.exp(-jnp.abs(xv)))
        r = lax.broadcasted_iota(jnp.int32, (tb, tb), 0)
        c = lax.broadcasted_iota(jnp.int32, (tb, tb), 1)
        tri = (r >= c).astype(BF16)
        cs = _tri_dot(tri, lf) + carry[...]
        c_ref[...] = cs
        carry[...] = cs[tb - 1:tb, :]

    blk = pl.BlockSpec((tb, H), lambda i: (i, 0))
    return _call(body, name='fox_cumsum', out_shape=_sds((T, H), F32), grid=(T // tb,),
                 in_specs=[blk, _full((1, H))], out_specs=blk, scratch=[pltpu.VMEM((1, H), F32)],
                 sem=('arbitrary',))(zf, bf)


def _fox_c_bwd(dc, zf, bf):
    T, H = zf.shape
    tb = LANES
    nb = T // tb

    def body(dc_ref, zf_ref, b_ref, dz_ref, db_ref, carry):
        @pl.when(pl.program_id(0) == 0)
        def _():
            carry[...] = jnp.zeros_like(carry)
            db_ref[...] = jnp.zeros_like(db_ref)

        r = lax.broadcasted_iota(jnp.int32, (tb, tb), 0)
        c = lax.broadcasted_iota(jnp.int32, (tb, tb), 1)
        tri = (c >= r).astype(BF16)
        dlf = _tri_dot(tri, dc_ref[...]) + carry[...]
        carry[...] = dlf[0:1, :]
        xv = zf_ref[...] + b_ref[...]
        dz = dlf * _sigmoid(-xv)
        dz_ref[...] = dz
        db_ref[...] += jnp.sum(dz, axis=0, keepdims=True)

    blk = pl.BlockSpec((tb, H), lambda i: (nb - 1 - i, 0))
    return _call(body, name='fox_cumsum_bwd', out_shape=(_sds((T, H), F32), _sds((1, H), F32)), grid=(nb,),
                 in_specs=[blk, blk, _full((1, H))], out_specs=(blk, _full((1, H))),
                 scratch=[pltpu.VMEM((1, H), F32)], sem=('arbitrary',))(dc, zf, bf)


N_PIECES = 3


def _head_col(c_ref, h):
    lane = lax.broadcasted_iota(jnp.int32, c_ref.shape, 1)
    return jnp.sum(jnp.where(lane == h, c_ref[...], 0.0), axis=1, keepdims=True)


def _pieces(v, sign=1.0):
    return [sign * p.astype(F32) for p in _split3(v)]


def _augment(block, hh, dh, a_cols, b_cols):
    lane = lax.broadcasted_iota(jnp.int32, block.shape, 1)
    base = (1 - hh) * dh
    out = jnp.where((lane >= hh * dh) & (lane < (hh + 1) * dh), block.astype(F32), 0.0)
    for n, col in enumerate(list(a_cols) + list(b_cols)):
        out = jnp.where(lane == base + n, col, out)
    return out.astype(BF16)


def _fox_probs(q_aug, k_aug, tri, q0, ke, shift=None):
    s = lax.dot_general(q_aug[q0:ke, :], k_aug[0:ke, :], _DN['nt'], preferred_element_type=F32)
    s_dg = jnp.where(tri, s[:, q0:ke], -jnp.inf)
    if shift is None:
        shift = jnp.max(s_dg, axis=-1, keepdims=True)
        if q0:
            shift = jnp.maximum(shift, jnp.max(s[:, 0:q0], axis=-1, keepdims=True))
    parts = ([jnp.exp(s[:, 0:q0] - shift)] if q0 else []) + [jnp.exp(s_dg - shift)]
    return parts, shift


def _cat(parts):
    parts = [p.astype(BF16) for p in parts]
    return parts[0] if len(parts) == 1 else jnp.concatenate(parts, axis=1)


def _fox_fwd(z, c):
    T = z.shape[0]
    D = z.shape[1] // 6
    dh = D // FOX_HEADS
    hp = LANES // dh
    ncb = D // LANES
    tq = _pick(T, (256, 128))
    scale = dh ** -0.5
    ones = [1.0] * N_PIECES

    def body(q_ref, k_ref, v_ref, c_ref, o_ref, lse_ref, o_all):
        j = pl.program_id(0)
        lse_ref[...] = jnp.zeros_like(lse_ref)
        r = lax.broadcasted_iota(jnp.int32, (tq, tq), 0)
        tri = r >= lax.broadcasted_iota(jnp.int32, (tq, tq), 1)
        for hh in range(hp):
            base = (1 - hh) * dh
            ch = _head_col(c_ref, j * hp + hh)
            q_aug = _augment(q_ref[...].astype(F32) * scale, hh, dh, _pieces(ch), ones)
            k_aug = _augment(k_ref[...], hh, dh, ones, _pieces(ch, -1.0))
            v_aug = _augment(v_ref[...], hh, dh, ones, [])
            for q0 in range(0, T, tq):
                ke = q0 + tq
                parts, m = _fox_probs(q_aug, k_aug, tri, q0, ke)
                o_aug = jnp.dot(_cat(parts), v_aug[0:ke, :], preferred_element_type=F32)
                l = o_aug[:, base:base + 1]
                o_all[hh, q0:ke, :] = o_aug * (1.0 / l)
                lse_ref[q0:ke, hh:hh + 1] = m + jnp.log(l)
        lane = lax.broadcasted_iota(jnp.int32, (T, LANES), 1)
        out = o_all[0]
        for hh in range(1, hp):
            out = jnp.where(lane >= hh * dh, o_all[hh], out)
        o_ref[...] = out.astype(o_ref.dtype)

    blk = lambda cb0: pl.BlockSpec((T, LANES), lambda j: (0, cb0 + j))
    return _call(body, name='fox_fwd', out_shape=(_sds((T, D), BF16), _sds((ncb, T, LANES), F32)), grid=(ncb,),
                 in_specs=[blk(3 * ncb), blk(4 * ncb), blk(5 * ncb), _full((T, FOX_HEADS))],
                 out_specs=(blk(0), pl.BlockSpec((None, T, LANES), lambda j: (j, 0, 0))),
                 scratch=[pltpu.VMEM((hp, T, LANES), F32)], sem=('parallel',))(z, z, z, c)


def _fox_bwd(z, c, o, lse, do):
    T = z.shape[0]
    D = o.shape[1]
    dh = D // FOX_HEADS
    hp = LANES // dh
    ncb = D // LANES
    tq = _pick(T, (256, 128))
    scale = dh ** -0.5
    ones = [1.0] * N_PIECES

    def body(q_ref, k_ref, v_ref, c_ref, o_ref, lse_ref, do_ref, dq_ref, dk_ref, dv_ref, dc_ref, dq_all, dk_acc,
             dv_acc):
        j = pl.program_id(0)
        dk_acc[...] = jnp.zeros_like(dk_acc)
        dv_acc[...] = jnp.zeros_like(dv_acc)
        dc_ref[...] = jnp.zeros_like(dc_ref)
        r = lax.broadcasted_iota(jnp.int32, (tq, tq), 0)
        tri = r >= lax.broadcasted_iota(jnp.int32, (tq, tq), 1)
        lane = lax.broadcasted_iota(jnp.int32, (T, LANES), 1)
        for hh in range(hp):
            base = (1 - hh) * dh
            own = (lane >= hh * dh) & (lane < (hh + 1) * dh)
            ch = _head_col(c_ref, j * hp + hh)
            dov = do_ref[...].astype(F32)
            delta = jnp.sum(jnp.where(own, dov * o_ref[...].astype(F32), 0.0), axis=1, keepdims=True)
            q_aug = _augment(q_ref[...].astype(F32) * scale, hh, dh, _pieces(ch - lse_ref[:, hh:hh + 1]), ones)
            k_aug = _augment(k_ref[...], hh, dh, ones, _pieces(ch, -1.0))
            v_aug = _augment(v_ref[...], hh, dh, ones, [])
            do_aug = _augment(dov, hh, dh, _pieces(delta, -1.0), [])
            for q0 in range(0, T, tq):
                ke = q0 + tq
                p, _ = _fox_probs(q_aug, k_aug, tri, q0, ke, shift=0.0)
                dp = lax.dot_general(do_aug[q0:ke, :], v_aug[0:ke, :], _DN['nt'], preferred_element_type=F32)
                ds = [p[0] * dp[:, 0:q0], p[1] * dp[:, q0:ke]] if q0 else [p[0] * dp]
                dsb, pb = _cat(ds), _cat(p)
                dq_all[hh, q0:ke, :] = jnp.dot(dsb, k_aug[0:ke, :], preferred_element_type=F32)
                dk_acc[hh, 0:ke, :] += lax.dot_general(dsb, q_aug[q0:ke, :], _DN['tn'], preferred_element_type=F32)
                dv_acc[hh, 0:ke, :] += lax.dot_general(pb, do_aug[q0:ke, :], _DN['tn'], preferred_element_type=F32)
            dc_ref[:, hh:hh + 1] = dq_all[hh, :, base:base + 1] - dk_acc[hh, :, base + N_PIECES:base + N_PIECES + 1]
        dq, dk, dv = dq_all[0], dk_acc[0], dv_acc[0]
        for hh in range(1, hp):
            dq = jnp.where(lane >= hh * dh, dq_all[hh], dq)
            dk = jnp.where(lane >= hh * dh, dk_acc[hh], dk)
            dv = jnp.where(lane >= hh * dh, dv_acc[hh], dv)
        dq_ref[...] = (dq * scale).astype(dq_ref.dtype)
        dk_ref[...] = dk.astype(dk_ref.dtype)
        dv_ref[...] = dv.astype(dv_ref.dtype)

    blk = lambda cb0: pl.BlockSpec((T, LANES), lambda j: (0, cb0 + j))
    pair = pl.BlockSpec((None, T, LANES), lambda j: (j, 0, 0))
    grad = _sds((T, D), BF16)
    acc = pltpu.VMEM((hp, T, LANES), F32)
    return _call(body, name='fox_bwd', out_shape=(grad, grad, grad, _sds((ncb, T, LANES), F32)), grid=(ncb,),
                 in_specs=[blk(3 * ncb), blk(4 * ncb), blk(5 * ncb), _full((T, FOX_HEADS)), blk(0), pair, blk(0)],
                 out_specs=(blk(0), blk(0), blk(0), pair), scratch=[acc, acc, acc],
                 sem=('parallel',))(z, z, z, c, o, lse, do)


def _ln_parts(u1, gain, bias):
    mu = jnp.mean(u1, axis=-1, keepdims=True)
    xc = u1 - mu
    rstd = lax.rsqrt(jnp.mean(xc * xc, axis=-1, keepdims=True) + EPS)
    xhat = xc * rstd
    return xhat, rstd, xhat * gain + bias


def _tap_groups(offsets):
    groups = {}
    for k, off in enumerate(offsets):
        groups.setdefault(off % SUBLANES, []).append((off - off % SUBLANES, k))
    return groups


def _for_taps(src_ref, r0, cols, groups, aligned, visit):
    for r, taps in groups.items():
        if r == 0:
            for base, k in taps:
                visit(k, src_ref[r0 + base:r0 + base + CONV_ROWS, cols])
            continue
        need = max(base for base, _ in taps) + CONV_ROWS
        aligned[r, 0:need, :] = src_ref[r0 + r:r0 + r + need, cols]
        for base, k in taps:
            visit(k, aligned[r, base:base + CONV_ROWS, :])


def _row_chunk(j, first=0):
    return slice(first + j * ROW_CHUNK, first + (j + 1) * ROW_CHUNK)


def _glu_rows(cv, cg, ext, halo, tb):
    for j in range(tb // ROW_CHUNK):
        rs = _row_chunk(j)
        ext[_row_chunk(j, halo), :] = cv[rs, :].astype(F32) * _sigmoid(cg[rs, :].astype(F32))


def _mixc_fwd(z, w, cb, gain, bias):
    T = z.shape[0]
    D = w.shape[1]
    tb = _pick(T, (256, 128))
    H = HALO_C

    groups = _tap_groups([H - (CONV_C_W - 1) + k for k in range(CONV_C_W)])

    def body(cv, cg, cvp, cgp, w_ref, cb_ref, g_ref, b_ref, o_ref, u1_ref, ext, aligned):
        first = pl.program_id(0) == 0
        ext[0:H, :] = jnp.where(first, 0.0, cvp[...].astype(F32) * _sigmoid(cgp[...].astype(F32)))
        _glu_rows(cv, cg, ext, H, tb)
        for r0 in range(0, tb, CONV_ROWS):
            for c0 in range(0, D, LANES):
                cols = slice(c0, c0 + LANES)
                acc = [jnp.zeros((CONV_ROWS, LANES), F32)]

                def tap(k, rows):
                    acc[0] = acc[0] + rows * w_ref[k:k + 1, cols]

                _for_taps(ext, r0, cols, groups, aligned, tap)
                u1_ref[r0:r0 + CONV_ROWS, cols] = acc[0] + cb_ref[:, cols]

        for j in range(tb // ROW_CHUNK):
            rs = _row_chunk(j)
            _, _, u2 = _ln_parts(u1_ref[rs, :], g_ref[...], b_ref[...])
            o_ref[rs, :] = (u2 * _sigmoid(u2)).astype(o_ref.dtype)

    vec = _full((1, D))
    row = _col(tb, D, 0)
    return _call(body, name='mixc_fwd', out_shape=(_sds((T, D), BF16), _sds((T, D), F32)), grid=(T // tb,),
                 in_specs=[_col(tb, D, 0), _col(tb, D, 1), _prev(tb, H, D, 0), _prev(tb, H, D, 1), _full((32, D)), vec,
                           vec, vec],
                 out_specs=(row, row), scratch=[pltpu.VMEM((H + tb, D), F32), pltpu.VMEM((SUBLANES, CONV_ROWS + H, LANES), F32)],
                 sem=('parallel',))(z, z, z, z, w, cb, gain, bias)


def _mixc_bwd(z, w, gain, bias, u1, du3):
    T = z.shape[0]
    D = w.shape[1]
    tb = _pick(T, (256, 128))
    H = HALO_C
    nb = T // tb

    def du1_of(u1v, du3v, gv, bv):
        xhat, rstd, u2 = _ln_parts(u1v, gv, bv)
        sg = _sigmoid(u2)
        du2 = du3v * (sg * (1.0 + u2 * (1.0 - sg)))
        dxh = du2 * gv
        m1 = jnp.mean(dxh, axis=-1, keepdims=True)
        m2 = jnp.mean(dxh * xhat, axis=-1, keepdims=True)
        return rstd * (dxh - m1 - xhat * m2), du2, xhat

    fwd_groups = _tap_groups([H - (CONV_C_W - 1) + k for k in range(CONV_C_W)])
    bwd_groups = _tap_groups([CONV_C_W - 1 - k for k in range(CONV_C_W)])

    def body(cv, cg, cvp, cgp, w_ref, g_ref, b_ref, u1_ref, u1n, du3_ref, du3n, dz_ref, dw_ref, dcb_ref, dg_ref,
             db_ref, uext, dext, dw_acc, aligned, sums):
        i = pl.program_id(0)
        uext[0:H, :] = jnp.where(i == 0, 0.0, cvp[...].astype(F32) * _sigmoid(cgp[...].astype(F32)))
        _glu_rows(cv, cg, uext, H, tb)
        du1n, _, _ = du1_of(u1n[...], du3n[...].astype(F32), g_ref[...], b_ref[...])
        dext[tb:tb + H, :] = jnp.where(i == nb - 1, 0.0, du1n)

        @pl.when(i == 0)
        def _():
            dw_acc[...] = jnp.zeros_like(dw_acc)
            sums[...] = jnp.zeros_like(sums)

        def halves(v):
            part = v[0:SUBLANES, :]
            for q in range(SUBLANES, ROW_CHUNK, SUBLANES):
                part = part + v[q:q + SUBLANES, :]
            return part

        for j in range(tb // ROW_CHUNK):
            rs = _row_chunk(j)
            du1, du2, xhat = du1_of(u1_ref[rs, :], du3_ref[rs, :].astype(F32), g_ref[...], b_ref[...])
            dext[rs, :] = du1
            sums[0:SUBLANES, :] += halves(du1)
            sums[SUBLANES:2 * SUBLANES, :] += halves(du2 * xhat)
            sums[2 * SUBLANES:3 * SUBLANES, :] += halves(du2)
        for r0 in range(0, tb, CONV_ROWS):
            rows = slice(r0, r0 + CONV_ROWS)
            for c0 in range(0, D, LANES):
                cols = slice(c0, c0 + LANES)
                acc = [jnp.zeros((CONV_ROWS, LANES), F32)]

                def tap(k, src):
                    acc[0] = acc[0] + src * w_ref[k:k + 1, cols]

                _for_taps(dext, r0, cols, bwd_groups, aligned, tap)
                c_val = cv[rows, cols].astype(F32)
                sg = _sigmoid(cg[rows, cols].astype(F32))
                dz_ref[rows, cols] = (acc[0] * sg).astype(dz_ref.dtype)
                dz_ref[rows, D + c0:D + c0 + LANES] = (acc[0] * c_val * sg * (1.0 - sg)).astype(dz_ref.dtype)
                du1_blk = dext[rows, cols]

                def tap_w(k, src):
                    prod = du1_blk * src
                    part = prod[0:SUBLANES, :]
                    for j in range(SUBLANES, CONV_ROWS, SUBLANES):
                        part = part + prod[j:j + SUBLANES, :]
                    dw_acc[k * SUBLANES:(k + 1) * SUBLANES, cols] += part

                _for_taps(uext, r0, cols, fwd_groups, aligned, tap_w)

        @pl.when(i == nb - 1)
        def _():
            dw_ref[...] = jnp.zeros_like(dw_ref)
            for k in range(CONV_C_W):
                dw_ref[k:k + 1, :] = jnp.sum(dw_acc[k * SUBLANES:(k + 1) * SUBLANES, :], axis=0, keepdims=True)
            for n, ref in enumerate((dcb_ref, dg_ref, db_ref)):
                ref[...] = jnp.sum(sums[n * SUBLANES:(n + 1) * SUBLANES, :], axis=0, keepdims=True)

    vec = _full((1, D))
    row = _col(tb, D, 0)
    nxt = _next(tb, H, D, 0, T)
    return _call(body, name='mixc_bwd',
                 out_shape=(_sds((T, 2 * D), BF16), _sds((32, D), F32), _sds((1, D), F32), _sds((1, D), F32),
                            _sds((1, D), F32)), grid=(nb,),
                 in_specs=[_col(tb, D, 0), _col(tb, D, 1), _prev(tb, H, D, 0), _prev(tb, H, D, 1), _full((32, D)), vec,
                           vec, row, nxt, row, nxt],
                 out_specs=(_col(tb, 2 * D, 0), _full((32, D)), vec, vec, vec),
                 scratch=[pltpu.VMEM((H + tb, D), F32), pltpu.VMEM((tb + H, D), F32),
                          pltpu.VMEM((32 * SUBLANES, D), F32), pltpu.VMEM((SUBLANES, CONV_ROWS + H, LANES), F32),
                          pltpu.VMEM((3 * SUBLANES, D), F32)],
                 sem=('arbitrary',))(z, z, z, z, w, gain, bias, u1, u1, du3, du3)


def _gate_fwd(z, bg, ya, yb, yc):
    T = z.shape[0]
    D = ya.shape[1]
    tb = _pick(T, (256, 128))

    def body(ga, gb, gc, bg_ref, ya_ref, yb_ref, yc_ref, o_ref):
        acc = jnp.zeros((tb, D), F32)
        for n, (g, y) in enumerate(((ga, ya_ref), (gb, yb_ref), (gc, yc_ref))):
            acc = acc + _sigmoid(g[...].astype(F32) + bg_ref[:, n * D:(n + 1) * D]) * y[...].astype(F32)
        o_ref[...] = acc.astype(o_ref.dtype)

    row = _col(tb, D, 0)
    return _call(body, name='gate_fwd', out_shape=_sds((T, D), BF16), grid=(T // tb,),
                 in_specs=[_col(tb, D, 2), _col(tb, D, 3), _col(tb, D, 4), _full((1, 3 * D)), row, row, row],
                 out_specs=row, sem=('parallel',))(z, z, z, bg, ya, yb, yc)


def _gate_bwd(z, bg, ya, yb, yc, dm):
    T = z.shape[0]
    D = ya.shape[1]
    tb = _pick(T, (256, 128))

    def body(ga, gb, gc, bg_ref, ya_ref, yb_ref, yc_ref, dm_ref, dya, dyb, dyc, dg_ref, dbg_ref):
        @pl.when(pl.program_id(0) == 0)
        def _():
            dbg_ref[...] = jnp.zeros_like(dbg_ref)

        dmv = dm_ref[...]
        for n, (g, y, dy) in enumerate(((ga, ya_ref, dya), (gb, yb_ref, dyb), (gc, yc_ref, dyc))):
            cols = slice(n * D, (n + 1) * D)
            sg = _sigmoid(g[...].astype(F32) + bg_ref[:, cols])
            dy[...] = (dmv * sg).astype(dy.dtype)
            dg = dmv * y[...].astype(F32) * sg * (1.0 - sg)
            dg_ref[:, cols] = dg.astype(dg_ref.dtype)
            dbg_ref[:, cols] += jnp.sum(dg, axis=0, keepdims=True)

    row = _col(tb, D, 0)
    act = _sds((T, D), BF16)
    return _call(body, name='gate_bwd', out_shape=(act, act, act, _sds((T, 3 * D), BF16), _sds((1, 3 * D), F32)),
                 grid=(T // tb,),
                 in_specs=[_col(tb, D, 2), _col(tb, D, 3), _col(tb, D, 4), _full((1, 3 * D)), row, row, row, row],
                 out_specs=(row, row, row, _col(tb, 3 * D, 0), _full((1, 3 * D))),
                 sem=('arbitrary',))(z, z, z, bg, ya, yb, yc, dm)


def _xattn_probs(qs, ks, scale):
    s = lax.dot_general(qs, ks, _DN['nt'], preferred_element_type=F32) * scale
    p = jnp.exp(s - jnp.max(s, axis=-1, keepdims=True))
    return p * (1.0 / jnp.sum(p, axis=-1, keepdims=True))


def _xattn_fwd(q, kv):
    T, DX = q.shape
    M = kv.shape[0]
    dh = DX // X_HEADS
    tb = _pick(T, (512, 256, 128))
    scale = dh ** -0.5

    def body(q_ref, kv_ref, o_ref):
        for h in range(X_HEADS):
            hs = slice(h * dh, (h + 1) * dh)
            p = _xattn_probs(q_ref[:, hs], kv_ref[:, hs], scale)
            o_ref[:, hs] = jnp.dot(p.astype(BF16), kv_ref[:, DX + h * dh:DX + (h + 1) * dh],
                                   preferred_element_type=F32).astype(o_ref.dtype)

    row = _col(tb, DX, 0)
    return _call(body, name='xattn_fwd', out_shape=_sds((T, DX), BF16), grid=(T // tb,),
                 in_specs=[row, _full((M, 2 * DX))], out_specs=row, sem=('parallel',))(q, kv)


def _xattn_bwd(q, kv, do):
    T, DX = q.shape
    M = kv.shape[0]
    dh = DX // X_HEADS
    tb = _pick(T, (512, 256, 128))
    scale = dh ** -0.5

    def body(q_ref, kv_ref, do_ref, dq_ref, dkv_ref):
        @pl.when(pl.program_id(0) == 0)
        def _():
            dkv_ref[...] = jnp.zeros_like(dkv_ref)

        for h in range(X_HEADS):
            hs = slice(h * dh, (h + 1) * dh)
            vs_cols = slice(DX + h * dh, DX + (h + 1) * dh)
            qs, ks, vs = q_ref[:, hs], kv_ref[:, hs], kv_ref[:, vs_cols]
            dos = do_ref[:, hs].astype(BF16)
            p = _xattn_probs(qs, ks, scale)
            dp = lax.dot_general(dos, vs, _DN['nt'], preferred_element_type=F32)
            ds = p * (dp - jnp.sum(p * dp, axis=-1, keepdims=True))
            dsb = ds.astype(BF16)
            dq_ref[:, hs] = (jnp.dot(dsb, ks, preferred_element_type=F32) * scale).astype(dq_ref.dtype)
            dkv_ref[:, hs] += lax.dot_general(dsb, qs, _DN['tn'], preferred_element_type=F32) * scale
            dkv_ref[:, vs_cols] += lax.dot_general(p.astype(BF16), dos, _DN['tn'], preferred_element_type=F32)

    row = _col(tb, DX, 0)
    return _call(body, name='xattn_bwd', out_shape=(_sds((T, DX), BF16), _sds((M, 2 * DX), F32)), grid=(T // tb,),
                 in_specs=[row, _full((M, 2 * DX)), row], out_specs=(row, _full((M, 2 * DX))),
                 sem=('arbitrary',))(q, kv, do)


def _swiglu_fwd(gu):
    T, F2 = gu.shape
    F = F2 // 2
    tb = _pick(T, (256, 128))

    def body(g_ref, u_ref, o_ref):
        g = g_ref[...].astype(F32)
        o_ref[...] = (g * _sigmoid(g) * u_ref[...].astype(F32)).astype(o_ref.dtype)

    return _call(body, name='swiglu_fwd', out_shape=_sds((T, F), BF16), grid=(T // tb,),
                 in_specs=[_col(tb, F, 0), _col(tb, F, 1)], out_specs=_col(tb, F, 0), sem=('parallel',))(gu, gu)


def _swiglu_bwd(gu, da):
    T, F2 = gu.shape
    F = F2 // 2
    tb = _pick(T, (256, 128))

    def body(g_ref, u_ref, da_ref, o_ref):
        g = g_ref[...].astype(F32)
        u = u_ref[...].astype(F32)
        dav = da_ref[...].astype(F32)
        sg = _sigmoid(g)
        o_ref[:, 0:F] = (dav * u * (sg * (1.0 + g * (1.0 - sg)))).astype(o_ref.dtype)
        o_ref[:, F:F2] = (dav * g * sg).astype(o_ref.dtype)

    return _call(body, name='swiglu_bwd', out_shape=_sds((T, F2), BF16), grid=(T // tb,),
                 in_specs=[_col(tb, F, 0), _col(tb, F, 1), _col(tb, F, 0)], out_specs=_col(tb, F2, 0),
                 sem=('parallel',))(gu, gu, da)


def _place():
    x, y, c = lax.axis_index('x'), lax.axis_index('y'), lax.axis_index('c')
    return x, y, c


def _allgather(arrs, lead, name):
    n = len(arrs)

    def out_shape(a, ld):
        return (N_DEV,) + a.shape if ld else (a.shape[0], N_DEV) + a.shape[1:]

    def body(*refs):
        ins, outs = refs[:n], refs[n:2 * n]
        send_sems, recv_sems, local_sems = refs[2 * n:]
        x, y, c = _place()
        me, sibling = (x, y, c), (x, y, 1 - c)
        chips = [(1 - x, y), (x, 1 - y), (1 - x, 1 - y)]

        def slot(a, dev):
            idx = 4 * dev[0] + 2 * dev[1] + dev[2]
            return outs[a].at[idx] if lead[a] else outs[a].at[:, idx]

        def copy(a, k, block, to, src=None):
            return pltpu.make_async_remote_copy(
                src_ref=slot(a, block) if src is None else src, dst_ref=slot(a, block),
                send_sem=send_sems.at[a * 7 + k], recv_sem=recv_sems.at[a * 7 + k], device_id=to, device_id_type=MESH)

        mine = [pltpu.make_async_copy(ins[a], slot(a, me), local_sems.at[a]) for a in range(n)]
        for cp in mine:
            cp.start()
        first = []
        for a in range(n):
            first.append(copy(a, 0, me, sibling, src=ins[a]))
            first += [copy(a, 1 + j, me, (*chip, c), src=ins[a]) for j, chip in enumerate(chips)]
        for cp in first:
            cp.start()
        passed = []
        for a in range(n):
            for j, chip in enumerate(chips):
                copy(a, 1 + j, (*chip, c), me).wait_recv()
                fwd = copy(a, 4 + j, (*chip, c), sibling)
                fwd.start()
                passed.append(fwd)
        for a in range(n):
            copy(a, 0, sibling, me).wait_recv()
            for j, chip in enumerate(chips):
                copy(a, 4 + j, (*chip, 1 - c), me).wait_recv()
        for cp in first + passed:
            cp.wait_send()
        for cp in mine:
            cp.wait()

    any_spec = pl.BlockSpec(memory_space=pl.ANY)
    return _call(body, name=name, out_shape=tuple(_sds(out_shape(a, ld), a.dtype) for a, ld in zip(arrs, lead)),
                 in_specs=[any_spec] * n, out_specs=tuple([any_spec] * n),
                 scratch=[pltpu.SemaphoreType.DMA((7 * n,)), pltpu.SemaphoreType.DMA((7 * n,)),
                          pltpu.SemaphoreType.DMA((n,))])(*arrs)


_PEER_ORDER = (1, 4, 2, 5, 3, 6, 7)
_HBM = pl.BlockSpec(memory_space=pltpu.HBM)
_SEM = pl.BlockSpec(memory_space=pltpu.SEMAPHORE)


def _my_index():
    x, y, c = _place()
    return 4 * x + 2 * y + c


def _remote(src, dst, send_sems, recv_sems, k, dev):
    return pltpu.make_async_remote_copy(src_ref=src, dst_ref=dst, send_sem=send_sems.at[k], recv_sem=recv_sems.at[k],
                                        device_id=dev, device_id_type=MESH)


def _own_part(src, kind, args):
    return src if args is None else src.at[pl.ds(args[0], args[1])]


def _block(land, kind, args, owner):
    if kind == 'rows':
        _, n, stride, d0 = args
        return land.at[pl.ds(pl.multiple_of(stride * owner + d0, ROW_TILE), n)]
    return land.at[owner]


def _chunk(src, args, dest):
    stride, n = args
    return src.at[pl.ds(pl.multiple_of(stride * dest, ROW_TILE), n)]


def _scatter_copies(plan, srcs, lands, send_sems, recv_sems):
    x, y, c = _place()
    me = 4 * x + 2 * y + c
    copies = []
    for r in _PEER_ORDER:
        px, py, pc = x ^ (r >> 2), y ^ ((r >> 1) & 1), c ^ (r & 1)
        for si, li, _, args in plan:
            copies.append(_remote(_chunk(srcs[si], args, 4 * px + 2 * py + pc), lands[li].at[me], send_sems, recv_sems,
                                  r - 1, (px, py, pc)))
    return copies


def _gather_copies(plan, srcs, lands, send_sems, recv_sems, passing):
    x, y, c = _place()
    me = 4 * x + 2 * y + c
    sibling = (x, y, 1 - c)
    chips = [(1 - x, y), (x, 1 - y), (1 - x, 1 - y)]
    copies = []
    if passing:
        for k, chip in enumerate(chips):
            for _, li, kind, args in plan:
                blk = _block(lands[li], kind, args, 4 * chip[0] + 2 * chip[1] + c)
                copies.append((k, _remote(blk, blk, send_sems, recv_sems, k, sibling)))
    else:
        for k, dev in enumerate([sibling] + [(*chip, c) for chip in chips]):
            for si, li, kind, args in plan:
                dst = _block(lands[li], kind, args, me)
                src = dst if srcs is None else _own_part(srcs[si], kind, args)
                copies.append((k, _remote(src, dst, send_sems, recv_sems, k, dev)))
    return copies


def _sem_call(name, bufs, sems_in, new_sems, after, action):
    nb, ni, nn = len(bufs), len(sems_in), len(new_sems)

    def body(*refs):
        action(refs[:nb], refs[nb:nb + ni], refs[-1 - nb - nn:-1 - nb])
        refs[-1][...] = jnp.zeros_like(refs[-1])

    bufs = [pltpu.with_memory_space_constraint(v, pltpu.HBM) for v in bufs]
    res = pl.pallas_call(
        body, name=name,
        out_shape=(*[pltpu.SemaphoreType.DMA((n,)) for n in new_sems], *[pltpu.HBM(v.shape, v.dtype) for v in bufs],
                   _sds((8, LANES), F32)),
        in_specs=[_HBM] * nb + [_SEM] * ni + [pl.BlockSpec(memory_space=pl.ANY)] * len(after),
        out_specs=(*[_SEM] * nn, *[_HBM] * nb, pl.BlockSpec(memory_space=pltpu.VMEM)),
        input_output_aliases={i: nn + i for i in range(nb)},
        compiler_params=pltpu.CompilerParams(has_side_effects=pltpu.SideEffectType.DATAFLOW_SIDE_EFFECTING),
        interpret=False)(*bufs, *sems_in, *after)
    return list(res[:nn]), list(res[nn:nn + nb]), res[-1]


def _place_own(srcs, plan, lands_like, name):
    ns, nl, ne = len(srcs), len(lands_like), len(plan)

    def part_shape(entry):
        si, _, kind, args = entry
        rows = args[1] if (kind == 'chunk' or args is not None) else None
        return srcs[si].shape if rows is None else (rows,) + srcs[si].shape[1:]

    def body(*refs):
        src_r, land_r = refs[:ns], refs[ns:ns + nl]
        stage, sems_in, sems_out = refs[ns + nl:ns + nl + ne], refs[-2], refs[-1]
        me = _my_index()
        loads, stores = [], []
        for i, (si, li, kind, args) in enumerate(plan):
            if kind == 'chunk':
                src, dst = _chunk(src_r[si], args, me), land_r[li].at[me]
            else:
                src, dst = _own_part(src_r[si], kind, args), _block(land_r[li], kind, args, me)
            loads.append(pltpu.make_async_copy(src, stage[i], sems_in.at[i]))
            stores.append(pltpu.make_async_copy(stage[i], dst, sems_out.at[i]))
        for cp in loads:
            cp.start()
        for load, store in zip(loads, stores):
            load.wait()
            store.start()
        for cp in stores:
            cp.wait()

    anywhere = pl.BlockSpec(memory_space=pl.ANY)
    scratch = [pltpu.VMEM(part_shape(e), srcs[e[0]].dtype) for e in plan]
    scratch += [pltpu.SemaphoreType.DMA((ne,)), pltpu.SemaphoreType.DMA((ne,))]
    return list(_call(body, name=name, out_shape=tuple(lands_like), in_specs=[anywhere] * ns,
                      out_specs=tuple([anywhere] * nl), scratch=scratch)(*srcs))


def _scatter_start(srcs, lands, plan, after, name):
    ns = len(srcs)

    def action(bufs, _, new):
        for cp in _scatter_copies(plan, bufs[:ns], bufs[ns:], new[0], new[1]):
            cp.start()

    sems, bufs, token = _sem_call(name, [*srcs, *lands], [], [N_DEV - 1, N_DEV - 1], after, action)
    return sems, bufs[:ns], bufs[ns:], token


def _scatter_wait(sems, srcs, lands, plan, after, name):
    ns = len(srcs)

    def action(bufs, old, _):
        copies = _scatter_copies(plan, bufs[:ns], bufs[ns:], old[0], old[1])
        for cp in copies:
            cp.wait_send()
        for cp in copies:
            cp.wait_recv()

    return _sem_call(name, [*srcs, *lands], sems, [], after, action)[1][ns:]


def _gather_start(srcs, lands, plan, name):
    ns = len(srcs)

    def action(bufs, _, new):
        for _, cp in _gather_copies(plan, bufs[:ns], bufs[ns:], new[0], new[1], False):
            cp.start()

    sems, bufs, token = _sem_call(name, [*srcs, *lands], [], [4, 4], [], action)
    return sems, bufs[:ns], bufs[ns:], token


def _gather_pass(recv_sems, lands, plan, after, name):
    def action(bufs, old, new):
        for _, cp in _gather_copies(plan, None, bufs, old[0], old[0], False):
            cp.wait_recv()
        for _, cp in _gather_copies(plan, None, bufs, new[0], new[1], True):
            cp.start()

    return _sem_call(name, lands, [recv_sems], [3, 3], after, action)


def _gather_wait(send_sems, pass_sems, srcs, lands, plan, after, name):
    ns = len(srcs)

    def action(bufs, old, _):
        for _, cp in _gather_copies(plan, bufs[:ns], bufs[ns:], old[0], old[0], False):
            cp.wait_send()
        passed = _gather_copies(plan, None, bufs[ns:], old[1], old[2], True)
        for _, cp in passed:
            cp.wait_send()
        for _, cp in passed:
            cp.wait_recv()

    return _sem_call(name, [*srcs, *lands], [send_sems, *pass_sems], [], after, action)[1][ns:]


def _merge_edges(wt, edges, tiles_per_dev):
    D = wt.shape[1]

    def body(w_ref, e_ref, o_ref):
        s = pl.program_id(0)
        o_ref[...] = jnp.where(s == 0, e_ref[...], w_ref[...] + e_ref[...])

    tile = pl.BlockSpec((ROW_TILE, D), lambda s: (s * tiles_per_dev, 0))
    return _call(body, name='merge_edges', out_shape=_sds(wt.shape, wt.dtype), grid=(N_DEV,),
                 in_specs=[tile, pl.BlockSpec((None, ROW_TILE, D), lambda s: (s, 0, 0))], out_specs=tile,
                 sem=('arbitrary',), aliases={0: 0})(wt, edges)


def _adam_update(g, w, m, v):
    c1 = 1.0 - ADAM_B1 ** ADAM_STEP
    c2 = 1.0 - ADAM_B2 ** ADAM_STEP
    mn = ADAM_B1 * m + (1.0 - ADAM_B1) * g
    vn = ADAM_B2 * v + (1.0 - ADAM_B2) * (g * g)
    m_hat = mn / c1
    v_hat = vn / c2
    return -ADAM_LR * (m_hat / (jnp.sqrt(v_hat) + ADAM_EPS) + ADAM_WD * w), mn, vn


def _adamw(parts, w, m, v, name):
    n_parts = parts.shape[0]
    R, C = w.shape
    tb = _pick(R, (128, 64, 32, 16, 8))

    def body(p_ref, w_ref, m_ref, v_ref, g_out, d_out, m_out, v_out):
        g = p_ref[0].astype(F32)
        for s in range(1, n_parts):
            g = g + p_ref[s].astype(F32)
        g_out[...] = g
        d_out[...], m_out[...], v_out[...] = _adam_update(g, w_ref[...], m_ref[...], v_ref[...])

    row = pl.BlockSpec((tb, C), lambda i: (i, 0))
    o = _sds((R, C), F32)
    return _call(body, name=name, out_shape=(o, o, o, o), grid=(R // tb,),
                 in_specs=[pl.BlockSpec((n_parts, tb, C), lambda i: (0, i, 0)), row, row, row],
                 out_specs=(row, row, row, row), sem=('parallel',))(parts, w, m, v)


def _adamw_layer(parts, w, m, v, layer, prev, shift, name):
    L, n, C = w.shape
    n_p = parts.shape[1]
    cb = C if N_DEV * n_p * C * parts.dtype.itemsize <= (6 << 20) else 2 * LANES
    n_prev = 0 if prev is None else 4

    def body(sh_ref, p_ref, w_ref, m_ref, v_ref, *rest):
        g_out, d_out, m_out, v_out = rest[n_prev:n_prev + 4]
        g = p_ref[0].astype(F32)
        for s in range(1, N_DEV):
            g = g + p_ref[s].astype(F32)
        if n_p != n:
            rolled = rest[-1]
            rolled[...] = pltpu.roll(g, n_p - sh_ref[0], 0)
            g = rolled[0:n, :]
        g_out[...] = g
        d_out[...], m_out[...], v_out[...] = _adam_update(g, w_ref[...], m_ref[...], v_ref[...])

    lay = pl.BlockSpec((None, n, cb), lambda j, sh: (layer, 0, j))
    stack = _sds((L, n, C), F32)
    grid_spec = pltpu.PrefetchScalarGridSpec(
        num_scalar_prefetch=1, grid=(C // cb,),
        in_specs=[pl.BlockSpec((N_DEV, n_p, cb), lambda j, sh: (0, 0, j)), lay, lay, lay]
        + [pl.BlockSpec(memory_space=pl.ANY)] * n_prev,
        out_specs=(lay, lay, lay, lay), scratch_shapes=[pltpu.VMEM((n_p, cb), F32)] if n_p != n else [])
    return pl.pallas_call(body, name=name, grid_spec=grid_spec, out_shape=(stack, stack, stack, stack),
                          input_output_aliases={5 + k: k for k in range(n_prev)},
                          compiler_params=pltpu.CompilerParams(dimension_semantics=('parallel',),
                                                               vmem_limit_bytes=VMEM_LIMIT),
                          interpret=False)(shift, parts, w, m, v, *(prev or ()))


def kernel(x, mem, mix_norm, w_in, b_gate, b_forget, conv_a, w_out_a, w_out_b, conv_c, conv_c_bias, ln_c_gain, ln_c_bias, w_out_c, w_o, xattn_norm, mem_norm, w_xq, w_xkv, w_xo, ffn_norm, w_gate_up, w_down, final_norm, loss_target, m_mix_norm, m_w_in, m_b_gate, m_b_forget, m_conv_a, m_w_out_a, m_w_out_b, m_conv_c, m_conv_c_bias, m_ln_c_gain, m_ln_c_bias, m_w_out_c, m_w_o, m_xattn_norm, m_mem_norm, m_w_xq, m_w_xkv, m_w_xo, m_ffn_norm, m_w_gate_up, m_w_down, m_final_norm, v_mix_norm, v_w_in, v_b_gate, v_b_forget, v_conv_a, v_w_out_a, v_w_out_b, v_conv_c, v_conv_c_bias, v_ln_c_gain, v_ln_c_bias, v_w_out_c, v_w_o, v_xattn_norm, v_mem_norm, v_w_xq, v_w_xkv, v_w_xo, v_ffn_norm, v_w_gate_up, v_w_down, v_final_norm):
    P = dict(zip(ARG_NAMES, (x, mem, mix_norm, w_in, b_gate, b_forget, conv_a, w_out_a, w_out_b, conv_c, conv_c_bias, ln_c_gain, ln_c_bias, w_out_c, w_o, xattn_norm, mem_norm, w_xq, w_xkv, w_xo, ffn_norm, w_gate_up, w_down, final_norm, loss_target, m_mix_norm, m_w_in, m_b_gate, m_b_forget, m_conv_a, m_w_out_a, m_w_out_b, m_conv_c, m_conv_c_bias, m_ln_c_gain, m_ln_c_bias, m_w_out_c, m_w_o, m_xattn_norm, m_mem_norm, m_w_xq, m_w_xkv, m_w_xo, m_ffn_norm, m_w_gate_up, m_w_down, m_final_norm, v_mix_norm, v_w_in, v_b_gate, v_b_forget, v_conv_a, v_w_out_a, v_w_out_b, v_conv_c, v_conv_c_bias, v_ln_c_gain, v_ln_c_bias, v_w_out_c, v_w_o, v_xattn_norm, v_mem_norm, v_w_xq, v_w_xkv, v_w_xo, v_ffn_norm, v_w_gate_up, v_w_down, v_final_norm)))
    return _step(P)


_T_VIEW = ('w_in', 'w_gate_up', 'w_xo')
_BIG = [w for w in SHARDED if w not in ('conv_a', 'conv_c')]
_OTHER = [w for w in _BIG if w != 'w_in']
_CONVS = (('conv_a', CONV_A_W, 8), ('conv_c', CONV_C_W, 32))
_EARLY = ['w_down', 'w_gate_up', 'w_xo', 'w_xq', 'w_xkv', 'w_o']
_LATE = ['w_out_c', 'w_out_b', 'w_out_a', 'w_in']


def _view(name, a):
    return jnp.transpose(a, (0, 2, 1)) if name in _T_VIEW else a


def _step(P):
    L, D = P['mix_norm'].shape
    T = P['x'].shape[1]
    n_in = P['w_in'].shape[2]
    stride = n_in // ROW_TILE * ROW_TILE
    rem = n_in - stride
    win = stride + ROW_TILE
    assert rem * N_DEV == ROW_TILE
    r_in = n_in * N_DEV
    f_off = 6 * D
    hi_off = f_off + FOX_HEADS
    x_i, y_i, c_i = _place()
    me = 4 * x_i + 2 * y_i + c_i
    V = {}
    for w in _BIG:
        for k in (w, 'm_' + w, 'v_' + w):
            V[k] = _view(w, P[k])
    n_own = {w: V[w].shape[1] for w in _OTHER}
    shift = jnp.reshape(rem * me, (1,)).astype(jnp.int32)
    no_shift = jnp.zeros((1,), jnp.int32)
    row = lambda a, l: a[l][None, :]
    dus = lax.dynamic_update_slice

    def gather_start(l):
        p_in = dus(jnp.zeros((win, D), BF16), V['w_in'][l].astype(BF16), (rem * me, 0))
        srcs = [p_in] + [V[w][l].astype(BF16) for w in _OTHER] + [P[w][l] for w, _, _ in _CONVS]
        lands = [_sds((r_in, D), BF16), _sds((N_DEV, ROW_TILE, D), BF16)]
        plan = [(0, 0, 'rows', (ROW_TILE, stride, stride, ROW_TILE)), (0, 1, 'slot', (0, ROW_TILE))]
        for i, w in enumerate(_OTHER):
            n, s = n_own[w], srcs[1 + i]
            lands.append(_sds((N_DEV * n, s.shape[1]), BF16))
            plan.append((1 + i, 2 + i, 'rows', (0, n, n, 0)))
        for j in range(len(_CONVS)):
            i = 1 + len(_OTHER) + j
            lands.append(_sds((N_DEV,) + srcs[i].shape, F32))
            plan.append((i, i + 1, 'slot', None))
        lands = _place_own(srcs, plan, lands, 'gather_own')
        sems, srcs, lands, token = _gather_start(srcs, lands, plan, f'gather_start_{l}')
        return dict(plan=plan, send=sems[0], recv=sems[1], srcs=srcs, lands=lands, token=token)

    def gather_pass(g, after, l):
        g['pass'], g['lands'], g['token'] = _gather_pass(g['recv'], g['lands'], g['plan'], after, f'gather_pass_{l}')

    def gather_wait(g, after, l):
        lands = _gather_wait(g['send'], g['pass'], g['srcs'], g['lands'], g['plan'], after, f'gather_wait_{l}')
        W = {'w_in': _merge_edges(lands[0], lands[1], stride // ROW_TILE)}
        for i, w in enumerate(_OTHER):
            W[w] = lands[2 + i]
        for j, (w, taps, padded) in enumerate(_CONVS):
            full = jnp.transpose(lands[2 + len(_OTHER) + j], (1, 0, 2)).reshape(taps, D)
            W[w] = jnp.pad(full, ((0, padded - taps), (0, 0)))
        return W

    gathers = [gather_start(l) for l in range(L)]
    gather_pass(gathers[0], [g['token'] for g in gathers], 0)

    mem_x = P['mem'][0]
    mem_n = _rms_fwd(mem_x, P['mem_norm'][None, :], 'rms_mem')
    xs = P['x'][0]
    saved = []
    for l in range(L):
        W = gather_wait(gathers[l], [xs], l)
        s = {'x0': xs, 'W': W}
        wt = W['w_in']
        s['h1'] = _rms_fwd(xs, row(P['mix_norm'], l), 'rms_mix')
        s['z1'] = z1 = _mm(s['h1'], wt, mode='nt', b_rows=(0, f_off), name='mm_in_lo')
        s['z2'] = z2 = _mm(s['h1'], wt, mode='nt', b_rows=(hi_off, r_in - hi_off), name='mm_in_hi')
        s['zf'] = _mm(s['h1'], wt, mode='nt', b_rows=(f_off, FOX_HEADS), out_dtype=F32, name='mm_in_f')
        s['ya_pre'] = _mixa_fwd(z1, W['conv_a'])
        s['ya'] = _mm(s['ya_pre'], W['w_out_a'], mode='nn', name='mm_out_a')
        s['c'] = _fox_c(s['zf'], row(P['b_forget'], l))
        s['o'], s['lse'] = _fox_fwd(z1, s['c'])
        s['yb'] = _mm(s['o'], W['w_out_b'], mode='nn', name='mm_out_b')
        s['u3'], s['u1'] = _mixc_fwd(z2, W['conv_c'], row(P['conv_c_bias'], l), row(P['ln_c_gain'], l),
                                     row(P['ln_c_bias'], l))
        s['yc'] = _mm(s['u3'], W['w_out_c'], mode='nn', name='mm_out_c')
        s['merged'] = _gate_fwd(z2, row(P['b_gate'], l), s['ya'], s['yb'], s['yc'])
        xs = _mm(s['merged'], W['w_o'], mode='nn', add=xs, out_dtype=F32, name='mm_o')
        s['x1'] = xs
        deps = []
        if l + 1 < L:
            gather_pass(gathers[l + 1], [xs], l + 1)
            deps = [gathers[l + 1]['token']]
        s['h2'] = _rms_fwd(xs, row(P['xattn_norm'], l), 'rms_xattn', deps=deps)
        s['qx'] = _mm(s['h2'], W['w_xq'], mode='nn', name='mm_xq')
        s['kv'] = _mm(mem_n, W['w_xkv'], mode='nn', name='mm_xkv')
        s['ox'] = _xattn_fwd(s['qx'], s['kv'])
        xs = _mm(s['ox'], W['w_xo'], mode='nt', add=xs, out_dtype=F32, name='mm_xo')
        s['x2'] = xs
        s['h3'] = _rms_fwd(xs, row(P['ffn_norm'], l), 'rms_ffn')
        s['gu'] = _mm(s['h3'], W['w_gate_up'], mode='nt', name='mm_gate_up')
        s['act'] = _swiglu_fwd(s['gu'])
        xs = _mm(s['act'], W['w_down'], mode='nn', add=xs, out_dtype=F32, name='mm_down')
        saved.append(s)

    dx, d_final_norm, loss_part = _final(xs, P['final_norm'][None, :], P['loss_target'][0])

    GS = {w: [None] * L for w in SMALL}
    G_conv = {'conv_a': [None] * L, 'conv_c': [None] * L}
    d_mem_n = None
    results = {w: None for w in _BIG}

    def exchange_start(G, names, l, tag, after):
        srcs = [G[w] for w in names]
        plan, lands = [], []
        for i, w in enumerate(names):
            step, n = (stride, win) if w == 'w_in' else (n_own[w], n_own[w])
            lands.append(_sds((N_DEV, n, srcs[i].shape[1]), BF16))
            plan.append((i, i, 'chunk', (step, n)))
        lands = _place_own(srcs, plan, lands, 'exchange_own')
        sems, srcs, lands, token = _scatter_start(srcs, lands, plan, after, f'exchange_start_{tag}{l}')
        return (names, l, tag, plan, sems, srcs, lands, token)

    def exchange_finish(e, after):
        names, l, tag, plan, sems, srcs, lands, _ = e
        lands = _scatter_wait(sems, srcs, lands, plan, after, f'exchange_wait_{tag}{l}')
        for i, w in enumerate(names):
            results[w] = _adamw_layer(lands[i], V[w], V['m_' + w], V['v_' + w], l, results[w],
                                      shift if w == 'w_in' else no_shift, 'adamw_' + w)

    pending = []
    for l in reversed(range(L)):
        s = saved[l]
        W, z1, z2 = s['W'], s['z1'], s['z2']
        wt = W['w_in']
        G = {}
        d_act = _mm(dx, W['w_down'], mode='nt', name='mmb_down', deps=[e[-1] for e in pending])
        G['w_down'] = _mm(s['act'], dx, mode='tn', name='mmg_w_down')
        dgu = _swiglu_bwd(s['gu'], d_act)
        dh3 = _mm(dgu, W['w_gate_up'], mode='nn', out_dtype=F32, name='mmb_gate_up')
        G['w_gate_up'] = _mm(dgu, s['h3'], mode='tn', name='mmg_w_gate_up')
        dx, GS['ffn_norm'][l] = _rms_bwd(s['x2'], row(P['ffn_norm'], l), dh3, dx, 'rms_ffn_bwd')
        d_ox = _mm(dx, W['w_xo'], mode='nn', out_dtype=F32, name='mmb_xo')
        G['w_xo'] = _mm(dx, s['ox'], mode='tn', name='mmg_w_xo')
        dqx, dkv = _xattn_bwd(s['qx'], s['kv'], d_ox)
        dh2 = _mm(dqx, W['w_xq'], mode='nt', out_dtype=F32, name='mmb_xq')
        G['w_xq'] = _mm(s['h2'], dqx, mode='tn', name='mmg_w_xq')
        G['w_xkv'] = _mm(mem_n, dkv, mode='tn', name='mmg_w_xkv')
        d_mem_n = _mm(dkv, W['w_xkv'], mode='nt', add=d_mem_n, out_dtype=F32, name='mmb_xkv')
        dx, GS['xattn_norm'][l] = _rms_bwd(s['x1'], row(P['xattn_norm'], l), dh2, dx, 'rms_xattn_bwd')
        dm = _mm(dx, W['w_o'], mode='nt', out_dtype=F32, name='mmb_o')
        G['w_o'] = _mm(s['merged'], dx, mode='tn', name='mmg_w_o')
        early = exchange_start(G, _EARLY, l, 'a', [])
        dya, dyb, dyc, dz_g, GS['b_gate'][l] = _gate_bwd(z2, row(P['b_gate'], l), s['ya'], s['yb'], s['yc'], dm)
        du3 = _mm(dyc, W['w_out_c'], mode='nt', out_dtype=F32, name='mmb_out_c', deps=[early[-1]])
        G['w_out_c'] = _mm(s['u3'], dyc, mode='tn', name='mmg_w_out_c')
        dz_c, G_conv['conv_c'][l], GS['conv_c_bias'][l], GS['ln_c_gain'][l], GS['ln_c_bias'][l] = _mixc_bwd(
            z2, W['conv_c'], row(P['ln_c_gain'], l), row(P['ln_c_bias'], l), s['u1'], du3)
        do = _mm(dyb, W['w_out_b'], mode='nt', name='mmb_out_b')
        G['w_out_b'] = _mm(s['o'], dyb, mode='tn', name='mmg_w_out_b')
        dq, dk, dv, dc_pairs = _fox_bwd(z1, s['c'], s['o'], s['lse'], do)
        hp = FOX_HEADS // dc_pairs.shape[0]
        dc = jnp.transpose(dc_pairs[:, :, :hp], (1, 0, 2)).reshape(T, FOX_HEADS)
        dzf, GS['b_forget'][l] = _fox_c_bwd(dc, s['zf'], row(P['b_forget'], l))
        dya_pre = _mm(dya, W['w_out_a'], mode='nt', out_dtype=F32, name='mmb_out_a')
        G['w_out_a'] = _mm(s['ya_pre'], dya, mode='tn', name='mmg_w_out_a')
        dz_a, G_conv['conv_a'][l] = _mixa_bwd(z1, W['conv_a'], dya_pre)
        segs = [(dzf, f_off), (dz_a, 0), (dq, 3 * D), (dk, 4 * D), (dv, 5 * D), (dz_c, hi_off), (dz_g, hi_off + 2 * D)]
        dh1, g_in = None, r_in
        for dz, off in segs:
            dh1 = _mm(dz, wt, mode='nn', b_rows=(off, dz.shape[1]), add=dh1, out_dtype=F32, name='mmb_in')
            g_in = _mm(dz, s['h1'], mode='tn', out=g_in, o_off=off, name='mmg_w_in')
        G['w_in'] = g_in
        dx, GS['mix_norm'][l] = _rms_bwd(s['x0'], row(P['mix_norm'], l), dh1, dx, 'rms_mix_bwd')
        for e in pending:
            exchange_finish(e, [dx])
        pending = [early]
        if l > 0:
            pending.append(exchange_start(G, _LATE, l, 'b', []))

    _, d_mem_norm = _rms_bwd(mem_x, P['mem_norm'][None, :], d_mem_n, None, 'rms_mem_bwd')

    small_parts = {w: jnp.concatenate(GS[w], axis=0) for w in SMALL if w not in ('mem_norm', 'final_norm')}
    small_parts['mem_norm'] = d_mem_norm
    small_parts['final_norm'] = d_final_norm
    conv_parts = [jnp.stack(G_conv[w])[:, :taps] for w, taps, _ in _CONVS]
    sizes = [P[w].size for w in SMALL]
    conv_sizes = [c.size for c in conv_parts]
    n_small = sum(sizes) + sum(conv_sizes) + LANES
    n_rows = -(-n_small // (8 * LANES)) * 8

    def pack(parts):
        flat = jnp.concatenate([p.reshape(-1) for p in parts])
        return jnp.pad(flat, (0, n_rows * LANES - flat.size)).reshape(n_rows, LANES)

    g_small = pack([small_parts[w] for w in SMALL] + conv_parts + [loss_part])
    (all_small,) = _allgather([g_small], [True], 'gather_small')
    pending.append(exchange_start(G, _LATE, 0, 'b', [all_small]))
    sm = _adamw(all_small, pack([P[w] for w in SMALL]), pack([P['m_' + w] for w in SMALL]),
                pack([P['v_' + w] for w in SMALL]), 'adamw_small')
    g_all = sm[0].reshape(-1)
    loss = g_all[sum(sizes) + sum(conv_sizes)]
    final = {}
    off = 0
    for w, n in zip(SMALL, sizes):
        final[w] = [o.reshape(-1)[off:off + n].reshape(P[w].shape) for o in sm]
        off += n
    for (w, taps, _), c, n in zip(_CONVS, conv_parts, conv_sizes):
        n_col = P[w].shape[2]
        g_own = lax.dynamic_slice(g_all[off:off + n].reshape(c.shape), (0, 0, n_col * me), (L, taps, n_col))
        flat = lambda a: a.reshape(L * taps, n_col)
        outs = _adamw(flat(g_own)[None], flat(P[w]), flat(P['m_' + w]), flat(P['v_' + w]), 'adamw_' + w)
        final[w] = [o.reshape(P[w].shape) for o in outs]
        off += n

    for e in pending:
        exchange_finish(e, [sm[0], final['conv_c'][0]])
    for w in _BIG:
        final[w] = [_view(w, o) for o in results[w]]

    out = [loss, dx[None]]
    for k in range(4):
        out += [final[w][k] for w in WEIGHTS]
    return tuple(out)
```

```python
import functools

import jax
import jax.numpy as jnp
from jax import lax
from jax.experimental import pallas as pl
from jax.experimental.pallas import tpu as pltpu

F32 = jnp.float32
BF16 = jnp.bfloat16
MESH = pl.DeviceIdType.MESH

N_DEV = 8
EPS = 1e-6
FOX_HEADS = 16
X_HEADS = 4
LANES = 128
ROW_TILE = 16
HALO_A = 16
HALO_C = 32
SUBLANES = 8
CONV_ROWS = 128
ROW_CHUNK = 32
CONV_A_W = 3
CONV_C_W = 31
VMEM_LIMIT = 56 << 20

ADAM_LR = 0.001
ADAM_B1 = 0.9
ADAM_B2 = 0.999
ADAM_EPS = 1e-08
ADAM_WD = 0.01
ADAM_STEP = 10

WEIGHTS = ['mix_norm', 'w_in', 'b_gate', 'b_forget', 'conv_a', 'w_out_a', 'w_out_b', 'conv_c', 'conv_c_bias',
           'ln_c_gain', 'ln_c_bias', 'w_out_c', 'w_o', 'xattn_norm', 'mem_norm', 'w_xq', 'w_xkv', 'w_xo',
           'ffn_norm', 'w_gate_up', 'w_down', 'final_norm']
SHARDED = ['w_in', 'conv_a', 'w_out_a', 'w_out_b', 'conv_c', 'w_out_c', 'w_o', 'w_xq', 'w_xkv', 'w_xo',
           'w_gate_up', 'w_down']
ROW_SHARDED = ['w_out_a', 'w_out_b', 'w_out_c', 'w_o', 'w_xq', 'w_xkv', 'w_down']
SMALL = [w for w in WEIGHTS if w not in SHARDED]
ARG_NAMES = (['x', 'mem'] + WEIGHTS + ['loss_target'] + ['m_' + w for w in WEIGHTS] + ['v_' + w for w in WEIGHTS])


def _pick(n, cands=(1024, 1408, 512, 256, 128)):
    for c in cands:
        if n % c == 0:
            return c
    return n


def _call(body, *, name, out_shape, grid=(), in_specs=None, out_specs=None, scratch=(), sem=None, aliases=None,
          deps=()):
    params = dict(vmem_limit_bytes=VMEM_LIMIT)
    if sem is not None:
        params['dimension_semantics'] = sem
    n_in, n_dep = len(in_specs), len(deps)

    def body_without_deps(*refs):
        body(*refs[:n_in], *refs[n_in + n_dep:])

    call = pl.pallas_call(body_without_deps, name=name, out_shape=out_shape, grid=grid, scratch_shapes=list(scratch),
                          in_specs=list(in_specs) + [pl.BlockSpec(memory_space=pl.ANY)] * n_dep, out_specs=out_specs,
                          input_output_aliases=aliases or {}, compiler_params=pltpu.CompilerParams(**params),
                          interpret=False)
    return lambda *args: call(*args, *deps)


def _sds(shape, dtype):
    return jax.ShapeDtypeStruct(tuple(shape), dtype)


def _sigmoid(x):
    return 1.0 / (1.0 + jnp.exp(-x))


_DN = {'nn': (((1,), (0,)), ((), ())), 'nt': (((1,), (1,)), ((), ())), 'tn': (((0,), (0,)), ((), ()))}


def _mm(a, b, *, mode, name, out_dtype=BF16, add=None, b_rows=None, out=None, o_off=0, deps=()):
    if mode == 'tn':
        K, M = a.shape
    else:
        M, K = a.shape
    b_off, b_n = (0, b.shape[0]) if b_rows is None else b_rows
    N = b_n if mode == 'nt' else b.shape[1]
    assert (b.shape[1] if mode == 'nt' else b_n) == K
    tm, tn = _pick(M), _pick(N)
    tk = K if K <= 2048 else _pick(K)
    ni, nj, nk = M // tm, N // tn, K // tk
    a_bytes, b_bytes = M * K * a.dtype.itemsize, K * N * b.dtype.itemsize
    n_outer = (b_bytes + nj * a_bytes) < (a_bytes + ni * b_bytes)
    if n_outer:
        grid = (nj, ni, nk)
        ij = lambda g0, g1: (g1, g0)
    else:
        grid = (ni, nj, nk)
        ij = lambda g0, g1: (g0, g1)

    def a_map(g0, g1, k):
        i, _ = ij(g0, g1)
        return (k, i) if mode == 'tn' else (i, k)

    def elems(blk):
        return tuple(pl.Element(s) for s in blk)

    def at(rows, cols):
        return pl.multiple_of(rows, ROW_TILE), pl.multiple_of(cols, LANES)

    a_blk = (tk, tm) if mode == 'tn' else (tm, tk)
    b_blk = (tn, tk) if mode == 'nt' else (tk, tn)
    if b_rows is None:
        def b_map(g0, g1, k):
            _, j = ij(g0, g1)
            return (j, k) if mode == 'nt' else (k, j)
        b_spec = pl.BlockSpec(b_blk, b_map)
    else:
        def b_map(g0, g1, k):
            _, j = ij(g0, g1)
            return at(b_off + j * tn, k * tk) if mode == 'nt' else at(b_off + k * tk, j * tn)
        b_spec = pl.BlockSpec(elems(b_blk), b_map)
    in_specs = [pl.BlockSpec(a_blk, a_map), b_spec]
    args = [a, b]
    has_add = add is not None
    if has_add:
        in_specs.append(pl.BlockSpec((tm, tn), lambda g0, g1, k: ij(g0, g1)))
        args.append(add)
    aliases = {}
    if out is None:
        out_shape = _sds((M, N), out_dtype)
        out_spec = pl.BlockSpec((tm, tn), lambda g0, g1, k: ij(g0, g1))
    else:
        if isinstance(out, int):
            out_shape = _sds((out, N), out_dtype)
        else:
            out_shape = _sds(out.shape, out.dtype)
            in_specs.append(pl.BlockSpec(memory_space=pl.ANY))
            aliases = {len(args): 0}
            args.append(out)

        def o_map(g0, g1, k):
            i, j = ij(g0, g1)
            return at(o_off + i * tm, j * tn)
        out_spec = pl.BlockSpec(elems((tm, tn)), o_map)
    dn = _DN[mode]
    n_in = len(args)

    def body(*refs):
        a_ref, b_ref = refs[0], refs[1]
        add_ref = refs[2] if has_add else None
        o_ref = refs[n_in]
        part = lax.dot_general(a_ref[...].astype(BF16), b_ref[...].astype(BF16), dn, preferred_element_type=F32)

        def finish(r):
            if has_add:
                r = r + add_ref[...]
            o_ref[...] = r.astype(o_ref.dtype)

        if nk == 1:
            finish(part)
        else:
            acc = refs[n_in + 1]
            k = pl.program_id(2)

            @pl.when(k == 0)
            def _():
                acc[...] = part

            @pl.when(k > 0)
            def _():
                acc[...] += part

            @pl.when(k == nk - 1)
            def _():
                finish(acc[...])

    scratch = [pltpu.VMEM((tm, tn), F32)] if nk > 1 else []
    return _call(body, name=name, out_shape=out_shape, grid=grid, in_specs=in_specs, out_specs=out_spec,
                 scratch=scratch, sem=('parallel', 'parallel', 'arbitrary'), aliases=aliases, deps=deps)(*args)


def _rms_fwd(x, g, name, deps=()):
    R, D = x.shape
    tb = _pick(R, (512, 256, 128))

    def body(x_ref, g_ref, o_ref):
        xv = x_ref[...]
        r = lax.rsqrt(jnp.mean(xv * xv, axis=-1, keepdims=True) + EPS)
        o_ref[...] = (xv * r * g_ref[...]).astype(o_ref.dtype)

    row = pl.BlockSpec((tb, D), lambda i: (i, 0))
    vec = pl.BlockSpec((1, D), lambda i: (0, 0))
    return _call(body, name=name, out_shape=_sds((R, D), BF16), grid=(R // tb,), in_specs=[row, vec],
                 out_specs=row, sem=('parallel',), deps=deps)(x, g)


def _rms_bwd(x, g, dh, dres, name, deps=()):
    R, D = x.shape
    tb = _pick(R, (512, 256, 128))
    has_res = dres is not None

    def body(*refs):
        x_ref, g_ref, dh_ref = refs[:3]
        dx_ref, dg_ref = refs[-2:]
        xv = x_ref[...]
        dhv = dh_ref[...].astype(F32)
        r = lax.rsqrt(jnp.mean(xv * xv, axis=-1, keepdims=True) + EPS)
        u = dhv * g_ref[...]
        m = jnp.mean(u * xv, axis=-1, keepdims=True)
        dx = r * u - xv * (r * r * r * m)
        if has_res:
            dx = dx + refs[3][...]
        dx_ref[...] = dx

        @pl.when(pl.program_id(0) == 0)
        def _():
            dg_ref[...] = jnp.zeros_like(dg_ref)

        dg_ref[...] += jnp.sum(dhv * xv * r, axis=0, keepdims=True)

    row = pl.BlockSpec((tb, D), lambda i: (i, 0))
    vec = pl.BlockSpec((1, D), lambda i: (0, 0))
    args = [x, g, dh] + ([dres] if has_res else [])
    return _call(body, name=name, out_shape=(_sds((R, D), F32), _sds((1, D), F32)), grid=(R // tb,),
                 in_specs=[row, vec, row] + ([row] if has_res else []), out_specs=(row, vec),
                 sem=('arbitrary',), deps=deps)(*args)


def _final(x, g, tgt):
    R, D = x.shape
    tb = _pick(R, (512, 256, 128))

    def body(x_ref, g_ref, t_ref, dx_ref, dg_ref, loss_ref):
        xv = x_ref[...]
        gv = g_ref[...]
        r = lax.rsqrt(jnp.mean(xv * xv, axis=-1, keepdims=True) + EPS)
        e = xv * r * gv - t_ref[...]
        row_loss = jnp.mean(e * e, axis=-1, keepdims=True)
        blk_loss = 0.5 * jnp.sum(row_loss, axis=0, keepdims=True)
        dy = e * (1.0 / D)
        u = dy * gv
        m = jnp.mean(u * xv, axis=-1, keepdims=True)
        dx_ref[...] = r * u - xv * (r * r * r * m)

        @pl.when(pl.program_id(0) == 0)
        def _():
            dg_ref[...] = jnp.zeros_like(dg_ref)
            loss_ref[...] = jnp.zeros_like(loss_ref)

        dg_ref[...] += jnp.sum(dy * xv * r, axis=0, keepdims=True)
        loss_ref[...] += jnp.broadcast_to(blk_loss, loss_ref.shape)

    row = pl.BlockSpec((tb, D), lambda i: (i, 0))
    vec = pl.BlockSpec((1, D), lambda i: (0, 0))
    one = pl.BlockSpec((1, LANES), lambda i: (0, 0))
    return _call(body, name='final_loss', out_shape=(_sds((R, D), F32), _sds((1, D), F32), _sds((1, LANES), F32)),
                 grid=(R // tb,), in_specs=[row, vec, row], out_specs=(row, vec, one), sem=('arbitrary',))(x, g, tgt)


def _col(tb, w, cb):
    return pl.BlockSpec((tb, w), lambda i: (i, cb))


def _prev(tb, h, w, cb):
    return pl.BlockSpec((h, w), lambda i: (jnp.maximum(i * (tb // h) - 1, 0), cb))


def _next(tb, h, w, cb, rows):
    return pl.BlockSpec((h, w), lambda i: (jnp.minimum((i + 1) * (tb // h), rows // h - 1), cb))


def _full(shape):
    return pl.BlockSpec(shape, lambda i: (0,) * len(shape))


def _mixa_fwd(z, w):
    T = z.shape[0]
    D = w.shape[1]
    tb = _pick(T, (256, 128))
    H = HALO_A

    def body(ab, ac, au, acp, aup, w_ref, o_ref, ext):
        first = pl.program_id(0) == 0
        ext[0:H, :] = jnp.where(first, 0.0, acp[...].astype(F32) * aup[...].astype(F32))
        ext[H:H + tb, :] = ac[...].astype(F32) * au[...].astype(F32)
        cp = jnp.zeros((tb, D), F32)
        for k in range(CONV_A_W):
            off = H - (CONV_A_W - 1) + k
            cp = cp + ext[off:off + tb, :] * w_ref[k:k + 1, :]
        o_ref[...] = (ab[...].astype(F32) * cp).astype(o_ref.dtype)

    return _call(body, name='mixa_fwd', out_shape=_sds((T, D), BF16), grid=(T // tb,),
                 in_specs=[_col(tb, D, 0), _col(tb, D, 1), _col(tb, D, 2), _prev(tb, H, D, 1), _prev(tb, H, D, 2),
                           _full((8, D))],
                 out_specs=_col(tb, D, 0), scratch=[pltpu.VMEM((H + tb, D), F32)], sem=('parallel',))(z, z, z, z, z, w)


def _mixa_bwd(z, w, dy):
    T = z.shape[0]
    D = w.shape[1]
    tb = _pick(T, (256, 128))
    H = HALO_A
    nb = T // tb

    def body(ab, ac, au, acp, aup, abn, dy_ref, dyn, w_ref, dz_ref, dw_ref, pext, dext):
        i = pl.program_id(0)
        a_b, a_c, a_u = ab[...].astype(F32), ac[...].astype(F32), au[...].astype(F32)
        pext[0:H, :] = jnp.where(i == 0, 0.0, acp[...].astype(F32) * aup[...].astype(F32))
        pext[H:H + tb, :] = a_c * a_u
        dyv = dy_ref[...].astype(F32)
        dcp = dyv * a_b
        dext[0:tb, :] = dcp
        dext[tb:tb + H, :] = jnp.where(i == nb - 1, 0.0, dyn[...].astype(F32) * abn[...].astype(F32))

        @pl.when(i == 0)
        def _():
            dw_ref[...] = jnp.zeros_like(dw_ref)

        cp = jnp.zeros((tb, D), F32)
        dp = jnp.zeros((tb, D), F32)
        for k in range(CONV_A_W):
            off = H - (CONV_A_W - 1) + k
            wk = w_ref[k:k + 1, :]
            pk = pext[off:off + tb, :]
            cp = cp + pk * wk
            dp = dp + dext[CONV_A_W - 1 - k:CONV_A_W - 1 - k + tb, :] * wk
            dw_ref[k:k + 1, :] += jnp.sum(dcp * pk, axis=0, keepdims=True)
        dz_ref[:, 0:D] = (dyv * cp).astype(dz_ref.dtype)
        dz_ref[:, D:2 * D] = (dp * a_u).astype(dz_ref.dtype)
        dz_ref[:, 2 * D:3 * D] = (dp * a_c).astype(dz_ref.dtype)

    return _call(body, name='mixa_bwd', out_shape=(_sds((T, 6 * D), BF16), _sds((8, D), F32)), grid=(nb,),
                 in_specs=[_col(tb, D, 0), _col(tb, D, 1), _col(tb, D, 2), _prev(tb, H, D, 1), _prev(tb, H, D, 2),
                           _next(tb, H, D, 0, T), _col(tb, D, 0), _next(tb, H, D, 0, T), _full((8, D))],
                 out_specs=(_col(tb, 3 * D, 0), _full((8, D))),
                 scratch=[pltpu.VMEM((H + tb, D), F32), pltpu.VMEM((tb + H, D), F32)],
                 sem=('arbitrary',))(z, z, z, z, z, z, dy, dy, w)


def _split3(v):
    hi = v.astype(BF16)
    r1 = v - hi.astype(F32)
    mid = r1.astype(BF16)
    lo = (r1 - mid.astype(F32)).astype(BF16)
    return hi, mid, lo


def _tri_dot(tri, v):
    hi, mid, lo = _split3(v)
    d = lambda p: jnp.dot(tri, p, preferred_element_type=F32)
    return d(hi) + d(mid) + d(lo)


def _fox_c(zf, bf):
    T, H = zf.shape
    tb = LANES

    def body(zf_ref, b_ref, c_ref, carry):
        @pl.when(pl.program_id(0) == 0)
        def _():
            carry[...] = jnp.zeros_like(carry)

        xv = zf_ref[...] + b_ref[...]
        lf = jnp.minimum(xv, 0.0) - jnp.log(1.0 + jnp.exp(-jnp.abs(xv)))
        r = lax.broadcasted_iota(jnp.int32, (tb, tb), 0)
        c = lax.broadcasted_iota(jnp.int32, (tb, tb), 1)
        tri = (r >= c).astype(BF16)
        cs = _tri_dot(tri, lf) + carry[...]
        c_ref[...] = cs
        carry[...] = cs[tb - 1:tb, :]

    blk = pl.BlockSpec((tb, H), lambda i: (i, 0))
    return _call(body, name='fox_cumsum', out_shape=_sds((T, H), F32), grid=(T // tb,),
                 in_specs=[blk, _full((1, H))], out_specs=blk, scratch=[pltpu.VMEM((1, H), F32)],
                 sem=('arbitrary',))(zf, bf)


def _fox_c_bwd(dc, zf, bf):
    T, H = zf.shape
    tb = LANES
    nb = T // tb

    def body(dc_ref, zf_ref, b_ref, dz_ref, db_ref, carry):
        @pl.when(pl.program_id(0) == 0)
        def _():
            carry[...] = jnp.zeros_like(carry)
            db_ref[...] = jnp.zeros_like(db_ref)

        r = lax.broadcasted_iota(jnp.int32, (tb, tb), 0)
        c = lax.broadcasted_iota(jnp.int32, (tb, tb), 1)
        tri = (c >= r).astype(BF16)
        dlf = _tri_dot(tri, dc_ref[...]) + carry[...]
        carry[...] = dlf[0:1, :]
        xv = zf_ref[...] + b_ref[...]
        dz = dlf * _sigmoid(-xv)
        dz_ref[...] = dz
        db_ref[...] += jnp.sum(dz, axis=0, keepdims=True)

    blk = pl.BlockSpec((tb, H), lambda i: (nb - 1 - i, 0))
    return _call(body, name='fox_cumsum_bwd', out_shape=(_sds((T, H), F32), _sds((1, H), F32)), grid=(nb,),
                 in_specs=[blk, blk, _full((1, H))], out_specs=(blk, _full((1, H))),
                 scratch=[pltpu.VMEM((1, H), F32)], sem=('arbitrary',))(dc, zf, bf)


N_PIECES = 3


def _head_col(c_ref, h):
    lane = lax.broadcasted_iota(jnp.int32, c_ref.shape, 1)
    return jnp.sum(jnp.where(lane == h, c_ref[...], 0.0), axis=1, keepdims=True)


def _pieces(v, sign=1.0):
    return [sign * p.astype(F32) for p in _split3(v)]


def _augment(block, hh, dh, a_cols, b_cols):
    lane = lax.broadcasted_iota(jnp.int32, block.shape, 1)
    base = (1 - hh) * dh
    out = jnp.where((lane >= hh * dh) & (lane < (hh + 1) * dh), block.astype(F32), 0.0)
    for n, col in enumerate(list(a_cols) + list(b_cols)):
        out = jnp.where(lane == base + n, col, out)
    return out.astype(BF16)


def _fox_probs(q_aug, k_aug, tri, q0, ke, shift=None):
    s = lax.dot_general(q_aug[q0:ke, :], k_aug[0:ke, :], _DN['nt'], preferred_element_type=F32)
    s_dg = jnp.where(tri, s[:, q0:ke], -jnp.inf)
    if shift is None:
        shift = jnp.max(s_dg, axis=-1, keepdims=True)
        if q0:
            shift = jnp.maximum(shift, jnp.max(s[:, 0:q0], axis=-1, keepdims=True))
    parts = ([jnp.exp(s[:, 0:q0] - shift)] if q0 else []) + [jnp.exp(s_dg - shift)]
    return parts, shift


def _cat(parts):
    parts = [p.astype(BF16) for p in parts]
    return parts[0] if len(parts) == 1 else jnp.concatenate(parts, axis=1)


def _fox_fwd(z, c):
    T = z.shape[0]
    D = z.shape[1] // 6
    dh = D // FOX_HEADS
    hp = LANES // dh
    ncb = D // LANES
    tq = _pick(T, (256, 128))
    scale = dh ** -0.5
    ones = [1.0] * N_PIECES

    def body(q_ref, k_ref, v_ref, c_ref, o_ref, lse_ref, o_all):
        j = pl.program_id(0)
        lse_ref[...] = jnp.zeros_like(lse_ref)
        r = lax.broadcasted_iota(jnp.int32, (tq, tq), 0)
        tri = r >= lax.broadcasted_iota(jnp.int32, (tq, tq), 1)
        for hh in range(hp):
            base = (1 - hh) * dh
            ch = _head_col(c_ref, j * hp + hh)
            q_aug = _augment(q_ref[...].astype(F32) * scale, hh, dh, _pieces(ch), ones)
            k_aug = _augment(k_ref[...], hh, dh, ones, _pieces(ch, -1.0))
            v_aug = _augment(v_ref[...], hh, dh, ones, [])
            for q0 in range(0, T, tq):
                ke = q0 + tq
                parts, m = _fox_probs(q_aug, k_aug, tri, q0, ke)
                o_aug = jnp.dot(_cat(parts), v_aug[0:ke, :], preferred_element_type=F32)
                l = o_aug[:, base:base + 1]
                o_all[hh, q0:ke, :] = o_aug * (1.0 / l)
                lse_ref[q0:ke, hh:hh + 1] = m + jnp.log(l)
        lane = lax.broadcasted_iota(jnp.int32, (T, LANES), 1)
        out = o_all[0]
        for hh in range(1, hp):
            out = jnp.where(lane >= hh * dh, o_all[hh], out)
        o_ref[...] = out.astype(o_ref.dtype)

    blk = lambda cb0: pl.BlockSpec((T, LANES), lambda j: (0, cb0 + j))
    return _call(body, name='fox_fwd', out_shape=(_sds((T, D), BF16), _sds((ncb, T, LANES), F32)), grid=(ncb,),
                 in_specs=[blk(3 * ncb), blk(4 * ncb), blk(5 * ncb), _full((T, FOX_HEADS))],
                 out_specs=(blk(0), pl.BlockSpec((None, T, LANES), lambda j: (j, 0, 0))),
                 scratch=[pltpu.VMEM((hp, T, LANES), F32)], sem=('parallel',))(z, z, z, c)


def _fox_bwd(z, c, o, lse, do, dz1):
    T = z.shape[0]
    D = o.shape[1]
    dh = D // FOX_HEADS
    hp = LANES // dh
    ncb = D // LANES
    tq = _pick(T, (256, 128))
    scale = dh ** -0.5
    ones = [1.0] * N_PIECES

    def body(q_ref, k_ref, v_ref, c_ref, o_ref, lse_ref, do_ref, _, dz_ref, dc_ref, dq_all, dk_acc, dv_acc, stage,
             sems):
        j = pl.program_id(0)
        dk_acc[...] = jnp.zeros_like(dk_acc)
        dv_acc[...] = jnp.zeros_like(dv_acc)
        dc_ref[...] = jnp.zeros_like(dc_ref)
        r = lax.broadcasted_iota(jnp.int32, (tq, tq), 0)
        tri = r >= lax.broadcasted_iota(jnp.int32, (tq, tq), 1)
        lane = lax.broadcasted_iota(jnp.int32, (T, LANES), 1)
        for hh in range(hp):
            base = (1 - hh) * dh
            own = (lane >= hh * dh) & (lane < (hh + 1) * dh)
            ch = _head_col(c_ref, j * hp + hh)
            dov = do_ref[...].astype(F32)
            delta = jnp.sum(jnp.where(own, dov * o_ref[...].astype(F32), 0.0), axis=1, keepdims=True)
            q_aug = _augment(q_ref[...].astype(F32) * scale, hh, dh, _pieces(ch - lse_ref[:, hh:hh + 1]), ones)
            k_aug = _augment(k_ref[...], hh, dh, ones, _pieces(ch, -1.0))
            v_aug = _augment(v_ref[...], hh, dh, ones, [])
            do_aug = _augment(dov, hh, dh, _pieces(delta, -1.0), [])
            for q0 in range(0, T, tq):
                ke = q0 + tq
                p, _ = _fox_probs(q_aug, k_aug, tri, q0, ke, shift=0.0)
                dp = lax.dot_general(do_aug[q0:ke, :], v_aug[0:ke, :], _DN['nt'], preferred_element_type=F32)
                ds = [p[0] * dp[:, 0:q0], p[1] * dp[:, q0:ke]] if q0 else [p[0] * dp]
                dsb, pb = _cat(ds), _cat(p)
                dq_all[hh, q0:ke, :] = jnp.dot(dsb, k_aug[0:ke, :], preferred_element_type=F32)
                dk_acc[hh, 0:ke, :] += lax.dot_general(dsb, q_aug[q0:ke, :], _DN['tn'], preferred_element_type=F32)
                dv_acc[hh, 0:ke, :] += lax.dot_general(pb, do_aug[q0:ke, :], _DN['tn'], preferred_element_type=F32)
            dc_ref[:, hh:hh + 1] = dq_all[hh, :, base:base + 1] - dk_acc[hh, :, base + N_PIECES:base + N_PIECES + 1]
        dq, dk, dv = dq_all[0], dk_acc[0], dv_acc[0]
        for hh in range(1, hp):
            dq = jnp.where(lane >= hh * dh, dq_all[hh], dq)
            dk = jnp.where(lane >= hh * dh, dk_acc[hh], dk)
            dv = jnp.where(lane >= hh * dh, dv_acc[hh], dv)
        copies = []
        for n, g in enumerate((dq * scale, dk, dv)):
            stage[n] = g.astype(stage.dtype)
            col = pl.multiple_of(((3 + n) * ncb + j) * LANES, LANES)
            copies.append(pltpu.make_async_copy(stage.at[n], dz_ref.at[:, pl.ds(col, LANES)], sems.at[n]))
            copies[n].start()
        for cp in copies:
            cp.wait()

    blk = lambda cb0: pl.BlockSpec((T, LANES), lambda j: (0, cb0 + j))
    pair = pl.BlockSpec((None, T, LANES), lambda j: (j, 0, 0))
    grad = _sds((T, D), BF16)
    acc = pltpu.VMEM((hp, T, LANES), F32)
    anywhere = pl.BlockSpec(memory_space=pl.ANY)
    return _call(body, name='fox_bwd', out_shape=(_sds(dz1.shape, dz1.dtype), _sds((ncb, T, LANES), F32)), grid=(ncb,),
                 in_specs=[blk(3 * ncb), blk(4 * ncb), blk(5 * ncb), _full((T, FOX_HEADS)), blk(0), pair, blk(0),
                           anywhere],
                 out_specs=(anywhere, pair),
                 scratch=[acc, acc, acc, pltpu.VMEM((3, T, LANES), BF16), pltpu.SemaphoreType.DMA((3,))],
                 sem=('arbitrary',), aliases={7: 0})(z, z, z, c, o, lse, do, dz1)


def _ln_parts(u1, gain, bias):
    mu = jnp.mean(u1, axis=-1, keepdims=True)
    xc = u1 - mu
    rstd = lax.rsqrt(jnp.mean(xc * xc, axis=-1, keepdims=True) + EPS)
    xhat = xc * rstd
    return xhat, rstd, xhat * gain + bias


def _tap_groups(offsets):
    groups = {}
    for k, off in enumerate(offsets):
        groups.setdefault(off % SUBLANES, []).append((off - off % SUBLANES, k))
    return groups


def _for_taps(src_ref, r0, cols, groups, aligned, visit):
    for r, taps in groups.items():
        if r == 0:
            for base, k in taps:
                visit(k, src_ref[r0 + base:r0 + base + CONV_ROWS, cols])
            continue
        need = max(base for base, _ in taps) + CONV_ROWS
        aligned[r, 0:need, :] = src_ref[r0 + r:r0 + r + need, cols]
        for base, k in taps:
            visit(k, aligned[r, base:base + CONV_ROWS, :])


def _row_chunk(j, first=0):
    return slice(first + j * ROW_CHUNK, first + (j + 1) * ROW_CHUNK)


def _glu_rows(cv, cg, ext, halo, tb):
    for j in range(tb // ROW_CHUNK):
        rs = _row_chunk(j)
        ext[_row_chunk(j, halo), :] = cv[rs, :].astype(F32) * _sigmoid(cg[rs, :].astype(F32))


def _mixc_fwd(z, w, cb, gain, bias):
    T = z.shape[0]
    D = w.shape[1]
    tb = _pick(T, (256, 128))
    H = HALO_C

    groups = _tap_groups([H - (CONV_C_W - 1) + k for k in range(CONV_C_W)])

    def body(cv, cg, cvp, cgp, w_ref, cb_ref, g_ref, b_ref, o_ref, u1_ref, ext, aligned):
        first = pl.program_id(0) == 0
        ext[0:H, :] = jnp.where(first, 0.0, cvp[...].astype(F32) * _sigmoid(cgp[...].astype(F32)))
        _glu_rows(cv, cg, ext, H, tb)
        for r0 in range(0, tb, CONV_ROWS):
            for c0 in range(0, D, LANES):
                cols = slice(c0, c0 + LANES)
                acc = [jnp.zeros((CONV_ROWS, LANES), F32)]

                def tap(k, rows):
                    acc[0] = acc[0] + rows * w_ref[k:k + 1, cols]

                _for_taps(ext, r0, cols, groups, aligned, tap)
                u1_ref[r0:r0 + CONV_ROWS, cols] = acc[0] + cb_ref[:, cols]

        for j in range(tb // ROW_CHUNK):
            rs = _row_chunk(j)
            _, _, u2 = _ln_parts(u1_ref[rs, :], g_ref[...], b_ref[...])
            o_ref[rs, :] = (u2 * _sigmoid(u2)).astype(o_ref.dtype)

    vec = _full((1, D))
    row = _col(tb, D, 0)
    return _call(body, name='mixc_fwd', out_shape=(_sds((T, D), BF16), _sds((T, D), F32)), grid=(T // tb,),
                 in_specs=[_col(tb, D, 0), _col(tb, D, 1), _prev(tb, H, D, 0), _prev(tb, H, D, 1), _full((32, D)), vec,
                           vec, vec],
                 out_specs=(row, row), scratch=[pltpu.VMEM((H + tb, D), F32), pltpu.VMEM((SUBLANES, CONV_ROWS + H, LANES), F32)],
                 sem=('parallel',))(z, z, z, z, w, cb, gain, bias)


def _mixc_bwd(z, w, gain, bias, u1, du3, dz2):
    T = z.shape[0]
    D = w.shape[1]
    tb = _pick(T, (256, 128))
    H = HALO_C
    nb = T // tb

    def du1_of(u1v, du3v, gv, bv):
        xhat, rstd, u2 = _ln_parts(u1v, gv, bv)
        sg = _sigmoid(u2)
        du2 = du3v * (sg * (1.0 + u2 * (1.0 - sg)))
        dxh = du2 * gv
        m1 = jnp.mean(dxh, axis=-1, keepdims=True)
        m2 = jnp.mean(dxh * xhat, axis=-1, keepdims=True)
        return rstd * (dxh - m1 - xhat * m2), du2, xhat

    fwd_groups = _tap_groups([H - (CONV_C_W - 1) + k for k in range(CONV_C_W)])
    bwd_groups = _tap_groups([CONV_C_W - 1 - k for k in range(CONV_C_W)])

    def body(cv, cg, cvp, cgp, w_ref, g_ref, b_ref, u1_ref, u1n, du3_ref, du3n, _, dz_ref, dw_ref, dcb_ref, dg_ref,
             db_ref, uext, dext, dw_acc, aligned, sums):
        i = pl.program_id(0)
        uext[0:H, :] = jnp.where(i == 0, 0.0, cvp[...].astype(F32) * _sigmoid(cgp[...].astype(F32)))
        _glu_rows(cv, cg, uext, H, tb)
        du1n, _, _ = du1_of(u1n[...], du3n[...].astype(F32), g_ref[...], b_ref[...])
        dext[tb:tb + H, :] = jnp.where(i == nb - 1, 0.0, du1n)

        @pl.when(i == 0)
        def _():
            dw_acc[...] = jnp.zeros_like(dw_acc)
            sums[...] = jnp.zeros_like(sums)

        def halves(v):
            part = v[0:SUBLANES, :]
            for q in range(SUBLANES, ROW_CHUNK, SUBLANES):
                part = part + v[q:q + SUBLANES, :]
            return part

        for j in range(tb // ROW_CHUNK):
            rs = _row_chunk(j)
            du1, du2, xhat = du1_of(u1_ref[rs, :], du3_ref[rs, :].astype(F32), g_ref[...], b_ref[...])
            dext[rs, :] = du1
            sums[0:SUBLANES, :] += halves(du1)
            sums[SUBLANES:2 * SUBLANES, :] += halves(du2 * xhat)
            sums[2 * SUBLANES:3 * SUBLANES, :] += halves(du2)
        for r0 in range(0, tb, CONV_ROWS):
            rows = slice(r0, r0 + CONV_ROWS)
            for c0 in range(0, D, LANES):
                cols = slice(c0, c0 + LANES)
                acc = [jnp.zeros((CONV_ROWS, LANES), F32)]

                def tap(k, src):
                    acc[0] = acc[0] + src * w_ref[k:k + 1, cols]

                _for_taps(dext, r0, cols, bwd_groups, aligned, tap)
                c_val = cv[rows, cols].astype(F32)
                sg = _sigmoid(cg[rows, cols].astype(F32))
                dz_ref[rows, cols] = (acc[0] * sg).astype(dz_ref.dtype)
                dz_ref[rows, D + c0:D + c0 + LANES] = (acc[0] * c_val * sg * (1.0 - sg)).astype(dz_ref.dtype)
                du1_blk = dext[rows, cols]

                def tap_w(k, src):
                    prod = du1_blk * src
                    part = prod[0:SUBLANES, :]
                    for j in range(SUBLANES, CONV_ROWS, SUBLANES):
                        part = part + prod[j:j + SUBLANES, :]
                    dw_acc[k * SUBLANES:(k + 1) * SUBLANES, cols] += part

                _for_taps(uext, r0, cols, fwd_groups, aligned, tap_w)

        @pl.when(i == nb - 1)
        def _():
            dw_ref[...] = jnp.zeros_like(dw_ref)
            for k in range(CONV_C_W):
                dw_ref[k:k + 1, :] = jnp.sum(dw_acc[k * SUBLANES:(k + 1) * SUBLANES, :], axis=0, keepdims=True)
            for n, ref in enumerate((dcb_ref, dg_ref, db_ref)):
                ref[...] = jnp.sum(sums[n * SUBLANES:(n + 1) * SUBLANES, :], axis=0, keepdims=True)

    vec = _full((1, D))
    row = _col(tb, D, 0)
    nxt = _next(tb, H, D, 0, T)
    return _call(body, name='mixc_bwd',
                 out_shape=(_sds(dz2.shape, dz2.dtype), _sds((32, D), F32), _sds((1, D), F32), _sds((1, D), F32),
                            _sds((1, D), F32)), grid=(nb,),
                 in_specs=[_col(tb, D, 0), _col(tb, D, 1), _prev(tb, H, D, 0), _prev(tb, H, D, 1), _full((32, D)), vec,
                           vec, row, nxt, row, nxt, pl.BlockSpec(memory_space=pl.ANY)],
                 out_specs=(_col(tb, 2 * D, 0), _full((32, D)), vec, vec, vec),
                 scratch=[pltpu.VMEM((H + tb, D), F32), pltpu.VMEM((tb + H, D), F32),
                          pltpu.VMEM((32 * SUBLANES, D), F32), pltpu.VMEM((SUBLANES, CONV_ROWS + H, LANES), F32),
                          pltpu.VMEM((3 * SUBLANES, D), F32)],
                 sem=('arbitrary',), aliases={11: 0})(z, z, z, z, w, gain, bias, u1, u1, du3, du3, dz2)


def _gate_fwd(z, bg, ya, yb, yc):
    T = z.shape[0]
    D = ya.shape[1]
    tb = _pick(T, (256, 128))

    def body(ga, gb, gc, bg_ref, ya_ref, yb_ref, yc_ref, o_ref):
        acc = jnp.zeros((tb, D), F32)
        for n, (g, y) in enumerate(((ga, ya_ref), (gb, yb_ref), (gc, yc_ref))):
            acc = acc + _sigmoid(g[...].astype(F32) + bg_ref[:, n * D:(n + 1) * D]) * y[...].astype(F32)
        o_ref[...] = acc.astype(o_ref.dtype)

    row = _col(tb, D, 0)
    return _call(body, name='gate_fwd', out_shape=_sds((T, D), BF16), grid=(T // tb,),
                 in_specs=[_col(tb, D, 2), _col(tb, D, 3), _col(tb, D, 4), _full((1, 3 * D)), row, row, row],
                 out_specs=row, sem=('parallel',))(z, z, z, bg, ya, yb, yc)


def _gate_bwd(z, bg, ya, yb, yc, dm):
    T = z.shape[0]
    D = ya.shape[1]
    tb = _pick(T, (256, 128))

    def body(ga, gb, gc, bg_ref, ya_ref, yb_ref, yc_ref, dm_ref, dya, dyb, dyc, dg_ref, dbg_ref):
        @pl.when(pl.program_id(0) == 0)
        def _():
            dbg_ref[...] = jnp.zeros_like(dbg_ref)

        dmv = dm_ref[...]
        for n, (g, y, dy) in enumerate(((ga, ya_ref, dya), (gb, yb_ref, dyb), (gc, yc_ref, dyc))):
            cols = slice(n * D, (n + 1) * D)
            sg = _sigmoid(g[...].astype(F32) + bg_ref[:, cols])
            dy[...] = (dmv * sg).astype(dy.dtype)
            dg = dmv * y[...].astype(F32) * sg * (1.0 - sg)
            dg_ref[:, 2 * D + n * D:2 * D + (n + 1) * D] = dg.astype(dg_ref.dtype)
            dbg_ref[:, cols] += jnp.sum(dg, axis=0, keepdims=True)

    row = _col(tb, D, 0)
    act = _sds((T, D), BF16)
    return _call(body, name='gate_bwd', out_shape=(act, act, act, _sds((T, 5 * D), BF16), _sds((1, 3 * D), F32)),
                 grid=(T // tb,),
                 in_specs=[_col(tb, D, 2), _col(tb, D, 3), _col(tb, D, 4), _full((1, 3 * D)), row, row, row, row],
                 out_specs=(row, row, row, _col(tb, 5 * D, 0), _full((1, 3 * D))),
                 sem=('arbitrary',))(z, z, z, bg, ya, yb, yc, dm)


def _xattn_probs(qs, ks, scale):
    s = lax.dot_general(qs, ks, _DN['nt'], preferred_element_type=F32) * scale
    p = jnp.exp(s - jnp.max(s, axis=-1, keepdims=True))
    return p * (1.0 / jnp.sum(p, axis=-1, keepdims=True))


def _xattn_fwd(q, kv):
    T, DX = q.shape
    M = kv.shape[0]
    dh = DX // X_HEADS
    tb = _pick(T, (512, 256, 128))
    scale = dh ** -0.5

    def body(q_ref, kv_ref, o_ref):
        for h in range(X_HEADS):
            hs = slice(h * dh, (h + 1) * dh)
            p = _xattn_probs(q_ref[:, hs], kv_ref[:, hs], scale)
            o_ref[:, hs] = jnp.dot(p.astype(BF16), kv_ref[:, DX + h * dh:DX + (h + 1) * dh],
                                   preferred_element_type=F32).astype(o_ref.dtype)

    row = _col(tb, DX, 0)
    return _call(body, name='xattn_fwd', out_shape=_sds((T, DX), BF16), grid=(T // tb,),
                 in_specs=[row, _full((M, 2 * DX))], out_specs=row, sem=('parallel',))(q, kv)


def _xattn_bwd(q, kv, do):
    T, DX = q.shape
    M = kv.shape[0]
    dh = DX // X_HEADS
    tb = _pick(T, (512, 256, 128))
    scale = dh ** -0.5

    def body(q_ref, kv_ref, do_ref, dq_ref, dkv_ref):
        @pl.when(pl.program_id(0) == 0)
        def _():
            dkv_ref[...] = jnp.zeros_like(dkv_ref)

        for h in range(X_HEADS):
            hs = slice(h * dh, (h + 1) * dh)
            vs_cols = slice(DX + h * dh, DX + (h + 1) * dh)
            qs, ks, vs = q_ref[:, hs], kv_ref[:, hs], kv_ref[:, vs_cols]
            dos = do_ref[:, hs].astype(BF16)
            p = _xattn_probs(qs, ks, scale)
            dp = lax.dot_general(dos, vs, _DN['nt'], preferred_element_type=F32)
            ds = p * (dp - jnp.sum(p * dp, axis=-1, keepdims=True))
            dsb = ds.astype(BF16)
            dq_ref[:, hs] = (jnp.dot(dsb, ks, preferred_element_type=F32) * scale).astype(dq_ref.dtype)
            dkv_ref[:, hs] += lax.dot_general(dsb, qs, _DN['tn'], preferred_element_type=F32) * scale
            dkv_ref[:, vs_cols] += lax.dot_general(p.astype(BF16), dos, _DN['tn'], preferred_element_type=F32)

    row = _col(tb, DX, 0)
    return _call(body, name='xattn_bwd', out_shape=(_sds((T, DX), BF16), _sds((M, 2 * DX), F32)), grid=(T // tb,),
                 in_specs=[row, _full((M, 2 * DX)), row], out_specs=(row, _full((M, 2 * DX))),
                 sem=('arbitrary',))(q, kv, do)


def _swiglu_fwd(gu):
    T, F2 = gu.shape
    F = F2 // 2
    tb = _pick(T, (256, 128))

    def body(g_ref, u_ref, o_ref):
        g = g_ref[...].astype(F32)
        o_ref[...] = (g * _sigmoid(g) * u_ref[...].astype(F32)).astype(o_ref.dtype)

    return _call(body, name='swiglu_fwd', out_shape=_sds((T, F), BF16), grid=(T // tb,),
                 in_specs=[_col(tb, F, 0), _col(tb, F, 1)], out_specs=_col(tb, F, 0), sem=('parallel',))(gu, gu)


def _swiglu_bwd(gu, da):
    T, F2 = gu.shape
    F = F2 // 2
    tb = _pick(T, (256, 128))

    def body(g_ref, u_ref, da_ref, o_ref):
        g = g_ref[...].astype(F32)
        u = u_ref[...].astype(F32)
        dav = da_ref[...].astype(F32)
        sg = _sigmoid(g)
        o_ref[:, 0:F] = (dav * u * (sg * (1.0 + g * (1.0 - sg)))).astype(o_ref.dtype)
        o_ref[:, F:F2] = (dav * g * sg).astype(o_ref.dtype)

    return _call(body, name='swiglu_bwd', out_shape=_sds((T, F2), BF16), grid=(T // tb,),
                 in_specs=[_col(tb, F, 0), _col(tb, F, 1), _col(tb, F, 0)], out_specs=_col(tb, F2, 0),
                 sem=('parallel',))(gu, gu, da)


def _place():
    x, y, c = lax.axis_index('x'), lax.axis_index('y'), lax.axis_index('c')
    return x, y, c


def _allgather(arrs, lead, name):
    n = len(arrs)

    def out_shape(a, ld):
        return (N_DEV,) + a.shape if ld else (a.shape[0], N_DEV) + a.shape[1:]

    def body(*refs):
        ins, outs = refs[:n], refs[n:2 * n]
        send_sems, recv_sems, local_sems = refs[2 * n:]
        x, y, c = _place()
        me, sibling = (x, y, c), (x, y, 1 - c)
        chips = [(1 - x, y), (x, 1 - y), (1 - x, 1 - y)]

        def slot(a, dev):
            idx = 4 * dev[0] + 2 * dev[1] + dev[2]
            return outs[a].at[idx] if lead[a] else outs[a].at[:, idx]

        def copy(a, k, block, to, src=None):
            return pltpu.make_async_remote_copy(
                src_ref=slot(a, block) if src is None else src, dst_ref=slot(a, block),
                send_sem=send_sems.at[a * 7 + k], recv_sem=recv_sems.at[a * 7 + k], device_id=to, device_id_type=MESH)

        mine = [pltpu.make_async_copy(ins[a], slot(a, me), local_sems.at[a]) for a in range(n)]
        for cp in mine:
            cp.start()
        first = []
        for a in range(n):
            first.append(copy(a, 0, me, sibling, src=ins[a]))
            first += [copy(a, 1 + j, me, (*chip, c), src=ins[a]) for j, chip in enumerate(chips)]
        for cp in first:
            cp.start()
        passed = []
        for a in range(n):
            for j, chip in enumerate(chips):
                copy(a, 1 + j, (*chip, c), me).wait_recv()
                fwd = copy(a, 4 + j, (*chip, c), sibling)
                fwd.start()
                passed.append(fwd)
        for a in range(n):
            copy(a, 0, sibling, me).wait_recv()
            for j, chip in enumerate(chips):
                copy(a, 4 + j, (*chip, 1 - c), me).wait_recv()
        for cp in first + passed:
            cp.wait_send()
        for cp in mine:
            cp.wait()

    any_spec = pl.BlockSpec(memory_space=pl.ANY)
    return _call(body, name=name, out_shape=tuple(_sds(out_shape(a, ld), a.dtype) for a, ld in zip(arrs, lead)),
                 in_specs=[any_spec] * n, out_specs=tuple([any_spec] * n),
                 scratch=[pltpu.SemaphoreType.DMA((7 * n,)), pltpu.SemaphoreType.DMA((7 * n,)),
                          pltpu.SemaphoreType.DMA((n,))])(*arrs)


_PEER_ORDER = (1, 4, 2, 5, 3, 6, 7)
_HBM = pl.BlockSpec(memory_space=pltpu.HBM)
_SEM = pl.BlockSpec(memory_space=pltpu.SEMAPHORE)


def _my_index():
    x, y, c = _place()
    return 4 * x + 2 * y + c


def _remote(src, dst, send_sems, recv_sems, k, dev):
    return pltpu.make_async_remote_copy(src_ref=src, dst_ref=dst, send_sem=send_sems.at[k], recv_sem=recv_sems.at[k],
                                        device_id=dev, device_id_type=MESH)


def _own_part(src, kind, args):
    return src if args is None else src.at[pl.ds(args[0], args[1])]


def _block(land, kind, args, owner):
    if kind == 'rows':
        _, n, stride, d0 = args
        return land.at[pl.ds(pl.multiple_of(stride * owner + d0, ROW_TILE), n)]
    return land.at[owner]


def _chunk(src, args, dest):
    stride, n = args
    return src.at[pl.ds(pl.multiple_of(stride * dest, ROW_TILE), n)]


def _scatter_copies(plan, srcs, lands, send_sems, recv_sems):
    x, y, c = _place()
    me = 4 * x + 2 * y + c
    copies = []
    for r in _PEER_ORDER:
        px, py, pc = x ^ (r >> 2), y ^ ((r >> 1) & 1), c ^ (r & 1)
        for si, li, _, args in plan:
            copies.append(_remote(_chunk(srcs[si], args, 4 * px + 2 * py + pc), lands[li].at[me], send_sems, recv_sems,
                                  r - 1, (px, py, pc)))
    return copies


def _gather_copies(plan, srcs, lands, send_sems, recv_sems, passing):
    x, y, c = _place()
    me = 4 * x + 2 * y + c
    sibling = (x, y, 1 - c)
    chips = [(1 - x, y), (x, 1 - y), (1 - x, 1 - y)]
    copies = []
    if passing:
        for k, chip in enumerate(chips):
            for _, li, kind, args in plan:
                blk = _block(lands[li], kind, args, 4 * chip[0] + 2 * chip[1] + c)
                copies.append((k, _remote(blk, blk, send_sems, recv_sems, k, sibling)))
    else:
        for k, dev in enumerate([sibling] + [(*chip, c) for chip in chips]):
            for si, li, kind, args in plan:
                dst = _block(lands[li], kind, args, me)
                src = dst if srcs is None else _own_part(srcs[si], kind, args)
                copies.append((k, _remote(src, dst, send_sems, recv_sems, k, dev)))
    return copies


def _sem_call(name, bufs, sems_in, new_sems, after, action):
    nb, ni, nn = len(bufs), len(sems_in), len(new_sems)

    def body(*refs):
        action(refs[:nb], refs[nb:nb + ni], refs[-1 - nb - nn:-1 - nb])
        refs[-1][...] = jnp.zeros_like(refs[-1])

    bufs = [pltpu.with_memory_space_constraint(v, pltpu.HBM) for v in bufs]
    res = pl.pallas_call(
        body, name=name,
        out_shape=(*[pltpu.SemaphoreType.DMA((n,)) for n in new_sems], *[pltpu.HBM(v.shape, v.dtype) for v in bufs],
                   _sds((8, LANES), F32)),
        in_specs=[_HBM] * nb + [_SEM] * ni + [pl.BlockSpec(memory_space=pl.ANY)] * len(after),
        out_specs=(*[_SEM] * nn, *[_HBM] * nb, pl.BlockSpec(memory_space=pltpu.VMEM)),
        input_output_aliases={i: nn + i for i in range(nb)},
        compiler_params=pltpu.CompilerParams(has_side_effects=pltpu.SideEffectType.DATAFLOW_SIDE_EFFECTING),
        interpret=False)(*bufs, *sems_in, *after)
    return list(res[:nn]), list(res[nn:nn + nb]), res[-1]


def _place_own(srcs, plan, lands_like, name, deps):
    ns, nl, ne = len(srcs), len(lands_like), len(plan)

    def part_shape(entry):
        si, _, kind, args = entry
        rows = args[1] if (kind == 'chunk' or args is not None) else None
        return srcs[si].shape if rows is None else (rows,) + srcs[si].shape[1:]

    def body(*refs):
        src_r, land_r = refs[:ns], refs[ns:ns + nl]
        stage, sems_in, sems_out = refs[ns + nl:ns + nl + ne], refs[-2], refs[-1]
        me = _my_index()
        loads, stores = [], []
        for i, (si, li, kind, args) in enumerate(plan):
            if kind == 'chunk':
                src, dst = _chunk(src_r[si], args, me), land_r[li].at[me]
            else:
                src, dst = _own_part(src_r[si], kind, args), _block(land_r[li], kind, args, me)
            loads.append(pltpu.make_async_copy(src, stage[i], sems_in.at[i]))
            stores.append(pltpu.make_async_copy(stage[i], dst, sems_out.at[i]))
        for cp in loads:
            cp.start()
        for load, store in zip(loads, stores):
            load.wait()
            store.start()
        for cp in stores:
            cp.wait()

    anywhere = pl.BlockSpec(memory_space=pl.ANY)
    scratch = [pltpu.VMEM(part_shape(e), srcs[e[0]].dtype) for e in plan]
    scratch += [pltpu.SemaphoreType.DMA((ne,)), pltpu.SemaphoreType.DMA((ne,))]
    return list(_call(body, name=name, out_shape=tuple(lands_like), in_specs=[anywhere] * ns,
                      out_specs=tuple([anywhere] * nl), scratch=scratch, deps=deps)(*srcs))


def _scatter_start(srcs, lands, plan, after, name):
    ns = len(srcs)

    def action(bufs, _, new):
        for cp in _scatter_copies(plan, bufs[:ns], bufs[ns:], new[0], new[1]):
            cp.start()

    sems, bufs, token = _sem_call(name, [*srcs, *lands], [], [N_DEV - 1, N_DEV - 1], after, action)
    return sems, bufs[:ns], bufs[ns:], token


def _scatter_wait(sems, srcs, lands, plan, after, name):
    ns = len(srcs)

    def action(bufs, old, _):
        copies = _scatter_copies(plan, bufs[:ns], bufs[ns:], old[0], old[1])
        for cp in copies:
            cp.wait_send()
        for cp in copies:
            cp.wait_recv()

    return _sem_call(name, [*srcs, *lands], sems, [], after, action)[1][ns:]


def _gather_start(srcs, lands, plan, name):
    ns = len(srcs)

    def action(bufs, _, new):
        for _, cp in _gather_copies(plan, bufs[:ns], bufs[ns:], new[0], new[1], False):
            cp.start()

    sems, bufs, token = _sem_call(name, [*srcs, *lands], [], [4, 4], [], action)
    return sems, bufs[:ns], bufs[ns:], token


def _gather_pass(recv_sems, lands, plan, after, name):
    def action(bufs, old, new):
        for _, cp in _gather_copies(plan, None, bufs, old[0], old[0], False):
            cp.wait_recv()
        for _, cp in _gather_copies(plan, None, bufs, new[0], new[1], True):
            cp.start()

    return _sem_call(name, lands, [recv_sems], [3, 3], after, action)


def _gather_wait(send_sems, pass_sems, srcs, lands, plan, after, name):
    ns = len(srcs)

    def action(bufs, old, _):
        for _, cp in _gather_copies(plan, bufs[:ns], bufs[ns:], old[0], old[0], False):
            cp.wait_send()
        passed = _gather_copies(plan, None, bufs[ns:], old[1], old[2], True)
        for _, cp in passed:
            cp.wait_send()
        for _, cp in passed:
            cp.wait_recv()

    return _sem_call(name, [*srcs, *lands], [send_sems, *pass_sems], [], after, action)[1][ns:]


def _merge_edges(wt, edges, tiles_per_dev):
    D = wt.shape[1]

    def body(w_ref, e_ref, o_ref):
        s = pl.program_id(0)
        o_ref[...] = jnp.where(s == 0, e_ref[...], w_ref[...] + e_ref[...])

    tile = pl.BlockSpec((ROW_TILE, D), lambda s: (s * tiles_per_dev, 0))
    return _call(body, name='merge_edges', out_shape=_sds(wt.shape, wt.dtype), grid=(N_DEV,),
                 in_specs=[tile, pl.BlockSpec((None, ROW_TILE, D), lambda s: (s, 0, 0))], out_specs=tile,
                 sem=('arbitrary',), aliases={0: 0})(wt, edges)


def _adam_update(g, w, m, v):
    c1 = 1.0 - ADAM_B1 ** ADAM_STEP
    c2 = 1.0 - ADAM_B2 ** ADAM_STEP
    mn = ADAM_B1 * m + (1.0 - ADAM_B1) * g
    vn = ADAM_B2 * v + (1.0 - ADAM_B2) * (g * g)
    m_hat = mn / c1
    v_hat = vn / c2
    return -ADAM_LR * (m_hat / (jnp.sqrt(v_hat) + ADAM_EPS) + ADAM_WD * w), mn, vn


def _adamw(parts, w, m, v, name):
    n_parts = parts.shape[0]
    R, C = w.shape
    tb = _pick(R, (128, 64, 32, 16, 8))

    def body(p_ref, w_ref, m_ref, v_ref, g_out, d_out, m_out, v_out):
        g = p_ref[0].astype(F32)
        for s in range(1, n_parts):
            g = g + p_ref[s].astype(F32)
        g_out[...] = g
        d_out[...], m_out[...], v_out[...] = _adam_update(g, w_ref[...], m_ref[...], v_ref[...])

    row = pl.BlockSpec((tb, C), lambda i: (i, 0))
    o = _sds((R, C), F32)
    return _call(body, name=name, out_shape=(o, o, o, o), grid=(R // tb,),
                 in_specs=[pl.BlockSpec((n_parts, tb, C), lambda i: (0, i, 0)), row, row, row],
                 out_specs=(row, row, row, row), sem=('parallel',))(parts, w, m, v)


def _adamw_layer(parts, w, m, v, layer, prev, shift, name):
    L, n, C = w.shape
    n_p = parts.shape[1]
    cb = C if N_DEV * n_p * C * parts.dtype.itemsize <= (6 << 20) else 2 * LANES
    n_prev = 0 if prev is None else 4

    def body(sh_ref, p_ref, w_ref, m_ref, v_ref, *rest):
        g_out, d_out, m_out, v_out = rest[n_prev:n_prev + 4]
        g = p_ref[0].astype(F32)
        for s in range(1, N_DEV):
            g = g + p_ref[s].astype(F32)
        if n_p != n:
            rolled = rest[-1]
            rolled[...] = pltpu.roll(g, n_p - sh_ref[0], 0)
            g = rolled[0:n, :]
        g_out[...] = g
        d_out[...], m_out[...], v_out[...] = _adam_update(g, w_ref[...], m_ref[...], v_ref[...])

    lay = pl.BlockSpec((None, n, cb), lambda j, sh: (layer, 0, j))
    stack = _sds((L, n, C), F32)
    grid_spec = pltpu.PrefetchScalarGridSpec(
        num_scalar_prefetch=1, grid=(C // cb,),
        in_specs=[pl.BlockSpec((N_DEV, n_p, cb), lambda j, sh: (0, 0, j)), lay, lay, lay]
        + [pl.BlockSpec(memory_space=pl.ANY)] * n_prev,
        out_specs=(lay, lay, lay, lay), scratch_shapes=[pltpu.VMEM((n_p, cb), F32)] if n_p != n else [])
    return pl.pallas_call(body, name=name, grid_spec=grid_spec, out_shape=(stack, stack, stack, stack),
                          input_output_aliases={5 + k: k for k in range(n_prev)},
                          compiler_params=pltpu.CompilerParams(dimension_semantics=('parallel',),
                                                               vmem_limit_bytes=VMEM_LIMIT),
                          interpret=False)(shift, parts, w, m, v, *(prev or ()))


def kernel(x, mem, mix_norm, w_in, b_gate, b_forget, conv_a, w_out_a, w_out_b, conv_c, conv_c_bias, ln_c_gain, ln_c_bias, w_out_c, w_o, xattn_norm, mem_norm, w_xq, w_xkv, w_xo, ffn_norm, w_gate_up, w_down, final_norm, loss_target, m_mix_norm, m_w_in, m_b_gate, m_b_forget, m_conv_a, m_w_out_a, m_w_out_b, m_conv_c, m_conv_c_bias, m_ln_c_gain, m_ln_c_bias, m_w_out_c, m_w_o, m_xattn_norm, m_mem_norm, m_w_xq, m_w_xkv, m_w_xo, m_ffn_norm, m_w_gate_up, m_w_down, m_final_norm, v_mix_norm, v_w_in, v_b_gate, v_b_forget, v_conv_a, v_w_out_a, v_w_out_b, v_conv_c, v_conv_c_bias, v_ln_c_gain, v_ln_c_bias, v_w_out_c, v_w_o, v_xattn_norm, v_mem_norm, v_w_xq, v_w_xkv, v_w_xo, v_ffn_norm, v_w_gate_up, v_w_down, v_final_norm):
    P = dict(zip(ARG_NAMES, (x, mem, mix_norm, w_in, b_gate, b_forget, conv_a, w_out_a, w_out_b, conv_c, conv_c_bias, ln_c_gain, ln_c_bias, w_out_c, w_o, xattn_norm, mem_norm, w_xq, w_xkv, w_xo, ffn_norm, w_gate_up, w_down, final_norm, loss_target, m_mix_norm, m_w_in, m_b_gate, m_b_forget, m_conv_a, m_w_out_a, m_w_out_b, m_conv_c, m_conv_c_bias, m_ln_c_gain, m_ln_c_bias, m_w_out_c, m_w_o, m_xattn_norm, m_mem_norm, m_w_xq, m_w_xkv, m_w_xo, m_ffn_norm, m_w_gate_up, m_w_down, m_final_norm, v_mix_norm, v_w_in, v_b_gate, v_b_forget, v_conv_a, v_w_out_a, v_w_out_b, v_conv_c, v_conv_c_bias, v_ln_c_gain, v_ln_c_bias, v_w_out_c, v_w_o, v_xattn_norm, v_mem_norm, v_w_xq, v_w_xkv, v_w_xo, v_ffn_norm, v_w_gate_up, v_w_down, v_final_norm)))
    return _step(P)


_T_VIEW = ('w_in', 'w_gate_up', 'w_xo')
_BIG = [w for w in SHARDED if w not in ('conv_a', 'conv_c')]
_OTHER = [w for w in _BIG if w != 'w_in']
_CONVS = (('conv_a', CONV_A_W, 8), ('conv_c', CONV_C_W, 32))
_EARLY = ['w_down', 'w_gate_up', 'w_xo', 'w_xq', 'w_xkv', 'w_o']
_LATE = ['w_out_c', 'w_out_b', 'w_out_a', 'w_in']


def _view(name, a):
    return jnp.transpose(a, (0, 2, 1)) if name in _T_VIEW else a


def _step(P):
    L, D = P['mix_norm'].shape
    T = P['x'].shape[1]
    n_in = P['w_in'].shape[2]
    stride = n_in // ROW_TILE * ROW_TILE
    rem = n_in - stride
    win = stride + ROW_TILE
    assert rem * N_DEV == ROW_TILE
    r_in = n_in * N_DEV
    f_off = 6 * D
    hi_off = f_off + FOX_HEADS
    x_i, y_i, c_i = _place()
    me = 4 * x_i + 2 * y_i + c_i
    V = {}
    for w in _BIG:
        for k in (w, 'm_' + w, 'v_' + w):
            V[k] = _view(w, P[k])
    n_own = {w: V[w].shape[1] for w in _OTHER}
    shift = jnp.reshape(rem * me, (1,)).astype(jnp.int32)
    no_shift = jnp.zeros((1,), jnp.int32)
    row = lambda a, l: a[l][None, :]
    dus = lax.dynamic_update_slice

    conv_taps = {w: (taps, padded) for w, taps, padded in _CONVS}

    def gather_start(l, names, tag, deps):
        srcs, lands, plan = [], [], []
        for w in names:
            si, li = len(srcs), len(lands)
            if w == 'w_in':
                srcs.append(dus(jnp.zeros((win, D), BF16), V[w][l].astype(BF16), (rem * me, 0)))
                lands += [_sds((r_in, D), BF16), _sds((N_DEV, ROW_TILE, D), BF16)]
                plan += [(si, li, 'rows', (ROW_TILE, stride, stride, ROW_TILE)), (si, li + 1, 'slot', (0, ROW_TILE))]
            elif w in conv_taps:
                srcs.append(P[w][l])
                lands.append(_sds((N_DEV,) + srcs[si].shape, F32))
                plan.append((si, li, 'slot', None))
            else:
                srcs.append(V[w][l].astype(BF16))
                lands.append(_sds((N_DEV * n_own[w], srcs[si].shape[1]), BF16))
                plan.append((si, li, 'rows', (0, n_own[w], n_own[w], 0)))
        lands = _place_own(srcs, plan, lands, 'gather_own', deps)
        sems, srcs, lands, token = _gather_start(srcs, lands, plan, f'gather_start_{tag}{l}')
        return dict(names=names, name=f'{tag}{l}', plan=plan, send=sems[0], recv=sems[1], srcs=srcs, lands=lands,
                    token=token)

    def gather_pass(g, after):
        g['pass'], g['lands'], g['token'] = _gather_pass(g['recv'], g['lands'], g['plan'], after,
                                                        'gather_pass_' + g['name'])

    def gather_wait(g, after):
        lands = iter(_gather_wait(g['send'], g['pass'], g['srcs'], g['lands'], g['plan'], after,
                                  'gather_wait_' + g['name']))
        W = {}
        for w in g['names']:
            if w == 'w_in':
                W[w] = _merge_edges(next(lands), next(lands), stride // ROW_TILE)
            elif w in conv_taps:
                taps, padded = conv_taps[w]
                full = jnp.transpose(next(lands), (1, 0, 2)).reshape(taps, D)
                W[w] = jnp.pad(full, ((0, padded - taps), (0, 0)))
            else:
                W[w] = next(lands)
        return W

    first_part = ['w_in'] + list(conv_taps)
    gathers = [[gather_start(0, first_part, 'a', [])]]
    gathers[0].append(gather_start(0, _OTHER, 'b', [gathers[0][0]['token']]))
    for l in range(1, L):
        gathers.append([gather_start(l, first_part + _OTHER, 'a', [gathers[-1][-1]['token']])])
    gather_pass(gathers[0][0], [gathers[-1][-1]['token']])

    mem_x = P['mem'][0]
    mem_n = _rms_fwd(mem_x, P['mem_norm'][None, :], 'rms_mem')
    xs = P['x'][0]
    saved = []
    for l in range(L):
        W = gather_wait(gathers[l][0], [xs])
        s = {'x0': xs, 'W': W}
        wt = W['w_in']
        s['h1'] = _rms_fwd(xs, row(P['mix_norm'], l), 'rms_mix')
        s['z1'] = z1 = _mm(s['h1'], wt, mode='nt', b_rows=(0, f_off), name='mm_in_lo')
        s['z2'] = z2 = _mm(s['h1'], wt, mode='nt', b_rows=(hi_off, r_in - hi_off), name='mm_in_hi')
        s['zf'] = _mm(s['h1'], wt, mode='nt', b_rows=(f_off, FOX_HEADS), out_dtype=F32, name='mm_in_f')
        s['ya_pre'] = _mixa_fwd(z1, W['conv_a'])
        s['c'] = _fox_c(s['zf'], row(P['b_forget'], l))
        s['o'], s['lse'] = _fox_fwd(z1, s['c'])
        for g in gathers[l][1:]:
            gather_pass(g, [s['o']])
        s['u3'], s['u1'] = _mixc_fwd(z2, W['conv_c'], row(P['conv_c_bias'], l), row(P['ln_c_gain'], l),
                                     row(P['ln_c_bias'], l))
        for g in gathers[l][1:]:
            W.update(gather_wait(g, [s['u3']]))
        s['ya'] = _mm(s['ya_pre'], W['w_out_a'], mode='nn', name='mm_out_a')
        s['yb'] = _mm(s['o'], W['w_out_b'], mode='nn', name='mm_out_b')
        s['yc'] = _mm(s['u3'], W['w_out_c'], mode='nn', name='mm_out_c')
        s['merged'] = _gate_fwd(z2, row(P['b_gate'], l), s['ya'], s['yb'], s['yc'])
        xs = _mm(s['merged'], W['w_o'], mode='nn', add=xs, out_dtype=F32, name='mm_o')
        s['x1'] = xs
        deps = []
        if l + 1 < L:
            gather_pass(gathers[l + 1][0], [xs])
            deps = [gathers[l + 1][0]['token']]
        s['h2'] = _rms_fwd(xs, row(P['xattn_norm'], l), 'rms_xattn', deps=deps)
        s['qx'] = _mm(s['h2'], W['w_xq'], mode='nn', name='mm_xq')
        s['kv'] = _mm(mem_n, W['w_xkv'], mode='nn', name='mm_xkv')
        s['ox'] = _xattn_fwd(s['qx'], s['kv'])
        xs = _mm(s['ox'], W['w_xo'], mode='nt', add=xs, out_dtype=F32, name='mm_xo')
        s['x2'] = xs
        s['h3'] = _rms_fwd(xs, row(P['ffn_norm'], l), 'rms_ffn')
        s['gu'] = _mm(s['h3'], W['w_gate_up'], mode='nt', name='mm_gate_up')
        s['act'] = _swiglu_fwd(s['gu'])
        xs = _mm(s['act'], W['w_down'], mode='nn', add=xs, out_dtype=F32, name='mm_down')
        saved.append(s)

    dx, d_final_norm, loss_part = _final(xs, P['final_norm'][None, :], P['loss_target'][0])

    GS = {w: [None] * L for w in SMALL}
    G_conv = {'conv_a': [None] * L, 'conv_c': [None] * L}
    d_mem_n = None
    results = {w: None for w in _BIG}

    def exchange_start(G, names, l, tag, after):
        srcs = [G[w] for w in names]
        plan, lands = [], []
        for i, w in enumerate(names):
            step, n = (stride, win) if w == 'w_in' else (n_own[w], n_own[w])
            lands.append(_sds((N_DEV, n, srcs[i].shape[1]), BF16))
            plan.append((i, i, 'chunk', (step, n)))
        lands = _place_own(srcs, plan, lands, 'exchange_own', [])
        sems, srcs, lands, token = _scatter_start(srcs, lands, plan, after, f'exchange_start_{tag}{l}')
        return (names, l, tag, plan, sems, srcs, lands, token)

    def exchange_finish(e, after):
        names, l, tag, plan, sems, srcs, lands, _ = e
        lands = _scatter_wait(sems, srcs, lands, plan, after, f'exchange_wait_{tag}{l}')
        for i, w in enumerate(names):
            results[w] = _adamw_layer(lands[i], V[w], V['m_' + w], V['v_' + w], l, results[w],
                                      shift if w == 'w_in' else no_shift, 'adamw_' + w)

    pending = []
    for l in reversed(range(L)):
        s = saved[l]
        W, z1, z2 = s['W'], s['z1'], s['z2']
        wt = W['w_in']
        G = {}
        d_act = _mm(dx, W['w_down'], mode='nt', name='mmb_down', deps=[e[-1] for e in pending])
        G['w_down'] = _mm(s['act'], dx, mode='tn', name='mmg_w_down')
        dgu = _swiglu_bwd(s['gu'], d_act)
        dh3 = _mm(dgu, W['w_gate_up'], mode='nn', out_dtype=F32, name='mmb_gate_up')
        G['w_gate_up'] = _mm(dgu, s['h3'], mode='tn', name='mmg_w_gate_up')
        dx, GS['ffn_norm'][l] = _rms_bwd(s['x2'], row(P['ffn_norm'], l), dh3, dx, 'rms_ffn_bwd')
        d_ox = _mm(dx, W['w_xo'], mode='nn', out_dtype=F32, name='mmb_xo')
        G['w_xo'] = _mm(dx, s['ox'], mode='tn', name='mmg_w_xo')
        dqx, dkv = _xattn_bwd(s['qx'], s['kv'], d_ox)
        dh2 = _mm(dqx, W['w_xq'], mode='nt', out_dtype=F32, name='mmb_xq')
        G['w_xq'] = _mm(s['h2'], dqx, mode='tn', name='mmg_w_xq')
        G['w_xkv'] = _mm(mem_n, dkv, mode='tn', name='mmg_w_xkv')
        d_mem_n = _mm(dkv, W['w_xkv'], mode='nt', add=d_mem_n, out_dtype=F32, name='mmb_xkv')
        dx, GS['xattn_norm'][l] = _rms_bwd(s['x1'], row(P['xattn_norm'], l), dh2, dx, 'rms_xattn_bwd')
        dm = _mm(dx, W['w_o'], mode='nt', out_dtype=F32, name='mmb_o')
        G['w_o'] = _mm(s['merged'], dx, mode='tn', name='mmg_w_o')
        early = exchange_start(G, _EARLY, l, 'a', [])
        dya, dyb, dyc, dz2, GS['b_gate'][l] = _gate_bwd(z2, row(P['b_gate'], l), s['ya'], s['yb'], s['yc'], dm)
        du3 = _mm(dyc, W['w_out_c'], mode='nt', out_dtype=F32, name='mmb_out_c', deps=[early[-1]])
        G['w_out_c'] = _mm(s['u3'], dyc, mode='tn', name='mmg_w_out_c')
        dz2, G_conv['conv_c'][l], GS['conv_c_bias'][l], GS['ln_c_gain'][l], GS['ln_c_bias'][l] = _mixc_bwd(
            z2, W['conv_c'], row(P['ln_c_gain'], l), row(P['ln_c_bias'], l), s['u1'], du3, dz2)
        dya_pre = _mm(dya, W['w_out_a'], mode='nt', out_dtype=F32, name='mmb_out_a')
        G['w_out_a'] = _mm(s['ya_pre'], dya, mode='tn', name='mmg_w_out_a')
        dz1, G_conv['conv_a'][l] = _mixa_bwd(z1, W['conv_a'], dya_pre)
        do = _mm(dyb, W['w_out_b'], mode='nt', name='mmb_out_b')
        G['w_out_b'] = _mm(s['o'], dyb, mode='tn', name='mmg_w_out_b')
        dz1, dc_pairs = _fox_bwd(z1, s['c'], s['o'], s['lse'], do, dz1)
        hp = FOX_HEADS // dc_pairs.shape[0]
        dc = jnp.transpose(dc_pairs[:, :, :hp], (1, 0, 2)).reshape(T, FOX_HEADS)
        dzf, GS['b_forget'][l] = _fox_c_bwd(dc, s['zf'], row(P['b_forget'], l))
        segs = [(dzf, f_off), (dz1, 0), (dz2, hi_off)]
        dh1, g_in = None, r_in
        for dz, off in segs:
            dh1 = _mm(dz, wt, mode='nn', b_rows=(off, dz.shape[1]), add=dh1, out_dtype=F32, name='mmb_in')
            g_in = _mm(dz, s['h1'], mode='tn', out=g_in, o_off=off, name='mmg_w_in')
        G['w_in'] = g_in
        dx, GS['mix_norm'][l] = _rms_bwd(s['x0'], row(P['mix_norm'], l), dh1, dx, 'rms_mix_bwd')
        for e in pending:
            exchange_finish(e, [dx])
        pending = [early]
        if l > 0:
            pending.append(exchange_start(G, _LATE, l, 'b', []))

    _, d_mem_norm = _rms_bwd(mem_x, P['mem_norm'][None, :], d_mem_n, None, 'rms_mem_bwd')

    small_parts = {w: jnp.concatenate(GS[w], axis=0) for w in SMALL if w not in ('mem_norm', 'final_norm')}
    small_parts['mem_norm'] = d_mem_norm
    small_parts['final_norm'] = d_final_norm
    conv_parts = [jnp.stack(G_conv[w])[:, :taps] for w, taps, _ in _CONVS]
    sizes = [P[w].size for w in SMALL]
    conv_sizes = [c.size for c in conv_parts]
    n_small = sum(sizes) + sum(conv_sizes) + LANES
    n_rows = -(-n_small // (8 * LANES)) * 8

    def pack(parts):
        flat = jnp.concatenate([p.reshape(-1) for p in parts])
        return jnp.pad(flat, (0, n_rows * LANES - flat.size)).reshape(n_rows, LANES)

    g_small = pack([small_parts[w] for w in SMALL] + conv_parts + [loss_part])
    (all_small,) = _allgather([g_small], [True], 'gather_small')
    pending.append(exchange_start(G, _LATE, 0, 'b', [all_small]))
    sm = _adamw(all_small, pack([P[w] for w in SMALL]), pack([P['m_' + w] for w in SMALL]),
                pack([P['v_' + w] for w in SMALL]), 'adamw_small')
    g_all = sm[0].reshape(-1)
    loss = g_all[sum(sizes) + sum(conv_sizes)]
    final = {}
    off = 0
    for w, n in zip(SMALL, sizes):
        final[w] = [o.reshape(-1)[off:off + n].reshape(P[w].shape) for o in sm]
        off += n
    for (w, taps, _), c, n in zip(_CONVS, conv_parts, conv_sizes):
        n_col = P[w].shape[2]
        g_own = lax.dynamic_slice(g_all[off:off + n].reshape(c.shape), (0, 0, n_col * me), (L, taps, n_col))
        flat = lambda a: a.reshape(L * taps, n_col)
        outs = _adamw(flat(g_own)[None], flat(P[w]), flat(P['m_' + w]), flat(P['v_' + w]), 'adamw_' + w)
        final[w] = [o.reshape(P[w].shape) for o in outs]
        off += n

    for e in pending:
        exchange_finish(e, [sm[0], final['conv_c'][0]])
    for w in _BIG:
        final[w] = [_view(w, o) for o in results[w]]

    out = [loss, dx[None]]
    for k in range(4):
        out += [final[w][k] for w in WEIGHTS]
    return tuple(out)
```

```python
import functools

import jax
import jax.numpy as jnp
from jax import lax
from jax.experimental import pallas as pl
from jax.experimental.pallas import tpu as pltpu

F32 = jnp.float32
BF16 = jnp.bfloat16
MESH = pl.DeviceIdType.MESH

N_DEV = 8
EPS = 1e-6
FOX_HEADS = 16
X_HEADS = 4
LANES = 128
ROW_TILE = 16
HALO_A = 16
HALO_C = 32
SUBLANES = 8
CONV_ROWS = 128
ROW_CHUNK = 32
CONV_A_W = 3
CONV_C_W = 31
VMEM_LIMIT = 56 << 20

ADAM_LR = 0.001
ADAM_B1 = 0.9
ADAM_B2 = 0.999
ADAM_EPS = 1e-08
ADAM_WD = 0.01
ADAM_STEP = 10

WEIGHTS = ['mix_norm', 'w_in', 'b_gate', 'b_forget', 'conv_a', 'w_out_a', 'w_out_b', 'conv_c', 'conv_c_bias',
           'ln_c_gain', 'ln_c_bias', 'w_out_c', 'w_o', 'xattn_norm', 'mem_norm', 'w_xq', 'w_xkv', 'w_xo',
           'ffn_norm', 'w_gate_up', 'w_down', 'final_norm']
SHARDED = ['w_in', 'conv_a', 'w_out_a', 'w_out_b', 'conv_c', 'w_out_c', 'w_o', 'w_xq', 'w_xkv', 'w_xo',
           'w_gate_up', 'w_down']
ROW_SHARDED = ['w_out_a', 'w_out_b', 'w_out_c', 'w_o', 'w_xq', 'w_xkv', 'w_down']
SMALL = [w for w in WEIGHTS if w not in SHARDED]
ARG_NAMES = (['x', 'mem'] + WEIGHTS + ['loss_target'] + ['m_' + w for w in WEIGHTS] + ['v_' + w for w in WEIGHTS])


def _pick(n, cands=(1024, 1408, 512, 256, 128)):
    for c in cands:
        if n % c == 0:
            return c
    return n


def _call(body, *, name, out_shape, grid=(), in_specs=None, out_specs=None, scratch=(), sem=None, aliases=None,
          deps=()):
    params = dict(vmem_limit_bytes=VMEM_LIMIT)
    if sem is not None:
        params['dimension_semantics'] = sem
    n_in, n_dep = len(in_specs), len(deps)

    def body_without_deps(*refs):
        body(*refs[:n_in], *refs[n_in + n_dep:])

    call = pl.pallas_call(body_without_deps, name=name, out_shape=out_shape, grid=grid, scratch_shapes=list(scratch),
                          in_specs=list(in_specs) + [pl.BlockSpec(memory_space=pl.ANY)] * n_dep, out_specs=out_specs,
                          input_output_aliases=aliases or {}, compiler_params=pltpu.CompilerParams(**params),
                          interpret=False)
    return lambda *args: call(*args, *deps)


def _sds(shape, dtype):
    return jax.ShapeDtypeStruct(tuple(shape), dtype)


def _sigmoid(x):
    return 1.0 / (1.0 + jnp.exp(-x))


_DN = {'nn': (((1,), (0,)), ((), ())), 'nt': (((1,), (1,)), ((), ())), 'tn': (((0,), (0,)), ((), ()))}


def _mm(a, b, *, mode, name, out_dtype=BF16, add=None, b_rows=None, out=None, o_off=0, deps=()):
    if mode == 'tn':
        K, M = a.shape
    else:
        M, K = a.shape
    b_off, b_n = (0, b.shape[0]) if b_rows is None else b_rows
    N = b_n if mode == 'nt' else b.shape[1]
    assert (b.shape[1] if mode == 'nt' else b_n) == K
    tm, tn = _pick(M), _pick(N)
    tk = K if K <= 2048 else _pick(K)
    ni, nj, nk = M // tm, N // tn, K // tk
    a_bytes, b_bytes = M * K * a.dtype.itemsize, K * N * b.dtype.itemsize
    n_outer = (b_bytes + nj * a_bytes) < (a_bytes + ni * b_bytes)
    if n_outer:
        grid = (nj, ni, nk)
        ij = lambda g0, g1: (g1, g0)
    else:
        grid = (ni, nj, nk)
        ij = lambda g0, g1: (g0, g1)

    def a_map(g0, g1, k):
        i, _ = ij(g0, g1)
        return (k, i) if mode == 'tn' else (i, k)

    def elems(blk):
        return tuple(pl.Element(s) for s in blk)

    def at(rows, cols):
        return pl.multiple_of(rows, ROW_TILE), pl.multiple_of(cols, LANES)

    a_blk = (tk, tm) if mode == 'tn' else (tm, tk)
    b_blk = (tn, tk) if mode == 'nt' else (tk, tn)
    if b_rows is None:
        def b_map(g0, g1, k):
            _, j = ij(g0, g1)
            return (j, k) if mode == 'nt' else (k, j)
        b_spec = pl.BlockSpec(b_blk, b_map)
    else:
        def b_map(g0, g1, k):
            _, j = ij(g0, g1)
            return at(b_off + j * tn, k * tk) if mode == 'nt' else at(b_off + k * tk, j * tn)
        b_spec = pl.BlockSpec(elems(b_blk), b_map)
    in_specs = [pl.BlockSpec(a_blk, a_map), b_spec]
    args = [a, b]
    has_add = add is not None
    if has_add:
        in_specs.append(pl.BlockSpec((tm, tn), lambda g0, g1, k: ij(g0, g1)))
        args.append(add)
    aliases = {}
    if out is None:
        out_shape = _sds((M, N), out_dtype)
        out_spec = pl.BlockSpec((tm, tn), lambda g0, g1, k: ij(g0, g1))
    else:
        if isinstance(out, int):
            out_shape = _sds((out, N), out_dtype)
        else:
            out_shape = _sds(out.shape, out.dtype)
            in_specs.append(pl.BlockSpec(memory_space=pl.ANY))
            aliases = {len(args): 0}
            args.append(out)

        def o_map(g0, g1, k):
            i, j = ij(g0, g1)
            return at(o_off + i * tm, j * tn)
        out_spec = pl.BlockSpec(elems((tm, tn)), o_map)
    dn = _DN[mode]
    n_in = len(args)

    def body(*refs):
        a_ref, b_ref = refs[0], refs[1]
        add_ref = refs[2] if has_add else None
        o_ref = refs[n_in]
        part = lax.dot_general(a_ref[...].astype(BF16), b_ref[...].astype(BF16), dn, preferred_element_type=F32)

        def finish(r):
            if has_add:
                r = r + add_ref[...]
            o_ref[...] = r.astype(o_ref.dtype)

        if nk == 1:
            finish(part)
        else:
            acc = refs[n_in + 1]
            k = pl.program_id(2)

            @pl.when(k == 0)
            def _():
                acc[...] = part

            @pl.when(k > 0)
            def _():
                acc[...] += part

            @pl.when(k == nk - 1)
            def _():
                finish(acc[...])

    scratch = [pltpu.VMEM((tm, tn), F32)] if nk > 1 else []
    return _call(body, name=name, out_shape=out_shape, grid=grid, in_specs=in_specs, out_specs=out_spec,
                 scratch=scratch, sem=('parallel', 'parallel', 'arbitrary'), aliases=aliases, deps=deps)(*args)


def _rms_fwd(x, g, name, deps=()):
    R, D = x.shape
    tb = _pick(R, (512, 256, 128))

    def body(x_ref, g_ref, o_ref):
        xv = x_ref[...]
        r = lax.rsqrt(jnp.mean(xv * xv, axis=-1, keepdims=True) + EPS)
        o_ref[...] = (xv * r * g_ref[...]).astype(o_ref.dtype)

    row = pl.BlockSpec((tb, D), lambda i: (i, 0))
    vec = pl.BlockSpec((1, D), lambda i: (0, 0))
    return _call(body, name=name, out_shape=_sds((R, D), BF16), grid=(R // tb,), in_specs=[row, vec],
                 out_specs=row, sem=('parallel',), deps=deps)(x, g)


def _rms_bwd(x, g, dh, dres, name, deps=()):
    R, D = x.shape
    tb = _pick(R, (512, 256, 128))
    has_res = dres is not None

    def body(*refs):
        x_ref, g_ref, dh_ref = refs[:3]
        dx_ref, dg_ref = refs[-2:]
        xv = x_ref[...]
        dhv = dh_ref[...].astype(F32)
        r = lax.rsqrt(jnp.mean(xv * xv, axis=-1, keepdims=True) + EPS)
        u = dhv * g_ref[...]
        m = jnp.mean(u * xv, axis=-1, keepdims=True)
        dx = r * u - xv * (r * r * r * m)
        if has_res:
            dx = dx + refs[3][...]
        dx_ref[...] = dx

        @pl.when(pl.program_id(0) == 0)
        def _():
            dg_ref[...] = jnp.zeros_like(dg_ref)

        dg_ref[...] += jnp.sum(dhv * xv * r, axis=0, keepdims=True)

    row = pl.BlockSpec((tb, D), lambda i: (i, 0))
    vec = pl.BlockSpec((1, D), lambda i: (0, 0))
    args = [x, g, dh] + ([dres] if has_res else [])
    return _call(body, name=name, out_shape=(_sds((R, D), F32), _sds((1, D), F32)), grid=(R // tb,),
                 in_specs=[row, vec, row] + ([row] if has_res else []), out_specs=(row, vec),
                 sem=('arbitrary',), deps=deps)(*args)


def _final(x, g, tgt):
    R, D = x.shape
    tb = _pick(R, (512, 256, 128))

    def body(x_ref, g_ref, t_ref, dx_ref, dg_ref, loss_ref):
        xv = x_ref[...]
        gv = g_ref[...]
        r = lax.rsqrt(jnp.mean(xv * xv, axis=-1, keepdims=True) + EPS)
        e = xv * r * gv - t_ref[...]
        row_loss = jnp.mean(e * e, axis=-1, keepdims=True)
        blk_loss = 0.5 * jnp.sum(row_loss, axis=0, keepdims=True)
        dy = e * (1.0 / D)
        u = dy * gv
        m = jnp.mean(u * xv, axis=-1, keepdims=True)
        dx_ref[...] = r * u - xv * (r * r * r * m)

        @pl.when(pl.program_id(0) == 0)
        def _():
            dg_ref[...] = jnp.zeros_like(dg_ref)
            loss_ref[...] = jnp.zeros_like(loss_ref)

        dg_ref[...] += jnp.sum(dy * xv * r, axis=0, keepdims=True)
        loss_ref[...] += jnp.broadcast_to(blk_loss, loss_ref.shape)

    row = pl.BlockSpec((tb, D), lambda i: (i, 0))
    vec = pl.BlockSpec((1, D), lambda i: (0, 0))
    one = pl.BlockSpec((1, LANES), lambda i: (0, 0))
    return _call(body, name='final_loss', out_shape=(_sds((R, D), F32), _sds((1, D), F32), _sds((1, LANES), F32)),
                 grid=(R // tb,), in_specs=[row, vec, row], out_specs=(row, vec, one), sem=('arbitrary',))(x, g, tgt)


def _col(tb, w, cb):
    return pl.BlockSpec((tb, w), lambda i: (i, cb))


def _prev(tb, h, w, cb):
    return pl.BlockSpec((h, w), lambda i: (jnp.maximum(i * (tb // h) - 1, 0), cb))


def _next(tb, h, w, cb, rows):
    return pl.BlockSpec((h, w), lambda i: (jnp.minimum((i + 1) * (tb // h), rows // h - 1), cb))


def _full(shape):
    return pl.BlockSpec(shape, lambda i: (0,) * len(shape))


def _mixa_fwd(z, w):
    T = z.shape[0]
    D = w.shape[1]
    tb = _pick(T, (256, 128))
    H = HALO_A

    def body(ab, ac, au, acp, aup, w_ref, o_ref, ext):
        first = pl.program_id(0) == 0
        ext[0:H, :] = jnp.where(first, 0.0, acp[...].astype(F32) * aup[...].astype(F32))
        ext[H:H + tb, :] = ac[...].astype(F32) * au[...].astype(F32)
        cp = jnp.zeros((tb, D), F32)
        for k in range(CONV_A_W):
            off = H - (CONV_A_W - 1) + k
            cp = cp + ext[off:off + tb, :] * w_ref[k:k + 1, :]
        o_ref[...] = (ab[...].astype(F32) * cp).astype(o_ref.dtype)

    return _call(body, name='mixa_fwd', out_shape=_sds((T, D), BF16), grid=(T // tb,),
                 in_specs=[_col(tb, D, 0), _col(tb, D, 1), _col(tb, D, 2), _prev(tb, H, D, 1), _prev(tb, H, D, 2),
                           _full((8, D))],
                 out_specs=_col(tb, D, 0), scratch=[pltpu.VMEM((H + tb, D), F32)], sem=('parallel',))(z, z, z, z, z, w)


def _mixa_bwd(z, w, dy):
    T = z.shape[0]
    D = w.shape[1]
    tb = _pick(T, (256, 128))
    H = HALO_A
    nb = T // tb

    def body(ab, ac, au, acp, aup, abn, dy_ref, dyn, w_ref, dz_ref, dw_ref, pext, dext):
        i = pl.program_id(0)
        a_b, a_c, a_u = ab[...].astype(F32), ac[...].astype(F32), au[...].astype(F32)
        pext[0:H, :] = jnp.where(i == 0, 0.0, acp[...].astype(F32) * aup[...].astype(F32))
        pext[H:H + tb, :] = a_c * a_u
        dyv = dy_ref[...].astype(F32)
        dcp = dyv * a_b
        dext[0:tb, :] = dcp
        dext[tb:tb + H, :] = jnp.where(i == nb - 1, 0.0, dyn[...].astype(F32) * abn[...].astype(F32))

        @pl.when(i == 0)
        def _():
            dw_ref[...] = jnp.zeros_like(dw_ref)

        cp = jnp.zeros((tb, D), F32)
        dp = jnp.zeros((tb, D), F32)
        for k in range(CONV_A_W):
            off = H - (CONV_A_W - 1) + k
            wk = w_ref[k:k + 1, :]
            pk = pext[off:off + tb, :]
            cp = cp + pk * wk
            dp = dp + dext[CONV_A_W - 1 - k:CONV_A_W - 1 - k + tb, :] * wk
            dw_ref[k:k + 1, :] += jnp.sum(dcp * pk, axis=0, keepdims=True)
        dz_ref[:, 0:D] = (dyv * cp).astype(dz_ref.dtype)
        dz_ref[:, D:2 * D] = (dp * a_u).astype(dz_ref.dtype)
        dz_ref[:, 2 * D:3 * D] = (dp * a_c).astype(dz_ref.dtype)

    return _call(body, name='mixa_bwd', out_shape=(_sds((T, 6 * D), BF16), _sds((8, D), F32)), grid=(nb,),
                 in_specs=[_col(tb, D, 0), _col(tb, D, 1), _col(tb, D, 2), _prev(tb, H, D, 1), _prev(tb, H, D, 2),
                           _next(tb, H, D, 0, T), _col(tb, D, 0), _next(tb, H, D, 0, T), _full((8, D))],
                 out_specs=(_col(tb, 3 * D, 0), _full((8, D))),
                 scratch=[pltpu.VMEM((H + tb, D), F32), pltpu.VMEM((tb + H, D), F32)],
                 sem=('arbitrary',))(z, z, z, z, z, z, dy, dy, w)


def _split3(v):
    hi = v.astype(BF16)
    r1 = v - hi.astype(F32)
    mid = r1.astype(BF16)
    lo = (r1 - mid.astype(F32)).astype(BF16)
    return hi, mid, lo


def _tri_dot(tri, v):
    hi, mid, lo = _split3(v)
    d = lambda p: jnp.dot(tri, p, preferred_element_type=F32)
    return d(hi) + d(mid) + d(lo)


def _fox_c(zf, bf):
    T, H = zf.shape
    tb = LANES

    def body(zf_ref, b_ref, c_ref, carry):
        @pl.when(pl.program_id(0) == 0)
        def _():
            carry[...] = jnp.zeros_like(carry)

        xv = zf_ref[...] + b_ref[...]
        lf = jnp.minimum(xv, 0.0) - jnp.log(1.0 + jnp.exp(-jnp.abs(xv)))
        r = lax.broadcasted_iota(jnp.int32, (tb, tb), 0)
        c = lax.broadcasted_iota(jnp.int32, (tb, tb), 1)
        tri = (r >= c).astype(BF16)
        cs = _tri_dot(tri, lf) + carry[...]
        c_ref[...] = cs
        carry[...] = cs[tb - 1:tb, :]

    blk = pl.BlockSpec((tb, H), lambda i: (i, 0))
    return _call(body, name='fox_cumsum', out_shape=_sds((T, H), F32), grid=(T // tb,),
                 in_specs=[blk, _full((1, H))], out_specs=blk, scratch=[pltpu.VMEM((1, H), F32)],
                 sem=('arbitrary',))(zf, bf)


def _fox_c_bwd(dc, zf, bf):
    T, H = zf.shape
    tb = LANES
    nb = T // tb

    def body(dc_ref, zf_ref, b_ref, dz_ref, db_ref, carry):
        @pl.when(pl.program_id(0) == 0)
        def _():
            carry[...] = jnp.zeros_like(carry)
            db_ref[...] = jnp.zeros_like(db_ref)

        r = lax.broadcasted_iota(jnp.int32, (tb, tb), 0)
        c = lax.broadcasted_iota(jnp.int32, (tb, tb), 1)
        tri = (c >= r).astype(BF16)
        dlf = _tri_dot(tri, dc_ref[...]) + carry[...]
        carry[...] = dlf[0:1, :]
        xv = zf_ref[...] + b_ref[...]
        dz = dlf * _sigmoid(-xv)
        dz_ref[...] = dz
        db_ref[...] += jnp.sum(dz, axis=0, keepdims=True)

    blk = pl.BlockSpec((tb, H), lambda i: (nb - 1 - i, 0))
    return _call(body, name='fox_cumsum_bwd', out_shape=(_sds((T, H), F32), _sds((1, H), F32)), grid=(nb,),
                 in_specs=[blk, blk, _full((1, H))], out_specs=(blk, _full((1, H))),
                 scratch=[pltpu.VMEM((1, H), F32)], sem=('arbitrary',))(dc, zf, bf)


N_PIECES = 3


def _head_col(c_ref, h):
    lane = lax.broadcasted_iota(jnp.int32, c_ref.shape, 1)
    return jnp.sum(jnp.where(lane == h, c_ref[...], 0.0), axis=1, keepdims=True)


def _pieces(v, sign=1.0):
    return [sign * p.astype(F32) for p in _split3(v)]


def _augment(block, hh, dh, a_cols, b_cols):
    lane = lax.broadcasted_iota(jnp.int32, block.shape, 1)
    base = (1 - hh) * dh
    out = jnp.where((lane >= hh * dh) & (lane < (hh + 1) * dh), block.astype(F32), 0.0)
    for n, col in enumerate(list(a_cols) + list(b_cols)):
        out = jnp.where(lane == base + n, col, out)
    return out.astype(BF16)


def _fox_probs(q_aug, k_aug, tri, q0, ke, shift=None):
    s = lax.dot_general(q_aug[q0:ke, :], k_aug[0:ke, :], _DN['nt'], preferred_element_type=F32)
    s_dg = jnp.where(tri, s[:, q0:ke], -jnp.inf)
    if shift is None:
        shift = jnp.max(s_dg, axis=-1, keepdims=True)
        if q0:
            shift = jnp.maximum(shift, jnp.max(s[:, 0:q0], axis=-1, keepdims=True))
    parts = ([jnp.exp(s[:, 0:q0] - shift)] if q0 else []) + [jnp.exp(s_dg - shift)]
    return parts, shift


def _cat(parts):
    parts = [p.astype(BF16) for p in parts]
    return parts[0] if len(parts) == 1 else jnp.concatenate(parts, axis=1)


def _fox_fwd(z, c):
    T = z.shape[0]
    D = z.shape[1] // 6
    dh = D // FOX_HEADS
    hp = LANES // dh
    ncb = D // LANES
    tq = _pick(T, (256, 128))
    scale = dh ** -0.5
    ones = [1.0] * N_PIECES

    def body(q_ref, k_ref, v_ref, c_ref, o_ref, lse_ref, o_all):
        j = pl.program_id(0)
        lse_ref[...] = jnp.zeros_like(lse_ref)
        r = lax.broadcasted_iota(jnp.int32, (tq, tq), 0)
        tri = r >= lax.broadcasted_iota(jnp.int32, (tq, tq), 1)
        for hh in range(hp):
            base = (1 - hh) * dh
            ch = _head_col(c_ref, j * hp + hh)
            q_aug = _augment(q_ref[...].astype(F32) * scale, hh, dh, _pieces(ch), ones)
            k_aug = _augment(k_ref[...], hh, dh, ones, _pieces(ch, -1.0))
            v_aug = _augment(v_ref[...], hh, dh, ones, [])
            for q0 in range(0, T, tq):
                ke = q0 + tq
                parts, m = _fox_probs(q_aug, k_aug, tri, q0, ke)
                o_aug = jnp.dot(_cat(parts), v_aug[0:ke, :], preferred_element_type=F32)
                l = o_aug[:, base:base + 1]
                o_all[hh, q0:ke, :] = o_aug * (1.0 / l)
                lse_ref[q0:ke, hh:hh + 1] = m + jnp.log(l)
        lane = lax.broadcasted_iota(jnp.int32, (T, LANES), 1)
        out = o_all[0]
        for hh in range(1, hp):
            out = jnp.where(lane >= hh * dh, o_all[hh], out)
        o_ref[...] = out.astype(o_ref.dtype)

    blk = lambda cb0: pl.BlockSpec((T, LANES), lambda j: (0, cb0 + j))
    return _call(body, name='fox_fwd', out_shape=(_sds((T, D), BF16), _sds((ncb, T, LANES), F32)), grid=(ncb,),
                 in_specs=[blk(3 * ncb), blk(4 * ncb), blk(5 * ncb), _full((T, FOX_HEADS))],
                 out_specs=(blk(0), pl.BlockSpec((None, T, LANES), lambda j: (j, 0, 0))),
                 scratch=[pltpu.VMEM((hp, T, LANES), F32)], sem=('parallel',))(z, z, z, c)


def _fox_bwd(z, c, o, lse, do, dz1):
    T = z.shape[0]
    D = o.shape[1]
    dh = D // FOX_HEADS
    hp = LANES // dh
    ncb = D // LANES
    tq = _pick(T, (256, 128))
    scale = dh ** -0.5
    ones = [1.0] * N_PIECES

    def body(q_ref, k_ref, v_ref, c_ref, o_ref, lse_ref, do_ref, _, dz_ref, dc_ref, dq_all, dk_acc, dv_acc, stage,
             sems):
        j = pl.program_id(0)
        dk_acc[...] = jnp.zeros_like(dk_acc)
        dv_acc[...] = jnp.zeros_like(dv_acc)
        dc_ref[...] = jnp.zeros_like(dc_ref)
        r = lax.broadcasted_iota(jnp.int32, (tq, tq), 0)
        tri = r >= lax.broadcasted_iota(jnp.int32, (tq, tq), 1)
        lane = lax.broadcasted_iota(jnp.int32, (T, LANES), 1)
        for hh in range(hp):
            base = (1 - hh) * dh
            own = (lane >= hh * dh) & (lane < (hh + 1) * dh)
            ch = _head_col(c_ref, j * hp + hh)
            dov = do_ref[...].astype(F32)
            delta = jnp.sum(jnp.where(own, dov * o_ref[...].astype(F32), 0.0), axis=1, keepdims=True)
            q_aug = _augment(q_ref[...].astype(F32) * scale, hh, dh, _pieces(ch - lse_ref[:, hh:hh + 1]), ones)
            k_aug = _augment(k_ref[...], hh, dh, ones, _pieces(ch, -1.0))
            v_aug = _augment(v_ref[...], hh, dh, ones, [])
            do_aug = _augment(dov, hh, dh, _pieces(delta, -1.0), [])
            for q0 in range(0, T, tq):
                ke = q0 + tq
                p, _ = _fox_probs(q_aug, k_aug, tri, q0, ke, shift=0.0)
                dp = lax.dot_general(do_aug[q0:ke, :], v_aug[0:ke, :], _DN['nt'], preferred_element_type=F32)
                ds = [p[0] * dp[:, 0:q0], p[1] * dp[:, q0:ke]] if q0 else [p[0] * dp]
                dsb, pb = _cat(ds), _cat(p)
                dq_all[hh, q0:ke, :] = jnp.dot(dsb, k_aug[0:ke, :], preferred_element_type=F32)
                dk_acc[hh, 0:ke, :] += lax.dot_general(dsb, q_aug[q0:ke, :], _DN['tn'], preferred_element_type=F32)
                dv_acc[hh, 0:ke, :] += lax.dot_general(pb, do_aug[q0:ke, :], _DN['tn'], preferred_element_type=F32)
            dc_ref[:, hh:hh + 1] = dq_all[hh, :, base:base + 1] - dk_acc[hh, :, base + N_PIECES:base + N_PIECES + 1]
        dq, dk, dv = dq_all[0], dk_acc[0], dv_acc[0]
        for hh in range(1, hp):
            dq = jnp.where(lane >= hh * dh, dq_all[hh], dq)
            dk = jnp.where(lane >= hh * dh, dk_acc[hh], dk)
            dv = jnp.where(lane >= hh * dh, dv_acc[hh], dv)
        copies = []
        for n, g in enumerate((dq * scale, dk, dv)):
            stage[n] = g.astype(stage.dtype)
            col = pl.multiple_of(((3 + n) * ncb + j) * LANES, LANES)
            copies.append(pltpu.make_async_copy(stage.at[n], dz_ref.at[:, pl.ds(col, LANES)], sems.at[n]))
            copies[n].start()
        for cp in copies:
            cp.wait()

    blk = lambda cb0: pl.BlockSpec((T, LANES), lambda j: (0, cb0 + j))
    pair = pl.BlockSpec((None, T, LANES), lambda j: (j, 0, 0))
    grad = _sds((T, D), BF16)
    acc = pltpu.VMEM((hp, T, LANES), F32)
    anywhere = pl.BlockSpec(memory_space=pl.ANY)
    return _call(body, name='fox_bwd', out_shape=(_sds(dz1.shape, dz1.dtype), _sds((ncb, T, LANES), F32)), grid=(ncb,),
                 in_specs=[blk(3 * ncb), blk(4 * ncb), blk(5 * ncb), _full((T, FOX_HEADS)), blk(0), pair, blk(0),
                           anywhere],
                 out_specs=(anywhere, pair),
                 scratch=[acc, acc, acc, pltpu.VMEM((3, T, LANES), BF16), pltpu.SemaphoreType.DMA((3,))],
                 sem=('arbitrary',), aliases={7: 0})(z, z, z, c, o, lse, do, dz1)


def _ln_parts(u1, gain, bias):
    mu = jnp.mean(u1, axis=-1, keepdims=True)
    xc = u1 - mu
    rstd = lax.rsqrt(jnp.mean(xc * xc, axis=-1, keepdims=True) + EPS)
    xhat = xc * rstd
    return xhat, rstd, xhat * gain + bias


def _tap_groups(offsets):
    groups = {}
    for k, off in enumerate(offsets):
        groups.setdefault(off % SUBLANES, []).append((off - off % SUBLANES, k))
    return groups


def _for_taps(src_ref, r0, cols, groups, aligned, visit):
    for r, taps in groups.items():
        if r == 0:
            for base, k in taps:
                visit(k, src_ref[r0 + base:r0 + base + CONV_ROWS, cols])
            continue
        need = max(base for base, _ in taps) + CONV_ROWS
        aligned[r, 0:need, :] = src_ref[r0 + r:r0 + r + need, cols]
        for base, k in taps:
            visit(k, aligned[r, base:base + CONV_ROWS, :])


def _row_chunk(j, first=0):
    return slice(first + j * ROW_CHUNK, first + (j + 1) * ROW_CHUNK)


def _glu_rows(cv, cg, ext, halo, tb):
    for j in range(tb // ROW_CHUNK):
        rs = _row_chunk(j)
        ext[_row_chunk(j, halo), :] = cv[rs, :].astype(F32) * _sigmoid(cg[rs, :].astype(F32))


def _mixc_fwd(z, w, cb, gain, bias):
    T = z.shape[0]
    D = w.shape[1]
    tb = _pick(T, (256, 128))
    H = HALO_C

    groups = _tap_groups([H - (CONV_C_W - 1) + k for k in range(CONV_C_W)])

    def body(cv, cg, cvp, cgp, w_ref, cb_ref, g_ref, b_ref, o_ref, u1_ref, ext, aligned):
        first = pl.program_id(0) == 0
        ext[0:H, :] = jnp.where(first, 0.0, cvp[...].astype(F32) * _sigmoid(cgp[...].astype(F32)))
        _glu_rows(cv, cg, ext, H, tb)
        for r0 in range(0, tb, CONV_ROWS):
            for c0 in range(0, D, LANES):
                cols = slice(c0, c0 + LANES)
                acc = [jnp.zeros((CONV_ROWS, LANES), F32)]

                def tap(k, rows):
                    acc[0] = acc[0] + rows * w_ref[k:k + 1, cols]

                _for_taps(ext, r0, cols, groups, aligned, tap)
                u1_ref[r0:r0 + CONV_ROWS, cols] = acc[0] + cb_ref[:, cols]

        for j in range(tb // ROW_CHUNK):
            rs = _row_chunk(j)
            _, _, u2 = _ln_parts(u1_ref[rs, :], g_ref[...], b_ref[...])
            o_ref[rs, :] = (u2 * _sigmoid(u2)).astype(o_ref.dtype)

    vec = _full((1, D))
    row = _col(tb, D, 0)
    return _call(body, name='mixc_fwd', out_shape=(_sds((T, D), BF16), _sds((T, D), F32)), grid=(T // tb,),
                 in_specs=[_col(tb, D, 0), _col(tb, D, 1), _prev(tb, H, D, 0), _prev(tb, H, D, 1), _full((32, D)), vec,
                           vec, vec],
                 out_specs=(row, row), scratch=[pltpu.VMEM((H + tb, D), F32), pltpu.VMEM((SUBLANES, CONV_ROWS + H, LANES), F32)],
                 sem=('parallel',))(z, z, z, z, w, cb, gain, bias)


def _mixc_bwd(z, w, gain, bias, u1, du3, dz2):
    T = z.shape[0]
    D = w.shape[1]
    tb = _pick(T, (256, 128))
    H = HALO_C
    nb = T // tb

    def du1_of(u1v, du3v, gv, bv):
        xhat, rstd, u2 = _ln_parts(u1v, gv, bv)
        sg = _sigmoid(u2)
        du2 = du3v * (sg * (1.0 + u2 * (1.0 - sg)))
        dxh = du2 * gv
        m1 = jnp.mean(dxh, axis=-1, keepdims=True)
        m2 = jnp.mean(dxh * xhat, axis=-1, keepdims=True)
        return rstd * (dxh - m1 - xhat * m2), du2, xhat

    fwd_groups = _tap_groups([H - (CONV_C_W - 1) + k for k in range(CONV_C_W)])
    bwd_groups = _tap_groups([CONV_C_W - 1 - k for k in range(CONV_C_W)])

    def body(cv, cg, cvp, cgp, w_ref, g_ref, b_ref, u1_ref, u1n, du3_ref, du3n, _, dz_ref, dw_ref, dcb_ref, dg_ref,
             db_ref, uext, dext, dw_acc, aligned, sums):
        i = pl.program_id(0)
        uext[0:H, :] = jnp.where(i == 0, 0.0, cvp[...].astype(F32) * _sigmoid(cgp[...].astype(F32)))
        _glu_rows(cv, cg, uext, H, tb)
        du1n, _, _ = du1_of(u1n[...], du3n[...].astype(F32), g_ref[...], b_ref[...])
        dext[tb:tb + H, :] = jnp.where(i == nb - 1, 0.0, du1n)

        @pl.when(i == 0)
        def _():
            dw_acc[...] = jnp.zeros_like(dw_acc)
            sums[...] = jnp.zeros_like(sums)

        def halves(v):
            part = v[0:SUBLANES, :]
            for q in range(SUBLANES, ROW_CHUNK, SUBLANES):
                part = part + v[q:q + SUBLANES, :]
            return part

        for j in range(tb // ROW_CHUNK):
            rs = _row_chunk(j)
            du1, du2, xhat = du1_of(u1_ref[rs, :], du3_ref[rs, :].astype(F32), g_ref[...], b_ref[...])
            dext[rs, :] = du1
            sums[0:SUBLANES, :] += halves(du1)
            sums[SUBLANES:2 * SUBLANES, :] += halves(du2 * xhat)
            sums[2 * SUBLANES:3 * SUBLANES, :] += halves(du2)
        for r0 in range(0, tb, CONV_ROWS):
            rows = slice(r0, r0 + CONV_ROWS)
            for c0 in range(0, D, LANES):
                cols = slice(c0, c0 + LANES)
                acc = [jnp.zeros((CONV_ROWS, LANES), F32)]

                def tap(k, src):
                    acc[0] = acc[0] + src * w_ref[k:k + 1, cols]

                _for_taps(dext, r0, cols, bwd_groups, aligned, tap)
                c_val = cv[rows, cols].astype(F32)
                sg = _sigmoid(cg[rows, cols].astype(F32))
                dz_ref[rows, cols] = (acc[0] * sg).astype(dz_ref.dtype)
                dz_ref[rows, D + c0:D + c0 + LANES] = (acc[0] * c_val * sg * (1.0 - sg)).astype(dz_ref.dtype)
                du1_blk = dext[rows, cols]

                def tap_w(k, src):
                    prod = du1_blk * src
                    part = prod[0:SUBLANES, :]
                    for j in range(SUBLANES, CONV_ROWS, SUBLANES):
                        part = part + prod[j:j + SUBLANES, :]
                    dw_acc[k * SUBLANES:(k + 1) * SUBLANES, cols] += part

                _for_taps(uext, r0, cols, fwd_groups, aligned, tap_w)

        @pl.when(i == nb - 1)
        def _():
            dw_ref[...] = jnp.zeros_like(dw_ref)
            for k in range(CONV_C_W):
                dw_ref[k:k + 1, :] = jnp.sum(dw_acc[k * SUBLANES:(k + 1) * SUBLANES, :], axis=0, keepdims=True)
            for n, ref in enumerate((dcb_ref, dg_ref, db_ref)):
                ref[...] = jnp.sum(sums[n * SUBLANES:(n + 1) * SUBLANES, :], axis=0, keepdims=True)

    vec = _full((1, D))
    row = _col(tb, D, 0)
    nxt = _next(tb, H, D, 0, T)
    return _call(body, name='mixc_bwd',
                 out_shape=(_sds(dz2.shape, dz2.dtype), _sds((32, D), F32), _sds((1, D), F32), _sds((1, D), F32),
                            _sds((1, D), F32)), grid=(nb,),
                 in_specs=[_col(tb, D, 0), _col(tb, D, 1), _prev(tb, H, D, 0), _prev(tb, H, D, 1), _full((32, D)), vec,
                           vec, row, nxt, row, nxt, pl.BlockSpec(memory_space=pl.ANY)],
                 out_specs=(_col(tb, 2 * D, 0), _full((32, D)), vec, vec, vec),
                 scratch=[pltpu.VMEM((H + tb, D), F32), pltpu.VMEM((tb + H, D), F32),
                          pltpu.VMEM((32 * SUBLANES, D), F32), pltpu.VMEM((SUBLANES, CONV_ROWS + H, LANES), F32),
                          pltpu.VMEM((3 * SUBLANES, D), F32)],
                 sem=('arbitrary',), aliases={11: 0})(z, z, z, z, w, gain, bias, u1, u1, du3, du3, dz2)


def _gate_fwd(z, bg, ya, yb, yc):
    T = z.shape[0]
    D = ya.shape[1]
    tb = _pick(T, (256, 128))

    def body(ga, gb, gc, bg_ref, ya_ref, yb_ref, yc_ref, o_ref):
        acc = jnp.zeros((tb, D), F32)
        for n, (g, y) in enumerate(((ga, ya_ref), (gb, yb_ref), (gc, yc_ref))):
            acc = acc + _sigmoid(g[...].astype(F32) + bg_ref[:, n * D:(n + 1) * D]) * y[...].astype(F32)
        o_ref[...] = acc.astype(o_ref.dtype)

    row = _col(tb, D, 0)
    return _call(body, name='gate_fwd', out_shape=_sds((T, D), BF16), grid=(T // tb,),
                 in_specs=[_col(tb, D, 2), _col(tb, D, 3), _col(tb, D, 4), _full((1, 3 * D)), row, row, row],
                 out_specs=row, sem=('parallel',))(z, z, z, bg, ya, yb, yc)


def _gate_bwd(z, bg, ya, yb, yc, dm):
    T = z.shape[0]
    D = ya.shape[1]
    tb = _pick(T, (256, 128))

    def body(ga, gb, gc, bg_ref, ya_ref, yb_ref, yc_ref, dm_ref, dya, dyb, dyc, dg_ref, dbg_ref):
        @pl.when(pl.program_id(0) == 0)
        def _():
            dbg_ref[...] = jnp.zeros_like(dbg_ref)

        dmv = dm_ref[...]
        for n, (g, y, dy) in enumerate(((ga, ya_ref, dya), (gb, yb_ref, dyb), (gc, yc_ref, dyc))):
            cols = slice(n * D, (n + 1) * D)
            sg = _sigmoid(g[...].astype(F32) + bg_ref[:, cols])
            dy[...] = (dmv * sg).astype(dy.dtype)
            dg = dmv * y[...].astype(F32) * sg * (1.0 - sg)
            dg_ref[:, 2 * D + n * D:2 * D + (n + 1) * D] = dg.astype(dg_ref.dtype)
            dbg_ref[:, cols] += jnp.sum(dg, axis=0, keepdims=True)

    row = _col(tb, D, 0)
    act = _sds((T, D), BF16)
    return _call(body, name='gate_bwd', out_shape=(act, act, act, _sds((T, 5 * D), BF16), _sds((1, 3 * D), F32)),
                 grid=(T // tb,),
                 in_specs=[_col(tb, D, 2), _col(tb, D, 3), _col(tb, D, 4), _full((1, 3 * D)), row, row, row, row],
                 out_specs=(row, row, row, _col(tb, 5 * D, 0), _full((1, 3 * D))),
                 sem=('arbitrary',))(z, z, z, bg, ya, yb, yc, dm)


def _xattn_probs(qs, ks, scale):
    s = lax.dot_general(qs, ks, _DN['nt'], preferred_element_type=F32) * scale
    p = jnp.exp(s - jnp.max(s, axis=-1, keepdims=True))
    return p * (1.0 / jnp.sum(p, axis=-1, keepdims=True))


def _xattn_fwd(q, kv):
    T, DX = q.shape
    M = kv.shape[0]
    dh = DX // X_HEADS
    tb = _pick(T, (512, 256, 128))
    scale = dh ** -0.5

    def body(q_ref, kv_ref, o_ref):
        for h in range(X_HEADS):
            hs = slice(h * dh, (h + 1) * dh)
            p = _xattn_probs(q_ref[:, hs], kv_ref[:, hs], scale)
            o_ref[:, hs] = jnp.dot(p.astype(BF16), kv_ref[:, DX + h * dh:DX + (h + 1) * dh],
                                   preferred_element_type=F32).astype(o_ref.dtype)

    row = _col(tb, DX, 0)
    return _call(body, name='xattn_fwd', out_shape=_sds((T, DX), BF16), grid=(T // tb,),
                 in_specs=[row, _full((M, 2 * DX))], out_specs=row, sem=('parallel',))(q, kv)


def _xattn_bwd(q, kv, do):
    T, DX = q.shape
    M = kv.shape[0]
    dh = DX // X_HEADS
    tb = _pick(T, (512, 256, 128))
    scale = dh ** -0.5

    def body(q_ref, kv_ref, do_ref, dq_ref, dkv_ref):
        @pl.when(pl.program_id(0) == 0)
        def _():
            dkv_ref[...] = jnp.zeros_like(dkv_ref)

        for h in range(X_HEADS):
            hs = slice(h * dh, (h + 1) * dh)
            vs_cols = slice(DX + h * dh, DX + (h + 1) * dh)
            qs, ks, vs = q_ref[:, hs], kv_ref[:, hs], kv_ref[:, vs_cols]
            dos = do_ref[:, hs].astype(BF16)
            p = _xattn_probs(qs, ks, scale)
            dp = lax.dot_general(dos, vs, _DN['nt'], preferred_element_type=F32)
            ds = p * (dp - jnp.sum(p * dp, axis=-1, keepdims=True))
            dsb = ds.astype(BF16)
            dq_ref[:, hs] = (jnp.dot(dsb, ks, preferred_element_type=F32) * scale).astype(dq_ref.dtype)
            dkv_ref[:, hs] += lax.dot_general(dsb, qs, _DN['tn'], preferred_element_type=F32) * scale
            dkv_ref[:, vs_cols] += lax.dot_general(p.astype(BF16), dos, _DN['tn'], preferred_element_type=F32)

    row = _col(tb, DX, 0)
    return _call(body, name='xattn_bwd', out_shape=(_sds((T, DX), BF16), _sds((M, 2 * DX), F32)), grid=(T // tb,),
                 in_specs=[row, _full((M, 2 * DX)), row], out_specs=(row, _full((M, 2 * DX))),
                 sem=('arbitrary',))(q, kv, do)


def _swiglu_fwd(gu):
    T, F2 = gu.shape
    F = F2 // 2
    tb = _pick(T, (256, 128))

    def body(g_ref, u_ref, o_ref):
        g = g_ref[...].astype(F32)
        o_ref[...] = (g * _sigmoid(g) * u_ref[...].astype(F32)).astype(o_ref.dtype)

    return _call(body, name='swiglu_fwd', out_shape=_sds((T, F), BF16), grid=(T // tb,),
                 in_specs=[_col(tb, F, 0), _col(tb, F, 1)], out_specs=_col(tb, F, 0), sem=('parallel',))(gu, gu)


def _swiglu_bwd(gu, da):
    T, F2 = gu.shape
    F = F2 // 2
    tb = _pick(T, (256, 128))

    def body(g_ref, u_ref, da_ref, o_ref):
        g = g_ref[...].astype(F32)
        u = u_ref[...].astype(F32)
        dav = da_ref[...].astype(F32)
        sg = _sigmoid(g)
        o_ref[:, 0:F] = (dav * u * (sg * (1.0 + g * (1.0 - sg)))).astype(o_ref.dtype)
        o_ref[:, F:F2] = (dav * g * sg).astype(o_ref.dtype)

    return _call(body, name='swiglu_bwd', out_shape=_sds((T, F2), BF16), grid=(T // tb,),
                 in_specs=[_col(tb, F, 0), _col(tb, F, 1), _col(tb, F, 0)], out_specs=_col(tb, F2, 0),
                 sem=('parallel',))(gu, gu, da)


def _place():
    x, y, c = lax.axis_index('x'), lax.axis_index('y'), lax.axis_index('c')
    return x, y, c


def _allgather(arrs, lead, name):
    n = len(arrs)

    def out_shape(a, ld):
        return (N_DEV,) + a.shape if ld else (a.shape[0], N_DEV) + a.shape[1:]

    def body(*refs):
        ins, outs = refs[:n], refs[n:2 * n]
        send_sems, recv_sems, local_sems = refs[2 * n:]
        x, y, c = _place()
        me, sibling = (x, y, c), (x, y, 1 - c)
        chips = [(1 - x, y), (x, 1 - y), (1 - x, 1 - y)]

        def slot(a, dev):
            idx = 4 * dev[0] + 2 * dev[1] + dev[2]
            return outs[a].at[idx] if lead[a] else outs[a].at[:, idx]

        def copy(a, k, block, to, src=None):
            return pltpu.make_async_remote_copy(
                src_ref=slot(a, block) if src is None else src, dst_ref=slot(a, block),
                send_sem=send_sems.at[a * 7 + k], recv_sem=recv_sems.at[a * 7 + k], device_id=to, device_id_type=MESH)

        mine = [pltpu.make_async_copy(ins[a], slot(a, me), local_sems.at[a]) for a in range(n)]
        for cp in mine:
            cp.start()
        first = []
        for a in range(n):
            first.append(copy(a, 0, me, sibling, src=ins[a]))
            first += [copy(a, 1 + j, me, (*chip, c), src=ins[a]) for j, chip in enumerate(chips)]
        for cp in first:
            cp.start()
        passed = []
        for a in range(n):
            for j, chip in enumerate(chips):
                copy(a, 1 + j, (*chip, c), me).wait_recv()
                fwd = copy(a, 4 + j, (*chip, c), sibling)
                fwd.start()
                passed.append(fwd)
        for a in range(n):
            copy(a, 0, sibling, me).wait_recv()
            for j, chip in enumerate(chips):
                copy(a, 4 + j, (*chip, 1 - c), me).wait_recv()
        for cp in first + passed:
            cp.wait_send()
        for cp in mine:
            cp.wait()

    any_spec = pl.BlockSpec(memory_space=pl.ANY)
    return _call(body, name=name, out_shape=tuple(_sds(out_shape(a, ld), a.dtype) for a, ld in zip(arrs, lead)),
                 in_specs=[any_spec] * n, out_specs=tuple([any_spec] * n),
                 scratch=[pltpu.SemaphoreType.DMA((7 * n,)), pltpu.SemaphoreType.DMA((7 * n,)),
                          pltpu.SemaphoreType.DMA((n,))])(*arrs)


_PEER_ORDER = (1, 4, 2, 5, 3, 6, 7)
_HBM = pl.BlockSpec(memory_space=pltpu.HBM)
_SEM = pl.BlockSpec(memory_space=pltpu.SEMAPHORE)


def _my_index():
    x, y, c = _place()
    return 4 * x + 2 * y + c


def _remote(src, dst, send_sems, recv_sems, k, dev):
    return pltpu.make_async_remote_copy(src_ref=src, dst_ref=dst, send_sem=send_sems.at[k], recv_sem=recv_sems.at[k],
                                        device_id=dev, device_id_type=MESH)


def _own_part(src, kind, args):
    return src if args is None else src.at[pl.ds(args[0], args[1])]


def _block(land, kind, args, owner):
    if kind == 'rows':
        _, n, stride, d0 = args
        return land.at[pl.ds(pl.multiple_of(stride * owner + d0, ROW_TILE), n)]
    return land.at[owner]


def _chunk(src, args, dest):
    stride, n = args
    return src.at[pl.ds(pl.multiple_of(stride * dest, ROW_TILE), n)]


def _scatter_copies(plan, srcs, lands, send_sems, recv_sems):
    x, y, c = _place()
    me = 4 * x + 2 * y + c
    copies = []
    for r in _PEER_ORDER:
        px, py, pc = x ^ (r >> 2), y ^ ((r >> 1) & 1), c ^ (r & 1)
        for si, li, _, args in plan:
            copies.append(_remote(_chunk(srcs[si], args, 4 * px + 2 * py + pc), lands[li].at[me], send_sems, recv_sems,
                                  r - 1, (px, py, pc)))
    return copies


def _gather_copies(plan, srcs, lands, send_sems, recv_sems, passing):
    x, y, c = _place()
    me = 4 * x + 2 * y + c
    sibling = (x, y, 1 - c)
    chips = [(1 - x, y), (x, 1 - y), (1 - x, 1 - y)]
    copies = []
    if passing:
        for k, chip in enumerate(chips):
            for _, li, kind, args in plan:
                blk = _block(lands[li], kind, args, 4 * chip[0] + 2 * chip[1] + c)
                copies.append((k, _remote(blk, blk, send_sems, recv_sems, k, sibling)))
    else:
        for k, dev in enumerate([sibling] + [(*chip, c) for chip in chips]):
            for si, li, kind, args in plan:
                dst = _block(lands[li], kind, args, me)
                src = dst if srcs is None else _own_part(srcs[si], kind, args)
                copies.append((k, _remote(src, dst, send_sems, recv_sems, k, dev)))
    return copies


def _sem_call(name, bufs, sems_in, new_sems, after, action):
    nb, ni, nn = len(bufs), len(sems_in), len(new_sems)

    def body(*refs):
        action(refs[:nb], refs[nb:nb + ni], refs[-1 - nb - nn:-1 - nb])
        refs[-1][...] = jnp.zeros_like(refs[-1])

    bufs = [pltpu.with_memory_space_constraint(v, pltpu.HBM) for v in bufs]
    res = pl.pallas_call(
        body, name=name,
        out_shape=(*[pltpu.SemaphoreType.DMA((n,)) for n in new_sems], *[pltpu.HBM(v.shape, v.dtype) for v in bufs],
                   _sds((8, LANES), F32)),
        in_specs=[_HBM] * nb + [_SEM] * ni + [pl.BlockSpec(memory_space=pl.ANY)] * len(after),
        out_specs=(*[_SEM] * nn, *[_HBM] * nb, pl.BlockSpec(memory_space=pltpu.VMEM)),
        input_output_aliases={i: nn + i for i in range(nb)},
        compiler_params=pltpu.CompilerParams(has_side_effects=pltpu.SideEffectType.DATAFLOW_SIDE_EFFECTING),
        interpret=False)(*bufs, *sems_in, *after)
    return list(res[:nn]), list(res[nn:nn + nb]), res[-1]


def _place_own(srcs, plan, lands_like, name, deps):
    ns, nl, ne = len(srcs), len(lands_like), len(plan)

    def part_shape(entry):
        si, _, kind, args = entry
        rows = args[1] if (kind == 'chunk' or args is not None) else None
        return srcs[si].shape if rows is None else (rows,) + srcs[si].shape[1:]

    def body(*refs):
        src_r, land_r = refs[:ns], refs[ns:ns + nl]
        stage, sems_in, sems_out = refs[ns + nl:ns + nl + ne], refs[-2], refs[-1]
        me = _my_index()
        loads, stores = [], []
        for i, (si, li, kind, args) in enumerate(plan):
            if kind == 'chunk':
                src, dst = _chunk(src_r[si], args, me), land_r[li].at[me]
            else:
                src, dst = _own_part(src_r[si], kind, args), _block(land_r[li], kind, args, me)
            loads.append(pltpu.make_async_copy(src, stage[i], sems_in.at[i]))
            stores.append(pltpu.make_async_copy(stage[i], dst, sems_out.at[i]))
        for cp in loads:
            cp.start()
        for load, store in zip(loads, stores):
            load.wait()
            store.start()
        for cp in stores:
            cp.wait()

    anywhere = pl.BlockSpec(memory_space=pl.ANY)
    scratch = [pltpu.VMEM(part_shape(e), srcs[e[0]].dtype) for e in plan]
    scratch += [pltpu.SemaphoreType.DMA((ne,)), pltpu.SemaphoreType.DMA((ne,))]
    return list(_call(body, name=name, out_shape=tuple(lands_like), in_specs=[anywhere] * ns,
                      out_specs=tuple([anywhere] * nl), scratch=scratch, deps=deps)(*srcs))


def _scatter_start(srcs, lands, plan, after, name):
    ns = len(srcs)

    def action(bufs, _, new):
        for cp in _scatter_copies(plan, bufs[:ns], bufs[ns:], new[0], new[1]):
            cp.start()

    sems, bufs, token = _sem_call(name, [*srcs, *lands], [], [N_DEV - 1, N_DEV - 1], after, action)
    return sems, bufs[:ns], bufs[ns:], token


def _scatter_wait(sems, srcs, lands, plan, after, name):
    ns = len(srcs)

    def action(bufs, old, _):
        copies = _scatter_copies(plan, bufs[:ns], bufs[ns:], old[0], old[1])
        for cp in copies:
            cp.wait_send()
        for cp in copies:
            cp.wait_recv()

    return _sem_call(name, [*srcs, *lands], sems, [], after, action)[1][ns:]


def _gather_start(srcs, lands, plan, name):
    ns = len(srcs)

    def action(bufs, _, new):
        for _, cp in _gather_copies(plan, bufs[:ns], bufs[ns:], new[0], new[1], False):
            cp.start()

    sems, bufs, token = _sem_call(name, [*srcs, *lands], [], [4, 4], [], action)
    return sems, bufs[:ns], bufs[ns:], token


def _gather_pass(recv_sems, lands, plan, after, name):
    def action(bufs, old, new):
        for _, cp in _gather_copies(plan, None, bufs, old[0], old[0], False):
            cp.wait_recv()
        for _, cp in _gather_copies(plan, None, bufs, new[0], new[1], True):
            cp.start()

    return _sem_call(name, lands, [recv_sems], [3, 3], after, action)


def _gather_wait(send_sems, pass_sems, srcs, lands, plan, after, name):
    ns = len(srcs)

    def action(bufs, old, _):
        for _, cp in _gather_copies(plan, bufs[:ns], bufs[ns:], old[0], old[0], False):
            cp.wait_send()
        passed = _gather_copies(plan, None, bufs[ns:], old[1], old[2], True)
        for _, cp in passed:
            cp.wait_send()
        for _, cp in passed:
            cp.wait_recv()

    return _sem_call(name, [*srcs, *lands], [send_sems, *pass_sems], [], after, action)[1][ns:]


def _merge_edges(wt, edges, tiles_per_dev):
    D = wt.shape[1]

    def body(w_ref, e_ref, o_ref):
        s = pl.program_id(0)
        o_ref[...] = jnp.where(s == 0, e_ref[...], w_ref[...] + e_ref[...])

    tile = pl.BlockSpec((ROW_TILE, D), lambda s: (s * tiles_per_dev, 0))
    return _call(body, name='merge_edges', out_shape=_sds(wt.shape, wt.dtype), grid=(N_DEV,),
                 in_specs=[tile, pl.BlockSpec((None, ROW_TILE, D), lambda s: (s, 0, 0))], out_specs=tile,
                 sem=('arbitrary',), aliases={0: 0})(wt, edges)


def _adam_update(g, w, m, v):
    c1 = 1.0 - ADAM_B1 ** ADAM_STEP
    c2 = 1.0 - ADAM_B2 ** ADAM_STEP
    mn = ADAM_B1 * m + (1.0 - ADAM_B1) * g
    vn = ADAM_B2 * v + (1.0 - ADAM_B2) * (g * g)
    m_hat = mn / c1
    v_hat = vn / c2
    return -ADAM_LR * (m_hat / (jnp.sqrt(v_hat) + ADAM_EPS) + ADAM_WD * w), mn, vn


def _adamw(parts, w, m, v, name):
    n_parts = parts.shape[0]
    R, C = w.shape
    tb = _pick(R, (128, 64, 32, 16, 8))

    def body(p_ref, w_ref, m_ref, v_ref, g_out, d_out, m_out, v_out):
        g = p_ref[0].astype(F32)
        for s in range(1, n_parts):
            g = g + p_ref[s].astype(F32)
        g_out[...] = g
        d_out[...], m_out[...], v_out[...] = _adam_update(g, w_ref[...], m_ref[...], v_ref[...])

    row = pl.BlockSpec((tb, C), lambda i: (i, 0))
    o = _sds((R, C), F32)
    return _call(body, name=name, out_shape=(o, o, o, o), grid=(R // tb,),
                 in_specs=[pl.BlockSpec((n_parts, tb, C), lambda i: (0, i, 0)), row, row, row],
                 out_specs=(row, row, row, row), sem=('parallel',))(parts, w, m, v)


def _adamw_layer(parts, w, m, v, layer, prev, shift, name):
    L, n, C = w.shape
    n_p = parts.shape[1]
    cb = C if N_DEV * n_p * C * parts.dtype.itemsize <= (6 << 20) else 2 * LANES
    n_prev = 0 if prev is None else 4

    def body(sh_ref, p_ref, w_ref, m_ref, v_ref, *rest):
        g_out, d_out, m_out, v_out = rest[n_prev:n_prev + 4]
        g = p_ref[0].astype(F32)
        for s in range(1, N_DEV):
            g = g + p_ref[s].astype(F32)
        if n_p != n:
            rolled = rest[-1]
            rolled[...] = pltpu.roll(g, n_p - sh_ref[0], 0)
            g = rolled[0:n, :]
        g_out[...] = g
        d_out[...], m_out[...], v_out[...] = _adam_update(g, w_ref[...], m_ref[...], v_ref[...])

    lay = pl.BlockSpec((None, n, cb), lambda j, sh: (layer, 0, j))
    stack = _sds((L, n, C), F32)
    grid_spec = pltpu.PrefetchScalarGridSpec(
        num_scalar_prefetch=1, grid=(C // cb,),
        in_specs=[pl.BlockSpec((N_DEV, n_p, cb), lambda j, sh: (0, 0, j)), lay, lay, lay]
        + [pl.BlockSpec(memory_space=pl.ANY)] * n_prev,
        out_specs=(lay, lay, lay, lay), scratch_shapes=[pltpu.VMEM((n_p, cb), F32)] if n_p != n else [])
    return pl.pallas_call(body, name=name, grid_spec=grid_spec, out_shape=(stack, stack, stack, stack),
                          input_output_aliases={5 + k: k for k in range(n_prev)},
                          compiler_params=pltpu.CompilerParams(dimension_semantics=('parallel',),
                                                               vmem_limit_bytes=VMEM_LIMIT),
                          interpret=False)(shift, parts, w, m, v, *(prev or ()))


def kernel(x, mem, mix_norm, w_in, b_gate, b_forget, conv_a, w_out_a, w_out_b, conv_c, conv_c_bias, ln_c_gain, ln_c_bias, w_out_c, w_o, xattn_norm, mem_norm, w_xq, w_xkv, w_xo, ffn_norm, w_gate_up, w_down, final_norm, loss_target, m_mix_norm, m_w_in, m_b_gate, m_b_forget, m_conv_a, m_w_out_a, m_w_out_b, m_conv_c, m_conv_c_bias, m_ln_c_gain, m_ln_c_bias, m_w_out_c, m_w_o, m_xattn_norm, m_mem_norm, m_w_xq, m_w_xkv, m_w_xo, m_ffn_norm, m_w_gate_up, m_w_down, m_final_norm, v_mix_norm, v_w_in, v_b_gate, v_b_forget, v_conv_a, v_w_out_a, v_w_out_b, v_conv_c, v_conv_c_bias, v_ln_c_gain, v_ln_c_bias, v_w_out_c, v_w_o, v_xattn_norm, v_mem_norm, v_w_xq, v_w_xkv, v_w_xo, v_ffn_norm, v_w_gate_up, v_w_down, v_final_norm):
    P = dict(zip(ARG_NAMES, (x, mem, mix_norm, w_in, b_gate, b_forget, conv_a, w_out_a, w_out_b, conv_c, conv_c_bias, ln_c_gain, ln_c_bias, w_out_c, w_o, xattn_norm, mem_norm, w_xq, w_xkv, w_xo, ffn_norm, w_gate_up, w_down, final_norm, loss_target, m_mix_norm, m_w_in, m_b_gate, m_b_forget, m_conv_a, m_w_out_a, m_w_out_b, m_conv_c, m_conv_c_bias, m_ln_c_gain, m_ln_c_bias, m_w_out_c, m_w_o, m_xattn_norm, m_mem_norm, m_w_xq, m_w_xkv, m_w_xo, m_ffn_norm, m_w_gate_up, m_w_down, m_final_norm, v_mix_norm, v_w_in, v_b_gate, v_b_forget, v_conv_a, v_w_out_a, v_w_out_b, v_conv_c, v_conv_c_bias, v_ln_c_gain, v_ln_c_bias, v_w_out_c, v_w_o, v_xattn_norm, v_mem_norm, v_w_xq, v_w_xkv, v_w_xo, v_ffn_norm, v_w_gate_up, v_w_down, v_final_norm)))
    return _step(P)


_T_VIEW = ('w_in', 'w_gate_up', 'w_xo')
_BIG = [w for w in SHARDED if w not in ('conv_a', 'conv_c')]
_OTHER = [w for w in _BIG if w != 'w_in']
_CONVS = (('conv_a', CONV_A_W, 8), ('conv_c', CONV_C_W, 32))
_EARLY = ['w_down', 'w_gate_up', 'w_xo', 'w_xq', 'w_xkv', 'w_o']
_LATE = ['w_out_c', 'w_out_b', 'w_out_a', 'w_in']


def _view(name, a):
    return jnp.transpose(a, (0, 2, 1)) if name in _T_VIEW else a


def _step(P):
    L, D = P['mix_norm'].shape
    T = P['x'].shape[1]
    n_in = P['w_in'].shape[2]
    stride = n_in // ROW_TILE * ROW_TILE
    rem = n_in - stride
    win = stride + ROW_TILE
    assert rem * N_DEV == ROW_TILE
    r_in = n_in * N_DEV
    f_off = 6 * D
    hi_off = f_off + FOX_HEADS
    x_i, y_i, c_i = _place()
    me = 4 * x_i + 2 * y_i + c_i
    V = {}
    for w in _BIG:
        for k in (w, 'm_' + w, 'v_' + w):
            V[k] = _view(w, P[k])
    n_own = {w: V[w].shape[1] for w in _OTHER}
    shift = jnp.reshape(rem * me, (1,)).astype(jnp.int32)
    no_shift = jnp.zeros((1,), jnp.int32)
    row = lambda a, l: a[l][None, :]
    dus = lax.dynamic_update_slice

    conv_taps = {w: (taps, padded) for w, taps, padded in _CONVS}

    def gather_start(l, names, tag, deps):
        srcs, lands, plan = [], [], []
        for w in names:
            si, li = len(srcs), len(lands)
            if w == 'w_in':
                first = rem * me + sum(t[0, 0] for t in deps).astype(jnp.int32) if deps else rem * me
                srcs.append(dus(jnp.zeros((win, D), BF16), V[w][l].astype(BF16), (first, 0)))
                lands += [_sds((r_in, D), BF16), _sds((N_DEV, ROW_TILE, D), BF16)]
                plan += [(si, li, 'rows', (ROW_TILE, stride, stride, ROW_TILE)), (si, li + 1, 'slot', (0, ROW_TILE))]
            elif w in conv_taps:
                srcs.append(P[w][l])
                lands.append(_sds((N_DEV,) + srcs[si].shape, F32))
                plan.append((si, li, 'slot', None))
            else:
                srcs.append(V[w][l].astype(BF16))
                lands.append(_sds((N_DEV * n_own[w], srcs[si].shape[1]), BF16))
                plan.append((si, li, 'rows', (0, n_own[w], n_own[w], 0)))
        lands = _place_own(srcs, plan, lands, 'gather_own', deps)
        sems, srcs, lands, token = _gather_start(srcs, lands, plan, f'gather_start_{tag}{l}')
        return dict(names=names, name=f'{tag}{l}', plan=plan, send=sems[0], recv=sems[1], srcs=srcs, lands=lands,
                    token=token)

    def gather_pass(g, after):
        g['pass'], g['lands'], g['token'] = _gather_pass(g['recv'], g['lands'], g['plan'], after,
                                                        'gather_pass_' + g['name'])

    def gather_wait(g, after):
        lands = iter(_gather_wait(g['send'], g['pass'], g['srcs'], g['lands'], g['plan'], after,
                                  'gather_wait_' + g['name']))
        W = {}
        for w in g['names']:
            if w == 'w_in':
                W[w] = _merge_edges(next(lands), next(lands), stride // ROW_TILE)
            elif w in conv_taps:
                taps, padded = conv_taps[w]
                full = jnp.transpose(next(lands), (1, 0, 2)).reshape(taps, D)
                W[w] = jnp.pad(full, ((0, padded - taps), (0, 0)))
            else:
                W[w] = next(lands)
        return W

    first_part = ['w_in'] + list(conv_taps)
    gathers, prev = [], []
    for l in range(L):
        part_a = gather_start(l, first_part, 'a', prev)
        part_b = gather_start(l, _OTHER, 'b', [part_a['token']])
        gathers.append([part_a, part_b])
        prev = [part_b['token']]
    gather_pass(gathers[0][0], prev)

    mem_x = P['mem'][0]
    mem_n = _rms_fwd(mem_x, P['mem_norm'][None, :], 'rms_mem')
    xs = P['x'][0]
    saved = []
    for l in range(L):
        W = gather_wait(gathers[l][0], [xs])
        s = {'x0': xs, 'W': W}
        wt = W['w_in']
        s['h1'] = _rms_fwd(xs, row(P['mix_norm'], l), 'rms_mix')
        s['z1'] = z1 = _mm(s['h1'], wt, mode='nt', b_rows=(0, f_off), name='mm_in_lo')
        s['z2'] = z2 = _mm(s['h1'], wt, mode='nt', b_rows=(hi_off, r_in - hi_off), name='mm_in_hi')
        s['zf'] = _mm(s['h1'], wt, mode='nt', b_rows=(f_off, FOX_HEADS), out_dtype=F32, name='mm_in_f')
        s['ya_pre'] = _mixa_fwd(z1, W['conv_a'])
        s['c'] = _fox_c(s['zf'], row(P['b_forget'], l))
        s['o'], s['lse'] = _fox_fwd(z1, s['c'])
        for g in gathers[l][1:]:
            gather_pass(g, [s['o']])
        s['u3'], s['u1'] = _mixc_fwd(z2, W['conv_c'], row(P['conv_c_bias'], l), row(P['ln_c_gain'], l),
                                     row(P['ln_c_bias'], l))
        for g in gathers[l][1:]:
            W.update(gather_wait(g, [s['u3']]))
        s['ya'] = _mm(s['ya_pre'], W['w_out_a'], mode='nn', name='mm_out_a')
        s['yb'] = _mm(s['o'], W['w_out_b'], mode='nn', name='mm_out_b')
        s['yc'] = _mm(s['u3'], W['w_out_c'], mode='nn', name='mm_out_c')
        s['merged'] = _gate_fwd(z2, row(P['b_gate'], l), s['ya'], s['yb'], s['yc'])
        xs = _mm(s['merged'], W['w_o'], mode='nn', add=xs, out_dtype=F32, name='mm_o')
        s['x1'] = xs
        deps = []
        if l + 1 < L:
            gather_pass(gathers[l + 1][0], [xs])
            deps = [gathers[l + 1][0]['token']]
        s['h2'] = _rms_fwd(xs, row(P['xattn_norm'], l), 'rms_xattn', deps=deps)
        s['qx'] = _mm(s['h2'], W['w_xq'], mode='nn', name='mm_xq')
        s['kv'] = _mm(mem_n, W['w_xkv'], mode='nn', name='mm_xkv')
        s['ox'] = _xattn_fwd(s['qx'], s['kv'])
        xs = _mm(s['ox'], W['w_xo'], mode='nt', add=xs, out_dtype=F32, name='mm_xo')
        s['x2'] = xs
        s['h3'] = _rms_fwd(xs, row(P['ffn_norm'], l), 'rms_ffn')
        s['gu'] = _mm(s['h3'], W['w_gate_up'], mode='nt', name='mm_gate_up')
        s['act'] = _swiglu_fwd(s['gu'])
        xs = _mm(s['act'], W['w_down'], mode='nn', add=xs, out_dtype=F32, name='mm_down')
        saved.append(s)

    dx, d_final_norm, loss_part = _final(xs, P['final_norm'][None, :], P['loss_target'][0])

    GS = {w: [None] * L for w in SMALL}
    G_conv = {'conv_a': [None] * L, 'conv_c': [None] * L}
    d_mem_n = None
    results = {w: None for w in _BIG}

    def exchange_start(G, names, l, tag, after):
        srcs = [G[w] for w in names]
        plan, lands = [], []
        for i, w in enumerate(names):
            step, n = (stride, win) if w == 'w_in' else (n_own[w], n_own[w])
            lands.append(_sds((N_DEV, n, srcs[i].shape[1]), BF16))
            plan.append((i, i, 'chunk', (step, n)))
        lands = _place_own(srcs, plan, lands, 'exchange_own', [])
        sems, srcs, lands, token = _scatter_start(srcs, lands, plan, after, f'exchange_start_{tag}{l}')
        return (names, l, tag, plan, sems, srcs, lands, token)

    def exchange_finish(e, after):
        names, l, tag, plan, sems, srcs, lands, _ = e
        lands = _scatter_wait(sems, srcs, lands, plan, after, f'exchange_wait_{tag}{l}')
        for i, w in enumerate(names):
            results[w] = _adamw_layer(lands[i], V[w], V['m_' + w], V['v_' + w], l, results[w],
                                      shift if w == 'w_in' else no_shift, 'adamw_' + w)

    pending = []
    for l in reversed(range(L)):
        s = saved[l]
        W, z1, z2 = s['W'], s['z1'], s['z2']
        wt = W['w_in']
        G = {}
        d_act = _mm(dx, W['w_down'], mode='nt', name='mmb_down', deps=[e[-1] for e in pending])
        G['w_down'] = _mm(s['act'], dx, mode='tn', name='mmg_w_down')
        dgu = _swiglu_bwd(s['gu'], d_act)
        dh3 = _mm(dgu, W['w_gate_up'], mode='nn', out_dtype=F32, name='mmb_gate_up')
        G['w_gate_up'] = _mm(dgu, s['h3'], mode='tn', name='mmg_w_gate_up')
        dx, GS['ffn_norm'][l] = _rms_bwd(s['x2'], row(P['ffn_norm'], l), dh3, dx, 'rms_ffn_bwd')
        d_ox = _mm(dx, W['w_xo'], mode='nn', out_dtype=F32, name='mmb_xo')
        G['w_xo'] = _mm(dx, s['ox'], mode='tn', name='mmg_w_xo')
        dqx, dkv = _xattn_bwd(s['qx'], s['kv'], d_ox)
        dh2 = _mm(dqx, W['w_xq'], mode='nt', out_dtype=F32, name='mmb_xq')
        G['w_xq'] = _mm(s['h2'], dqx, mode='tn', name='mmg_w_xq')
        G['w_xkv'] = _mm(mem_n, dkv, mode='tn', name='mmg_w_xkv')
        d_mem_n = _mm(dkv, W['w_xkv'], mode='nt', add=d_mem_n, out_dtype=F32, name='mmb_xkv')
        dx, GS['xattn_norm'][l] = _rms_bwd(s['x1'], row(P['xattn_norm'], l), dh2, dx, 'rms_xattn_bwd')
        dm = _mm(dx, W['w_o'], mode='nt', out_dtype=F32, name='mmb_o')
        G['w_o'] = _mm(s['merged'], dx, mode='tn', name='mmg_w_o')
        early = exchange_start(G, _EARLY, l, 'a', [])
        dya, dyb, dyc, dz2, GS['b_gate'][l] = _gate_bwd(z2, row(P['b_gate'], l), s['ya'], s['yb'], s['yc'], dm)
        du3 = _mm(dyc, W['w_out_c'], mode='nt', out_dtype=F32, name='mmb_out_c', deps=[early[-1]])
        G['w_out_c'] = _mm(s['u3'], dyc, mode='tn', name='mmg_w_out_c')
        dz2, G_conv['conv_c'][l], GS['conv_c_bias'][l], GS['ln_c_gain'][l], GS['ln_c_bias'][l] = _mixc_bwd(
            z2, W['conv_c'], row(P['ln_c_gain'], l), row(P['ln_c_bias'], l), s['u1'], du3, dz2)
        dya_pre = _mm(dya, W['w_out_a'], mode='nt', out_dtype=F32, name='mmb_out_a')
        G['w_out_a'] = _mm(s['ya_pre'], dya, mode='tn', name='mmg_w_out_a')
        dz1, G_conv['conv_a'][l] = _mixa_bwd(z1, W['conv_a'], dya_pre)
        do = _mm(dyb, W['w_out_b'], mode='nt', name='mmb_out_b')
        G['w_out_b'] = _mm(s['o'], dyb, mode='tn', name='mmg_w_out_b')
        dz1, dc_pairs = _fox_bwd(z1, s['c'], s['o'], s['lse'], do, dz1)
        hp = FOX_HEADS // dc_pairs.shape[0]
        dc = jnp.transpose(dc_pairs[:, :, :hp], (1, 0, 2)).reshape(T, FOX_HEADS)
        dzf, GS['b_forget'][l] = _fox_c_bwd(dc, s['zf'], row(P['b_forget'], l))
        segs = [(dzf, f_off), (dz1, 0), (dz2, hi_off)]
        dh1, g_in = None, r_in
        for dz, off in segs:
            dh1 = _mm(dz, wt, mode='nn', b_rows=(off, dz.shape[1]), add=dh1, out_dtype=F32, name='mmb_in')
            g_in = _mm(dz, s['h1'], mode='tn', out=g_in, o_off=off, name='mmg_w_in')
        G['w_in'] = g_in
        dx, GS['mix_norm'][l] = _rms_bwd(s['x0'], row(P['mix_norm'], l), dh1, dx, 'rms_mix_bwd')
        for e in pending:
            exchange_finish(e, [dx])
        pending = [early]
        if l > 0:
            pending.append(exchange_start(G, _LATE, l, 'b', []))

    _, d_mem_norm = _rms_bwd(mem_x, P['mem_norm'][None, :], d_mem_n, None, 'rms_mem_bwd')

    small_parts = {w: jnp.concatenate(GS[w], axis=0) for w in SMALL if w not in ('mem_norm', 'final_norm')}
    small_parts['mem_norm'] = d_mem_norm
    small_parts['final_norm'] = d_final_norm
    conv_parts = [jnp.stack(G_conv[w])[:, :taps] for w, taps, _ in _CONVS]
    sizes = [P[w].size for w in SMALL]
    conv_sizes = [c.size for c in conv_parts]
    n_small = sum(sizes) + sum(conv_sizes) + LANES
    n_rows = -(-n_small // (8 * LANES)) * 8

    def pack(parts):
        flat = jnp.concatenate([p.reshape(-1) for p in parts])
        return jnp.pad(flat, (0, n_rows * LANES - flat.size)).reshape(n_rows, LANES)

    g_small = pack([small_parts[w] for w in SMALL] + conv_parts + [loss_part])
    (all_small,) = _allgather([g_small], [True], 'gather_small')
    pending.append(exchange_start(G, _LATE, 0, 'b', [all_small]))
    sm = _adamw(all_small, pack([P[w] for w in SMALL]), pack([P['m_' + w] for w in SMALL]),
                pack([P['v_' + w] for w in SMALL]), 'adamw_small')
    g_all = sm[0].reshape(-1)
    loss = g_all[sum(sizes) + sum(conv_sizes)]
    final = {}
    off = 0
    for w, n in zip(SMALL, sizes):
        final[w] = [o.reshape(-1)[off:off + n].reshape(P[w].shape) for o in sm]
        off += n
    for (w, taps, _), c, n in zip(_CONVS, conv_parts, conv_sizes):
        n_col = P[w].shape[2]
        g_own = lax.dynamic_slice(g_all[off:off + n].reshape(c.shape), (0, 0, n_col * me), (L, taps, n_col))
        flat = lambda a: a.reshape(L * taps, n_col)
        outs = _adamw(flat(g_own)[None], flat(P[w]), flat(P['m_' + w]), flat(P['v_' + w]), 'adamw_' + w)
        final[w] = [o.reshape(P[w].shape) for o in outs]
        off += n

    early0, late0 = pending
    exchange_finish(early0, [sm[0], final['conv_c'][0]])
    exchange_finish(late0, [results[w][0] for w in _EARLY])
    for w in _BIG:
        final[w] = [_view(w, o) for o in results[w]]

    out = [loss, dx[None]]
    for k in range(4):
        out += [final[w][k] for w in WEIGHTS]
    return tuple(out)
```

```python
import functools

import jax
import jax.numpy as jnp
from jax import lax
from jax.experimental import pallas as pl
from jax.experimental.pallas import tpu as pltpu

F32 = jnp.float32
BF16 = jnp.bfloat16
MESH = pl.DeviceIdType.MESH

N_DEV = 8
EPS = 1e-6
FOX_HEADS = 16
X_HEADS = 4
LANES = 128
ROW_TILE = 16
HALO_A = 16
HALO_C = 32
SUBLANES = 8
CONV_ROWS = 128
ROW_CHUNK = 32
CONV_A_W = 3
CONV_C_W = 31
VMEM_LIMIT = 56 << 20

ADAM_LR = 0.001
ADAM_B1 = 0.9
ADAM_B2 = 0.999
ADAM_EPS = 1e-08
ADAM_WD = 0.01
ADAM_STEP = 10

WEIGHTS = ['mix_norm', 'w_in', 'b_gate', 'b_forget', 'conv_a', 'w_out_a', 'w_out_b', 'conv_c', 'conv_c_bias',
           'ln_c_gain', 'ln_c_bias', 'w_out_c', 'w_o', 'xattn_norm', 'mem_norm', 'w_xq', 'w_xkv', 'w_xo',
           'ffn_norm', 'w_gate_up', 'w_down', 'final_norm']
SHARDED = ['w_in', 'conv_a', 'w_out_a', 'w_out_b', 'conv_c', 'w_out_c', 'w_o', 'w_xq', 'w_xkv', 'w_xo',
           'w_gate_up', 'w_down']
ROW_SHARDED = ['w_out_a', 'w_out_b', 'w_out_c', 'w_o', 'w_xq', 'w_xkv', 'w_down']
SMALL = [w for w in WEIGHTS if w not in SHARDED]
ARG_NAMES = (['x', 'mem'] + WEIGHTS + ['loss_target'] + ['m_' + w for w in WEIGHTS] + ['v_' + w for w in WEIGHTS])


def _pick(n, cands=(1024, 1408, 512, 256, 128)):
    for c in cands:
        if n % c == 0:
            return c
    return n


def _call(body, *, name, out_shape, grid=(), in_specs=None, out_specs=None, scratch=(), sem=None, aliases=None,
          deps=()):
    params = dict(vmem_limit_bytes=VMEM_LIMIT)
    if sem is not None:
        params['dimension_semantics'] = sem
    n_in, n_dep = len(in_specs), len(deps)

    def body_without_deps(*refs):
        body(*refs[:n_in], *refs[n_in + n_dep:])

    call = pl.pallas_call(body_without_deps, name=name, out_shape=out_shape, grid=grid, scratch_shapes=list(scratch),
                          in_specs=list(in_specs) + [pl.BlockSpec(memory_space=pl.ANY)] * n_dep, out_specs=out_specs,
                          input_output_aliases=aliases or {}, compiler_params=pltpu.CompilerParams(**params),
                          interpret=False)
    return lambda *args: call(*args, *deps)


def _sds(shape, dtype):
    return jax.ShapeDtypeStruct(tuple(shape), dtype)


def _sigmoid(x):
    return 1.0 / (1.0 + jnp.exp(-x))


_DN = {'nn': (((1,), (0,)), ((), ())), 'nt': (((1,), (1,)), ((), ())), 'tn': (((0,), (0,)), ((), ()))}


def _mm(a, b, *, mode, name, out_dtype=BF16, add=None, b_rows=None, out=None, o_off=0, deps=()):
    if mode == 'tn':
        K, M = a.shape
    else:
        M, K = a.shape
    b_off, b_n = (0, b.shape[0]) if b_rows is None else b_rows
    N = b_n if mode == 'nt' else b.shape[1]
    assert (b.shape[1] if mode == 'nt' else b_n) == K
    tm, tn = _pick(M), _pick(N)
    tk = K if K <= 2048 else _pick(K)
    ni, nj, nk = M // tm, N // tn, K // tk
    a_bytes, b_bytes = M * K * a.dtype.itemsize, K * N * b.dtype.itemsize
    n_outer = (b_bytes + nj * a_bytes) < (a_bytes + ni * b_bytes)
    if n_outer:
        grid = (nj, ni, nk)
        ij = lambda g0, g1: (g1, g0)
    else:
        grid = (ni, nj, nk)
        ij = lambda g0, g1: (g0, g1)

    def a_map(g0, g1, k):
        i, _ = ij(g0, g1)
        return (k, i) if mode == 'tn' else (i, k)

    def elems(blk):
        return tuple(pl.Element(s) for s in blk)

    def at(rows, cols):
        return pl.multiple_of(rows, ROW_TILE), pl.multiple_of(cols, LANES)

    a_blk = (tk, tm) if mode == 'tn' else (tm, tk)
    b_blk = (tn, tk) if mode == 'nt' else (tk, tn)
    if b_rows is None:
        def b_map(g0, g1, k):
            _, j = ij(g0, g1)
            return (j, k) if mode == 'nt' else (k, j)
        b_spec = pl.BlockSpec(b_blk, b_map)
    else:
        def b_map(g0, g1, k):
            _, j = ij(g0, g1)
            return at(b_off + j * tn, k * tk) if mode == 'nt' else at(b_off + k * tk, j * tn)
        b_spec = pl.BlockSpec(elems(b_blk), b_map)
    in_specs = [pl.BlockSpec(a_blk, a_map), b_spec]
    args = [a, b]
    has_add = add is not None
    if has_add:
        in_specs.append(pl.BlockSpec((tm, tn), lambda g0, g1, k: ij(g0, g1)))
        args.append(add)
    aliases = {}
    if out is None:
        out_shape = _sds((M, N), out_dtype)
        out_spec = pl.BlockSpec((tm, tn), lambda g0, g1, k: ij(g0, g1))
    else:
        if isinstance(out, int):
            out_shape = _sds((out, N), out_dtype)
        else:
            out_shape = _sds(out.shape, out.dtype)
            in_specs.append(pl.BlockSpec(memory_space=pl.ANY))
            aliases = {len(args): 0}
            args.append(out)

        def o_map(g0, g1, k):
            i, j = ij(g0, g1)
            return at(o_off + i * tm, j * tn)
        out_spec = pl.BlockSpec(elems((tm, tn)), o_map)
    dn = _DN[mode]
    n_in = len(args)

    def body(*refs):
        a_ref, b_ref = refs[0], refs[1]
        add_ref = refs[2] if has_add else None
        o_ref = refs[n_in]
        part = lax.dot_general(a_ref[...].astype(BF16), b_ref[...].astype(BF16), dn, preferred_element_type=F32)

        def finish(r):
            if has_add:
                r = r + add_ref[...]
            o_ref[...] = r.astype(o_ref.dtype)

        if nk == 1:
            finish(part)
        else:
            acc = refs[n_in + 1]
            k = pl.program_id(2)

            @pl.when(k == 0)
            def _():
                acc[...] = part

            @pl.when(k > 0)
            def _():
                acc[...] += part

            @pl.when(k == nk - 1)
            def _():
                finish(acc[...])

    scratch = [pltpu.VMEM((tm, tn), F32)] if nk > 1 else []
    return _call(body, name=name, out_shape=out_shape, grid=grid, in_specs=in_specs, out_specs=out_spec,
                 scratch=scratch, sem=('parallel', 'parallel', 'arbitrary'), aliases=aliases, deps=deps)(*args)


def _rms_fwd(x, g, name, deps=()):
    R, D = x.shape
    tb = _pick(R, (512, 256, 128))

    def body(x_ref, g_ref, o_ref):
        xv = x_ref[...]
        r = lax.rsqrt(jnp.mean(xv * xv, axis=-1, keepdims=True) + EPS)
        o_ref[...] = (xv * r * g_ref[...]).astype(o_ref.dtype)

    row = pl.BlockSpec((tb, D), lambda i: (i, 0))
    vec = pl.BlockSpec((1, D), lambda i: (0, 0))
    return _call(body, name=name, out_shape=_sds((R, D), BF16), grid=(R // tb,), in_specs=[row, vec],
                 out_specs=row, sem=('parallel',), deps=deps)(x, g)


def _rms_bwd(x, g, dh, dres, name, deps=()):
    R, D = x.shape
    tb = _pick(R, (512, 256, 128))
    has_res = dres is not None

    def body(*refs):
        x_ref, g_ref, dh_ref = refs[:3]
        dx_ref, dg_ref = refs[-2:]
        xv = x_ref[...]
        dhv = dh_ref[...].astype(F32)
        r = lax.rsqrt(jnp.mean(xv * xv, axis=-1, keepdims=True) + EPS)
        u = dhv * g_ref[...]
        m = jnp.mean(u * xv, axis=-1, keepdims=True)
        dx = r * u - xv * (r * r * r * m)
        if has_res:
            dx = dx + refs[3][...]
        dx_ref[...] = dx

        @pl.when(pl.program_id(0) == 0)
        def _():
            dg_ref[...] = jnp.zeros_like(dg_ref)

        dg_ref[...] += jnp.sum(dhv * xv * r, axis=0, keepdims=True)

    row = pl.BlockSpec((tb, D), lambda i: (i, 0))
    vec = pl.BlockSpec((1, D), lambda i: (0, 0))
    args = [x, g, dh] + ([dres] if has_res else [])
    return _call(body, name=name, out_shape=(_sds((R, D), F32), _sds((1, D), F32)), grid=(R // tb,),
                 in_specs=[row, vec, row] + ([row] if has_res else []), out_specs=(row, vec),
                 sem=('arbitrary',), deps=deps)(*args)


def _final(x, g, tgt):
    R, D = x.shape
    tb = _pick(R, (512, 256, 128))

    def body(x_ref, g_ref, t_ref, dx_ref, dg_ref, loss_ref):
        xv = x_ref[...]
        gv = g_ref[...]
        r = lax.rsqrt(jnp.mean(xv * xv, axis=-1, keepdims=True) + EPS)
        e = xv * r * gv - t_ref[...]
        row_loss = jnp.mean(e * e, axis=-1, keepdims=True)
        blk_loss = 0.5 * jnp.sum(row_loss, axis=0, keepdims=True)
        dy = e * (1.0 / D)
        u = dy * gv
        m = jnp.mean(u * xv, axis=-1, keepdims=True)
        dx_ref[...] = r * u - xv * (r * r * r * m)

        @pl.when(pl.program_id(0) == 0)
        def _():
            dg_ref[...] = jnp.zeros_like(dg_ref)
            loss_ref[...] = jnp.zeros_like(loss_ref)

        dg_ref[...] += jnp.sum(dy * xv * r, axis=0, keepdims=True)
        loss_ref[...] += jnp.broadcast_to(blk_loss, loss_ref.shape)

    row = pl.BlockSpec((tb, D), lambda i: (i, 0))
    vec = pl.BlockSpec((1, D), lambda i: (0, 0))
    one = pl.BlockSpec((1, LANES), lambda i: (0, 0))
    return _call(body, name='final_loss', out_shape=(_sds((R, D), F32), _sds((1, D), F32), _sds((1, LANES), F32)),
                 grid=(R // tb,), in_specs=[row, vec, row], out_specs=(row, vec, one), sem=('arbitrary',))(x, g, tgt)


def _col(tb, w, cb):
    return pl.BlockSpec((tb, w), lambda i: (i, cb))


def _prev(tb, h, w, cb):
    return pl.BlockSpec((h, w), lambda i: (jnp.maximum(i * (tb // h) - 1, 0), cb))


def _next(tb, h, w, cb, rows):
    return pl.BlockSpec((h, w), lambda i: (jnp.minimum((i + 1) * (tb // h), rows // h - 1), cb))


def _full(shape):
    return pl.BlockSpec(shape, lambda i: (0,) * len(shape))


def _mixa_fwd(z, w):
    T = z.shape[0]
    D = w.shape[1]
    tb = _pick(T, (256, 128))
    H = HALO_A

    def body(ab, ac, au, acp, aup, w_ref, o_ref, ext):
        first = pl.program_id(0) == 0
        ext[0:H, :] = jnp.where(first, 0.0, acp[...].astype(F32) * aup[...].astype(F32))
        ext[H:H + tb, :] = ac[...].astype(F32) * au[...].astype(F32)
        cp = jnp.zeros((tb, D), F32)
        for k in range(CONV_A_W):
            off = H - (CONV_A_W - 1) + k
            cp = cp + ext[off:off + tb, :] * w_ref[k:k + 1, :]
        o_ref[...] = (ab[...].astype(F32) * cp).astype(o_ref.dtype)

    return _call(body, name='mixa_fwd', out_shape=_sds((T, D), BF16), grid=(T // tb,),
                 in_specs=[_col(tb, D, 0), _col(tb, D, 1), _col(tb, D, 2), _prev(tb, H, D, 1), _prev(tb, H, D, 2),
                           _full((8, D))],
                 out_specs=_col(tb, D, 0), scratch=[pltpu.VMEM((H + tb, D), F32)], sem=('parallel',))(z, z, z, z, z, w)


def _mixa_bwd(z, w, dy):
    T = z.shape[0]
    D = w.shape[1]
    tb = _pick(T, (256, 128))
    H = HALO_A
    nb = T // tb

    def body(ab, ac, au, acp, aup, abn, dy_ref, dyn, w_ref, dz_ref, dw_ref, pext, dext):
        i = pl.program_id(0)
        a_b, a_c, a_u = ab[...].astype(F32), ac[...].astype(F32), au[...].astype(F32)
        pext[0:H, :] = jnp.where(i == 0, 0.0, acp[...].astype(F32) * aup[...].astype(F32))
        pext[H:H + tb, :] = a_c * a_u
        dyv = dy_ref[...].astype(F32)
        dcp = dyv * a_b
        dext[0:tb, :] = dcp
        dext[tb:tb + H, :] = jnp.where(i == nb - 1, 0.0, dyn[...].astype(F32) * abn[...].astype(F32))

        @pl.when(i == 0)
        def _():
            dw_ref[...] = jnp.zeros_like(dw_ref)

        cp = jnp.zeros((tb, D), F32)
        dp = jnp.zeros((tb, D), F32)
        for k in range(CONV_A_W):
            off = H - (CONV_A_W - 1) + k
            wk = w_ref[k:k + 1, :]
            pk = pext[off:off + tb, :]
            cp = cp + pk * wk
            dp = dp + dext[CONV_A_W - 1 - k:CONV_A_W - 1 - k + tb, :] * wk
            dw_ref[k:k + 1, :] += jnp.sum(dcp * pk, axis=0, keepdims=True)
        dz_ref[:, 0:D] = (dyv * cp).astype(dz_ref.dtype)
        dz_ref[:, D:2 * D] = (dp * a_u).astype(dz_ref.dtype)
        dz_ref[:, 2 * D:3 * D] = (dp * a_c).astype(dz_ref.dtype)

    return _call(body, name='mixa_bwd', out_shape=(_sds((T, 6 * D), BF16), _sds((8, D), F32)), grid=(nb,),
                 in_specs=[_col(tb, D, 0), _col(tb, D, 1), _col(tb, D, 2), _prev(tb, H, D, 1), _prev(tb, H, D, 2),
                           _next(tb, H, D, 0, T), _col(tb, D, 0), _next(tb, H, D, 0, T), _full((8, D))],
                 out_specs=(_col(tb, 3 * D, 0), _full((8, D))),
                 scratch=[pltpu.VMEM((H + tb, D), F32), pltpu.VMEM((tb + H, D), F32)],
                 sem=('arbitrary',))(z, z, z, z, z, z, dy, dy, w)


def _split3(v):
    hi = v.astype(BF16)
    r1 = v - hi.astype(F32)
    mid = r1.astype(BF16)
    lo = (r1 - mid.astype(F32)).astype(BF16)
    return hi, mid, lo


def _tri_dot(tri, v):
    hi, mid, lo = _split3(v)
    d = lambda p: jnp.dot(tri, p, preferred_element_type=F32)
    return d(hi) + d(mid) + d(lo)


def _fox_c(zf, bf):
    T, H = zf.shape
    tb = LANES

    def body(zf_ref, b_ref, c_ref, carry):
        @pl.when(pl.program_id(0) == 0)
        def _():
            carry[...] = jnp.zeros_like(carry)

        xv = zf_ref[...] + b_ref[...]
        lf = jnp.minimum(xv, 0.0) - jnp.log(1.0 + jnp.exp(-jnp.abs(xv)))
        r = lax.broadcasted_iota(jnp.int32, (tb, tb), 0)
        c = lax.broadcasted_iota(jnp.int32, (tb, tb), 1)
        tri = (r >= c).astype(BF16)
        cs = _tri_dot(tri, lf) + carry[...]
        c_ref[...] = cs
        carry[...] = cs[tb - 1:tb, :]

    blk = pl.BlockSpec((tb, H), lambda i: (i, 0))
    return _call(body, name='fox_cumsum', out_shape=_sds((T, H), F32), grid=(T // tb,),
                 in_specs=[blk, _full((1, H))], out_specs=blk, scratch=[pltpu.VMEM((1, H), F32)],
                 sem=('arbitrary',))(zf, bf)


def _fox_c_bwd(dc, zf, bf):
    T, H = zf.shape
    tb = LANES
    nb = T // tb

    def body(dc_ref, zf_ref, b_ref, dz_ref, db_ref, carry):
        @pl.when(pl.program_id(0) == 0)
        def _():
            carry[...] = jnp.zeros_like(carry)
            db_ref[...] = jnp.zeros_like(db_ref)

        r = lax.broadcasted_iota(jnp.int32, (tb, tb), 0)
        c = lax.broadcasted_iota(jnp.int32, (tb, tb), 1)
        tri = (c >= r).astype(BF16)
        dlf = _tri_dot(tri, dc_ref[...]) + carry[...]
        carry[...] = dlf[0:1, :]
        xv = zf_ref[...] + b_ref[...]
        dz = dlf * _sigmoid(-xv)
        dz_ref[...] = dz
        db_ref[...] += jnp.sum(dz, axis=0, keepdims=True)

    blk = pl.BlockSpec((tb, H), lambda i: (nb - 1 - i, 0))
    return _call(body, name='fox_cumsum_bwd', out_shape=(_sds((T, H), F32), _sds((1, H), F32)), grid=(nb,),
                 in_specs=[blk, blk, _full((1, H))], out_specs=(blk, _full((1, H))),
                 scratch=[pltpu.VMEM((1, H), F32)], sem=('arbitrary',))(dc, zf, bf)


N_PIECES = 3


def _head_col(c_ref, h):
    lane = lax.broadcasted_iota(jnp.int32, c_ref.shape, 1)
    return jnp.sum(jnp.where(lane == h, c_ref[...], 0.0), axis=1, keepdims=True)


def _pieces(v, sign=1.0):
    return [sign * p.astype(F32) for p in _split3(v)]


def _augment(block, hh, dh, a_cols, b_cols):
    lane = lax.broadcasted_iota(jnp.int32, block.shape, 1)
    base = (1 - hh) * dh
    out = jnp.where((lane >= hh * dh) & (lane < (hh + 1) * dh), block.astype(F32), 0.0)
    for n, col in enumerate(list(a_cols) + list(b_cols)):
        out = jnp.where(lane == base + n, col, out)
    return out.astype(BF16)


def _fox_probs(q_aug, k_aug, tri, q0, ke, shift=None):
    s = lax.dot_general(q_aug[q0:ke, :], k_aug[0:ke, :], _DN['nt'], preferred_element_type=F32)
    s_dg = jnp.where(tri, s[:, q0:ke], -jnp.inf)
    if shift is None:
        shift = jnp.max(s_dg, axis=-1, keepdims=True)
        if q0:
            shift = jnp.maximum(shift, jnp.max(s[:, 0:q0], axis=-1, keepdims=True))
    parts = ([jnp.exp(s[:, 0:q0] - shift)] if q0 else []) + [jnp.exp(s_dg - shift)]
    return parts, shift


def _cat(parts):
    parts = [p.astype(BF16) for p in parts]
    return parts[0] if len(parts) == 1 else jnp.concatenate(parts, axis=1)


def _fox_fwd(z, c):
    T = z.shape[0]
    D = z.shape[1] // 6
    dh = D // FOX_HEADS
    hp = LANES // dh
    ncb = D // LANES
    tq = _pick(T, (256, 128))
    scale = dh ** -0.5
    ones = [1.0] * N_PIECES

    def body(q_ref, k_ref, v_ref, c_ref, o_ref, lse_ref, o_all):
        j = pl.program_id(0)
        lse_ref[...] = jnp.zeros_like(lse_ref)
        r = lax.broadcasted_iota(jnp.int32, (tq, tq), 0)
        tri = r >= lax.broadcasted_iota(jnp.int32, (tq, tq), 1)
        for hh in range(hp):
            base = (1 - hh) * dh
            ch = _head_col(c_ref, j * hp + hh)
            q_aug = _augment(q_ref[...].astype(F32) * scale, hh, dh, _pieces(ch), ones)
            k_aug = _augment(k_ref[...], hh, dh, ones, _pieces(ch, -1.0))
            v_aug = _augment(v_ref[...], hh, dh, ones, [])
            for q0 in range(0, T, tq):
                ke = q0 + tq
                parts, m = _fox_probs(q_aug, k_aug, tri, q0, ke)
                o_aug = jnp.dot(_cat(parts), v_aug[0:ke, :], preferred_element_type=F32)
                l = o_aug[:, base:base + 1]
                o_all[hh, q0:ke, :] = o_aug * (1.0 / l)
                lse_ref[q0:ke, hh:hh + 1] = m + jnp.log(l)
        lane = lax.broadcasted_iota(jnp.int32, (T, LANES), 1)
        out = o_all[0]
        for hh in range(1, hp):
            out = jnp.where(lane >= hh * dh, o_all[hh], out)
        o_ref[...] = out.astype(o_ref.dtype)

    blk = lambda cb0: pl.BlockSpec((T, LANES), lambda j: (0, cb0 + j))
    return _call(body, name='fox_fwd', out_shape=(_sds((T, D), BF16), _sds((ncb, T, LANES), F32)), grid=(ncb,),
                 in_specs=[blk(3 * ncb), blk(4 * ncb), blk(5 * ncb), _full((T, FOX_HEADS))],
                 out_specs=(blk(0), pl.BlockSpec((None, T, LANES), lambda j: (j, 0, 0))),
                 scratch=[pltpu.VMEM((hp, T, LANES), F32)], sem=('parallel',))(z, z, z, c)


def _fox_bwd(z, c, o, lse, do, dz1):
    T = z.shape[0]
    D = o.shape[1]
    dh = D // FOX_HEADS
    hp = LANES // dh
    ncb = D // LANES
    tq = _pick(T, (256, 128))
    scale = dh ** -0.5
    ones = [1.0] * N_PIECES

    def body(q_ref, k_ref, v_ref, c_ref, o_ref, lse_ref, do_ref, _, dz_ref, dc_ref, dq_all, dk_acc, dv_acc, stage,
             sems):
        j = pl.program_id(0)
        dk_acc[...] = jnp.zeros_like(dk_acc)
        dv_acc[...] = jnp.zeros_like(dv_acc)
        dc_ref[...] = jnp.zeros_like(dc_ref)
        r = lax.broadcasted_iota(jnp.int32, (tq, tq), 0)
        tri = r >= lax.broadcasted_iota(jnp.int32, (tq, tq), 1)
        lane = lax.broadcasted_iota(jnp.int32, (T, LANES), 1)
        for hh in range(hp):
            base = (1 - hh) * dh
            own = (lane >= hh * dh) & (lane < (hh + 1) * dh)
            ch = _head_col(c_ref, j * hp + hh)
            dov = do_ref[...].astype(F32)
            delta = jnp.sum(jnp.where(own, dov * o_ref[...].astype(F32), 0.0), axis=1, keepdims=True)
            q_aug = _augment(q_ref[...].astype(F32) * scale, hh, dh, _pieces(ch - lse_ref[:, hh:hh + 1]), ones)
            k_aug = _augment(k_ref[...], hh, dh, ones, _pieces(ch, -1.0))
            v_aug = _augment(v_ref[...], hh, dh, ones, [])
            do_aug = _augment(dov, hh, dh, _pieces(delta, -1.0), [])
            for q0 in range(0, T, tq):
                ke = q0 + tq
                p, _ = _fox_probs(q_aug, k_aug, tri, q0, ke, shift=0.0)
                dp = lax.dot_general(do_aug[q0:ke, :], v_aug[0:ke, :], _DN['nt'], preferred_element_type=F32)
                ds = [p[0] * dp[:, 0:q0], p[1] * dp[:, q0:ke]] if q0 else [p[0] * dp]
                dsb, pb = _cat(ds), _cat(p)
                dq_all[hh, q0:ke, :] = jnp.dot(dsb, k_aug[0:ke, :], preferred_element_type=F32)
                dk_acc[hh, 0:ke, :] += lax.dot_general(dsb, q_aug[q0:ke, :], _DN['tn'], preferred_element_type=F32)
                dv_acc[hh, 0:ke, :] += lax.dot_general(pb, do_aug[q0:ke, :], _DN['tn'], preferred_element_type=F32)
            dc_ref[:, hh:hh + 1] = dq_all[hh, :, base:base + 1] - dk_acc[hh, :, base + N_PIECES:base + N_PIECES + 1]
        dq, dk, dv = dq_all[0], dk_acc[0], dv_acc[0]
        for hh in range(1, hp):
            dq = jnp.where(lane >= hh * dh, dq_all[hh], dq)
            dk = jnp.where(lane >= hh * dh, dk_acc[hh], dk)
            dv = jnp.where(lane >= hh * dh, dv_acc[hh], dv)
        copies = []
        for n, g in enumerate((dq * scale, dk, dv)):
            stage[n] = g.astype(stage.dtype)
            col = pl.multiple_of(((3 + n) * ncb + j) * LANES, LANES)
            copies.append(pltpu.make_async_copy(stage.at[n], dz_ref.at[:, pl.ds(col, LANES)], sems.at[n]))
            copies[n].start()
        for cp in copies:
            cp.wait()

    blk = lambda cb0: pl.BlockSpec((T, LANES), lambda j: (0, cb0 + j))
    pair = pl.BlockSpec((None, T, LANES), lambda j: (j, 0, 0))
    acc = pltpu.VMEM((hp, T, LANES), F32)
    anywhere = pl.BlockSpec(memory_space=pl.ANY)
    return _call(body, name='fox_bwd', out_shape=(_sds(dz1.shape, dz1.dtype), _sds((ncb, T, LANES), F32)), grid=(ncb,),
                 in_specs=[blk(3 * ncb), blk(4 * ncb), blk(5 * ncb), _full((T, FOX_HEADS)), blk(0), pair, blk(0),
                           anywhere],
                 out_specs=(anywhere, pair),
                 scratch=[acc, acc, acc, pltpu.VMEM((3, T, LANES), BF16), pltpu.SemaphoreType.DMA((3,))],
                 sem=('arbitrary',), aliases={7: 0})(z, z, z, c, o, lse, do, dz1)


def _ln_parts(u1, gain, bias):
    mu = jnp.mean(u1, axis=-1, keepdims=True)
    xc = u1 - mu
    rstd = lax.rsqrt(jnp.mean(xc * xc, axis=-1, keepdims=True) + EPS)
    xhat = xc * rstd
    return xhat, rstd, xhat * gain + bias


def _tap_groups(offsets):
    groups = {}
    for k, off in enumerate(offsets):
        groups.setdefault(off % SUBLANES, []).append((off - off % SUBLANES, k))
    return groups


def _for_taps(src_ref, r0, cols, groups, aligned, visit):
    for r, taps in groups.items():
        if r == 0:
            for base, k in taps:
                visit(k, src_ref[r0 + base:r0 + base + CONV_ROWS, cols])
            continue
        need = max(base for base, _ in taps) + CONV_ROWS
        aligned[r, 0:need, :] = src_ref[r0 + r:r0 + r + need, cols]
        for base, k in taps:
            visit(k, aligned[r, base:base + CONV_ROWS, :])


def _row_chunk(j, first=0):
    return slice(first + j * ROW_CHUNK, first + (j + 1) * ROW_CHUNK)


def _glu_rows(cv, cg, ext, halo, tb):
    for j in range(tb // ROW_CHUNK):
        rs = _row_chunk(j)
        ext[_row_chunk(j, halo), :] = cv[rs, :].astype(F32) * _sigmoid(cg[rs, :].astype(F32))


def _mixc_fwd(z, w, cb, gain, bias):
    T = z.shape[0]
    D = w.shape[1]
    tb = _pick(T, (256, 128))
    H = HALO_C

    groups = _tap_groups([H - (CONV_C_W - 1) + k for k in range(CONV_C_W)])

    def body(cv, cg, cvp, cgp, w_ref, cb_ref, g_ref, b_ref, o_ref, u1_ref, ext, aligned):
        first = pl.program_id(0) == 0
        ext[0:H, :] = jnp.where(first, 0.0, cvp[...].astype(F32) * _sigmoid(cgp[...].astype(F32)))
        _glu_rows(cv, cg, ext, H, tb)
        for r0 in range(0, tb, CONV_ROWS):
            for c0 in range(0, D, LANES):
                cols = slice(c0, c0 + LANES)
                acc = [jnp.zeros((CONV_ROWS, LANES), F32)]

                def tap(k, rows):
                    acc[0] = acc[0] + rows * w_ref[k:k + 1, cols]

                _for_taps(ext, r0, cols, groups, aligned, tap)
                u1_ref[r0:r0 + CONV_ROWS, cols] = acc[0] + cb_ref[:, cols]

        for j in range(tb // ROW_CHUNK):
            rs = _row_chunk(j)
            _, _, u2 = _ln_parts(u1_ref[rs, :], g_ref[...], b_ref[...])
            o_ref[rs, :] = (u2 * _sigmoid(u2)).astype(o_ref.dtype)

    vec = _full((1, D))
    row = _col(tb, D, 0)
    return _call(body, name='mixc_fwd', out_shape=(_sds((T, D), BF16), _sds((T, D), F32)), grid=(T // tb,),
                 in_specs=[_col(tb, D, 0), _col(tb, D, 1), _prev(tb, H, D, 0), _prev(tb, H, D, 1), _full((32, D)), vec,
                           vec, vec],
                 out_specs=(row, row), scratch=[pltpu.VMEM((H + tb, D), F32), pltpu.VMEM((SUBLANES, CONV_ROWS + H, LANES), F32)],
                 sem=('parallel',))(z, z, z, z, w, cb, gain, bias)


def _mixc_bwd(z, w, gain, bias, u1, du3, dz2):
    T = z.shape[0]
    D = w.shape[1]
    tb = _pick(T, (256, 128))
    H = HALO_C
    nb = T // tb

    def du1_of(u1v, du3v, gv, bv):
        xhat, rstd, u2 = _ln_parts(u1v, gv, bv)
        sg = _sigmoid(u2)
        du2 = du3v * (sg * (1.0 + u2 * (1.0 - sg)))
        dxh = du2 * gv
        m1 = jnp.mean(dxh, axis=-1, keepdims=True)
        m2 = jnp.mean(dxh * xhat, axis=-1, keepdims=True)
        return rstd * (dxh - m1 - xhat * m2), du2, xhat

    fwd_groups = _tap_groups([H - (CONV_C_W - 1) + k for k in range(CONV_C_W)])
    bwd_groups = _tap_groups([CONV_C_W - 1 - k for k in range(CONV_C_W)])

    def body(cv, cg, cvp, cgp, w_ref, g_ref, b_ref, u1_ref, u1n, du3_ref, du3n, _, dz_ref, dw_ref, dcb_ref, dg_ref,
             db_ref, uext, dext, dw_acc, aligned, sums):
        i = pl.program_id(0)
        uext[0:H, :] = jnp.where(i == 0, 0.0, cvp[...].astype(F32) * _sigmoid(cgp[...].astype(F32)))
        _glu_rows(cv, cg, uext, H, tb)
        du1n, _, _ = du1_of(u1n[...], du3n[...].astype(F32), g_ref[...], b_ref[...])
        dext[tb:tb + H, :] = jnp.where(i == nb - 1, 0.0, du1n)

        @pl.when(i == 0)
        def _():
            dw_acc[...] = jnp.zeros_like(dw_acc)
            sums[...] = jnp.zeros_like(sums)

        def halves(v):
            part = v[0:SUBLANES, :]
            for q in range(SUBLANES, ROW_CHUNK, SUBLANES):
                part = part + v[q:q + SUBLANES, :]
            return part

        for j in range(tb // ROW_CHUNK):
            rs = _row_chunk(j)
            du1, du2, xhat = du1_of(u1_ref[rs, :], du3_ref[rs, :].astype(F32), g_ref[...], b_ref[...])
            dext[rs, :] = du1
            sums[0:SUBLANES, :] += halves(du1)
            sums[SUBLANES:2 * SUBLANES, :] += halves(du2 * xhat)
            sums[2 * SUBLANES:3 * SUBLANES, :] += halves(du2)
        for r0 in range(0, tb, CONV_ROWS):
            rows = slice(r0, r0 + CONV_ROWS)
            for c0 in range(0, D, LANES):
                cols = slice(c0, c0 + LANES)
                acc = [jnp.zeros((CONV_ROWS, LANES), F32)]

                def tap(k, src):
                    acc[0] = acc[0] + src * w_ref[k:k + 1, cols]

                _for_taps(dext, r0, cols, bwd_groups, aligned, tap)
                c_val = cv[rows, cols].astype(F32)
                sg = _sigmoid(cg[rows, cols].astype(F32))
                dz_ref[rows, cols] = (acc[0] * sg).astype(dz_ref.dtype)
                dz_ref[rows, D + c0:D + c0 + LANES] = (acc[0] * c_val * sg * (1.0 - sg)).astype(dz_ref.dtype)
                du1_blk = dext[rows, cols]

                def tap_w(k, src):
                    prod = du1_blk * src
                    part = prod[0:SUBLANES, :]
                    for j in range(SUBLANES, CONV_ROWS, SUBLANES):
                        part = part + prod[j:j + SUBLANES, :]
                    dw_acc[k * SUBLANES:(k + 1) * SUBLANES, cols] += part

                _for_taps(uext, r0, cols, fwd_groups, aligned, tap_w)

        @pl.when(i == nb - 1)
        def _():
            dw_ref[...] = jnp.zeros_like(dw_ref)
            for k in range(CONV_C_W):
                dw_ref[k:k + 1, :] = jnp.sum(dw_acc[k * SUBLANES:(k + 1) * SUBLANES, :], axis=0, keepdims=True)
            for n, ref in enumerate((dcb_ref, dg_ref, db_ref)):
                ref[...] = jnp.sum(sums[n * SUBLANES:(n + 1) * SUBLANES, :], axis=0, keepdims=True)

    vec = _full((1, D))
    row = _col(tb, D, 0)
    nxt = _next(tb, H, D, 0, T)
    return _call(body, name='mixc_bwd',
                 out_shape=(_sds(dz2.shape, dz2.dtype), _sds((32, D), F32), _sds((1, D), F32), _sds((1, D), F32),
                            _sds((1, D), F32)), grid=(nb,),
                 in_specs=[_col(tb, D, 0), _col(tb, D, 1), _prev(tb, H, D, 0), _prev(tb, H, D, 1), _full((32, D)), vec,
                           vec, row, nxt, row, nxt, pl.BlockSpec(memory_space=pl.ANY)],
                 out_specs=(_col(tb, 2 * D, 0), _full((32, D)), vec, vec, vec),
                 scratch=[pltpu.VMEM((H + tb, D), F32), pltpu.VMEM((tb + H, D), F32),
                          pltpu.VMEM((32 * SUBLANES, D), F32), pltpu.VMEM((SUBLANES, CONV_ROWS + H, LANES), F32),
                          pltpu.VMEM((3 * SUBLANES, D), F32)],
                 sem=('arbitrary',), aliases={11: 0})(z, z, z, z, w, gain, bias, u1, u1, du3, du3, dz2)


def _gate_fwd(z, bg, ya, yb, yc):
    T = z.shape[0]
    D = ya.shape[1]
    tb = _pick(T, (256, 128))

    def body(ga, gb, gc, bg_ref, ya_ref, yb_ref, yc_ref, o_ref):
        acc = jnp.zeros((tb, D), F32)
        for n, (g, y) in enumerate(((ga, ya_ref), (gb, yb_ref), (gc, yc_ref))):
            acc = acc + _sigmoid(g[...].astype(F32) + bg_ref[:, n * D:(n + 1) * D]) * y[...].astype(F32)
        o_ref[...] = acc.astype(o_ref.dtype)

    row = _col(tb, D, 0)
    return _call(body, name='gate_fwd', out_shape=_sds((T, D), BF16), grid=(T // tb,),
                 in_specs=[_col(tb, D, 2), _col(tb, D, 3), _col(tb, D, 4), _full((1, 3 * D)), row, row, row],
                 out_specs=row, sem=('parallel',))(z, z, z, bg, ya, yb, yc)


def _gate_bwd(z, bg, ya, yb, yc, dm):
    T = z.shape[0]
    D = ya.shape[1]
    tb = _pick(T, (256, 128))

    def body(ga, gb, gc, bg_ref, ya_ref, yb_ref, yc_ref, dm_ref, dya, dyb, dyc, dg_ref, dbg_ref):
        @pl.when(pl.program_id(0) == 0)
        def _():
            dbg_ref[...] = jnp.zeros_like(dbg_ref)

        dmv = dm_ref[...]
        for n, (g, y, dy) in enumerate(((ga, ya_ref, dya), (gb, yb_ref, dyb), (gc, yc_ref, dyc))):
            cols = slice(n * D, (n + 1) * D)
            sg = _sigmoid(g[...].astype(F32) + bg_ref[:, cols])
            dy[...] = (dmv * sg).astype(dy.dtype)
            dg = dmv * y[...].astype(F32) * sg * (1.0 - sg)
            dg_ref[:, 2 * D + n * D:2 * D + (n + 1) * D] = dg.astype(dg_ref.dtype)
            dbg_ref[:, cols] += jnp.sum(dg, axis=0, keepdims=True)

    row = _col(tb, D, 0)
    act = _sds((T, D), BF16)
    return _call(body, name='gate_bwd', out_shape=(act, act, act, _sds((T, 5 * D), BF16), _sds((1, 3 * D), F32)),
                 grid=(T // tb,),
                 in_specs=[_col(tb, D, 2), _col(tb, D, 3), _col(tb, D, 4), _full((1, 3 * D)), row, row, row, row],
                 out_specs=(row, row, row, _col(tb, 5 * D, 0), _full((1, 3 * D))),
                 sem=('arbitrary',))(z, z, z, bg, ya, yb, yc, dm)


def _xattn_probs(qs, ks, scale):
    s = lax.dot_general(qs, ks, _DN['nt'], preferred_element_type=F32) * scale
    p = jnp.exp(s - jnp.max(s, axis=-1, keepdims=True))
    return p * (1.0 / jnp.sum(p, axis=-1, keepdims=True))


def _xattn_fwd(q, kv):
    T, DX = q.shape
    M = kv.shape[0]
    dh = DX // X_HEADS
    tb = _pick(T, (512, 256, 128))
    scale = dh ** -0.5

    def body(q_ref, kv_ref, o_ref):
        for h in range(X_HEADS):
            hs = slice(h * dh, (h + 1) * dh)
            p = _xattn_probs(q_ref[:, hs], kv_ref[:, hs], scale)
            o_ref[:, hs] = jnp.dot(p.astype(BF16), kv_ref[:, DX + h * dh:DX + (h + 1) * dh],
                                   preferred_element_type=F32).astype(o_ref.dtype)

    row = _col(tb, DX, 0)
    return _call(body, name='xattn_fwd', out_shape=_sds((T, DX), BF16), grid=(T // tb,),
                 in_specs=[row, _full((M, 2 * DX))], out_specs=row, sem=('parallel',))(q, kv)


def _xattn_bwd(q, kv, do):
    T, DX = q.shape
    M = kv.shape[0]
    dh = DX // X_HEADS
    tb = _pick(T, (512, 256, 128))
    scale = dh ** -0.5

    def body(q_ref, kv_ref, do_ref, dq_ref, dkv_ref):
        @pl.when(pl.program_id(0) == 0)
        def _():
            dkv_ref[...] = jnp.zeros_like(dkv_ref)

        for h in range(X_HEADS):
            hs = slice(h * dh, (h + 1) * dh)
            vs_cols = slice(DX + h * dh, DX + (h + 1) * dh)
            qs, ks, vs = q_ref[:, hs], kv_ref[:, hs], kv_ref[:, vs_cols]
            dos = do_ref[:, hs].astype(BF16)
            p = _xattn_probs(qs, ks, scale)
            dp = lax.dot_general(dos, vs, _DN['nt'], preferred_element_type=F32)
            ds = p * (dp - jnp.sum(p * dp, axis=-1, keepdims=True))
            dsb = ds.astype(BF16)
            dq_ref[:, hs] = (jnp.dot(dsb, ks, preferred_element_type=F32) * scale).astype(dq_ref.dtype)
            dkv_ref[:, hs] += lax.dot_general(dsb, qs, _DN['tn'], preferred_element_type=F32) * scale
            dkv_ref[:, vs_cols] += lax.dot_general(p.astype(BF16), dos, _DN['tn'], preferred_element_type=F32)

    row = _col(tb, DX, 0)
    return _call(body, name='xattn_bwd', out_shape=(_sds((T, DX), BF16), _sds((M, 2 * DX), F32)), grid=(T // tb,),
                 in_specs=[row, _full((M, 2 * DX)), row], out_specs=(row, _full((M, 2 * DX))),
                 sem=('arbitrary',))(q, kv, do)


def _swiglu_fwd(gu):
    T, F2 = gu.shape
    F = F2 // 2
    tb = _pick(T, (256, 128))

    def body(g_ref, u_ref, o_ref):
        g = g_ref[...].astype(F32)
        o_ref[...] = (g * _sigmoid(g) * u_ref[...].astype(F32)).astype(o_ref.dtype)

    return _call(body, name='swiglu_fwd', out_shape=_sds((T, F), BF16), grid=(T // tb,),
                 in_specs=[_col(tb, F, 0), _col(tb, F, 1)], out_specs=_col(tb, F, 0), sem=('parallel',))(gu, gu)


def _swiglu_bwd(gu, da):
    T, F2 = gu.shape
    F = F2 // 2
    tb = _pick(T, (256, 128))

    def body(g_ref, u_ref, da_ref, o_ref):
        g = g_ref[...].astype(F32)
        u = u_ref[...].astype(F32)
        dav = da_ref[...].astype(F32)
        sg = _sigmoid(g)
        o_ref[:, 0:F] = (dav * u * (sg * (1.0 + g * (1.0 - sg)))).astype(o_ref.dtype)
        o_ref[:, F:F2] = (dav * g * sg).astype(o_ref.dtype)

    return _call(body, name='swiglu_bwd', out_shape=_sds((T, F2), BF16), grid=(T // tb,),
                 in_specs=[_col(tb, F, 0), _col(tb, F, 1), _col(tb, F, 0)], out_specs=_col(tb, F2, 0),
                 sem=('parallel',))(gu, gu, da)


def _place():
    x, y, c = lax.axis_index('x'), lax.axis_index('y'), lax.axis_index('c')
    return x, y, c


_PEER_ORDER = (1, 4, 2, 5, 3, 6, 7)
_HBM = pl.BlockSpec(memory_space=pltpu.HBM)
_SEM = pl.BlockSpec(memory_space=pltpu.SEMAPHORE)


def _my_index():
    x, y, c = _place()
    return 4 * x + 2 * y + c


def _remote(src, dst, send_sems, recv_sems, k, dev):
    return pltpu.make_async_remote_copy(src_ref=src, dst_ref=dst, send_sem=send_sems.at[k], recv_sem=recv_sems.at[k],
                                        device_id=dev, device_id_type=MESH)


def _own_part(src, kind, args):
    return src if args is None else src.at[pl.ds(args[0], args[1])]


def _block(land, kind, args, owner):
    if kind == 'rows':
        _, n, stride, d0 = args
        return land.at[pl.ds(pl.multiple_of(stride * owner + d0, ROW_TILE), n)]
    return land.at[owner]


def _chunk(src, args, dest):
    stride, n = args
    return src.at[pl.ds(pl.multiple_of(stride * dest, ROW_TILE), n)]


def _scatter_copies(plan, srcs, lands, send_sems, recv_sems):
    x, y, c = _place()
    me = 4 * x + 2 * y + c
    copies = []
    for r in _PEER_ORDER:
        px, py, pc = x ^ (r >> 2), y ^ ((r >> 1) & 1), c ^ (r & 1)
        for si, li, kind, args in plan:
            src = srcs[si] if kind == 'slot' else _chunk(srcs[si], args, 4 * px + 2 * py + pc)
            copies.append(_remote(src, lands[li].at[me], send_sems, recv_sems, r - 1, (px, py, pc)))
    return copies


def _gather_copies(plan, srcs, lands, send_sems, recv_sems, passing):
    x, y, c = _place()
    me = 4 * x + 2 * y + c
    sibling = (x, y, 1 - c)
    chips = [(1 - x, y), (x, 1 - y), (1 - x, 1 - y)]
    copies = []
    if passing:
        for k, chip in enumerate(chips):
            for _, li, kind, args in plan:
                blk = _block(lands[li], kind, args, 4 * chip[0] + 2 * chip[1] + c)
                copies.append((k, _remote(blk, blk, send_sems, recv_sems, k, sibling)))
    else:
        for k, dev in enumerate([sibling] + [(*chip, c) for chip in chips]):
            for si, li, kind, args in plan:
                dst = _block(lands[li], kind, args, me)
                src = dst if srcs is None else _own_part(srcs[si], kind, args)
                copies.append((k, _remote(src, dst, send_sems, recv_sems, k, dev)))
    return copies


def _sem_call(name, bufs, sems_in, new_sems, after, action):
    nb, ni, nn = len(bufs), len(sems_in), len(new_sems)

    def body(*refs):
        action(refs[:nb], refs[nb:nb + ni], refs[-1 - nb - nn:-1 - nb])
        refs[-1][...] = jnp.zeros_like(refs[-1])

    bufs = [pltpu.with_memory_space_constraint(v, pltpu.HBM) for v in bufs]
    res = pl.pallas_call(
        body, name=name,
        out_shape=(*[pltpu.SemaphoreType.DMA((n,)) for n in new_sems], *[pltpu.HBM(v.shape, v.dtype) for v in bufs],
                   _sds((8, LANES), F32)),
        in_specs=[_HBM] * nb + [_SEM] * ni + [pl.BlockSpec(memory_space=pl.ANY)] * len(after),
        out_specs=(*[_SEM] * nn, *[_HBM] * nb, pl.BlockSpec(memory_space=pltpu.VMEM)),
        input_output_aliases={i: nn + i for i in range(nb)},
        compiler_params=pltpu.CompilerParams(has_side_effects=pltpu.SideEffectType.DATAFLOW_SIDE_EFFECTING),
        interpret=False)(*bufs, *sems_in, *after)
    return list(res[:nn]), list(res[nn:nn + nb]), res[-1]


def _place_own(srcs, plan, lands_like, name, deps):
    ns, nl, ne = len(srcs), len(lands_like), len(plan)

    def part_shape(entry):
        si, _, kind, args = entry
        rows = args[1] if (kind == 'chunk' or args is not None) else None
        return srcs[si].shape if rows is None else (rows,) + srcs[si].shape[1:]

    def body(*refs):
        src_r, land_r = refs[:ns], refs[ns:ns + nl]
        stage, sems_in, sems_out = refs[ns + nl:ns + nl + ne], refs[-2], refs[-1]
        me = _my_index()
        loads, stores = [], []
        for i, (si, li, kind, args) in enumerate(plan):
            if kind == 'chunk':
                src, dst = _chunk(src_r[si], args, me), land_r[li].at[me]
            else:
                src, dst = _own_part(src_r[si], kind, args), _block(land_r[li], kind, args, me)
            loads.append(pltpu.make_async_copy(src, stage[i], sems_in.at[i]))
            stores.append(pltpu.make_async_copy(stage[i], dst, sems_out.at[i]))
        for cp in loads:
            cp.start()
        for load, store in zip(loads, stores):
            load.wait()
            store.start()
        for cp in stores:
            cp.wait()

    anywhere = pl.BlockSpec(memory_space=pl.ANY)
    scratch = [pltpu.VMEM(part_shape(e), srcs[e[0]].dtype) for e in plan]
    scratch += [pltpu.SemaphoreType.DMA((ne,)), pltpu.SemaphoreType.DMA((ne,))]
    return list(_call(body, name=name, out_shape=tuple(lands_like), in_specs=[anywhere] * ns,
                      out_specs=tuple([anywhere] * nl), scratch=scratch, deps=deps)(*srcs))


def _scatter_start(srcs, lands, plan, after, name):
    ns = len(srcs)

    def action(bufs, _, new):
        for cp in _scatter_copies(plan, bufs[:ns], bufs[ns:], new[0], new[1]):
            cp.start()

    sems, bufs, token = _sem_call(name, [*srcs, *lands], [], [N_DEV - 1, N_DEV - 1], after, action)
    return sems, bufs[:ns], bufs[ns:], token


def _scatter_wait(sems, srcs, lands, plan, after, name):
    ns = len(srcs)

    def action(bufs, old, _):
        copies = _scatter_copies(plan, bufs[:ns], bufs[ns:], old[0], old[1])
        for cp in copies:
            cp.wait_send()
        for cp in copies:
            cp.wait_recv()

    return _sem_call(name, [*srcs, *lands], sems, [], after, action)[1][ns:]


def _gather_start(srcs, lands, plan, name):
    ns = len(srcs)

    def action(bufs, _, new):
        for _, cp in _gather_copies(plan, bufs[:ns], bufs[ns:], new[0], new[1], False):
            cp.start()

    sems, bufs, token = _sem_call(name, [*srcs, *lands], [], [4, 4], [], action)
    return sems, bufs[:ns], bufs[ns:], token


def _gather_pass(recv_sems, lands, plan, after, name):
    def action(bufs, old, new):
        for _, cp in _gather_copies(plan, None, bufs, old[0], old[0], False):
            cp.wait_recv()
        for _, cp in _gather_copies(plan, None, bufs, new[0], new[1], True):
            cp.start()

    return _sem_call(name, lands, [recv_sems], [3, 3], after, action)


def _gather_wait(send_sems, pass_sems, srcs, lands, plan, after, name):
    ns = len(srcs)

    def action(bufs, old, _):
        for _, cp in _gather_copies(plan, bufs[:ns], bufs[ns:], old[0], old[0], False):
            cp.wait_send()
        passed = _gather_copies(plan, None, bufs[ns:], old[1], old[2], True)
        for _, cp in passed:
            cp.wait_send()
        for _, cp in passed:
            cp.wait_recv()

    return _sem_call(name, [*srcs, *lands], [send_sems, *pass_sems], [], after, action)[1][ns:]


def _merge_edges(wt, edges, tiles_per_dev):
    D = wt.shape[1]

    def body(w_ref, e_ref, o_ref):
        s = pl.program_id(0)
        o_ref[...] = jnp.where(s == 0, e_ref[...], w_ref[...] + e_ref[...])

    tile = pl.BlockSpec((ROW_TILE, D), lambda s: (s * tiles_per_dev, 0))
    return _call(body, name='merge_edges', out_shape=_sds(wt.shape, wt.dtype), grid=(N_DEV,),
                 in_specs=[tile, pl.BlockSpec((None, ROW_TILE, D), lambda s: (s, 0, 0))], out_specs=tile,
                 sem=('arbitrary',), aliases={0: 0})(wt, edges)


def _adam_update(g, w, m, v):
    c1 = 1.0 - ADAM_B1 ** ADAM_STEP
    c2 = 1.0 - ADAM_B2 ** ADAM_STEP
    mn = ADAM_B1 * m + (1.0 - ADAM_B1) * g
    vn = ADAM_B2 * v + (1.0 - ADAM_B2) * (g * g)
    m_hat = mn / c1
    v_hat = vn / c2
    return -ADAM_LR * (m_hat / (jnp.sqrt(v_hat) + ADAM_EPS) + ADAM_WD * w), mn, vn


def _adamw(parts, w, m, v, name):
    n_parts = parts.shape[0]
    R, C = w.shape
    tb = R if (n_parts + 7) * R * C * 4 <= (12 << 20) else _pick(R, (128, 64, 32, 16, 8))

    def body(p_ref, w_ref, m_ref, v_ref, g_out, d_out, m_out, v_out):
        g = p_ref[0].astype(F32)
        for s in range(1, n_parts):
            g = g + p_ref[s].astype(F32)
        g_out[...] = g
        d_out[...], m_out[...], v_out[...] = _adam_update(g, w_ref[...], m_ref[...], v_ref[...])

    row = pl.BlockSpec((tb, C), lambda i: (i, 0))
    o = _sds((R, C), F32)
    return _call(body, name=name, out_shape=(o, o, o, o), grid=(R // tb,),
                 in_specs=[pl.BlockSpec((n_parts, tb, C), lambda i: (0, i, 0)), row, row, row],
                 out_specs=(row, row, row, row), sem=('parallel',))(parts, w, m, v)


def _adamw_layer(parts, w, m, v, layer, prev, shift, name):
    L, n, C = w.shape
    n_p = parts.shape[1]
    cb = C if N_DEV * n_p * C * parts.dtype.itemsize <= (6 << 20) else 2 * LANES
    n_prev = 0 if prev is None else 4

    def body(sh_ref, p_ref, w_ref, m_ref, v_ref, *rest):
        g_out, d_out, m_out, v_out = rest[n_prev:n_prev + 4]
        g = p_ref[0].astype(F32)
        for s in range(1, N_DEV):
            g = g + p_ref[s].astype(F32)
        if n_p != n:
            rolled = rest[-1]
            rolled[...] = pltpu.roll(g, n_p - sh_ref[0], 0)
            g = rolled[0:n, :]
        g_out[...] = g
        d_out[...], m_out[...], v_out[...] = _adam_update(g, w_ref[...], m_ref[...], v_ref[...])

    lay = pl.BlockSpec((None, n, cb), lambda j, sh: (layer, 0, j))
    stack = _sds((L, n, C), F32)
    grid_spec = pltpu.PrefetchScalarGridSpec(
        num_scalar_prefetch=1, grid=(C // cb,),
        in_specs=[pl.BlockSpec((N_DEV, n_p, cb), lambda j, sh: (0, 0, j)), lay, lay, lay]
        + [pl.BlockSpec(memory_space=pl.ANY)] * n_prev,
        out_specs=(lay, lay, lay, lay), scratch_shapes=[pltpu.VMEM((n_p, cb), F32)] if n_p != n else [])
    return pl.pallas_call(body, name=name, grid_spec=grid_spec, out_shape=(stack, stack, stack, stack),
                          input_output_aliases={5 + k: k for k in range(n_prev)},
                          compiler_params=pltpu.CompilerParams(dimension_semantics=('parallel',),
                                                               vmem_limit_bytes=VMEM_LIMIT),
                          interpret=False)(shift, parts, w, m, v, *(prev or ()))


def kernel(x, mem, mix_norm, w_in, b_gate, b_forget, conv_a, w_out_a, w_out_b, conv_c, conv_c_bias, ln_c_gain, ln_c_bias, w_out_c, w_o, xattn_norm, mem_norm, w_xq, w_xkv, w_xo, ffn_norm, w_gate_up, w_down, final_norm, loss_target, m_mix_norm, m_w_in, m_b_gate, m_b_forget, m_conv_a, m_w_out_a, m_w_out_b, m_conv_c, m_conv_c_bias, m_ln_c_gain, m_ln_c_bias, m_w_out_c, m_w_o, m_xattn_norm, m_mem_norm, m_w_xq, m_w_xkv, m_w_xo, m_ffn_norm, m_w_gate_up, m_w_down, m_final_norm, v_mix_norm, v_w_in, v_b_gate, v_b_forget, v_conv_a, v_w_out_a, v_w_out_b, v_conv_c, v_conv_c_bias, v_ln_c_gain, v_ln_c_bias, v_w_out_c, v_w_o, v_xattn_norm, v_mem_norm, v_w_xq, v_w_xkv, v_w_xo, v_ffn_norm, v_w_gate_up, v_w_down, v_final_norm):
    P = dict(zip(ARG_NAMES, (x, mem, mix_norm, w_in, b_gate, b_forget, conv_a, w_out_a, w_out_b, conv_c, conv_c_bias, ln_c_gain, ln_c_bias, w_out_c, w_o, xattn_norm, mem_norm, w_xq, w_xkv, w_xo, ffn_norm, w_gate_up, w_down, final_norm, loss_target, m_mix_norm, m_w_in, m_b_gate, m_b_forget, m_conv_a, m_w_out_a, m_w_out_b, m_conv_c, m_conv_c_bias, m_ln_c_gain, m_ln_c_bias, m_w_out_c, m_w_o, m_xattn_norm, m_mem_norm, m_w_xq, m_w_xkv, m_w_xo, m_ffn_norm, m_w_gate_up, m_w_down, m_final_norm, v_mix_norm, v_w_in, v_b_gate, v_b_forget, v_conv_a, v_w_out_a, v_w_out_b, v_conv_c, v_conv_c_bias, v_ln_c_gain, v_ln_c_bias, v_w_out_c, v_w_o, v_xattn_norm, v_mem_norm, v_w_xq, v_w_xkv, v_w_xo, v_ffn_norm, v_w_gate_up, v_w_down, v_final_norm)))
    return _step(P)


_T_VIEW = ('w_in', 'w_gate_up', 'w_xo')
_BIG = [w for w in SHARDED if w not in ('conv_a', 'conv_c')]
_OTHER = [w for w in _BIG if w != 'w_in']
_CONVS = (('conv_a', CONV_A_W, 8), ('conv_c', CONV_C_W, 32))
_EARLY = ['w_down', 'w_gate_up', 'w_xo', 'w_xq', 'w_xkv', 'w_o']
_LATE = ['w_out_c', 'w_out_b', 'w_out_a', 'w_in']


def _view(name, a):
    return jnp.transpose(a, (0, 2, 1)) if name in _T_VIEW else a


def _step(P):
    L, D = P['mix_norm'].shape
    T = P['x'].shape[1]
    n_in = P['w_in'].shape[2]
    stride = n_in // ROW_TILE * ROW_TILE
    rem = n_in - stride
    win = stride + ROW_TILE
    assert rem * N_DEV == ROW_TILE
    r_in = n_in * N_DEV
    f_off = 6 * D
    hi_off = f_off + FOX_HEADS
    x_i, y_i, c_i = _place()
    me = 4 * x_i + 2 * y_i + c_i
    V = {}
    for w in _BIG:
        for k in (w, 'm_' + w, 'v_' + w):
            V[k] = _view(w, P[k])
    n_own = {w: V[w].shape[1] for w in _OTHER}
    shift = jnp.reshape(rem * me, (1,)).astype(jnp.int32)
    no_shift = jnp.zeros((1,), jnp.int32)
    row = lambda a, l: a[l][None, :]
    dus = lax.dynamic_update_slice

    conv_taps = {w: (taps, padded) for w, taps, padded in _CONVS}

    def gather_start(l, names, tag, deps):
        srcs, lands, plan = [], [], []
        for w in names:
            si, li = len(srcs), len(lands)
            if w == 'w_in':
                first = rem * me + sum(t[0, 0] for t in deps).astype(jnp.int32) if deps else rem * me
                srcs.append(dus(jnp.zeros((win, D), BF16), V[w][l].astype(BF16), (first, 0)))
                lands += [_sds((r_in, D), BF16), _sds((N_DEV, ROW_TILE, D), BF16)]
                plan += [(si, li, 'rows', (ROW_TILE, stride, stride, ROW_TILE)), (si, li + 1, 'slot', (0, ROW_TILE))]
            elif w in conv_taps:
                srcs.append(P[w][l])
                lands.append(_sds((N_DEV,) + srcs[si].shape, F32))
                plan.append((si, li, 'slot', None))
            else:
                srcs.append(V[w][l].astype(BF16))
                lands.append(_sds((N_DEV * n_own[w], srcs[si].shape[1]), BF16))
                plan.append((si, li, 'rows', (0, n_own[w], n_own[w], 0)))
        lands = _place_own(srcs, plan, lands, 'gather_own', deps)
        sems, srcs, lands, token = _gather_start(srcs, lands, plan, f'gather_start_{tag}{l}')
        return dict(names=names, name=f'{tag}{l}', plan=plan, send=sems[0], recv=sems[1], srcs=srcs, lands=lands,
                    token=token)

    def gather_pass(g, after):
        g['pass'], g['lands'], g['token'] = _gather_pass(g['recv'], g['lands'], g['plan'], after,
                                                        'gather_pass_' + g['name'])

    def gather_wait(g, after):
        lands = iter(_gather_wait(g['send'], g['pass'], g['srcs'], g['lands'], g['plan'], after,
                                  'gather_wait_' + g['name']))
        W = {}
        for w in g['names']:
            if w == 'w_in':
                W[w] = _merge_edges(next(lands), next(lands), stride // ROW_TILE)
            elif w in conv_taps:
                taps, padded = conv_taps[w]
                full = jnp.transpose(next(lands), (1, 0, 2)).reshape(taps, D)
                W[w] = jnp.pad(full, ((0, padded - taps), (0, 0)))
            else:
                W[w] = next(lands)
        return W

    first_part = ['w_in'] + list(conv_taps)
    gathers, prev = [], []
    for l in range(L):
        part_a = gather_start(l, first_part, 'a', prev)
        part_b = gather_start(l, _OTHER, 'b', [part_a['token']])
        gathers.append([part_a, part_b])
        prev = [part_b['token']]
    gather_pass(gathers[0][0], prev)

    mem_x = P['mem'][0]
    mem_n = _rms_fwd(mem_x, P['mem_norm'][None, :], 'rms_mem')
    xs = P['x'][0]
    saved = []
    for l in range(L):
        W = gather_wait(gathers[l][0], [xs])
        s = {'x0': xs, 'W': W}
        wt = W['w_in']
        s['h1'] = _rms_fwd(xs, row(P['mix_norm'], l), 'rms_mix')
        s['z1'] = z1 = _mm(s['h1'], wt, mode='nt', b_rows=(0, f_off), name='mm_in_lo')
        s['z2'] = z2 = _mm(s['h1'], wt, mode='nt', b_rows=(hi_off, r_in - hi_off), name='mm_in_hi')
        s['zf'] = _mm(s['h1'], wt, mode='nt', b_rows=(f_off, FOX_HEADS), out_dtype=F32, name='mm_in_f')
        s['ya_pre'] = _mixa_fwd(z1, W['conv_a'])
        s['c'] = _fox_c(s['zf'], row(P['b_forget'], l))
        s['o'], s['lse'] = _fox_fwd(z1, s['c'])
        for g in gathers[l][1:]:
            gather_pass(g, [s['o']])
        s['u3'], s['u1'] = _mixc_fwd(z2, W['conv_c'], row(P['conv_c_bias'], l), row(P['ln_c_gain'], l),
                                     row(P['ln_c_bias'], l))
        for g in gathers[l][1:]:
            W.update(gather_wait(g, [s['u3']]))
        s['ya'] = _mm(s['ya_pre'], W['w_out_a'], mode='nn', name='mm_out_a')
        s['yb'] = _mm(s['o'], W['w_out_b'], mode='nn', name='mm_out_b')
        s['yc'] = _mm(s['u3'], W['w_out_c'], mode='nn', name='mm_out_c')
        s['merged'] = _gate_fwd(z2, row(P['b_gate'], l), s['ya'], s['yb'], s['yc'])
        xs = _mm(s['merged'], W['w_o'], mode='nn', add=xs, out_dtype=F32, name='mm_o')
        s['x1'] = xs
        deps = []
        if l + 1 < L:
            gather_pass(gathers[l + 1][0], [xs])
            deps = [gathers[l + 1][0]['token']]
        s['h2'] = _rms_fwd(xs, row(P['xattn_norm'], l), 'rms_xattn', deps=deps)
        s['qx'] = _mm(s['h2'], W['w_xq'], mode='nn', name='mm_xq')
        s['kv'] = _mm(mem_n, W['w_xkv'], mode='nn', name='mm_xkv')
        s['ox'] = _xattn_fwd(s['qx'], s['kv'])
        xs = _mm(s['ox'], W['w_xo'], mode='nt', add=xs, out_dtype=F32, name='mm_xo')
        s['x2'] = xs
        s['h3'] = _rms_fwd(xs, row(P['ffn_norm'], l), 'rms_ffn')
        s['gu'] = _mm(s['h3'], W['w_gate_up'], mode='nt', name='mm_gate_up')
        s['act'] = _swiglu_fwd(s['gu'])
        xs = _mm(s['act'], W['w_down'], mode='nn', add=xs, out_dtype=F32, name='mm_down')
        saved.append(s)

    dx, d_final_norm, loss_part = _final(xs, P['final_norm'][None, :], P['loss_target'][0])

    GS = {w: [None] * L for w in SMALL}
    G_conv = {'conv_a': [None] * L, 'conv_c': [None] * L}
    d_mem_n = None
    results = {w: None for w in _BIG}

    def exchange_start(G, names, l, tag, after):
        srcs = [G[w] for w in names]
        plan, lands = [], []
        for i, w in enumerate(names):
            step, n = (stride, win) if w == 'w_in' else (n_own[w], n_own[w])
            lands.append(_sds((N_DEV, n, srcs[i].shape[1]), BF16))
            plan.append((i, i, 'chunk', (step, n)))
        lands = _place_own(srcs, plan, lands, 'exchange_own', [])
        sems, srcs, lands, token = _scatter_start(srcs, lands, plan, after, f'exchange_start_{tag}{l}')
        return (names, l, tag, plan, sems, srcs, lands, token)

    def exchange_finish(e, after):
        names, l, tag, plan, sems, srcs, lands, _ = e
        lands = _scatter_wait(sems, srcs, lands, plan, after, f'exchange_wait_{tag}{l}')
        for i, w in enumerate(names):
            results[w] = _adamw_layer(lands[i], V[w], V['m_' + w], V['v_' + w], l, results[w],
                                      shift if w == 'w_in' else no_shift, 'adamw_' + w)

    pending = []
    for l in reversed(range(L)):
        s = saved[l]
        W, z1, z2 = s['W'], s['z1'], s['z2']
        wt = W['w_in']
        G = {}
        d_act = _mm(dx, W['w_down'], mode='nt', name='mmb_down', deps=[e[-1] for e in pending])
        G['w_down'] = _mm(s['act'], dx, mode='tn', name='mmg_w_down')
        dgu = _swiglu_bwd(s['gu'], d_act)
        dh3 = _mm(dgu, W['w_gate_up'], mode='nn', out_dtype=F32, name='mmb_gate_up')
        G['w_gate_up'] = _mm(dgu, s['h3'], mode='tn', name='mmg_w_gate_up')
        dx, GS['ffn_norm'][l] = _rms_bwd(s['x2'], row(P['ffn_norm'], l), dh3, dx, 'rms_ffn_bwd')
        d_ox = _mm(dx, W['w_xo'], mode='nn', out_dtype=F32, name='mmb_xo')
        G['w_xo'] = _mm(dx, s['ox'], mode='tn', name='mmg_w_xo')
        dqx, dkv = _xattn_bwd(s['qx'], s['kv'], d_ox)
        dh2 = _mm(dqx, W['w_xq'], mode='nt', out_dtype=F32, name='mmb_xq')
        G['w_xq'] = _mm(s['h2'], dqx, mode='tn', name='mmg_w_xq')
        G['w_xkv'] = _mm(mem_n, dkv, mode='tn', name='mmg_w_xkv')
        d_mem_n = _mm(dkv, W['w_xkv'], mode='nt', add=d_mem_n, out_dtype=F32, name='mmb_xkv')
        dx, GS['xattn_norm'][l] = _rms_bwd(s['x1'], row(P['xattn_norm'], l), dh2, dx, 'rms_xattn_bwd')
        dm = _mm(dx, W['w_o'], mode='nt', out_dtype=F32, name='mmb_o')
        G['w_o'] = _mm(s['merged'], dx, mode='tn', name='mmg_w_o')
        early = exchange_start(G, _EARLY, l, 'a', [])
        dya, dyb, dyc, dz2, GS['b_gate'][l] = _gate_bwd(z2, row(P['b_gate'], l), s['ya'], s['yb'], s['yc'], dm)
        du3 = _mm(dyc, W['w_out_c'], mode='nt', out_dtype=F32, name='mmb_out_c', deps=[early[-1]])
        G['w_out_c'] = _mm(s['u3'], dyc, mode='tn', name='mmg_w_out_c')
        dz2, G_conv['conv_c'][l], GS['conv_c_bias'][l], GS['ln_c_gain'][l], GS['ln_c_bias'][l] = _mixc_bwd(
            z2, W['conv_c'], row(P['ln_c_gain'], l), row(P['ln_c_bias'], l), s['u1'], du3, dz2)
        dya_pre = _mm(dya, W['w_out_a'], mode='nt', out_dtype=F32, name='mmb_out_a')
        G['w_out_a'] = _mm(s['ya_pre'], dya, mode='tn', name='mmg_w_out_a')
        dz1, G_conv['conv_a'][l] = _mixa_bwd(z1, W['conv_a'], dya_pre)
        do = _mm(dyb, W['w_out_b'], mode='nt', name='mmb_out_b')
        G['w_out_b'] = _mm(s['o'], dyb, mode='tn', name='mmg_w_out_b')
        dz1, dc_pairs = _fox_bwd(z1, s['c'], s['o'], s['lse'], do, dz1)
        hp = FOX_HEADS // dc_pairs.shape[0]
        dc = jnp.transpose(dc_pairs[:, :, :hp], (1, 0, 2)).reshape(T, FOX_HEADS)
        dzf, GS['b_forget'][l] = _fox_c_bwd(dc, s['zf'], row(P['b_forget'], l))
        segs = [(dzf, f_off), (dz1, 0), (dz2, hi_off)]
        g_in = r_in
        for dz, off in segs:
            g_in = _mm(dz, s['h1'], mode='tn', out=g_in, o_off=off, name='mmg_w_in')
        G['w_in'] = g_in
        late = exchange_start(G, _LATE, l, 'b', [])
        dh1 = None
        for n, (dz, off) in enumerate(segs):
            dh1 = _mm(dz, wt, mode='nn', b_rows=(off, dz.shape[1]), add=dh1, out_dtype=F32, name='mmb_in',
                      deps=[late[-1]] if n == 0 else ())
        dx, GS['mix_norm'][l] = _rms_bwd(s['x0'], row(P['mix_norm'], l), dh1, dx, 'rms_mix_bwd')
        for e in pending:
            exchange_finish(e, [dx])
        pending = [early, late]

    _, d_mem_norm = _rms_bwd(mem_x, P['mem_norm'][None, :], d_mem_n, None, 'rms_mem_bwd')

    small_parts = {w: jnp.concatenate(GS[w], axis=0) for w in SMALL if w not in ('mem_norm', 'final_norm')}
    small_parts['mem_norm'] = d_mem_norm
    small_parts['final_norm'] = d_final_norm
    conv_parts = [jnp.stack(G_conv[w])[:, :taps] for w, taps, _ in _CONVS]
    sizes = [P[w].size for w in SMALL]
    conv_sizes = [c.size for c in conv_parts]
    n_small = sum(sizes) + sum(conv_sizes) + LANES
    n_rows = -(-n_small // (8 * LANES)) * 8

    def pack(parts):
        flat = jnp.concatenate([p.reshape(-1) for p in parts])
        return jnp.pad(flat, (0, n_rows * LANES - flat.size)).reshape(n_rows, LANES)

    g_small = pack([small_parts[w] for w in SMALL] + conv_parts + [loss_part])
    small_plan = [(0, 0, 'slot', None)]
    small_land = _place_own([g_small], small_plan, [_sds((N_DEV,) + g_small.shape, F32)], 'small_own', [])
    small_sems, small_src, small_land, _ = _scatter_start([g_small], small_land, small_plan, [], 'small_start')
    early0, late0 = pending
    exchange_finish(early0, [small_land[0]])
    exchange_finish(late0, [results[w][0] for w in _EARLY])
    (all_small,) = _scatter_wait(small_sems, small_src, small_land, small_plan, [results['w_in'][0]], 'small_wait')
    sm = _adamw(all_small, pack([P[w] for w in SMALL]), pack([P['m_' + w] for w in SMALL]),
                pack([P['v_' + w] for w in SMALL]), 'adamw_small')
    g_all = sm[0].reshape(-1)
    loss = g_all[sum(sizes) + sum(conv_sizes)]
    final = {}
    off = 0
    for w, n in zip(SMALL, sizes):
        final[w] = [o.reshape(-1)[off:off + n].reshape(P[w].shape) for o in sm]
        off += n
    for (w, taps, _), c, n in zip(_CONVS, conv_parts, conv_sizes):
        n_col = P[w].shape[2]
        g_own = lax.dynamic_slice(g_all[off:off + n].reshape(c.shape), (0, 0, n_col * me), (L, taps, n_col))
        flat = lambda a: a.reshape(L * taps, n_col)
        outs = _adamw(flat(g_own)[None], flat(P[w]), flat(P['m_' + w]), flat(P['v_' + w]), 'adamw_' + w)
        final[w] = [o.reshape(P[w].shape) for o in outs]
        off += n

    for w in _BIG:
        final[w] = [_view(w, o) for o in results[w]]

    out = [loss, dx[None]]
    for k in range(4):
        out += [final[w][k] for w in WEIGHTS]
    return tuple(out)
```

```python
import functools

import jax
import jax.numpy as jnp
from jax import lax
from jax.experimental import pallas as pl
from jax.experimental.pallas import tpu as pltpu

F32 = jnp.float32
BF16 = jnp.bfloat16
MESH = pl.DeviceIdType.MESH

N_DEV = 8
EPS = 1e-6
FOX_HEADS = 16
X_HEADS = 4
LANES = 128
ROW_TILE = 16
HALO_A = 16
HALO_C = 32
SUBLANES = 8
CONV_ROWS = 128
ROW_CHUNK = 32
CONV_A_W = 3
CONV_C_W = 31
VMEM_LIMIT = 56 << 20

ADAM_LR = 0.001
ADAM_B1 = 0.9
ADAM_B2 = 0.999
ADAM_EPS = 1e-08
ADAM_WD = 0.01
ADAM_STEP = 10

WEIGHTS = ['mix_norm', 'w_in', 'b_gate', 'b_forget', 'conv_a', 'w_out_a', 'w_out_b', 'conv_c', 'conv_c_bias',
           'ln_c_gain', 'ln_c_bias', 'w_out_c', 'w_o', 'xattn_norm', 'mem_norm', 'w_xq', 'w_xkv', 'w_xo',
           'ffn_norm', 'w_gate_up', 'w_down', 'final_norm']
SHARDED = ['w_in', 'conv_a', 'w_out_a', 'w_out_b', 'conv_c', 'w_out_c', 'w_o', 'w_xq', 'w_xkv', 'w_xo',
           'w_gate_up', 'w_down']
ROW_SHARDED = ['w_out_a', 'w_out_b', 'w_out_c', 'w_o', 'w_xq', 'w_xkv', 'w_down']
SMALL = [w for w in WEIGHTS if w not in SHARDED]
ARG_NAMES = (['x', 'mem'] + WEIGHTS + ['loss_target'] + ['m_' + w for w in WEIGHTS] + ['v_' + w for w in WEIGHTS])


def _pick(n, cands=(1024, 1408, 512, 256, 128)):
    for c in cands:
        if n % c == 0:
            return c
    return n


def _call(body, *, name, out_shape, grid=(), in_specs=None, out_specs=None, scratch=(), sem=None, aliases=None,
          deps=()):
    params = dict(vmem_limit_bytes=VMEM_LIMIT)
    if sem is not None:
        params['dimension_semantics'] = sem
    n_in, n_dep = len(in_specs), len(deps)

    def body_without_deps(*refs):
        body(*refs[:n_in], *refs[n_in + n_dep:])

    call = pl.pallas_call(body_without_deps, name=name, out_shape=out_shape, grid=grid, scratch_shapes=list(scratch),
                          in_specs=list(in_specs) + [pl.BlockSpec(memory_space=pl.ANY)] * n_dep, out_specs=out_specs,
                          input_output_aliases=aliases or {}, compiler_params=pltpu.CompilerParams(**params),
                          interpret=False)
    return lambda *args: call(*args, *deps)


def _sds(shape, dtype):
    return jax.ShapeDtypeStruct(tuple(shape), dtype)


def _sigmoid(x):
    return 1.0 / (1.0 + jnp.exp(-x))


_DN = {'nn': (((1,), (0,)), ((), ())), 'nt': (((1,), (1,)), ((), ())), 'tn': (((0,), (0,)), ((), ()))}


def _mm(a, b, *, mode, name, out_dtype=BF16, add=None, b_rows=None, out=None, o_off=0, deps=()):
    if mode == 'tn':
        K, M = a.shape
    else:
        M, K = a.shape
    b_off, b_n = (0, b.shape[0]) if b_rows is None else b_rows
    N = b_n if mode == 'nt' else b.shape[1]
    assert (b.shape[1] if mode == 'nt' else b_n) == K
    tm, tn = _pick(M), _pick(N)
    tk = K if K <= 2048 else _pick(K)
    ni, nj, nk = M // tm, N // tn, K // tk
    a_bytes, b_bytes = M * K * a.dtype.itemsize, K * N * b.dtype.itemsize
    n_outer = (b_bytes + nj * a_bytes) < (a_bytes + ni * b_bytes)
    if n_outer:
        grid = (nj, ni, nk)
        ij = lambda g0, g1: (g1, g0)
    else:
        grid = (ni, nj, nk)
        ij = lambda g0, g1: (g0, g1)

    def a_map(g0, g1, k):
        i, _ = ij(g0, g1)
        return (k, i) if mode == 'tn' else (i, k)

    def elems(blk):
        return tuple(pl.Element(s) for s in blk)

    def at(rows, cols):
        return pl.multiple_of(rows, ROW_TILE), pl.multiple_of(cols, LANES)

    a_blk = (tk, tm) if mode == 'tn' else (tm, tk)
    b_blk = (tn, tk) if mode == 'nt' else (tk, tn)
    if b_rows is None:
        def b_map(g0, g1, k):
            _, j = ij(g0, g1)
            return (j, k) if mode == 'nt' else (k, j)
        b_spec = pl.BlockSpec(b_blk, b_map)
    else:
        def b_map(g0, g1, k):
            _, j = ij(g0, g1)
            return at(b_off + j * tn, k * tk) if mode == 'nt' else at(b_off + k * tk, j * tn)
        b_spec = pl.BlockSpec(elems(b_blk), b_map)
    in_specs = [pl.BlockSpec(a_blk, a_map), b_spec]
    args = [a, b]
    has_add = add is not None
    if has_add:
        in_specs.append(pl.BlockSpec((tm, tn), lambda g0, g1, k: ij(g0, g1)))
        args.append(add)
    aliases = {}
    if out is None:
        out_shape = _sds((M, N), out_dtype)
        out_spec = pl.BlockSpec((tm, tn), lambda g0, g1, k: ij(g0, g1))
    else:
        if isinstance(out, int):
            out_shape = _sds((out, N), out_dtype)
        else:
            out_shape = _sds(out.shape, out.dtype)
            in_specs.append(pl.BlockSpec(memory_space=pl.ANY))
            aliases = {len(args): 0}
            args.append(out)

        def o_map(g0, g1, k):
            i, j = ij(g0, g1)
            return at(o_off + i * tm, j * tn)
        out_spec = pl.BlockSpec(elems((tm, tn)), o_map)
    dn = _DN[mode]
    n_in = len(args)

    def body(*refs):
        a_ref, b_ref = refs[0], refs[1]
        add_ref = refs[2] if has_add else None
        o_ref = refs[n_in]
        part = lax.dot_general(a_ref[...].astype(BF16), b_ref[...].astype(BF16), dn, preferred_element_type=F32)

        def finish(r):
            if has_add:
                r = r + add_ref[...]
            o_ref[...] = r.astype(o_ref.dtype)

        if nk == 1:
            finish(part)
        else:
            acc = refs[n_in + 1]
            k = pl.program_id(2)

            @pl.when(k == 0)
            def _():
                acc[...] = part

            @pl.when(k > 0)
            def _():
                acc[...] += part

            @pl.when(k == nk - 1)
            def _():
                finish(acc[...])

    scratch = [pltpu.VMEM((tm, tn), F32)] if nk > 1 else []
    return _call(body, name=name, out_shape=out_shape, grid=grid, in_specs=in_specs, out_specs=out_spec,
                 scratch=scratch, sem=('parallel', 'parallel', 'arbitrary'), aliases=aliases, deps=deps)(*args)


def _rms_fwd(x, g, name, deps=()):
    R, D = x.shape
    tb = _pick(R, (512, 256, 128))

    def body(x_ref, g_ref, o_ref):
        xv = x_ref[...]
        r = lax.rsqrt(jnp.mean(xv * xv, axis=-1, keepdims=True) + EPS)
        o_ref[...] = (xv * r * g_ref[...]).astype(o_ref.dtype)

    row = pl.BlockSpec((tb, D), lambda i: (i, 0))
    vec = pl.BlockSpec((1, D), lambda i: (0, 0))
    return _call(body, name=name, out_shape=_sds((R, D), BF16), grid=(R // tb,), in_specs=[row, vec],
                 out_specs=row, sem=('parallel',), deps=deps)(x, g)


def _rms_bwd(x, g, dh, dres, name, deps=()):
    R, D = x.shape
    tb = _pick(R, (512, 256, 128))
    has_res = dres is not None

    def body(*refs):
        x_ref, g_ref, dh_ref = refs[:3]
        dx_ref, dg_ref = refs[-2:]
        xv = x_ref[...]
        dhv = dh_ref[...].astype(F32)
        r = lax.rsqrt(jnp.mean(xv * xv, axis=-1, keepdims=True) + EPS)
        u = dhv * g_ref[...]
        m = jnp.mean(u * xv, axis=-1, keepdims=True)
        dx = r * u - xv * (r * r * r * m)
        if has_res:
            dx = dx + refs[3][...]
        dx_ref[...] = dx

        @pl.when(pl.program_id(0) == 0)
        def _():
            dg_ref[...] = jnp.zeros_like(dg_ref)

        dg_ref[...] += jnp.sum(dhv * xv * r, axis=0, keepdims=True)

    row = pl.BlockSpec((tb, D), lambda i: (i, 0))
    vec = pl.BlockSpec((1, D), lambda i: (0, 0))
    args = [x, g, dh] + ([dres] if has_res else [])
    return _call(body, name=name, out_shape=(_sds((R, D), F32), _sds((1, D), F32)), grid=(R // tb,),
                 in_specs=[row, vec, row] + ([row] if has_res else []), out_specs=(row, vec),
                 sem=('arbitrary',), deps=deps)(*args)


def _final(x, g, tgt):
    R, D = x.shape
    tb = _pick(R, (512, 256, 128))

    def body(x_ref, g_ref, t_ref, dx_ref, dg_ref, loss_ref):
        xv = x_ref[...]
        gv = g_ref[...]
        r = lax.rsqrt(jnp.mean(xv * xv, axis=-1, keepdims=True) + EPS)
        e = xv * r * gv - t_ref[...]
        row_loss = jnp.mean(e * e, axis=-1, keepdims=True)
        blk_loss = 0.5 * jnp.sum(row_loss, axis=0, keepdims=True)
        dy = e * (1.0 / D)
        u = dy * gv
        m = jnp.mean(u * xv, axis=-1, keepdims=True)
        dx_ref[...] = r * u - xv * (r * r * r * m)

        @pl.when(pl.program_id(0) == 0)
        def _():
            dg_ref[...] = jnp.zeros_like(dg_ref)
            loss_ref[...] = jnp.zeros_like(loss_ref)

        dg_ref[...] += jnp.sum(dy * xv * r, axis=0, keepdims=True)
        loss_ref[...] += jnp.broadcast_to(blk_loss, loss_ref.shape)

    row = pl.BlockSpec((tb, D), lambda i: (i, 0))
    vec = pl.BlockSpec((1, D), lambda i: (0, 0))
    one = pl.BlockSpec((1, LANES), lambda i: (0, 0))
    return _call(body, name='final_loss', out_shape=(_sds((R, D), F32), _sds((1, D), F32), _sds((1, LANES), F32)),
                 grid=(R // tb,), in_specs=[row, vec, row], out_specs=(row, vec, one), sem=('arbitrary',))(x, g, tgt)


def _col(tb, w, cb):
    return pl.BlockSpec((tb, w), lambda i: (i, cb))


def _prev(tb, h, w, cb):
    return pl.BlockSpec((h, w), lambda i: (jnp.maximum(i * (tb // h) - 1, 0), cb))


def _next(tb, h, w, cb, rows):
    return pl.BlockSpec((h, w), lambda i: (jnp.minimum((i + 1) * (tb // h), rows // h - 1), cb))


def _full(shape):
    return pl.BlockSpec(shape, lambda i: (0,) * len(shape))


def _mixa_fwd(z, w):
    T = z.shape[0]
    D = w.shape[1]
    tb = _pick(T, (256, 128))
    H = HALO_A

    def body(ab, ac, au, acp, aup, w_ref, o_ref, ext):
        first = pl.program_id(0) == 0
        ext[0:H, :] = jnp.where(first, 0.0, acp[...].astype(F32) * aup[...].astype(F32))
        ext[H:H + tb, :] = ac[...].astype(F32) * au[...].astype(F32)
        cp = jnp.zeros((tb, D), F32)
        for k in range(CONV_A_W):
            off = H - (CONV_A_W - 1) + k
            cp = cp + ext[off:off + tb, :] * w_ref[k:k + 1, :]
        o_ref[...] = (ab[...].astype(F32) * cp).astype(o_ref.dtype)

    return _call(body, name='mixa_fwd', out_shape=_sds((T, D), BF16), grid=(T // tb,),
                 in_specs=[_col(tb, D, 0), _col(tb, D, 1), _col(tb, D, 2), _prev(tb, H, D, 1), _prev(tb, H, D, 2),
                           _full((8, D))],
                 out_specs=_col(tb, D, 0), scratch=[pltpu.VMEM((H + tb, D), F32)], sem=('parallel',))(z, z, z, z, z, w)


def _mixa_bwd(z, w, dy):
    T = z.shape[0]
    D = w.shape[1]
    tb = _pick(T, (256, 128))
    H = HALO_A
    nb = T // tb

    def body(ab, ac, au, acp, aup, abn, dy_ref, dyn, w_ref, dz_ref, dw_ref, pext, dext):
        i = pl.program_id(0)
        a_b, a_c, a_u = ab[...].astype(F32), ac[...].astype(F32), au[...].astype(F32)
        pext[0:H, :] = jnp.where(i == 0, 0.0, acp[...].astype(F32) * aup[...].astype(F32))
        pext[H:H + tb, :] = a_c * a_u
        dyv = dy_ref[...].astype(F32)
        dcp = dyv * a_b
        dext[0:tb, :] = dcp
        dext[tb:tb + H, :] = jnp.where(i == nb - 1, 0.0, dyn[...].astype(F32) * abn[...].astype(F32))

        @pl.when(i == 0)
        def _():
            dw_ref[...] = jnp.zeros_like(dw_ref)

        cp = jnp.zeros((tb, D), F32)
        dp = jnp.zeros((tb, D), F32)
        for k in range(CONV_A_W):
            off = H - (CONV_A_W - 1) + k
            wk = w_ref[k:k + 1, :]
            pk = pext[off:off + tb, :]
            cp = cp + pk * wk
            dp = dp + dext[CONV_A_W - 1 - k:CONV_A_W - 1 - k + tb, :] * wk
            dw_ref[k:k + 1, :] += jnp.sum(dcp * pk, axis=0, keepdims=True)
        dz_ref[:, 0:D] = (dyv * cp).astype(dz_ref.dtype)
        dz_ref[:, D:2 * D] = (dp * a_u).astype(dz_ref.dtype)
        dz_ref[:, 2 * D:3 * D] = (dp * a_c).astype(dz_ref.dtype)

    return _call(body, name='mixa_bwd', out_shape=(_sds((T, 6 * D), BF16), _sds((8, D), F32)), grid=(nb,),
                 in_specs=[_col(tb, D, 0), _col(tb, D, 1), _col(tb, D, 2), _prev(tb, H, D, 1), _prev(tb, H, D, 2),
                           _next(tb, H, D, 0, T), _col(tb, D, 0), _next(tb, H, D, 0, T), _full((8, D))],
                 out_specs=(_col(tb, 3 * D, 0), _full((8, D))),
                 scratch=[pltpu.VMEM((H + tb, D), F32), pltpu.VMEM((tb + H, D), F32)],
                 sem=('arbitrary',))(z, z, z, z, z, z, dy, dy, w)


def _split3(v):
    hi = v.astype(BF16)
    r1 = v - hi.astype(F32)
    mid = r1.astype(BF16)
    lo = (r1 - mid.astype(F32)).astype(BF16)
    return hi, mid, lo


def _tri_dot(tri, v):
    hi, mid, lo = _split3(v)
    d = lambda p: jnp.dot(tri, p, preferred_element_type=F32)
    return d(hi) + d(mid) + d(lo)


def _fox_c(zf, bf):
    T, H = zf.shape
    tb = LANES

    def body(zf_ref, b_ref, c_ref, carry):
        @pl.when(pl.program_id(0) == 0)
        def _():
            carry[...] = jnp.zeros_like(carry)

        xv = zf_ref[...] + b_ref[...]
        lf = jnp.minimum(xv, 0.0) - jnp.log(1.0 + jnp.exp(-jnp.abs(xv)))
        r = lax.broadcasted_iota(jnp.int32, (tb, tb), 0)
        c = lax.broadcasted_iota(jnp.int32, (tb, tb), 1)
        tri = (r >= c).astype(BF16)
        cs = _tri_dot(tri, lf) + carry[...]
        c_ref[...] = cs
        carry[...] = cs[tb - 1:tb, :]

    blk = pl.BlockSpec((tb, H), lambda i: (i, 0))
    return _call(body, name='fox_cumsum', out_shape=_sds((T, H), F32), grid=(T // tb,),
                 in_specs=[blk, _full((1, H))], out_specs=blk, scratch=[pltpu.VMEM((1, H), F32)],
                 sem=('arbitrary',))(zf, bf)


def _fox_c_bwd(dc, zf, bf):
    T, H = zf.shape
    tb = LANES
    nb = T // tb

    def body(dc_ref, zf_ref, b_ref, dz_ref, db_ref, carry):
        @pl.when(pl.program_id(0) == 0)
        def _():
            carry[...] = jnp.zeros_like(carry)
            db_ref[...] = jnp.zeros_like(db_ref)

        r = lax.broadcasted_iota(jnp.int32, (tb, tb), 0)
        c = lax.broadcasted_iota(jnp.int32, (tb, tb), 1)
        tri = (c >= r).astype(BF16)
        dlf = _tri_dot(tri, dc_ref[...]) + carry[...]
        carry[...] = dlf[0:1, :]
        xv = zf_ref[...] + b_ref[...]
        dz = dlf * _sigmoid(-xv)
        dz_ref[...] = dz
        db_ref[...] += jnp.sum(dz, axis=0, keepdims=True)

    blk = pl.BlockSpec((tb, H), lambda i: (nb - 1 - i, 0))
    return _call(body, name='fox_cumsum_bwd', out_shape=(_sds((T, H), F32), _sds((1, H), F32)), grid=(nb,),
                 in_specs=[blk, blk, _full((1, H))], out_specs=(blk, _full((1, H))),
                 scratch=[pltpu.VMEM((1, H), F32)], sem=('arbitrary',))(dc, zf, bf)


N_PIECES = 3


def _head_col(c_ref, h):
    lane = lax.broadcasted_iota(jnp.int32, c_ref.shape, 1)
    return jnp.sum(jnp.where(lane == h, c_ref[...], 0.0), axis=1, keepdims=True)


def _pieces(v, sign=1.0):
    return [sign * p.astype(F32) for p in _split3(v)]


def _augment(block, hh, dh, a_cols, b_cols):
    lane = lax.broadcasted_iota(jnp.int32, block.shape, 1)
    base = (1 - hh) * dh
    out = jnp.where((lane >= hh * dh) & (lane < (hh + 1) * dh), block.astype(F32), 0.0)
    for n, col in enumerate(list(a_cols) + list(b_cols)):
        out = jnp.where(lane == base + n, col, out)
    return out.astype(BF16)


def _fox_probs(q_aug, k_aug, tri, q0, ke, shift=None):
    s = lax.dot_general(q_aug[q0:ke, :], k_aug[0:ke, :], _DN['nt'], preferred_element_type=F32)
    s_dg = jnp.where(tri, s[:, q0:ke], -jnp.inf)
    if shift is None:
        shift = jnp.max(s_dg, axis=-1, keepdims=True)
        if q0:
            shift = jnp.maximum(shift, jnp.max(s[:, 0:q0], axis=-1, keepdims=True))
    parts = ([jnp.exp(s[:, 0:q0] - shift)] if q0 else []) + [jnp.exp(s_dg - shift)]
    return parts, shift


def _cat(parts):
    parts = [p.astype(BF16) for p in parts]
    return parts[0] if len(parts) == 1 else jnp.concatenate(parts, axis=1)


def _fox_fwd(z, c):
    T = z.shape[0]
    D = z.shape[1] // 6
    dh = D // FOX_HEADS
    hp = LANES // dh
    ncb = D // LANES
    tq = _pick(T, (256, 128))
    scale = dh ** -0.5
    ones = [1.0] * N_PIECES

    def body(q_ref, k_ref, v_ref, c_ref, o_ref, lse_ref, o_all):
        j = pl.program_id(0)
        lse_ref[...] = jnp.zeros_like(lse_ref)
        r = lax.broadcasted_iota(jnp.int32, (tq, tq), 0)
        tri = r >= lax.broadcasted_iota(jnp.int32, (tq, tq), 1)
        for hh in range(hp):
            base = (1 - hh) * dh
            ch = _head_col(c_ref, j * hp + hh)
            q_aug = _augment(q_ref[...].astype(F32) * scale, hh, dh, _pieces(ch), ones)
            k_aug = _augment(k_ref[...], hh, dh, ones, _pieces(ch, -1.0))
            v_aug = _augment(v_ref[...], hh, dh, ones, [])
            for q0 in range(0, T, tq):
                ke = q0 + tq
                parts, m = _fox_probs(q_aug, k_aug, tri, q0, ke)
                o_aug = jnp.dot(_cat(parts), v_aug[0:ke, :], preferred_element_type=F32)
                l = o_aug[:, base:base + 1]
                o_all[hh, q0:ke, :] = o_aug * (1.0 / l)
                lse_ref[q0:ke, hh:hh + 1] = m + jnp.log(l)
        lane = lax.broadcasted_iota(jnp.int32, (T, LANES), 1)
        out = o_all[0]
        for hh in range(1, hp):
            out = jnp.where(lane >= hh * dh, o_all[hh], out)
        o_ref[...] = out.astype(o_ref.dtype)

    blk = lambda cb0: pl.BlockSpec((T, LANES), lambda j: (0, cb0 + j))
    return _call(body, name='fox_fwd', out_shape=(_sds((T, D), BF16), _sds((ncb, T, LANES), F32)), grid=(ncb,),
                 in_specs=[blk(3 * ncb), blk(4 * ncb), blk(5 * ncb), _full((T, FOX_HEADS))],
                 out_specs=(blk(0), pl.BlockSpec((None, T, LANES), lambda j: (j, 0, 0))),
                 scratch=[pltpu.VMEM((hp, T, LANES), F32)], sem=('parallel',))(z, z, z, c)


def _fox_bwd(z, c, o, lse, do, dz1):
    T = z.shape[0]
    D = o.shape[1]
    dh = D // FOX_HEADS
    hp = LANES // dh
    ncb = D // LANES
    tq = _pick(T, (256, 128))
    scale = dh ** -0.5
    ones = [1.0] * N_PIECES

    def body(q_ref, k_ref, v_ref, c_ref, o_ref, lse_ref, do_ref, _, dz_ref, dc_ref, dq_all, dk_acc, dv_acc, stage,
             sems):
        j = pl.program_id(0)
        dk_acc[...] = jnp.zeros_like(dk_acc)
        dv_acc[...] = jnp.zeros_like(dv_acc)
        dc_ref[...] = jnp.zeros_like(dc_ref)
        r = lax.broadcasted_iota(jnp.int32, (tq, tq), 0)
        tri = r >= lax.broadcasted_iota(jnp.int32, (tq, tq), 1)
        lane = lax.broadcasted_iota(jnp.int32, (T, LANES), 1)
        for hh in range(hp):
            base = (1 - hh) * dh
            own = (lane >= hh * dh) & (lane < (hh + 1) * dh)
            ch = _head_col(c_ref, j * hp + hh)
            dov = do_ref[...].astype(F32)
            delta = jnp.sum(jnp.where(own, dov * o_ref[...].astype(F32), 0.0), axis=1, keepdims=True)
            q_aug = _augment(q_ref[...].astype(F32) * scale, hh, dh, _pieces(ch - lse_ref[:, hh:hh + 1]), ones)
            k_aug = _augment(k_ref[...], hh, dh, ones, _pieces(ch, -1.0))
            v_aug = _augment(v_ref[...], hh, dh, ones, [])
            do_aug = _augment(dov, hh, dh, _pieces(delta, -1.0), [])
            for q0 in range(0, T, tq):
                ke = q0 + tq
                p, _ = _fox_probs(q_aug, k_aug, tri, q0, ke, shift=0.0)
                dp = lax.dot_general(do_aug[q0:ke, :], v_aug[0:ke, :], _DN['nt'], preferred_element_type=F32)
                ds = [p[0] * dp[:, 0:q0], p[1] * dp[:, q0:ke]] if q0 else [p[0] * dp]
                dsb, pb = _cat(ds), _cat(p)
                dq_all[hh, q0:ke, :] = jnp.dot(dsb, k_aug[0:ke, :], preferred_element_type=F32)
                dk_acc[hh, 0:ke, :] += lax.dot_general(dsb, q_aug[q0:ke, :], _DN['tn'], preferred_element_type=F32)
                dv_acc[hh, 0:ke, :] += lax.dot_general(pb, do_aug[q0:ke, :], _DN['tn'], preferred_element_type=F32)
            dc_ref[:, hh:hh + 1] = dq_all[hh, :, base:base + 1] - dk_acc[hh, :, base + N_PIECES:base + N_PIECES + 1]
        dq, dk, dv = dq_all[0], dk_acc[0], dv_acc[0]
        for hh in range(1, hp):
            dq = jnp.where(lane >= hh * dh, dq_all[hh], dq)
            dk = jnp.where(lane >= hh * dh, dk_acc[hh], dk)
            dv = jnp.where(lane >= hh * dh, dv_acc[hh], dv)
        copies = []
        for n, g in enumerate((dq * scale, dk, dv)):
            stage[n] = g.astype(stage.dtype)
            col = pl.multiple_of(((3 + n) * ncb + j) * LANES, LANES)
            copies.append(pltpu.make_async_copy(stage.at[n], dz_ref.at[:, pl.ds(col, LANES)], sems.at[n]))
            copies[n].start()
        for cp in copies:
            cp.wait()

    blk = lambda cb0: pl.BlockSpec((T, LANES), lambda j: (0, cb0 + j))
    pair = pl.BlockSpec((None, T, LANES), lambda j: (j, 0, 0))
    acc = pltpu.VMEM((hp, T, LANES), F32)
    anywhere = pl.BlockSpec(memory_space=pl.ANY)
    return _call(body, name='fox_bwd', out_shape=(_sds(dz1.shape, dz1.dtype), _sds((ncb, T, LANES), F32)), grid=(ncb,),
                 in_specs=[blk(3 * ncb), blk(4 * ncb), blk(5 * ncb), _full((T, FOX_HEADS)), blk(0), pair, blk(0),
                           anywhere],
                 out_specs=(anywhere, pair),
                 scratch=[acc, acc, acc, pltpu.VMEM((3, T, LANES), BF16), pltpu.SemaphoreType.DMA((3,))],
                 sem=('arbitrary',), aliases={7: 0})(z, z, z, c, o, lse, do, dz1)


def _ln_parts(u1, gain, bias):
    mu = jnp.mean(u1, axis=-1, keepdims=True)
    xc = u1 - mu
    rstd = lax.rsqrt(jnp.mean(xc * xc, axis=-1, keepdims=True) + EPS)
    xhat = xc * rstd
    return xhat, rstd, xhat * gain + bias


def _tap_groups(offsets):
    groups = {}
    for k, off in enumerate(offsets):
        groups.setdefault(off % SUBLANES, []).append((off - off % SUBLANES, k))
    return groups


def _for_taps(src_ref, r0, cols, groups, aligned, visit):
    for r, taps in groups.items():
        if r == 0:
            for base, k in taps:
                visit(k, src_ref[r0 + base:r0 + base + CONV_ROWS, cols])
            continue
        need = max(base for base, _ in taps) + CONV_ROWS
        aligned[r, 0:need, :] = src_ref[r0 + r:r0 + r + need, cols]
        for base, k in taps:
            visit(k, aligned[r, base:base + CONV_ROWS, :])


def _row_chunk(j, first=0):
    return slice(first + j * ROW_CHUNK, first + (j + 1) * ROW_CHUNK)


def _glu_rows(cv, cg, ext, halo, tb):
    for j in range(tb // ROW_CHUNK):
        rs = _row_chunk(j)
        ext[_row_chunk(j, halo), :] = cv[rs, :].astype(F32) * _sigmoid(cg[rs, :].astype(F32))


def _mixc_fwd(z, w, cb, gain, bias):
    T = z.shape[0]
    D = w.shape[1]
    tb = _pick(T, (256, 128))
    H = HALO_C

    groups = _tap_groups([H - (CONV_C_W - 1) + k for k in range(CONV_C_W)])

    def body(cv, cg, cvp, cgp, w_ref, cb_ref, g_ref, b_ref, o_ref, u1_ref, ext, aligned):
        first = pl.program_id(0) == 0
        ext[0:H, :] = jnp.where(first, 0.0, cvp[...].astype(F32) * _sigmoid(cgp[...].astype(F32)))
        _glu_rows(cv, cg, ext, H, tb)
        for r0 in range(0, tb, CONV_ROWS):
            for c0 in range(0, D, LANES):
                cols = slice(c0, c0 + LANES)
                acc = [jnp.zeros((CONV_ROWS, LANES), F32)]

                def tap(k, rows):
                    acc[0] = acc[0] + rows * w_ref[k:k + 1, cols]

                _for_taps(ext, r0, cols, groups, aligned, tap)
                u1_ref[r0:r0 + CONV_ROWS, cols] = acc[0] + cb_ref[:, cols]

        for j in range(tb // ROW_CHUNK):
            rs = _row_chunk(j)
            _, _, u2 = _ln_parts(u1_ref[rs, :], g_ref[...], b_ref[...])
            o_ref[rs, :] = (u2 * _sigmoid(u2)).astype(o_ref.dtype)

    vec = _full((1, D))
    row = _col(tb, D, 0)
    return _call(body, name='mixc_fwd', out_shape=(_sds((T, D), BF16), _sds((T, D), F32)), grid=(T // tb,),
                 in_specs=[_col(tb, D, 0), _col(tb, D, 1), _prev(tb, H, D, 0), _prev(tb, H, D, 1), _full((32, D)), vec,
                           vec, vec],
                 out_specs=(row, row), scratch=[pltpu.VMEM((H + tb, D), F32), pltpu.VMEM((SUBLANES, CONV_ROWS + H, LANES), F32)],
                 sem=('parallel',))(z, z, z, z, w, cb, gain, bias)


def _mixc_bwd(z, w, gain, bias, u1, du3, dz2):
    T = z.shape[0]
    D = w.shape[1]
    tb = _pick(T, (256, 128))
    H = HALO_C
    nb = T // tb

    def du1_of(u1v, du3v, gv, bv):
        xhat, rstd, u2 = _ln_parts(u1v, gv, bv)
        sg = _sigmoid(u2)
        du2 = du3v * (sg * (1.0 + u2 * (1.0 - sg)))
        dxh = du2 * gv
        m1 = jnp.mean(dxh, axis=-1, keepdims=True)
        m2 = jnp.mean(dxh * xhat, axis=-1, keepdims=True)
        return rstd * (dxh - m1 - xhat * m2), du2, xhat

    fwd_groups = _tap_groups([H - (CONV_C_W - 1) + k for k in range(CONV_C_W)])
    bwd_groups = _tap_groups([CONV_C_W - 1 - k for k in range(CONV_C_W)])

    def body(cv, cg, cvp, cgp, w_ref, g_ref, b_ref, u1_ref, u1n, du3_ref, du3n, _, dz_ref, dw_ref, dcb_ref, dg_ref,
             db_ref, uext, dext, dw_acc, aligned, sums):
        i = pl.program_id(0)
        uext[0:H, :] = jnp.where(i == 0, 0.0, cvp[...].astype(F32) * _sigmoid(cgp[...].astype(F32)))
        _glu_rows(cv, cg, uext, H, tb)
        du1n, _, _ = du1_of(u1n[...], du3n[...].astype(F32), g_ref[...], b_ref[...])
        dext[tb:tb + H, :] = jnp.where(i == nb - 1, 0.0, du1n)

        @pl.when(i == 0)
        def _():
            dw_acc[...] = jnp.zeros_like(dw_acc)
            sums[...] = jnp.zeros_like(sums)

        def halves(v):
            part = v[0:SUBLANES, :]
            for q in range(SUBLANES, ROW_CHUNK, SUBLANES):
                part = part + v[q:q + SUBLANES, :]
            return part

        for j in range(tb // ROW_CHUNK):
            rs = _row_chunk(j)
            du1, du2, xhat = du1_of(u1_ref[rs, :], du3_ref[rs, :].astype(F32), g_ref[...], b_ref[...])
            dext[rs, :] = du1
            sums[0:SUBLANES, :] += halves(du1)
            sums[SUBLANES:2 * SUBLANES, :] += halves(du2 * xhat)
            sums[2 * SUBLANES:3 * SUBLANES, :] += halves(du2)
        for r0 in range(0, tb, CONV_ROWS):
            rows = slice(r0, r0 + CONV_ROWS)
            for c0 in range(0, D, LANES):
                cols = slice(c0, c0 + LANES)
                acc = [jnp.zeros((CONV_ROWS, LANES), F32)]

                def tap(k, src):
                    acc[0] = acc[0] + src * w_ref[k:k + 1, cols]

                _for_taps(dext, r0, cols, bwd_groups, aligned, tap)
                c_val = cv[rows, cols].astype(F32)
                sg = _sigmoid(cg[rows, cols].astype(F32))
                dz_ref[rows, cols] = (acc[0] * sg).astype(dz_ref.dtype)
                dz_ref[rows, D + c0:D + c0 + LANES] = (acc[0] * c_val * sg * (1.0 - sg)).astype(dz_ref.dtype)
                du1_blk = dext[rows, cols]

                def tap_w(k, src):
                    prod = du1_blk * src
                    part = prod[0:SUBLANES, :]
                    for j in range(SUBLANES, CONV_ROWS, SUBLANES):
                        part = part + prod[j:j + SUBLANES, :]
                    dw_acc[k * SUBLANES:(k + 1) * SUBLANES, cols] += part

                _for_taps(uext, r0, cols, fwd_groups, aligned, tap_w)

        @pl.when(i == nb - 1)
        def _():
            dw_ref[...] = jnp.zeros_like(dw_ref)
            for k in range(CONV_C_W):
                dw_ref[k:k + 1, :] = jnp.sum(dw_acc[k * SUBLANES:(k + 1) * SUBLANES, :], axis=0, keepdims=True)
            for n, ref in enumerate((dcb_ref, dg_ref, db_ref)):
                ref[...] = jnp.sum(sums[n * SUBLANES:(n + 1) * SUBLANES, :], axis=0, keepdims=True)

    vec = _full((1, D))
    row = _col(tb, D, 0)
    nxt = _next(tb, H, D, 0, T)
    return _call(body, name='mixc_bwd',
                 out_shape=(_sds(dz2.shape, dz2.dtype), _sds((32, D), F32), _sds((1, D), F32), _sds((1, D), F32),
                            _sds((1, D), F32)), grid=(nb,),
                 in_specs=[_col(tb, D, 0), _col(tb, D, 1), _prev(tb, H, D, 0), _prev(tb, H, D, 1), _full((32, D)), vec,
                           vec, row, nxt, row, nxt, pl.BlockSpec(memory_space=pl.ANY)],
                 out_specs=(_col(tb, 2 * D, 0), _full((32, D)), vec, vec, vec),
                 scratch=[pltpu.VMEM((H + tb, D), F32), pltpu.VMEM((tb + H, D), F32),
                          pltpu.VMEM((32 * SUBLANES, D), F32), pltpu.VMEM((SUBLANES, CONV_ROWS + H, LANES), F32),
                          pltpu.VMEM((3 * SUBLANES, D), F32)],
                 sem=('arbitrary',), aliases={11: 0})(z, z, z, z, w, gain, bias, u1, u1, du3, du3, dz2)


def _gate_fwd(z, bg, ya, yb, yc):
    T = z.shape[0]
    D = ya.shape[1]
    tb = _pick(T, (256, 128))

    def body(ga, gb, gc, bg_ref, ya_ref, yb_ref, yc_ref, o_ref):
        acc = jnp.zeros((tb, D), F32)
        for n, (g, y) in enumerate(((ga, ya_ref), (gb, yb_ref), (gc, yc_ref))):
            acc = acc + _sigmoid(g[...].astype(F32) + bg_ref[:, n * D:(n + 1) * D]) * y[...].astype(F32)
        o_ref[...] = acc.astype(o_ref.dtype)

    row = _col(tb, D, 0)
    return _call(body, name='gate_fwd', out_shape=_sds((T, D), BF16), grid=(T // tb,),
                 in_specs=[_col(tb, D, 2), _col(tb, D, 3), _col(tb, D, 4), _full((1, 3 * D)), row, row, row],
                 out_specs=row, sem=('parallel',))(z, z, z, bg, ya, yb, yc)


def _gate_bwd(z, bg, ya, yb, yc, dm):
    T = z.shape[0]
    D = ya.shape[1]
    tb = _pick(T, (256, 128))

    def body(ga, gb, gc, bg_ref, ya_ref, yb_ref, yc_ref, dm_ref, dya, dyb, dyc, dg_ref, dbg_ref):
        @pl.when(pl.program_id(0) == 0)
        def _():
            dbg_ref[...] = jnp.zeros_like(dbg_ref)

        dmv = dm_ref[...]
        for n, (g, y, dy) in enumerate(((ga, ya_ref, dya), (gb, yb_ref, dyb), (gc, yc_ref, dyc))):
            cols = slice(n * D, (n + 1) * D)
            sg = _sigmoid(g[...].astype(F32) + bg_ref[:, cols])
            dy[...] = (dmv * sg).astype(dy.dtype)
            dg = dmv * y[...].astype(F32) * sg * (1.0 - sg)
            dg_ref[:, 2 * D + n * D:2 * D + (n + 1) * D] = dg.astype(dg_ref.dtype)
            dbg_ref[:, cols] += jnp.sum(dg, axis=0, keepdims=True)

    row = _col(tb, D, 0)
    act = _sds((T, D), BF16)
    return _call(body, name='gate_bwd', out_shape=(act, act, act, _sds((T, 5 * D), BF16), _sds((1, 3 * D), F32)),
                 grid=(T // tb,),
                 in_specs=[_col(tb, D, 2), _col(tb, D, 3), _col(tb, D, 4), _full((1, 3 * D)), row, row, row, row],
                 out_specs=(row, row, row, _col(tb, 5 * D, 0), _full((1, 3 * D))),
                 sem=('arbitrary',))(z, z, z, bg, ya, yb, yc, dm)


def _xattn_probs(qs, ks, scale):
    s = lax.dot_general(qs, ks, _DN['nt'], preferred_element_type=F32) * scale
    p = jnp.exp(s - jnp.max(s, axis=-1, keepdims=True))
    return p * (1.0 / jnp.sum(p, axis=-1, keepdims=True))


def _xattn_fwd(q, kv):
    T, DX = q.shape
    M = kv.shape[0]
    dh = DX // X_HEADS
    tb = _pick(T, (512, 256, 128))
    scale = dh ** -0.5

    def body(q_ref, kv_ref, o_ref):
        for h in range(X_HEADS):
            hs = slice(h * dh, (h + 1) * dh)
            p = _xattn_probs(q_ref[:, hs], kv_ref[:, hs], scale)
            o_ref[:, hs] = jnp.dot(p.astype(BF16), kv_ref[:, DX + h * dh:DX + (h + 1) * dh],
                                   preferred_element_type=F32).astype(o_ref.dtype)

    row = _col(tb, DX, 0)
    return _call(body, name='xattn_fwd', out_shape=_sds((T, DX), BF16), grid=(T // tb,),
                 in_specs=[row, _full((M, 2 * DX))], out_specs=row, sem=('parallel',))(q, kv)


def _xattn_bwd(q, kv, do):
    T, DX = q.shape
    M = kv.shape[0]
    dh = DX // X_HEADS
    tb = _pick(T, (512, 256, 128))
    scale = dh ** -0.5

    def body(q_ref, kv_ref, do_ref, dq_ref, dkv_ref):
        @pl.when(pl.program_id(0) == 0)
        def _():
            dkv_ref[...] = jnp.zeros_like(dkv_ref)

        for h in range(X_HEADS):
            hs = slice(h * dh, (h + 1) * dh)
            vs_cols = slice(DX + h * dh, DX + (h + 1) * dh)
            qs, ks, vs = q_ref[:, hs], kv_ref[:, hs], kv_ref[:, vs_cols]
            dos = do_ref[:, hs].astype(BF16)
            p = _xattn_probs(qs, ks, scale)
            dp = lax.dot_general(dos, vs, _DN['nt'], preferred_element_type=F32)
            ds = p * (dp - jnp.sum(p * dp, axis=-1, keepdims=True))
            dsb = ds.astype(BF16)
            dq_ref[:, hs] = (jnp.dot(dsb, ks, preferred_element_type=F32) * scale).astype(dq_ref.dtype)
            dkv_ref[:, hs] += lax.dot_general(dsb, qs, _DN['tn'], preferred_element_type=F32) * scale
            dkv_ref[:, vs_cols] += lax.dot_general(p.astype(BF16), dos, _DN['tn'], preferred_element_type=F32)

    row = _col(tb, DX, 0)
    return _call(body, name='xattn_bwd', out_shape=(_sds((T, DX), BF16), _sds((M, 2 * DX), F32)), grid=(T // tb,),
                 in_specs=[row, _full((M, 2 * DX)), row], out_specs=(row, _full((M, 2 * DX))),
                 sem=('arbitrary',))(q, kv, do)


def _swiglu_fwd(gu):
    T, F2 = gu.shape
    F = F2 // 2
    tb = _pick(T, (256, 128))

    def body(g_ref, u_ref, o_ref):
        g = g_ref[...].astype(F32)
        o_ref[...] = (g * _sigmoid(g) * u_ref[...].astype(F32)).astype(o_ref.dtype)

    return _call(body, name='swiglu_fwd', out_shape=_sds((T, F), BF16), grid=(T // tb,),
                 in_specs=[_col(tb, F, 0), _col(tb, F, 1)], out_specs=_col(tb, F, 0), sem=('parallel',))(gu, gu)


def _swiglu_bwd(gu, da):
    T, F2 = gu.shape
    F = F2 // 2
    tb = _pick(T, (256, 128))

    def body(g_ref, u_ref, da_ref, o_ref):
        g = g_ref[...].astype(F32)
        u = u_ref[...].astype(F32)
        dav = da_ref[...].astype(F32)
        sg = _sigmoid(g)
        o_ref[:, 0:F] = (dav * u * (sg * (1.0 + g * (1.0 - sg)))).astype(o_ref.dtype)
        o_ref[:, F:F2] = (dav * g * sg).astype(o_ref.dtype)

    return _call(body, name='swiglu_bwd', out_shape=_sds((T, F2), BF16), grid=(T // tb,),
                 in_specs=[_col(tb, F, 0), _col(tb, F, 1), _col(tb, F, 0)], out_specs=_col(tb, F2, 0),
                 sem=('parallel',))(gu, gu, da)


def _place():
    x, y, c = lax.axis_index('x'), lax.axis_index('y'), lax.axis_index('c')
    return x, y, c


_PEER_ORDER = (1, 4, 2, 5, 3, 6, 7)
_HBM = pl.BlockSpec(memory_space=pltpu.HBM)
_SEM = pl.BlockSpec(memory_space=pltpu.SEMAPHORE)


def _my_index():
    x, y, c = _place()
    return 4 * x + 2 * y + c


def _remote(src, dst, send_sems, recv_sems, k, dev):
    return pltpu.make_async_remote_copy(src_ref=src, dst_ref=dst, send_sem=send_sems.at[k], recv_sem=recv_sems.at[k],
                                        device_id=dev, device_id_type=MESH)


def _own_part(src, kind, args):
    return src if args is None else src.at[pl.ds(args[0], args[1])]


def _block(land, kind, args, owner):
    if kind == 'rows':
        _, n, stride, d0 = args
        return land.at[pl.ds(pl.multiple_of(stride * owner + d0, ROW_TILE), n)]
    return land.at[owner]


def _chunk(src, args, dest):
    stride, n = args
    return src.at[pl.ds(pl.multiple_of(stride * dest, ROW_TILE), n)]


def _scatter_copies(plan, srcs, lands, send_sems, recv_sems):
    x, y, c = _place()
    me = 4 * x + 2 * y + c
    copies = []
    for r in _PEER_ORDER:
        px, py, pc = x ^ (r >> 2), y ^ ((r >> 1) & 1), c ^ (r & 1)
        for si, li, kind, args in plan:
            src = srcs[si] if kind == 'slot' else _chunk(srcs[si], args, 4 * px + 2 * py + pc)
            copies.append(_remote(src, lands[li].at[me], send_sems, recv_sems, r - 1, (px, py, pc)))
    return copies


def _gather_copies(plan, srcs, lands, send_sems, recv_sems, passing):
    x, y, c = _place()
    me = 4 * x + 2 * y + c
    sibling = (x, y, 1 - c)
    chips = [(1 - x, y), (x, 1 - y), (1 - x, 1 - y)]
    copies = []
    if passing:
        for k, chip in enumerate(chips):
            for _, li, kind, args in plan:
                blk = _block(lands[li], kind, args, 4 * chip[0] + 2 * chip[1] + c)
                copies.append((k, _remote(blk, blk, send_sems, recv_sems, k, sibling)))
    else:
        for k, dev in enumerate([sibling] + [(*chip, c) for chip in chips]):
            for si, li, kind, args in plan:
                dst = _block(lands[li], kind, args, me)
                src = dst if srcs is None else _own_part(srcs[si], kind, args)
                copies.append((k, _remote(src, dst, send_sems, recv_sems, k, dev)))
    return copies


def _sem_call(name, bufs, sems_in, new_sems, after, action):
    nb, ni, nn = len(bufs), len(sems_in), len(new_sems)

    def body(*refs):
        action(refs[:nb], refs[nb:nb + ni], refs[-1 - nb - nn:-1 - nb])
        refs[-1][...] = jnp.zeros_like(refs[-1])

    bufs = [pltpu.with_memory_space_constraint(v, pltpu.HBM) for v in bufs]
    res = pl.pallas_call(
        body, name=name,
        out_shape=(*[pltpu.SemaphoreType.DMA((n,)) for n in new_sems], *[pltpu.HBM(v.shape, v.dtype) for v in bufs],
                   _sds((8, LANES), F32)),
        in_specs=[_HBM] * nb + [_SEM] * ni + [pl.BlockSpec(memory_space=pl.ANY)] * len(after),
        out_specs=(*[_SEM] * nn, *[_HBM] * nb, pl.BlockSpec(memory_space=pltpu.VMEM)),
        input_output_aliases={i: nn + i for i in range(nb)},
        compiler_params=pltpu.CompilerParams(has_side_effects=pltpu.SideEffectType.DATAFLOW_SIDE_EFFECTING),
        interpret=False)(*bufs, *sems_in, *after)
    return list(res[:nn]), list(res[nn:nn + nb]), res[-1]


def _place_own(srcs, plan, lands_like, name, deps):
    ns, nl, ne = len(srcs), len(lands_like), len(plan)

    def part_shape(entry):
        si, _, kind, args = entry
        rows = args[1] if (kind == 'chunk' or args is not None) else None
        return srcs[si].shape if rows is None else (rows,) + srcs[si].shape[1:]

    def body(*refs):
        src_r, land_r = refs[:ns], refs[ns:ns + nl]
        stage, sems_in, sems_out = refs[ns + nl:ns + nl + ne], refs[-2], refs[-1]
        me = _my_index()
        loads, stores = [], []
        for i, (si, li, kind, args) in enumerate(plan):
            if kind == 'chunk':
                src, dst = _chunk(src_r[si], args, me), land_r[li].at[me]
            else:
                src, dst = _own_part(src_r[si], kind, args), _block(land_r[li], kind, args, me)
            loads.append(pltpu.make_async_copy(src, stage[i], sems_in.at[i]))
            stores.append(pltpu.make_async_copy(stage[i], dst, sems_out.at[i]))
        for cp in loads:
            cp.start()
        for load, store in zip(loads, stores):
            load.wait()
            store.start()
        for cp in stores:
            cp.wait()

    anywhere = pl.BlockSpec(memory_space=pl.ANY)
    scratch = [pltpu.VMEM(part_shape(e), srcs[e[0]].dtype) for e in plan]
    scratch += [pltpu.SemaphoreType.DMA((ne,)), pltpu.SemaphoreType.DMA((ne,))]
    return list(_call(body, name=name, out_shape=tuple(lands_like), in_specs=[anywhere] * ns,
                      out_specs=tuple([anywhere] * nl), scratch=scratch, deps=deps)(*srcs))


def _scatter_start(srcs, lands, plan, after, name):
    ns = len(srcs)

    def action(bufs, _, new):
        for cp in _scatter_copies(plan, bufs[:ns], bufs[ns:], new[0], new[1]):
            cp.start()

    sems, bufs, token = _sem_call(name, [*srcs, *lands], [], [N_DEV - 1, N_DEV - 1], after, action)
    return sems, bufs[:ns], bufs[ns:], token


def _scatter_wait(sems, srcs, lands, plan, after, name):
    ns = len(srcs)

    def action(bufs, old, _):
        copies = _scatter_copies(plan, bufs[:ns], bufs[ns:], old[0], old[1])
        for cp in copies:
            cp.wait_send()
        for cp in copies:
            cp.wait_recv()

    return _sem_call(name, [*srcs, *lands], sems, [], after, action)[1][ns:]


def _gather_start(srcs, lands, plan, name):
    ns = len(srcs)

    def action(bufs, _, new):
        for _, cp in _gather_copies(plan, bufs[:ns], bufs[ns:], new[0], new[1], False):
            cp.start()

    sems, bufs, token = _sem_call(name, [*srcs, *lands], [], [4, 4], [], action)
    return sems, bufs[:ns], bufs[ns:], token


def _gather_pass(recv_sems, lands, plan, after, name):
    def action(bufs, old, new):
        for _, cp in _gather_copies(plan, None, bufs, old[0], old[0], False):
            cp.wait_recv()
        for _, cp in _gather_copies(plan, None, bufs, new[0], new[1], True):
            cp.start()

    return _sem_call(name, lands, [recv_sems], [3, 3], after, action)


def _gather_wait(send_sems, pass_sems, srcs, lands, plan, after, name):
    ns = len(srcs)

    def action(bufs, old, _):
        for _, cp in _gather_copies(plan, bufs[:ns], bufs[ns:], old[0], old[0], False):
            cp.wait_send()
        passed = _gather_copies(plan, None, bufs[ns:], old[1], old[2], True)
        for _, cp in passed:
            cp.wait_send()
        for _, cp in passed:
            cp.wait_recv()

    return _sem_call(name, [*srcs, *lands], [send_sems, *pass_sems], [], after, action)[1][ns:]


def _merge_edges(wt, edges, tiles_per_dev):
    D = wt.shape[1]

    def body(w_ref, e_ref, o_ref):
        s = pl.program_id(0)
        o_ref[...] = jnp.where(s == 0, e_ref[...], w_ref[...] + e_ref[...])

    tile = pl.BlockSpec((ROW_TILE, D), lambda s: (s * tiles_per_dev, 0))
    return _call(body, name='merge_edges', out_shape=_sds(wt.shape, wt.dtype), grid=(N_DEV,),
                 in_specs=[tile, pl.BlockSpec((None, ROW_TILE, D), lambda s: (s, 0, 0))], out_specs=tile,
                 sem=('arbitrary',), aliases={0: 0})(wt, edges)


def _adam_update(g, w, m, v):
    c1 = 1.0 - ADAM_B1 ** ADAM_STEP
    c2 = 1.0 - ADAM_B2 ** ADAM_STEP
    mn = ADAM_B1 * m + (1.0 - ADAM_B1) * g
    vn = ADAM_B2 * v + (1.0 - ADAM_B2) * (g * g)
    m_hat = mn / c1
    v_hat = vn / c2
    return -ADAM_LR * (m_hat / (jnp.sqrt(v_hat) + ADAM_EPS) + ADAM_WD * w), mn, vn


def _adamw(parts, w, m, v, name):
    n_parts = parts.shape[0]
    R, C = w.shape
    tb = R if (n_parts + 7) * R * C * 4 <= (12 << 20) else _pick(R, (128, 64, 32, 16, 8))

    def body(p_ref, w_ref, m_ref, v_ref, g_out, d_out, m_out, v_out):
        g = p_ref[0].astype(F32)
        for s in range(1, n_parts):
            g = g + p_ref[s].astype(F32)
        g_out[...] = g
        d_out[...], m_out[...], v_out[...] = _adam_update(g, w_ref[...], m_ref[...], v_ref[...])

    row = pl.BlockSpec((tb, C), lambda i: (i, 0))
    o = _sds((R, C), F32)
    return _call(body, name=name, out_shape=(o, o, o, o), grid=(R // tb,),
                 in_specs=[pl.BlockSpec((n_parts, tb, C), lambda i: (0, i, 0)), row, row, row],
                 out_specs=(row, row, row, row), sem=('parallel',))(parts, w, m, v)


def _interleave_layers(stack):
    L, n, C = stack.shape

    def body(*refs):
        o_ref = refs[L]
        for l in range(L):
            o_ref[:, l, :] = refs[l][...]

    return _call(body, name='interleave_layers', out_shape=_sds((n, L, C), stack.dtype), grid=(C // LANES,),
                 in_specs=[pl.BlockSpec((None, n, LANES), functools.partial(lambda l, j: (l, 0, j), l)) for l in range(L)],
                 out_specs=pl.BlockSpec((n, L, LANES), lambda j: (0, 0, j)), sem=('parallel',))(*([stack] * L))


def _adamw_layer(parts, w, m, v, layer, prev, shift, name):
    L, n, C = w.shape
    n_p = parts.shape[1]
    cb = C if N_DEV * n_p * C * parts.dtype.itemsize <= (6 << 20) else 2 * LANES
    n_prev = 0 if prev is None else 4

    def body(sh_ref, p_ref, w_ref, m_ref, v_ref, *rest):
        g_out, d_out, m_out, v_out = rest[n_prev:n_prev + 4]
        g = p_ref[0].astype(F32)
        for s in range(1, N_DEV):
            g = g + p_ref[s].astype(F32)
        if n_p != n:
            rolled = rest[-1]
            rolled[...] = pltpu.roll(g, n_p - sh_ref[0], 0)
            g = rolled[0:n, :]
        g_out[...] = g
        d_out[...], m_out[...], v_out[...] = _adam_update(g, w_ref[...], m_ref[...], v_ref[...])

    lay = pl.BlockSpec((None, n, cb), lambda j, sh: (layer, 0, j))
    stack = _sds((L, n, C), F32)
    grid_spec = pltpu.PrefetchScalarGridSpec(
        num_scalar_prefetch=1, grid=(C // cb,),
        in_specs=[pl.BlockSpec((N_DEV, n_p, cb), lambda j, sh: (0, 0, j)), lay, lay, lay]
        + [pl.BlockSpec(memory_space=pl.ANY)] * n_prev,
        out_specs=(lay, lay, lay, lay), scratch_shapes=[pltpu.VMEM((n_p, cb), F32)] if n_p != n else [])
    return pl.pallas_call(body, name=name, grid_spec=grid_spec, out_shape=(stack, stack, stack, stack),
                          input_output_aliases={5 + k: k for k in range(n_prev)},
                          compiler_params=pltpu.CompilerParams(dimension_semantics=('parallel',),
                                                               vmem_limit_bytes=VMEM_LIMIT),
                          interpret=False)(shift, parts, w, m, v, *(prev or ()))


def kernel(x, mem, mix_norm, w_in, b_gate, b_forget, conv_a, w_out_a, w_out_b, conv_c, conv_c_bias, ln_c_gain, ln_c_bias, w_out_c, w_o, xattn_norm, mem_norm, w_xq, w_xkv, w_xo, ffn_norm, w_gate_up, w_down, final_norm, loss_target, m_mix_norm, m_w_in, m_b_gate, m_b_forget, m_conv_a, m_w_out_a, m_w_out_b, m_conv_c, m_conv_c_bias, m_ln_c_gain, m_ln_c_bias, m_w_out_c, m_w_o, m_xattn_norm, m_mem_norm, m_w_xq, m_w_xkv, m_w_xo, m_ffn_norm, m_w_gate_up, m_w_down, m_final_norm, v_mix_norm, v_w_in, v_b_gate, v_b_forget, v_conv_a, v_w_out_a, v_w_out_b, v_conv_c, v_conv_c_bias, v_ln_c_gain, v_ln_c_bias, v_w_out_c, v_w_o, v_xattn_norm, v_mem_norm, v_w_xq, v_w_xkv, v_w_xo, v_ffn_norm, v_w_gate_up, v_w_down, v_final_norm):
    P = dict(zip(ARG_NAMES, (x, mem, mix_norm, w_in, b_gate, b_forget, conv_a, w_out_a, w_out_b, conv_c, conv_c_bias, ln_c_gain, ln_c_bias, w_out_c, w_o, xattn_norm, mem_norm, w_xq, w_xkv, w_xo, ffn_norm, w_gate_up, w_down, final_norm, loss_target, m_mix_norm, m_w_in, m_b_gate, m_b_forget, m_conv_a, m_w_out_a, m_w_out_b, m_conv_c, m_conv_c_bias, m_ln_c_gain, m_ln_c_bias, m_w_out_c, m_w_o, m_xattn_norm, m_mem_norm, m_w_xq, m_w_xkv, m_w_xo, m_ffn_norm, m_w_gate_up, m_w_down, m_final_norm, v_mix_norm, v_w_in, v_b_gate, v_b_forget, v_conv_a, v_w_out_a, v_w_out_b, v_conv_c, v_conv_c_bias, v_ln_c_gain, v_ln_c_bias, v_w_out_c, v_w_o, v_xattn_norm, v_mem_norm, v_w_xq, v_w_xkv, v_w_xo, v_ffn_norm, v_w_gate_up, v_w_down, v_final_norm)))
    return _step(P)


_T_VIEW = ('w_in', 'w_gate_up', 'w_xo')
_BIG = [w for w in SHARDED if w not in ('conv_a', 'conv_c')]
_OTHER = [w for w in _BIG if w != 'w_in']
_CONVS = (('conv_a', CONV_A_W, 8), ('conv_c', CONV_C_W, 32))
_EARLY = ['w_down', 'w_gate_up', 'w_xo', 'w_xq', 'w_xkv', 'w_o']
_LATE = ['w_out_c', 'w_out_b', 'w_out_a', 'w_in']


def _view(name, a):
    return jnp.transpose(a, (0, 2, 1)) if name in _T_VIEW else a


def _step(P):
    L, D = P['mix_norm'].shape
    T = P['x'].shape[1]
    n_in = P['w_in'].shape[2]
    stride = n_in // ROW_TILE * ROW_TILE
    rem = n_in - stride
    win = stride + ROW_TILE
    assert rem * N_DEV == ROW_TILE
    r_in = n_in * N_DEV
    f_off = 6 * D
    hi_off = f_off + FOX_HEADS
    x_i, y_i, c_i = _place()
    me = 4 * x_i + 2 * y_i + c_i
    V = {}
    for w in _BIG:
        for k in (w, 'm_' + w, 'v_' + w):
            V[k] = _view(w, P[k])
    n_own = {w: V[w].shape[1] for w in _OTHER}
    shift = jnp.reshape(rem * me, (1,)).astype(jnp.int32)
    no_shift = jnp.zeros((1,), jnp.int32)
    row = lambda a, l: a[l][None, :]
    dus = lax.dynamic_update_slice

    conv_taps = {w: (taps, padded) for w, taps, padded in _CONVS}

    def gather_start(l, names, tag, deps):
        srcs, lands, plan = [], [], []
        for w in names:
            si, li = len(srcs), len(lands)
            if w == 'w_in':
                first = rem * me + sum(t[0, 0] for t in deps).astype(jnp.int32) if deps else rem * me
                srcs.append(dus(jnp.zeros((win, D), BF16), V[w][l].astype(BF16), (first, 0)))
                lands += [_sds((r_in, D), BF16), _sds((N_DEV, ROW_TILE, D), BF16)]
                plan += [(si, li, 'rows', (ROW_TILE, stride, stride, ROW_TILE)), (si, li + 1, 'slot', (0, ROW_TILE))]
            elif w in conv_taps:
                srcs.append(P[w][l])
                lands.append(_sds((N_DEV,) + srcs[si].shape, F32))
                plan.append((si, li, 'slot', None))
            else:
                srcs.append(V[w][l].astype(BF16))
                lands.append(_sds((N_DEV * n_own[w], srcs[si].shape[1]), BF16))
                plan.append((si, li, 'rows', (0, n_own[w], n_own[w], 0)))
        lands = _place_own(srcs, plan, lands, 'gather_own', deps)
        sems, srcs, lands, token = _gather_start(srcs, lands, plan, f'gather_start_{tag}{l}')
        return dict(names=names, name=f'{tag}{l}', plan=plan, send=sems[0], recv=sems[1], srcs=srcs, lands=lands,
                    token=token)

    def gather_pass(g, after):
        g['pass'], g['lands'], g['token'] = _gather_pass(g['recv'], g['lands'], g['plan'], after,
                                                        'gather_pass_' + g['name'])

    def gather_wait(g, after):
        lands = iter(_gather_wait(g['send'], g['pass'], g['srcs'], g['lands'], g['plan'], after,
                                  'gather_wait_' + g['name']))
        W = {}
        for w in g['names']:
            if w == 'w_in':
                W[w] = _merge_edges(next(lands), next(lands), stride // ROW_TILE)
            elif w in conv_taps:
                taps, padded = conv_taps[w]
                full = jnp.transpose(next(lands), (1, 0, 2)).reshape(taps, D)
                W[w] = jnp.pad(full, ((0, padded - taps), (0, 0)))
            else:
                W[w] = next(lands)
        return W

    first_part = ['w_in'] + list(conv_taps)
    gathers, prev = [], []
    for l in range(L):
        part_a = gather_start(l, first_part, 'a', prev)
        part_b = gather_start(l, _OTHER, 'b', [part_a['token']])
        gathers.append([part_a, part_b])
        prev = [part_b['token']]
    gather_pass(gathers[0][0], prev)

    mem_x = P['mem'][0]
    mem_n = _rms_fwd(mem_x, P['mem_norm'][None, :], 'rms_mem')
    xs = P['x'][0]
    saved = []
    for l in range(L):
        W = gather_wait(gathers[l][0], [xs])
        s = {'x0': xs, 'W': W}
        wt = W['w_in']
        s['h1'] = _rms_fwd(xs, row(P['mix_norm'], l), 'rms_mix')
        s['z1'] = z1 = _mm(s['h1'], wt, mode='nt', b_rows=(0, f_off), name='mm_in_lo')
        s['z2'] = z2 = _mm(s['h1'], wt, mode='nt', b_rows=(hi_off, r_in - hi_off), name='mm_in_hi')
        s['zf'] = _mm(s['h1'], wt, mode='nt', b_rows=(f_off, FOX_HEADS), out_dtype=F32, name='mm_in_f')
        s['ya_pre'] = _mixa_fwd(z1, W['conv_a'])
        s['c'] = _fox_c(s['zf'], row(P['b_forget'], l))
        s['o'], s['lse'] = _fox_fwd(z1, s['c'])
        for g in gathers[l][1:]:
            gather_pass(g, [s['o']])
        s['u3'], s['u1'] = _mixc_fwd(z2, W['conv_c'], row(P['conv_c_bias'], l), row(P['ln_c_gain'], l),
                                     row(P['ln_c_bias'], l))
        for g in gathers[l][1:]:
            W.update(gather_wait(g, [s['u3']]))
        s['ya'] = _mm(s['ya_pre'], W['w_out_a'], mode='nn', name='mm_out_a')
        s['yb'] = _mm(s['o'], W['w_out_b'], mode='nn', name='mm_out_b')
        s['yc'] = _mm(s['u3'], W['w_out_c'], mode='nn', name='mm_out_c')
        s['merged'] = _gate_fwd(z2, row(P['b_gate'], l), s['ya'], s['yb'], s['yc'])
        xs = _mm(s['merged'], W['w_o'], mode='nn', add=xs, out_dtype=F32, name='mm_o')
        s['x1'] = xs
        deps = []
        if l + 1 < L:
            gather_pass(gathers[l + 1][0], [xs])
            deps = [gathers[l + 1][0]['token']]
        s['h2'] = _rms_fwd(xs, row(P['xattn_norm'], l), 'rms_xattn', deps=deps)
        s['qx'] = _mm(s['h2'], W['w_xq'], mode='nn', name='mm_xq')
        s['kv'] = _mm(mem_n, W['w_xkv'], mode='nn', name='mm_xkv')
        s['ox'] = _xattn_fwd(s['qx'], s['kv'])
        xs = _mm(s['ox'], W['w_xo'], mode='nt', add=xs, out_dtype=F32, name='mm_xo')
        s['x2'] = xs
        s['h3'] = _rms_fwd(xs, row(P['ffn_norm'], l), 'rms_ffn')
        s['gu'] = _mm(s['h3'], W['w_gate_up'], mode='nt', name='mm_gate_up')
        s['act'] = _swiglu_fwd(s['gu'])
        xs = _mm(s['act'], W['w_down'], mode='nn', add=xs, out_dtype=F32, name='mm_down')
        saved.append(s)

    dx, d_final_norm, loss_part = _final(xs, P['final_norm'][None, :], P['loss_target'][0])

    GS = {w: [None] * L for w in SMALL}
    G_conv = {'conv_a': [None] * L, 'conv_c': [None] * L}
    d_mem_n = None
    results = {w: None for w in _BIG}

    def exchange_start(G, names, l, tag, after):
        srcs = [G[w] for w in names]
        plan, lands = [], []
        for i, w in enumerate(names):
            step, n = (stride, win) if w == 'w_in' else (n_own[w], n_own[w])
            lands.append(_sds((N_DEV, n, srcs[i].shape[1]), BF16))
            plan.append((i, i, 'chunk', (step, n)))
        lands = _place_own(srcs, plan, lands, 'exchange_own', [])
        sems, srcs, lands, token = _scatter_start(srcs, lands, plan, after, f'exchange_start_{tag}{l}')
        return (names, l, tag, plan, sems, srcs, lands, token)

    def exchange_finish(e, after):
        names, l, tag, plan, sems, srcs, lands, _ = e
        lands = _scatter_wait(sems, srcs, lands, plan, after, f'exchange_wait_{tag}{l}')
        for i, w in enumerate(names):
            results[w] = _adamw_layer(lands[i], V[w], V['m_' + w], V['v_' + w], l, results[w],
                                      shift if w == 'w_in' else no_shift, 'adamw_' + w)

    pending = []
    for l in reversed(range(L)):
        s = saved[l]
        W, z1, z2 = s['W'], s['z1'], s['z2']
        wt = W['w_in']
        G = {}
        d_act = _mm(dx, W['w_down'], mode='nt', name='mmb_down', deps=[e[-1] for e in pending])
        G['w_down'] = _mm(s['act'], dx, mode='tn', name='mmg_w_down')
        dgu = _swiglu_bwd(s['gu'], d_act)
        dh3 = _mm(dgu, W['w_gate_up'], mode='nn', out_dtype=F32, name='mmb_gate_up')
        G['w_gate_up'] = _mm(dgu, s['h3'], mode='tn', name='mmg_w_gate_up')
        dx, GS['ffn_norm'][l] = _rms_bwd(s['x2'], row(P['ffn_norm'], l), dh3, dx, 'rms_ffn_bwd')
        d_ox = _mm(dx, W['w_xo'], mode='nn', out_dtype=F32, name='mmb_xo')
        G['w_xo'] = _mm(dx, s['ox'], mode='tn', name='mmg_w_xo')
        dqx, dkv = _xattn_bwd(s['qx'], s['kv'], d_ox)
        dh2 = _mm(dqx, W['w_xq'], mode='nt', out_dtype=F32, name='mmb_xq')
        G['w_xq'] = _mm(s['h2'], dqx, mode='tn', name='mmg_w_xq')
        G['w_xkv'] = _mm(mem_n, dkv, mode='tn', name='mmg_w_xkv')
        d_mem_n = _mm(dkv, W['w_xkv'], mode='nt', add=d_mem_n, out_dtype=F32, name='mmb_xkv')
        dx, GS['xattn_norm'][l] = _rms_bwd(s['x1'], row(P['xattn_norm'], l), dh2, dx, 'rms_xattn_bwd')
        dm = _mm(dx, W['w_o'], mode='nt', out_dtype=F32, name='mmb_o')
        G['w_o'] = _mm(s['merged'], dx, mode='tn', name='mmg_w_o')
        early = exchange_start(G, _EARLY, l, 'a', [])
        dya, dyb, dyc, dz2, GS['b_gate'][l] = _gate_bwd(z2, row(P['b_gate'], l), s['ya'], s['yb'], s['yc'], dm)
        du3 = _mm(dyc, W['w_out_c'], mode='nt', out_dtype=F32, name='mmb_out_c', deps=[early[-1]])
        G['w_out_c'] = _mm(s['u3'], dyc, mode='tn', name='mmg_w_out_c')
        dz2, G_conv['conv_c'][l], GS['conv_c_bias'][l], GS['ln_c_gain'][l], GS['ln_c_bias'][l] = _mixc_bwd(
            z2, W['conv_c'], row(P['ln_c_gain'], l), row(P['ln_c_bias'], l), s['u1'], du3, dz2)
        dya_pre = _mm(dya, W['w_out_a'], mode='nt', out_dtype=F32, name='mmb_out_a')
        G['w_out_a'] = _mm(s['ya_pre'], dya, mode='tn', name='mmg_w_out_a')
        dz1, G_conv['conv_a'][l] = _mixa_bwd(z1, W['conv_a'], dya_pre)
        do = _mm(dyb, W['w_out_b'], mode='nt', name='mmb_out_b')
        G['w_out_b'] = _mm(s['o'], dyb, mode='tn', name='mmg_w_out_b')
        dz1, dc_pairs = _fox_bwd(z1, s['c'], s['o'], s['lse'], do, dz1)
        hp = FOX_HEADS // dc_pairs.shape[0]
        dc = jnp.transpose(dc_pairs[:, :, :hp], (1, 0, 2)).reshape(T, FOX_HEADS)
        dzf, GS['b_forget'][l] = _fox_c_bwd(dc, s['zf'], row(P['b_forget'], l))
        segs = [(dzf, f_off), (dz1, 0), (dz2, hi_off)]
        g_in = r_in
        for dz, off in segs:
            g_in = _mm(dz, s['h1'], mode='tn', out=g_in, o_off=off, name='mmg_w_in')
        G['w_in'] = g_in
        late = exchange_start(G, _LATE, l, 'b', [])
        dh1 = None
        for n, (dz, off) in enumerate(segs):
            dh1 = _mm(dz, wt, mode='nn', b_rows=(off, dz.shape[1]), add=dh1, out_dtype=F32, name='mmb_in',
                      deps=[late[-1]] if n == 0 else ())
        dx, GS['mix_norm'][l] = _rms_bwd(s['x0'], row(P['mix_norm'], l), dh1, dx, 'rms_mix_bwd')
        for e in pending:
            exchange_finish(e, [dx])
        pending = [early, late]

    _, d_mem_norm = _rms_bwd(mem_x, P['mem_norm'][None, :], d_mem_n, None, 'rms_mem_bwd')

    small_parts = {w: jnp.concatenate(GS[w], axis=0) for w in SMALL if w not in ('mem_norm', 'final_norm')}
    small_parts['mem_norm'] = d_mem_norm
    small_parts['final_norm'] = d_final_norm
    conv_parts = [jnp.stack(G_conv[w])[:, :taps] for w, taps, _ in _CONVS]
    sizes = [P[w].size for w in SMALL]
    conv_sizes = [c.size for c in conv_parts]
    n_small = sum(sizes) + sum(conv_sizes) + LANES
    n_rows = -(-n_small // (8 * LANES)) * 8

    def pack(parts):
        flat = jnp.concatenate([p.reshape(-1) for p in parts])
        return jnp.pad(flat, (0, n_rows * LANES - flat.size)).reshape(n_rows, LANES)

    g_small = pack([small_parts[w] for w in SMALL] + conv_parts + [loss_part])
    small_plan = [(0, 0, 'slot', None)]
    small_land = _place_own([g_small], small_plan, [_sds((N_DEV,) + g_small.shape, F32)], 'small_own', [])
    small_sems, small_src, small_land, _ = _scatter_start([g_small], small_land, small_plan, [], 'small_start')
    early0, late0 = pending
    exchange_finish(early0, [small_land[0]])
    exchange_finish(late0, [results[w][0] for w in _EARLY])
    (all_small,) = _scatter_wait(small_sems, small_src, small_land, small_plan, [results['w_in'][0]], 'small_wait')
    sm = _adamw(all_small, pack([P[w] for w in SMALL]), pack([P['m_' + w] for w in SMALL]),
                pack([P['v_' + w] for w in SMALL]), 'adamw_small')
    g_all = sm[0].reshape(-1)
    loss = g_all[sum(sizes) + sum(conv_sizes)]
    final = {}
    off = 0
    for w, n in zip(SMALL, sizes):
        final[w] = [o.reshape(-1)[off:off + n].reshape(P[w].shape) for o in sm]
        off += n
    for (w, taps, _), c, n in zip(_CONVS, conv_parts, conv_sizes):
        n_col = P[w].shape[2]
        g_own = lax.dynamic_slice(g_all[off:off + n].reshape(c.shape), (0, 0, n_col * me), (L, taps, n_col))
        flat = lambda a: a.reshape(L * taps, n_col)
        outs = _adamw(flat(g_own)[None], flat(P[w]), flat(P['m_' + w]), flat(P['v_' + w]), 'adamw_' + w)
        final[w] = [o.reshape(P[w].shape) for o in outs]
        off += n

    for w in _BIG:
        final[w] = [_view(w, o) for o in results[w]]
    final['w_in'] = [jnp.transpose(_interleave_layers(o), (1, 2, 0)) for o in results['w_in']]

    out = [loss, dx[None]]
    for k in range(4):
        out += [final[w][k] for w in WEIGHTS]
    return tuple(out)
```

```python
import functools

import jax
import jax.numpy as jnp
from jax import lax
from jax.experimental import pallas as pl
from jax.experimental.pallas import tpu as pltpu

F32 = jnp.float32
BF16 = jnp.bfloat16
MESH = pl.DeviceIdType.MESH

N_DEV = 8
EPS = 1e-6
FOX_HEADS = 16
X_HEADS = 4
LANES = 128
ROW_TILE = 16
HALO_A = 16
HALO_C = 32
SUBLANES = 8
CONV_ROWS = 128
ROW_CHUNK = 32
CONV_A_W = 3
CONV_C_W = 31
VMEM_LIMIT = 56 << 20

ADAM_LR = 0.001
ADAM_B1 = 0.9
ADAM_B2 = 0.999
ADAM_EPS = 1e-08
ADAM_WD = 0.01
ADAM_STEP = 10

WEIGHTS = ['mix_norm', 'w_in', 'b_gate', 'b_forget', 'conv_a', 'w_out_a', 'w_out_b', 'conv_c', 'conv_c_bias',
           'ln_c_gain', 'ln_c_bias', 'w_out_c', 'w_o', 'xattn_norm', 'mem_norm', 'w_xq', 'w_xkv', 'w_xo',
           'ffn_norm', 'w_gate_up', 'w_down', 'final_norm']
SHARDED = ['w_in', 'conv_a', 'w_out_a', 'w_out_b', 'conv_c', 'w_out_c', 'w_o', 'w_xq', 'w_xkv', 'w_xo',
           'w_gate_up', 'w_down']
ROW_SHARDED = ['w_out_a', 'w_out_b', 'w_out_c', 'w_o', 'w_xq', 'w_xkv', 'w_down']
SMALL = [w for w in WEIGHTS if w not in SHARDED]
ARG_NAMES = (['x', 'mem'] + WEIGHTS + ['loss_target'] + ['m_' + w for w in WEIGHTS] + ['v_' + w for w in WEIGHTS])


def _pick(n, cands=(1024, 1408, 512, 256, 128)):
    for c in cands:
        if n % c == 0:
            return c
    return n


def _call(body, *, name, out_shape, grid=(), in_specs=None, out_specs=None, scratch=(), sem=None, aliases=None,
          deps=()):
    params = dict(vmem_limit_bytes=VMEM_LIMIT)
    if sem is not None:
        params['dimension_semantics'] = sem
    n_in, n_dep = len(in_specs), len(deps)

    def body_without_deps(*refs):
        body(*refs[:n_in], *refs[n_in + n_dep:])

    call = pl.pallas_call(body_without_deps, name=name, out_shape=out_shape, grid=grid, scratch_shapes=list(scratch),
                          in_specs=list(in_specs) + [pl.BlockSpec(memory_space=pl.ANY)] * n_dep, out_specs=out_specs,
                          input_output_aliases=aliases or {}, compiler_params=pltpu.CompilerParams(**params),
                          interpret=False)
    return lambda *args: call(*args, *deps)


def _sds(shape, dtype):
    return jax.ShapeDtypeStruct(tuple(shape), dtype)


def _sigmoid(x):
    return 1.0 / (1.0 + jnp.exp(-x))


_DN = {'nn': (((1,), (0,)), ((), ())), 'nt': (((1,), (1,)), ((), ())), 'tn': (((0,), (0,)), ((), ()))}


def _mm(a, b, *, mode, name, out_dtype=BF16, add=None, b_rows=None, out=None, o_off=0, deps=()):
    if mode == 'tn':
        K, M = a.shape
    else:
        M, K = a.shape
    b_off, b_n = (0, b.shape[0]) if b_rows is None else b_rows
    N = b_n if mode == 'nt' else b.shape[1]
    assert (b.shape[1] if mode == 'nt' else b_n) == K
    tm, tn = _pick(M), _pick(N)
    tk = K if K <= 2048 else _pick(K)
    ni, nj, nk = M // tm, N // tn, K // tk
    a_bytes, b_bytes = M * K * a.dtype.itemsize, K * N * b.dtype.itemsize
    n_outer = (b_bytes + nj * a_bytes) < (a_bytes + ni * b_bytes)
    if n_outer:
        grid = (nj, ni, nk)
        ij = lambda g0, g1: (g1, g0)
    else:
        grid = (ni, nj, nk)
        ij = lambda g0, g1: (g0, g1)

    def a_map(g0, g1, k):
        i, _ = ij(g0, g1)
        return (k, i) if mode == 'tn' else (i, k)

    def elems(blk):
        return tuple(pl.Element(s) for s in blk)

    def at(rows, cols):
        return pl.multiple_of(rows, ROW_TILE), pl.multiple_of(cols, LANES)

    a_blk = (tk, tm) if mode == 'tn' else (tm, tk)
    b_blk = (tn, tk) if mode == 'nt' else (tk, tn)
    if b_rows is None:
        def b_map(g0, g1, k):
            _, j = ij(g0, g1)
            return (j, k) if mode == 'nt' else (k, j)
        b_spec = pl.BlockSpec(b_blk, b_map)
    else:
        def b_map(g0, g1, k):
            _, j = ij(g0, g1)
            return at(b_off + j * tn, k * tk) if mode == 'nt' else at(b_off + k * tk, j * tn)
        b_spec = pl.BlockSpec(elems(b_blk), b_map)
    in_specs = [pl.BlockSpec(a_blk, a_map), b_spec]
    args = [a, b]
    has_add = add is not None
    if has_add:
        in_specs.append(pl.BlockSpec((tm, tn), lambda g0, g1, k: ij(g0, g1)))
        args.append(add)
    aliases = {}
    if out is None:
        out_shape = _sds((M, N), out_dtype)
        out_spec = pl.BlockSpec((tm, tn), lambda g0, g1, k: ij(g0, g1))
    else:
        if isinstance(out, int):
            out_shape = _sds((out, N), out_dtype)
        else:
            out_shape = _sds(out.shape, out.dtype)
            in_specs.append(pl.BlockSpec(memory_space=pl.ANY))
            aliases = {len(args): 0}
            args.append(out)

        def o_map(g0, g1, k):
            i, j = ij(g0, g1)
            return at(o_off + i * tm, j * tn)
        out_spec = pl.BlockSpec(elems((tm, tn)), o_map)
    dn = _DN[mode]
    n_in = len(args)

    def body(*refs):
        a_ref, b_ref = refs[0], refs[1]
        add_ref = refs[2] if has_add else None
        o_ref = refs[n_in]
        part = lax.dot_general(a_ref[...].astype(BF16), b_ref[...].astype(BF16), dn, preferred_element_type=F32)

        def finish(r):
            if has_add:
                r = r + add_ref[...]
            o_ref[...] = r.astype(o_ref.dtype)

        if nk == 1:
            finish(part)
        else:
            acc = refs[n_in + 1]
            k = pl.program_id(2)

            @pl.when(k == 0)
            def _():
                acc[...] = part

            @pl.when(k > 0)
            def _():
                acc[...] += part

            @pl.when(k == nk - 1)
            def _():
                finish(acc[...])

    scratch = [pltpu.VMEM((tm, tn), F32)] if nk > 1 else []
    return _call(body, name=name, out_shape=out_shape, grid=grid, in_specs=in_specs, out_specs=out_spec,
                 scratch=scratch, sem=('parallel', 'parallel', 'arbitrary'), aliases=aliases, deps=deps)(*args)


def _rms_fwd(x, g, name, deps=()):
    R, D = x.shape
    tb = _pick(R, (512, 256, 128))

    def body(x_ref, g_ref, o_ref):
        xv = x_ref[...]
        r = lax.rsqrt(jnp.mean(xv * xv, axis=-1, keepdims=True) + EPS)
        o_ref[...] = (xv * r * g_ref[...]).astype(o_ref.dtype)

    row = pl.BlockSpec((tb, D), lambda i: (i, 0))
    vec = pl.BlockSpec((1, D), lambda i: (0, 0))
    return _call(body, name=name, out_shape=_sds((R, D), BF16), grid=(R // tb,), in_specs=[row, vec],
                 out_specs=row, sem=('parallel',), deps=deps)(x, g)


def _rms_bwd(x, g, dh, dres, name, deps=()):
    R, D = x.shape
    tb = _pick(R, (512, 256, 128))
    has_res = dres is not None

    def body(*refs):
        x_ref, g_ref, dh_ref = refs[:3]
        dx_ref, dg_ref = refs[-2:]
        xv = x_ref[...]
        dhv = dh_ref[...].astype(F32)
        r = lax.rsqrt(jnp.mean(xv * xv, axis=-1, keepdims=True) + EPS)
        u = dhv * g_ref[...]
        m = jnp.mean(u * xv, axis=-1, keepdims=True)
        dx = r * u - xv * (r * r * r * m)
        if has_res:
            dx = dx + refs[3][...]
        dx_ref[...] = dx

        @pl.when(pl.program_id(0) == 0)
        def _():
            dg_ref[...] = jnp.zeros_like(dg_ref)

        dg_ref[...] += jnp.sum(dhv * xv * r, axis=0, keepdims=True)

    row = pl.BlockSpec((tb, D), lambda i: (i, 0))
    vec = pl.BlockSpec((1, D), lambda i: (0, 0))
    args = [x, g, dh] + ([dres] if has_res else [])
    return _call(body, name=name, out_shape=(_sds((R, D), F32), _sds((1, D), F32)), grid=(R // tb,),
                 in_specs=[row, vec, row] + ([row] if has_res else []), out_specs=(row, vec),
                 sem=('arbitrary',), deps=deps)(*args)


def _final(x, g, tgt):
    R, D = x.shape
    tb = _pick(R, (512, 256, 128))

    def body(x_ref, g_ref, t_ref, dx_ref, dg_ref, loss_ref):
        xv = x_ref[...]
        gv = g_ref[...]
        r = lax.rsqrt(jnp.mean(xv * xv, axis=-1, keepdims=True) + EPS)
        e = xv * r * gv - t_ref[...]
        row_loss = jnp.mean(e * e, axis=-1, keepdims=True)
        blk_loss = 0.5 * jnp.sum(row_loss, axis=0, keepdims=True)
        dy = e * (1.0 / D)
        u = dy * gv
        m = jnp.mean(u * xv, axis=-1, keepdims=True)
        dx_ref[...] = r * u - xv * (r * r * r * m)

        @pl.when(pl.program_id(0) == 0)
        def _():
            dg_ref[...] = jnp.zeros_like(dg_ref)
            loss_ref[...] = jnp.zeros_like(loss_ref)

        dg_ref[...] += jnp.sum(dy * xv * r, axis=0, keepdims=True)
        loss_ref[...] += jnp.broadcast_to(blk_loss, loss_ref.shape)

    row = pl.BlockSpec((tb, D), lambda i: (i, 0))
    vec = pl.BlockSpec((1, D), lambda i: (0, 0))
    one = pl.BlockSpec((1, LANES), lambda i: (0, 0))
    return _call(body, name='final_loss', out_shape=(_sds((R, D), F32), _sds((1, D), F32), _sds((1, LANES), F32)),
                 grid=(R // tb,), in_specs=[row, vec, row], out_specs=(row, vec, one), sem=('arbitrary',))(x, g, tgt)


def _col(tb, w, cb):
    return pl.BlockSpec((tb, w), lambda i: (i, cb))


def _prev(tb, h, w, cb):
    return pl.BlockSpec((h, w), lambda i: (jnp.maximum(i * (tb // h) - 1, 0), cb))


def _next(tb, h, w, cb, rows):
    return pl.BlockSpec((h, w), lambda i: (jnp.minimum((i + 1) * (tb // h), rows // h - 1), cb))


def _full(shape):
    return pl.BlockSpec(shape, lambda i: (0,) * len(shape))


def _mixa_fwd(z, w):
    T = z.shape[0]
    D = w.shape[1]
    tb = _pick(T, (256, 128))
    H = HALO_A

    def body(ab, ac, au, acp, aup, w_ref, o_ref, ext):
        first = pl.program_id(0) == 0
        ext[0:H, :] = jnp.where(first, 0.0, acp[...].astype(F32) * aup[...].astype(F32))
        ext[H:H + tb, :] = ac[...].astype(F32) * au[...].astype(F32)
        cp = jnp.zeros((tb, D), F32)
        for k in range(CONV_A_W):
            off = H - (CONV_A_W - 1) + k
            cp = cp + ext[off:off + tb, :] * w_ref[k:k + 1, :]
        o_ref[...] = (ab[...].astype(F32) * cp).astype(o_ref.dtype)

    return _call(body, name='mixa_fwd', out_shape=_sds((T, D), BF16), grid=(T // tb,),
                 in_specs=[_col(tb, D, 0), _col(tb, D, 1), _col(tb, D, 2), _prev(tb, H, D, 1), _prev(tb, H, D, 2),
                           _full((8, D))],
                 out_specs=_col(tb, D, 0), scratch=[pltpu.VMEM((H + tb, D), F32)], sem=('parallel',))(z, z, z, z, z, w)


def _mixa_bwd(z, w, dy):
    T = z.shape[0]
    D = w.shape[1]
    tb = _pick(T, (256, 128))
    H = HALO_A
    nb = T // tb

    def body(ab, ac, au, acp, aup, abn, dy_ref, dyn, w_ref, dz_ref, dw_ref, pext, dext):
        i = pl.program_id(0)
        a_b, a_c, a_u = ab[...].astype(F32), ac[...].astype(F32), au[...].astype(F32)
        pext[0:H, :] = jnp.where(i == 0, 0.0, acp[...].astype(F32) * aup[...].astype(F32))
        pext[H:H + tb, :] = a_c * a_u
        dyv = dy_ref[...].astype(F32)
        dcp = dyv * a_b
        dext[0:tb, :] = dcp
        dext[tb:tb + H, :] = jnp.where(i == nb - 1, 0.0, dyn[...].astype(F32) * abn[...].astype(F32))

        @pl.when(i == 0)
        def _():
            dw_ref[...] = jnp.zeros_like(dw_ref)

        cp = jnp.zeros((tb, D), F32)
        dp = jnp.zeros((tb, D), F32)
        for k in range(CONV_A_W):
            off = H - (CONV_A_W - 1) + k
            wk = w_ref[k:k + 1, :]
            pk = pext[off:off + tb, :]
            cp = cp + pk * wk
            dp = dp + dext[CONV_A_W - 1 - k:CONV_A_W - 1 - k + tb, :] * wk
            dw_ref[k:k + 1, :] += jnp.sum(dcp * pk, axis=0, keepdims=True)
        dz_ref[:, 0:D] = (dyv * cp).astype(dz_ref.dtype)
        dz_ref[:, D:2 * D] = (dp * a_u).astype(dz_ref.dtype)
        dz_ref[:, 2 * D:3 * D] = (dp * a_c).astype(dz_ref.dtype)

    return _call(body, name='mixa_bwd', out_shape=(_sds((T, 6 * D), BF16), _sds((8, D), F32)), grid=(nb,),
                 in_specs=[_col(tb, D, 0), _col(tb, D, 1), _col(tb, D, 2), _prev(tb, H, D, 1), _prev(tb, H, D, 2),
                           _next(tb, H, D, 0, T), _col(tb, D, 0), _next(tb, H, D, 0, T), _full((8, D))],
                 out_specs=(_col(tb, 3 * D, 0), _full((8, D))),
                 scratch=[pltpu.VMEM((H + tb, D), F32), pltpu.VMEM((tb + H, D), F32)],
                 sem=('arbitrary',))(z, z, z, z, z, z, dy, dy, w)


def _split3(v):
    hi = v.astype(BF16)
    r1 = v - hi.astype(F32)
    mid = r1.astype(BF16)
    lo = (r1 - mid.astype(F32)).astype(BF16)
    return hi, mid, lo


def _tri_dot(tri, v):
    hi, mid, lo = _split3(v)
    d = lambda p: jnp.dot(tri, p, preferred_element_type=F32)
    return d(hi) + d(mid) + d(lo)


def _fox_c(zf, bf):
    T, H = zf.shape
    tb = LANES

    def body(zf_ref, b_ref, c_ref, carry):
        @pl.when(pl.program_id(0) == 0)
        def _():
            carry[...] = jnp.zeros_like(carry)

        xv = zf_ref[...] + b_ref[...]
        lf = jnp.minimum(xv, 0.0) - jnp.log(1.0 + jnp.exp(-jnp.abs(xv)))
        r = lax.broadcasted_iota(jnp.int32, (tb, tb), 0)
        c = lax.broadcasted_iota(jnp.int32, (tb, tb), 1)
        tri = (r >= c).astype(BF16)
        cs = _tri_dot(tri, lf) + carry[...]
        c_ref[...] = cs
        carry[...] = cs[tb - 1:tb, :]

    blk = pl.BlockSpec((tb, H), lambda i: (i, 0))
    return _call(body, name='fox_cumsum', out_shape=_sds((T, H), F32), grid=(T // tb,),
                 in_specs=[blk, _full((1, H))], out_specs=blk, scratch=[pltpu.VMEM((1, H), F32)],
                 sem=('arbitrary',))(zf, bf)


def _fox_c_bwd(dc, zf, bf):
    T, H = zf.shape
    tb = LANES
    nb = T // tb

    def body(dc_ref, zf_ref, b_ref, dz_ref, db_ref, carry):
        @pl.when(pl.program_id(0) == 0)
        def _():
            carry[...] = jnp.zeros_like(carry)
            db_ref[...] = jnp.zeros_like(db_ref)

        r = lax.broadcasted_iota(jnp.int32, (tb, tb), 0)
        c = lax.broadcasted_iota(jnp.int32, (tb, tb), 1)
        tri = (c >= r).astype(BF16)
        dlf = _tri_dot(tri, dc_ref[...]) + carry[...]
        carry[...] = dlf[0:1, :]
        xv = zf_ref[...] + b_ref[...]
        dz = dlf * _sigmoid(-xv)
        dz_ref[...] = dz
        db_ref[...] += jnp.sum(dz, axis=0, keepdims=True)

    blk = pl.BlockSpec((tb, H), lambda i: (nb - 1 - i, 0))
    return _call(body, name='fox_cumsum_bwd', out_shape=(_sds((T, H), F32), _sds((1, H), F32)), grid=(nb,),
                 in_specs=[blk, blk, _full((1, H))], out_specs=(blk, _full((1, H))),
                 scratch=[pltpu.VMEM((1, H), F32)], sem=('arbitrary',))(dc, zf, bf)


N_PIECES = 3


def _head_col(c_ref, h):
    lane = lax.broadcasted_iota(jnp.int32, c_ref.shape, 1)
    return jnp.sum(jnp.where(lane == h, c_ref[...], 0.0), axis=1, keepdims=True)


def _pieces(v, sign=1.0):
    return [sign * p.astype(F32) for p in _split3(v)]


def _augment(block, hh, dh, a_cols, b_cols):
    lane = lax.broadcasted_iota(jnp.int32, block.shape, 1)
    base = (1 - hh) * dh
    out = jnp.where((lane >= hh * dh) & (lane < (hh + 1) * dh), block.astype(F32), 0.0)
    for first, cols in ((base, list(a_cols)), (base + N_PIECES, list(b_cols))):
        if cols and all(isinstance(c, float) and c == cols[0] for c in cols):
            out = jnp.where((lane >= first) & (lane < first + len(cols)), cols[0], out)
        else:
            for n, col in enumerate(cols):
                out = jnp.where(lane == first + n, col, out)
    return out.astype(BF16)


def _fox_probs(q_aug, k_aug, tri, q0, ke, shift=None):
    s = lax.dot_general(q_aug[q0:ke, :], k_aug[0:ke, :], _DN['nt'], preferred_element_type=F32)
    s_dg = jnp.where(tri, s[:, q0:ke], -jnp.inf)
    if shift is None:
        shift = jnp.max(s_dg, axis=-1, keepdims=True)
        if q0:
            shift = jnp.maximum(shift, jnp.max(s[:, 0:q0], axis=-1, keepdims=True))
    parts = ([jnp.exp(s[:, 0:q0] - shift)] if q0 else []) + [jnp.exp(s_dg - shift)]
    return parts, shift


def _cat(parts):
    parts = [p.astype(BF16) for p in parts]
    return parts[0] if len(parts) == 1 else jnp.concatenate(parts, axis=1)


def _fox_fwd(z, c):
    T = z.shape[0]
    D = z.shape[1] // 6
    dh = D // FOX_HEADS
    hp = LANES // dh
    ncb = D // LANES
    tq = _pick(T, (256, 128))
    scale = dh ** -0.5
    ones = [1.0] * N_PIECES

    def body(q_ref, k_ref, v_ref, c_ref, o_ref, lse_ref, o_all):
        j = pl.program_id(0)
        lse_ref[...] = jnp.zeros_like(lse_ref)
        r = lax.broadcasted_iota(jnp.int32, (tq, tq), 0)
        tri = r >= lax.broadcasted_iota(jnp.int32, (tq, tq), 1)
        for hh in range(hp):
            base = (1 - hh) * dh
            ch = _head_col(c_ref, j * hp + hh)
            q_aug = _augment(q_ref[...].astype(F32) * scale, hh, dh, _pieces(ch), ones)
            k_aug = _augment(k_ref[...], hh, dh, ones, _pieces(ch, -1.0))
            v_aug = _augment(v_ref[...], hh, dh, ones, [])
            for q0 in range(0, T, tq):
                ke = q0 + tq
                parts, m = _fox_probs(q_aug, k_aug, tri, q0, ke)
                o_aug = jnp.dot(_cat(parts), v_aug[0:ke, :], preferred_element_type=F32)
                l = o_aug[:, base:base + 1]
                o_all[hh, q0:ke, :] = o_aug * (1.0 / l)
                lse_ref[q0:ke, hh:hh + 1] = m + jnp.log(l)
        lane = lax.broadcasted_iota(jnp.int32, (T, LANES), 1)
        out = o_all[0]
        for hh in range(1, hp):
            out = jnp.where(lane >= hh * dh, o_all[hh], out)
        o_ref[...] = out.astype(o_ref.dtype)

    blk = lambda cb0: pl.BlockSpec((T, LANES), lambda j: (0, cb0 + j))
    return _call(body, name='fox_fwd', out_shape=(_sds((T, D), BF16), _sds((ncb, T, LANES), F32)), grid=(ncb,),
                 in_specs=[blk(3 * ncb), blk(4 * ncb), blk(5 * ncb), _full((T, FOX_HEADS))],
                 out_specs=(blk(0), pl.BlockSpec((None, T, LANES), lambda j: (j, 0, 0))),
                 scratch=[pltpu.VMEM((hp, T, LANES), F32)], sem=('parallel',))(z, z, z, c)


def _fox_bwd(z, c, o, lse, do, dz1):
    T = z.shape[0]
    D = o.shape[1]
    dh = D // FOX_HEADS
    hp = LANES // dh
    ncb = D // LANES
    tq = _pick(T, (256, 128))
    scale = dh ** -0.5
    ones = [1.0] * N_PIECES

    def body(q_ref, k_ref, v_ref, c_ref, o_ref, lse_ref, do_ref, _, dz_ref, dc_ref, dq_all, dk_acc, dv_acc, stage,
             sems):
        j = pl.program_id(0)
        dk_acc[...] = jnp.zeros_like(dk_acc)
        dv_acc[...] = jnp.zeros_like(dv_acc)
        dc_ref[...] = jnp.zeros_like(dc_ref)
        r = lax.broadcasted_iota(jnp.int32, (tq, tq), 0)
        tri = r >= lax.broadcasted_iota(jnp.int32, (tq, tq), 1)
        lane = lax.broadcasted_iota(jnp.int32, (T, LANES), 1)
        for hh in range(hp):
            base = (1 - hh) * dh
            own = (lane >= hh * dh) & (lane < (hh + 1) * dh)
            ch = _head_col(c_ref, j * hp + hh)
            dov = do_ref[...].astype(F32)
            delta = jnp.sum(jnp.where(own, dov * o_ref[...].astype(F32), 0.0), axis=1, keepdims=True)
            q_aug = _augment(q_ref[...].astype(F32) * scale, hh, dh, _pieces(ch - lse_ref[:, hh:hh + 1]), ones)
            k_aug = _augment(k_ref[...], hh, dh, ones, _pieces(ch, -1.0))
            v_aug = _augment(v_ref[...], hh, dh, ones, [])
            do_aug = _augment(dov, hh, dh, _pieces(delta, -1.0), [])
            for q0 in range(0, T, tq):
                ke = q0 + tq
                p, _ = _fox_probs(q_aug, k_aug, tri, q0, ke, shift=0.0)
                dp = lax.dot_general(do_aug[q0:ke, :], v_aug[0:ke, :], _DN['nt'], preferred_element_type=F32)
                ds = [p[0] * dp[:, 0:q0], p[1] * dp[:, q0:ke]] if q0 else [p[0] * dp]
                dsb, pb = _cat(ds), _cat(p)
                dq_all[hh, q0:ke, :] = jnp.dot(dsb, k_aug[0:ke, :], preferred_element_type=F32)
                dk_acc[hh, 0:ke, :] += lax.dot_general(dsb, q_aug[q0:ke, :], _DN['tn'], preferred_element_type=F32)
                dv_acc[hh, 0:ke, :] += lax.dot_general(pb, do_aug[q0:ke, :], _DN['tn'], preferred_element_type=F32)
            dc_ref[:, hh:hh + 1] = dq_all[hh, :, base:base + 1] - dk_acc[hh, :, base + N_PIECES:base + N_PIECES + 1]
        dq, dk, dv = dq_all[0], dk_acc[0], dv_acc[0]
        for hh in range(1, hp):
            dq = jnp.where(lane >= hh * dh, dq_all[hh], dq)
            dk = jnp.where(lane >= hh * dh, dk_acc[hh], dk)
            dv = jnp.where(lane >= hh * dh, dv_acc[hh], dv)
        copies = []
        for n, g in enumerate((dq * scale, dk, dv)):
            stage[n] = g.astype(stage.dtype)
            col = pl.multiple_of(((3 + n) * ncb + j) * LANES, LANES)
            copies.append(pltpu.make_async_copy(stage.at[n], dz_ref.at[:, pl.ds(col, LANES)], sems.at[n]))
            copies[n].start()
        for cp in copies:
            cp.wait()

    blk = lambda cb0: pl.BlockSpec((T, LANES), lambda j: (0, cb0 + j))
    pair = pl.BlockSpec((None, T, LANES), lambda j: (j, 0, 0))
    acc = pltpu.VMEM((hp, T, LANES), F32)
    anywhere = pl.BlockSpec(memory_space=pl.ANY)
    return _call(body, name='fox_bwd', out_shape=(_sds(dz1.shape, dz1.dtype), _sds((ncb, T, LANES), F32)), grid=(ncb,),
                 in_specs=[blk(3 * ncb), blk(4 * ncb), blk(5 * ncb), _full((T, FOX_HEADS)), blk(0), pair, blk(0),
                           anywhere],
                 out_specs=(anywhere, pair),
                 scratch=[acc, acc, acc, pltpu.VMEM((3, T, LANES), BF16), pltpu.SemaphoreType.DMA((3,))],
                 sem=('arbitrary',), aliases={7: 0})(z, z, z, c, o, lse, do, dz1)


def _ln_parts(u1, gain, bias):
    mu = jnp.mean(u1, axis=-1, keepdims=True)
    xc = u1 - mu
    rstd = lax.rsqrt(jnp.mean(xc * xc, axis=-1, keepdims=True) + EPS)
    xhat = xc * rstd
    return xhat, rstd, xhat * gain + bias


def _tap_groups(offsets):
    groups = {}
    for k, off in enumerate(offsets):
        groups.setdefault(off % SUBLANES, []).append((off - off % SUBLANES, k))
    return groups


def _for_taps(src_ref, r0, cols, groups, aligned, visit):
    for r, taps in groups.items():
        if r == 0:
            for base, k in taps:
                visit(k, src_ref[r0 + base:r0 + base + CONV_ROWS, cols])
            continue
        need = max(base for base, _ in taps) + CONV_ROWS
        aligned[r, 0:need, :] = src_ref[r0 + r:r0 + r + need, cols]
        for base, k in taps:
            visit(k, aligned[r, base:base + CONV_ROWS, :])


def _row_chunk(j, first=0):
    return slice(first + j * ROW_CHUNK, first + (j + 1) * ROW_CHUNK)


def _glu_rows(cv, cg, ext, halo, tb):
    for j in range(tb // ROW_CHUNK):
        rs = _row_chunk(j)
        ext[_row_chunk(j, halo), :] = cv[rs, :].astype(F32) * _sigmoid(cg[rs, :].astype(F32))


def _mixc_fwd(z, w, cb, gain, bias):
    T = z.shape[0]
    D = w.shape[1]
    tb = _pick(T, (256, 128))
    H = HALO_C

    groups = _tap_groups([H - (CONV_C_W - 1) + k for k in range(CONV_C_W)])

    def body(cv, cg, cvp, cgp, w_ref, cb_ref, g_ref, b_ref, o_ref, u1_ref, ext, aligned):
        first = pl.program_id(0) == 0
        ext[0:H, :] = jnp.where(first, 0.0, cvp[...].astype(F32) * _sigmoid(cgp[...].astype(F32)))
        _glu_rows(cv, cg, ext, H, tb)
        for r0 in range(0, tb, CONV_ROWS):
            for c0 in range(0, D, LANES):
                cols = slice(c0, c0 + LANES)
                acc = [jnp.zeros((CONV_ROWS, LANES), F32)]

                def tap(k, rows):
                    acc[0] = acc[0] + rows * w_ref[k:k + 1, cols]

                _for_taps(ext, r0, cols, groups, aligned, tap)
                u1_ref[r0:r0 + CONV_ROWS, cols] = acc[0] + cb_ref[:, cols]

        for j in range(tb // ROW_CHUNK):
            rs = _row_chunk(j)
            _, _, u2 = _ln_parts(u1_ref[rs, :], g_ref[...], b_ref[...])
            o_ref[rs, :] = (u2 * _sigmoid(u2)).astype(o_ref.dtype)

    vec = _full((1, D))
    row = _col(tb, D, 0)
    return _call(body, name='mixc_fwd', out_shape=(_sds((T, D), BF16), _sds((T, D), F32)), grid=(T // tb,),
                 in_specs=[_col(tb, D, 0), _col(tb, D, 1), _prev(tb, H, D, 0), _prev(tb, H, D, 1), _full((32, D)), vec,
                           vec, vec],
                 out_specs=(row, row), scratch=[pltpu.VMEM((H + tb, D), F32), pltpu.VMEM((SUBLANES, CONV_ROWS + H, LANES), F32)],
                 sem=('parallel',))(z, z, z, z, w, cb, gain, bias)


def _mixc_bwd(z, w, gain, bias, u1, du3, dz2):
    T = z.shape[0]
    D = w.shape[1]
    tb = _pick(T, (256, 128))
    H = HALO_C
    nb = T // tb

    def du1_of(u1v, du3v, gv, bv):
        xhat, rstd, u2 = _ln_parts(u1v, gv, bv)
        sg = _sigmoid(u2)
        du2 = du3v * (sg * (1.0 + u2 * (1.0 - sg)))
        dxh = du2 * gv
        m1 = jnp.mean(dxh, axis=-1, keepdims=True)
        m2 = jnp.mean(dxh * xhat, axis=-1, keepdims=True)
        return rstd * (dxh - m1 - xhat * m2), du2, xhat

    fwd_groups = _tap_groups([H - (CONV_C_W - 1) + k for k in range(CONV_C_W)])
    bwd_groups = _tap_groups([CONV_C_W - 1 - k for k in range(CONV_C_W)])

    def body(cv, cg, cvp, cgp, w_ref, g_ref, b_ref, u1_ref, u1n, du3_ref, du3n, _, dz_ref, dw_ref, dcb_ref, dg_ref,
             db_ref, uext, dext, dw_acc, aligned, sums):
        i = pl.program_id(0)
        uext[0:H, :] = jnp.where(i == 0, 0.0, cvp[...].astype(F32) * _sigmoid(cgp[...].astype(F32)))
        _glu_rows(cv, cg, uext, H, tb)
        du1n, _, _ = du1_of(u1n[...], du3n[...].astype(F32), g_ref[...], b_ref[...])
        dext[tb:tb + H, :] = jnp.where(i == nb - 1, 0.0, du1n)

        @pl.when(i == 0)
        def _():
            dw_acc[...] = jnp.zeros_like(dw_acc)
            sums[...] = jnp.zeros_like(sums)

        def halves(v):
            part = v[0:SUBLANES, :]
            for q in range(SUBLANES, ROW_CHUNK, SUBLANES):
                part = part + v[q:q + SUBLANES, :]
            return part

        for j in range(tb // ROW_CHUNK):
            rs = _row_chunk(j)
            du1, du2, xhat = du1_of(u1_ref[rs, :], du3_ref[rs, :].astype(F32), g_ref[...], b_ref[...])
            dext[rs, :] = du1
            sums[0:SUBLANES, :] += halves(du1)
            sums[SUBLANES:2 * SUBLANES, :] += halves(du2 * xhat)
            sums[2 * SUBLANES:3 * SUBLANES, :] += halves(du2)
        for r0 in range(0, tb, CONV_ROWS):
            rows = slice(r0, r0 + CONV_ROWS)
            for c0 in range(0, D, LANES):
                cols = slice(c0, c0 + LANES)
                acc = [jnp.zeros((CONV_ROWS, LANES), F32)]

                def tap(k, src):
                    acc[0] = acc[0] + src * w_ref[k:k + 1, cols]

                _for_taps(dext, r0, cols, bwd_groups, aligned, tap)
                c_val = cv[rows, cols].astype(F32)
                sg = _sigmoid(cg[rows, cols].astype(F32))
                dz_ref[rows, cols] = (acc[0] * sg).astype(dz_ref.dtype)
                dz_ref[rows, D + c0:D + c0 + LANES] = (acc[0] * c_val * sg * (1.0 - sg)).astype(dz_ref.dtype)
                du1_blk = dext[rows, cols]

                def tap_w(k, src):
                    prod = du1_blk * src
                    part = prod[0:SUBLANES, :]
                    for j in range(SUBLANES, CONV_ROWS, SUBLANES):
                        part = part + prod[j:j + SUBLANES, :]
                    dw_acc[k * SUBLANES:(k + 1) * SUBLANES, cols] += part

                _for_taps(uext, r0, cols, fwd_groups, aligned, tap_w)

        @pl.when(i == nb - 1)
        def _():
            dw_ref[...] = jnp.zeros_like(dw_ref)
            for k in range(CONV_C_W):
                dw_ref[k:k + 1, :] = jnp.sum(dw_acc[k * SUBLANES:(k + 1) * SUBLANES, :], axis=0, keepdims=True)
            for n, ref in enumerate((dcb_ref, dg_ref, db_ref)):
                ref[...] = jnp.sum(sums[n * SUBLANES:(n + 1) * SUBLANES, :], axis=0, keepdims=True)

    vec = _full((1, D))
    row = _col(tb, D, 0)
    nxt = _next(tb, H, D, 0, T)
    return _call(body, name='mixc_bwd',
                 out_shape=(_sds(dz2.shape, dz2.dtype), _sds((32, D), F32), _sds((1, D), F32), _sds((1, D), F32),
                            _sds((1, D), F32)), grid=(nb,),
                 in_specs=[_col(tb, D, 0), _col(tb, D, 1), _prev(tb, H, D, 0), _prev(tb, H, D, 1), _full((32, D)), vec,
                           vec, row, nxt, row, nxt, pl.BlockSpec(memory_space=pl.ANY)],
                 out_specs=(_col(tb, 2 * D, 0), _full((32, D)), vec, vec, vec),
                 scratch=[pltpu.VMEM((H + tb, D), F32), pltpu.VMEM((tb + H, D), F32),
                          pltpu.VMEM((32 * SUBLANES, D), F32), pltpu.VMEM((SUBLANES, CONV_ROWS + H, LANES), F32),
                          pltpu.VMEM((3 * SUBLANES, D), F32)],
                 sem=('arbitrary',), aliases={11: 0})(z, z, z, z, w, gain, bias, u1, u1, du3, du3, dz2)


def _gate_fwd(z, bg, ya, yb, yc):
    T = z.shape[0]
    D = ya.shape[1]
    tb = _pick(T, (256, 128))

    def body(ga, gb, gc, bg_ref, ya_ref, yb_ref, yc_ref, o_ref):
        acc = jnp.zeros((tb, D), F32)
        for n, (g, y) in enumerate(((ga, ya_ref), (gb, yb_ref), (gc, yc_ref))):
            acc = acc + _sigmoid(g[...].astype(F32) + bg_ref[:, n * D:(n + 1) * D]) * y[...].astype(F32)
        o_ref[...] = acc.astype(o_ref.dtype)

    row = _col(tb, D, 0)
    return _call(body, name='gate_fwd', out_shape=_sds((T, D), BF16), grid=(T // tb,),
                 in_specs=[_col(tb, D, 2), _col(tb, D, 3), _col(tb, D, 4), _full((1, 3 * D)), row, row, row],
                 out_specs=row, sem=('parallel',))(z, z, z, bg, ya, yb, yc)


def _gate_bwd(z, bg, ya, yb, yc, dm):
    T = z.shape[0]
    D = ya.shape[1]
    tb = _pick(T, (256, 128))

    def body(ga, gb, gc, bg_ref, ya_ref, yb_ref, yc_ref, dm_ref, dya, dyb, dyc, dg_ref, dbg_ref):
        @pl.when(pl.program_id(0) == 0)
        def _():
            dbg_ref[...] = jnp.zeros_like(dbg_ref)

        dmv = dm_ref[...]
        for n, (g, y, dy) in enumerate(((ga, ya_ref, dya), (gb, yb_ref, dyb), (gc, yc_ref, dyc))):
            cols = slice(n * D, (n + 1) * D)
            sg = _sigmoid(g[...].astype(F32) + bg_ref[:, cols])
            dy[...] = (dmv * sg).astype(dy.dtype)
            dg = dmv * y[...].astype(F32) * sg * (1.0 - sg)
            dg_ref[:, 2 * D + n * D:2 * D + (n + 1) * D] = dg.astype(dg_ref.dtype)
            dbg_ref[:, cols] += jnp.sum(dg, axis=0, keepdims=True)

    row = _col(tb, D, 0)
    act = _sds((T, D), BF16)
    return _call(body, name='gate_bwd', out_shape=(act, act, act, _sds((T, 5 * D), BF16), _sds((1, 3 * D), F32)),
                 grid=(T // tb,),
                 in_specs=[_col(tb, D, 2), _col(tb, D, 3), _col(tb, D, 4), _full((1, 3 * D)), row, row, row, row],
                 out_specs=(row, row, row, _col(tb, 5 * D, 0), _full((1, 3 * D))),
                 sem=('arbitrary',))(z, z, z, bg, ya, yb, yc, dm)


def _xattn_probs(qs, ks, scale):
    s = lax.dot_general(qs, ks, _DN['nt'], preferred_element_type=F32) * scale
    p = jnp.exp(s - jnp.max(s, axis=-1, keepdims=True))
    return p * (1.0 / jnp.sum(p, axis=-1, keepdims=True))


def _xattn_fwd(q, kv):
    T, DX = q.shape
    M = kv.shape[0]
    dh = DX // X_HEADS
    tb = _pick(T, (512, 256, 128))
    scale = dh ** -0.5

    def body(q_ref, kv_ref, o_ref):
        for h in range(X_HEADS):
            hs = slice(h * dh, (h + 1) * dh)
            p = _xattn_probs(q_ref[:, hs], kv_ref[:, hs], scale)
            o_ref[:, hs] = jnp.dot(p.astype(BF16), kv_ref[:, DX + h * dh:DX + (h + 1) * dh],
                                   preferred_element_type=F32).astype(o_ref.dtype)

    row = _col(tb, DX, 0)
    return _call(body, name='xattn_fwd', out_shape=_sds((T, DX), BF16), grid=(T // tb,),
                 in_specs=[row, _full((M, 2 * DX))], out_specs=row, sem=('parallel',))(q, kv)


def _xattn_bwd(q, kv, do):
    T, DX = q.shape
    M = kv.shape[0]
    dh = DX // X_HEADS
    tb = _pick(T, (512, 256, 128))
    scale = dh ** -0.5

    def body(q_ref, kv_ref, do_ref, dq_ref, dkv_ref):
        @pl.when(pl.program_id(0) == 0)
        def _():
            dkv_ref[...] = jnp.zeros_like(dkv_ref)

        for h in range(X_HEADS):
            hs = slice(h * dh, (h + 1) * dh)
            vs_cols = slice(DX + h * dh, DX + (h + 1) * dh)
            qs, ks, vs = q_ref[:, hs], kv_ref[:, hs], kv_ref[:, vs_cols]
            dos = do_ref[:, hs].astype(BF16)
            p = _xattn_probs(qs, ks, scale)
            dp = lax.dot_general(dos, vs, _DN['nt'], preferred_element_type=F32)
            ds = p * (dp - jnp.sum(p * dp, axis=-1, keepdims=True))
            dsb = ds.astype(BF16)
            dq_ref[:, hs] = (jnp.dot(dsb, ks, preferred_element_type=F32) * scale).astype(dq_ref.dtype)
            dkv_ref[:, hs] += lax.dot_general(dsb, qs, _DN['tn'], preferred_element_type=F32) * scale
            dkv_ref[:, vs_cols] += lax.dot_general(p.astype(BF16), dos, _DN['tn'], preferred_element_type=F32)

    row = _col(tb, DX, 0)
    return _call(body, name='xattn_bwd', out_shape=(_sds((T, DX), BF16), _sds((M, 2 * DX), F32)), grid=(T // tb,),
                 in_specs=[row, _full((M, 2 * DX)), row], out_specs=(row, _full((M, 2 * DX))),
                 sem=('arbitrary',))(q, kv, do)


def _swiglu_fwd(gu):
    T, F2 = gu.shape
    F = F2 // 2
    tb = _pick(T, (256, 128))

    def body(g_ref, u_ref, o_ref):
        g = g_ref[...].astype(F32)
        o_ref[...] = (g * _sigmoid(g) * u_ref[...].astype(F32)).astype(o_ref.dtype)

    return _call(body, name='swiglu_fwd', out_shape=_sds((T, F), BF16), grid=(T // tb,),
                 in_specs=[_col(tb, F, 0), _col(tb, F, 1)], out_specs=_col(tb, F, 0), sem=('parallel',))(gu, gu)


def _swiglu_bwd(gu, da):
    T, F2 = gu.shape
    F = F2 // 2
    tb = _pick(T, (256, 128))

    def body(g_ref, u_ref, da_ref, o_ref):
        g = g_ref[...].astype(F32)
        u = u_ref[...].astype(F32)
        dav = da_ref[...].astype(F32)
        sg = _sigmoid(g)
        o_ref[:, 0:F] = (dav * u * (sg * (1.0 + g * (1.0 - sg)))).astype(o_ref.dtype)
        o_ref[:, F:F2] = (dav * g * sg).astype(o_ref.dtype)

    return _call(body, name='swiglu_bwd', out_shape=_sds((T, F2), BF16), grid=(T // tb,),
                 in_specs=[_col(tb, F, 0), _col(tb, F, 1), _col(tb, F, 0)], out_specs=_col(tb, F2, 0),
                 sem=('parallel',))(gu, gu, da)


def _place():
    x, y, c = lax.axis_index('x'), lax.axis_index('y'), lax.axis_index('c')
    return x, y, c


_PEER_ORDER = (1, 4, 2, 5, 3, 6, 7)
_HBM = pl.BlockSpec(memory_space=pltpu.HBM)
_SEM = pl.BlockSpec(memory_space=pltpu.SEMAPHORE)


def _my_index():
    x, y, c = _place()
    return 4 * x + 2 * y + c


def _remote(src, dst, send_sems, recv_sems, k, dev):
    return pltpu.make_async_remote_copy(src_ref=src, dst_ref=dst, send_sem=send_sems.at[k], recv_sem=recv_sems.at[k],
                                        device_id=dev, device_id_type=MESH)


def _own_part(src, kind, args):
    return src if args is None else src.at[pl.ds(args[0], args[1])]


def _block(land, kind, args, owner):
    if kind == 'rows':
        _, n, stride, d0 = args
        return land.at[pl.ds(pl.multiple_of(stride * owner + d0, ROW_TILE), n)]
    return land.at[owner]


def _chunk(src, args, dest):
    stride, n = args
    return src.at[pl.ds(pl.multiple_of(stride * dest, ROW_TILE), n)]


def _scatter_copies(plan, srcs, lands, send_sems, recv_sems):
    x, y, c = _place()
    me = 4 * x + 2 * y + c
    copies = []
    for r in _PEER_ORDER:
        px, py, pc = x ^ (r >> 2), y ^ ((r >> 1) & 1), c ^ (r & 1)
        for si, li, kind, args in plan:
            src = srcs[si] if kind == 'slot' else _chunk(srcs[si], args, 4 * px + 2 * py + pc)
            copies.append(_remote(src, lands[li].at[me], send_sems, recv_sems, r - 1, (px, py, pc)))
    return copies


def _gather_copies(plan, srcs, lands, send_sems, recv_sems, passing):
    x, y, c = _place()
    me = 4 * x + 2 * y + c
    sibling = (x, y, 1 - c)
    chips = [(1 - x, y), (x, 1 - y), (1 - x, 1 - y)]
    copies = []
    if passing:
        for k, chip in enumerate(chips):
            for _, li, kind, args in plan:
                blk = _block(lands[li], kind, args, 4 * chip[0] + 2 * chip[1] + c)
                copies.append((k, _remote(blk, blk, send_sems, recv_sems, k, sibling)))
    else:
        for k, dev in enumerate([sibling] + [(*chip, c) for chip in chips]):
            for si, li, kind, args in plan:
                dst = _block(lands[li], kind, args, me)
                src = dst if srcs is None else _own_part(srcs[si], kind, args)
                copies.append((k, _remote(src, dst, send_sems, recv_sems, k, dev)))
    return copies


def _sem_call(name, bufs, sems_in, new_sems, after, action):
    nb, ni, nn = len(bufs), len(sems_in), len(new_sems)

    def body(*refs):
        action(refs[:nb], refs[nb:nb + ni], refs[-1 - nb - nn:-1 - nb])
        refs[-1][...] = jnp.zeros_like(refs[-1])

    bufs = [pltpu.with_memory_space_constraint(v, pltpu.HBM) for v in bufs]
    res = pl.pallas_call(
        body, name=name,
        out_shape=(*[pltpu.SemaphoreType.DMA((n,)) for n in new_sems], *[pltpu.HBM(v.shape, v.dtype) for v in bufs],
                   _sds((8, LANES), F32)),
        in_specs=[_HBM] * nb + [_SEM] * ni + [pl.BlockSpec(memory_space=pl.ANY)] * len(after),
        out_specs=(*[_SEM] * nn, *[_HBM] * nb, pl.BlockSpec(memory_space=pltpu.VMEM)),
        input_output_aliases={i: nn + i for i in range(nb)},
        compiler_params=pltpu.CompilerParams(has_side_effects=pltpu.SideEffectType.DATAFLOW_SIDE_EFFECTING),
        interpret=False)(*bufs, *sems_in, *after)
    return list(res[:nn]), list(res[nn:nn + nb]), res[-1]


def _place_own(srcs, plan, lands_like, name, deps):
    ns, nl, ne = len(srcs), len(lands_like), len(plan)

    def part_shape(entry):
        si, _, kind, args = entry
        rows = args[1] if (kind == 'chunk' or args is not None) else None
        return srcs[si].shape if rows is None else (rows,) + srcs[si].shape[1:]

    def body(*refs):
        src_r, land_r = refs[:ns], refs[ns:ns + nl]
        stage, sems_in, sems_out = refs[ns + nl:ns + nl + ne], refs[-2], refs[-1]
        me = _my_index()
        loads, stores = [], []
        for i, (si, li, kind, args) in enumerate(plan):
            if kind == 'chunk':
                src, dst = _chunk(src_r[si], args, me), land_r[li].at[me]
            else:
                src, dst = _own_part(src_r[si], kind, args), _block(land_r[li], kind, args, me)
            loads.append(pltpu.make_async_copy(src, stage[i], sems_in.at[i]))
            stores.append(pltpu.make_async_copy(stage[i], dst, sems_out.at[i]))
        for cp in loads:
            cp.start()
        for load, store in zip(loads, stores):
            load.wait()
            store.start()
        for cp in stores:
            cp.wait()

    anywhere = pl.BlockSpec(memory_space=pl.ANY)
    scratch = [pltpu.VMEM(part_shape(e), srcs[e[0]].dtype) for e in plan]
    scratch += [pltpu.SemaphoreType.DMA((ne,)), pltpu.SemaphoreType.DMA((ne,))]
    return list(_call(body, name=name, out_shape=tuple(lands_like), in_specs=[anywhere] * ns,
                      out_specs=tuple([anywhere] * nl), scratch=scratch, deps=deps)(*srcs))


def _scatter_start(srcs, lands, plan, after, name):
    ns = len(srcs)

    def action(bufs, _, new):
        for cp in _scatter_copies(plan, bufs[:ns], bufs[ns:], new[0], new[1]):
            cp.start()

    sems, bufs, token = _sem_call(name, [*srcs, *lands], [], [N_DEV - 1, N_DEV - 1], after, action)
    return sems, bufs[:ns], bufs[ns:], token


def _scatter_wait(sems, srcs, lands, plan, after, name):
    ns = len(srcs)

    def action(bufs, old, _):
        copies = _scatter_copies(plan, bufs[:ns], bufs[ns:], old[0], old[1])
        for cp in copies:
            cp.wait_send()
        for cp in copies:
            cp.wait_recv()

    return _sem_call(name, [*srcs, *lands], sems, [], after, action)[1][ns:]


def _gather_start(srcs, lands, plan, name):
    ns = len(srcs)

    def action(bufs, _, new):
        for _, cp in _gather_copies(plan, bufs[:ns], bufs[ns:], new[0], new[1], False):
            cp.start()

    sems, bufs, token = _sem_call(name, [*srcs, *lands], [], [4, 4], [], action)
    return sems, bufs[:ns], bufs[ns:], token


def _gather_pass(recv_sems, lands, plan, after, name):
    def action(bufs, old, new):
        for _, cp in _gather_copies(plan, None, bufs, old[0], old[0], False):
            cp.wait_recv()
        for _, cp in _gather_copies(plan, None, bufs, new[0], new[1], True):
            cp.start()

    return _sem_call(name, lands, [recv_sems], [3, 3], after, action)


def _gather_wait(send_sems, pass_sems, srcs, lands, plan, after, name):
    ns = len(srcs)

    def action(bufs, old, _):
        for _, cp in _gather_copies(plan, bufs[:ns], bufs[ns:], old[0], old[0], False):
            cp.wait_send()
        passed = _gather_copies(plan, None, bufs[ns:], old[1], old[2], True)
        for _, cp in passed:
            cp.wait_send()
        for _, cp in passed:
            cp.wait_recv()

    return _sem_call(name, [*srcs, *lands], [send_sems, *pass_sems], [], after, action)[1][ns:]


def _merge_edges(wt, edges, tiles_per_dev):
    D = wt.shape[1]

    def body(w_ref, e_ref, o_ref):
        s = pl.program_id(0)
        o_ref[...] = jnp.where(s == 0, e_ref[...], w_ref[...] + e_ref[...])

    tile = pl.BlockSpec((ROW_TILE, D), lambda s: (s * tiles_per_dev, 0))
    return _call(body, name='merge_edges', out_shape=_sds(wt.shape, wt.dtype), grid=(N_DEV,),
                 in_specs=[tile, pl.BlockSpec((None, ROW_TILE, D), lambda s: (s, 0, 0))], out_specs=tile,
                 sem=('arbitrary',), aliases={0: 0})(wt, edges)


def _adam_update(g, w, m, v):
    c1 = 1.0 - ADAM_B1 ** ADAM_STEP
    c2 = 1.0 - ADAM_B2 ** ADAM_STEP
    mn = ADAM_B1 * m + (1.0 - ADAM_B1) * g
    vn = ADAM_B2 * v + (1.0 - ADAM_B2) * (g * g)
    m_hat = mn / c1
    v_hat = vn / c2
    return -ADAM_LR * (m_hat / (jnp.sqrt(v_hat) + ADAM_EPS) + ADAM_WD * w), mn, vn


def _adamw(parts, w, m, v, name):
    n_parts = parts.shape[0]
    R, C = w.shape
    tb = R if (n_parts + 7) * R * C * 4 <= (12 << 20) else _pick(R, (128, 64, 32, 16, 8))

    def body(p_ref, w_ref, m_ref, v_ref, g_out, d_out, m_out, v_out):
        g = p_ref[0].astype(F32)
        for s in range(1, n_parts):
            g = g + p_ref[s].astype(F32)
        g_out[...] = g
        d_out[...], m_out[...], v_out[...] = _adam_update(g, w_ref[...], m_ref[...], v_ref[...])

    row = pl.BlockSpec((tb, C), lambda i: (i, 0))
    o = _sds((R, C), F32)
    return _call(body, name=name, out_shape=(o, o, o, o), grid=(R // tb,),
                 in_specs=[pl.BlockSpec((n_parts, tb, C), lambda i: (0, i, 0)), row, row, row],
                 out_specs=(row, row, row, row), sem=('parallel',))(parts, w, m, v)


def _adamw_layer(parts, w, m, v, layer, prev, shift, name):
    L, n, C = w.shape
    n_p = parts.shape[1]
    cb = C if N_DEV * n_p * C * parts.dtype.itemsize <= (6 << 20) else 2 * LANES
    n_prev = 0 if prev is None else 4

    def body(sh_ref, p_ref, w_ref, m_ref, v_ref, *rest):
        g_out, d_out, m_out, v_out = rest[n_prev:n_prev + 4]
        g = p_ref[0].astype(F32)
        for s in range(1, N_DEV):
            g = g + p_ref[s].astype(F32)
        if n_p != n:
            rolled = rest[-1]
            rolled[...] = pltpu.roll(g, n_p - sh_ref[0], 0)
            g = rolled[0:n, :]
        g_out[...] = g
        d_out[...], m_out[...], v_out[...] = _adam_update(g, w_ref[...], m_ref[...], v_ref[...])

    lay = pl.BlockSpec((None, n, cb), lambda j, sh: (layer, 0, j))
    stack = _sds((L, n, C), F32)
    grid_spec = pltpu.PrefetchScalarGridSpec(
        num_scalar_prefetch=1, grid=(C // cb,),
        in_specs=[pl.BlockSpec((N_DEV, n_p, cb), lambda j, sh: (0, 0, j)), lay, lay, lay]
        + [pl.BlockSpec(memory_space=pl.ANY)] * n_prev,
        out_specs=(lay, lay, lay, lay), scratch_shapes=[pltpu.VMEM((n_p, cb), F32)] if n_p != n else [])
    return pl.pallas_call(body, name=name, grid_spec=grid_spec, out_shape=(stack, stack, stack, stack),
                          input_output_aliases={5 + k: k for k in range(n_prev)},
                          compiler_params=pltpu.CompilerParams(dimension_semantics=('parallel',),
                                                               vmem_limit_bytes=VMEM_LIMIT),
                          interpret=False)(shift, parts, w, m, v, *(prev or ()))


def kernel(x, mem, mix_norm, w_in, b_gate, b_forget, conv_a, w_out_a, w_out_b, conv_c, conv_c_bias, ln_c_gain, ln_c_bias, w_out_c, w_o, xattn_norm, mem_norm, w_xq, w_xkv, w_xo, ffn_norm, w_gate_up, w_down, final_norm, loss_target, m_mix_norm, m_w_in, m_b_gate, m_b_forget, m_conv_a, m_w_out_a, m_w_out_b, m_conv_c, m_conv_c_bias, m_ln_c_gain, m_ln_c_bias, m_w_out_c, m_w_o, m_xattn_norm, m_mem_norm, m_w_xq, m_w_xkv, m_w_xo, m_ffn_norm, m_w_gate_up, m_w_down, m_final_norm, v_mix_norm, v_w_in, v_b_gate, v_b_forget, v_conv_a, v_w_out_a, v_w_out_b, v_conv_c, v_conv_c_bias, v_ln_c_gain, v_ln_c_bias, v_w_out_c, v_w_o, v_xattn_norm, v_mem_norm, v_w_xq, v_w_xkv, v_w_xo, v_ffn_norm, v_w_gate_up, v_w_down, v_final_norm):
    P = dict(zip(ARG_NAMES, (x, mem, mix_norm, w_in, b_gate, b_forget, conv_a, w_out_a, w_out_b, conv_c, conv_c_bias, ln_c_gain, ln_c_bias, w_out_c, w_o, xattn_norm, mem_norm, w_xq, w_xkv, w_xo, ffn_norm, w_gate_up, w_down, final_norm, loss_target, m_mix_norm, m_w_in, m_b_gate, m_b_forget, m_conv_a, m_w_out_a, m_w_out_b, m_conv_c, m_conv_c_bias, m_ln_c_gain, m_ln_c_bias, m_w_out_c, m_w_o, m_xattn_norm, m_mem_norm, m_w_xq, m_w_xkv, m_w_xo, m_ffn_norm, m_w_gate_up, m_w_down, m_final_norm, v_mix_norm, v_w_in, v_b_gate, v_b_forget, v_conv_a, v_w_out_a, v_w_out_b, v_conv_c, v_conv_c_bias, v_ln_c_gain, v_ln_c_bias, v_w_out_c, v_w_o, v_xattn_norm, v_mem_norm, v_w_xq, v_w_xkv, v_w_xo, v_ffn_norm, v_w_gate_up, v_w_down, v_final_norm)))
    return _step(P)


_T_VIEW = ('w_in', 'w_gate_up', 'w_xo')
_BIG = [w for w in SHARDED if w not in ('conv_a', 'conv_c')]
_OTHER = [w for w in _BIG if w != 'w_in']
_CONVS = (('conv_a', CONV_A_W, 8), ('conv_c', CONV_C_W, 32))
_EARLY = ['w_down', 'w_gate_up', 'w_xo', 'w_xq', 'w_xkv', 'w_o']
_LATE = ['w_out_c', 'w_out_b', 'w_out_a', 'w_in']


def _view(name, a):
    return jnp.transpose(a, (0, 2, 1)) if name in _T_VIEW else a


def _step(P):
    L, D = P['mix_norm'].shape
    T = P['x'].shape[1]
    n_in = P['w_in'].shape[2]
    stride = n_in // ROW_TILE * ROW_TILE
    rem = n_in - stride
    win = stride + ROW_TILE
    assert rem * N_DEV == ROW_TILE
    r_in = n_in * N_DEV
    f_off = 6 * D
    hi_off = f_off + FOX_HEADS
    x_i, y_i, c_i = _place()
    me = 4 * x_i + 2 * y_i + c_i
    V = {}
    for w in _BIG:
        for k in (w, 'm_' + w, 'v_' + w):
            V[k] = _view(w, P[k])
    n_own = {w: V[w].shape[1] for w in _OTHER}
    shift = jnp.reshape(rem * me, (1,)).astype(jnp.int32)
    no_shift = jnp.zeros((1,), jnp.int32)
    row = lambda a, l: a[l][None, :]
    dus = lax.dynamic_update_slice

    conv_taps = {w: (taps, padded) for w, taps, padded in _CONVS}

    def gather_start(l, names, tag, deps):
        srcs, lands, plan = [], [], []
        for w in names:
            si, li = len(srcs), len(lands)
            if w == 'w_in':
                first = rem * me + sum(t[0, 0] for t in deps).astype(jnp.int32) if deps else rem * me
                srcs.append(dus(jnp.zeros((win, D), BF16), V[w][l].astype(BF16), (first, 0)))
                lands += [_sds((r_in, D), BF16), _sds((N_DEV, ROW_TILE, D), BF16)]
                plan += [(si, li, 'rows', (ROW_TILE, stride, stride, ROW_TILE)), (si, li + 1, 'slot', (0, ROW_TILE))]
            elif w in conv_taps:
                srcs.append(P[w][l])
                lands.append(_sds((N_DEV,) + srcs[si].shape, F32))
                plan.append((si, li, 'slot', None))
            else:
                srcs.append(V[w][l].astype(BF16))
                lands.append(_sds((N_DEV * n_own[w], srcs[si].shape[1]), BF16))
                plan.append((si, li, 'rows', (0, n_own[w], n_own[w], 0)))
        lands = _place_own(srcs, plan, lands, 'gather_own', deps)
        sems, srcs, lands, token = _gather_start(srcs, lands, plan, f'gather_start_{tag}{l}')
        return dict(names=names, name=f'{tag}{l}', plan=plan, send=sems[0], recv=sems[1], srcs=srcs, lands=lands,
                    token=token)

    def gather_pass(g, after):
        g['pass'], g['lands'], g['token'] = _gather_pass(g['recv'], g['lands'], g['plan'], after,
                                                        'gather_pass_' + g['name'])

    def gather_wait(g, after):
        lands = iter(_gather_wait(g['send'], g['pass'], g['srcs'], g['lands'], g['plan'], after,
                                  'gather_wait_' + g['name']))
        W = {}
        for w in g['names']:
            if w == 'w_in':
                W[w] = _merge_edges(next(lands), next(lands), stride // ROW_TILE)
            elif w in conv_taps:
                taps, padded = conv_taps[w]
                full = jnp.transpose(next(lands), (1, 0, 2)).reshape(taps, D)
                W[w] = jnp.pad(full, ((0, padded - taps), (0, 0)))
            else:
                W[w] = next(lands)
        return W

    first_part = ['w_in'] + list(conv_taps)
    gathers, prev = [], []
    for l in range(L):
        part_a = gather_start(l, first_part, 'a', prev)
        part_b = gather_start(l, _OTHER, 'b', [part_a['token']])
        gathers.append([part_a, part_b])
        prev = [part_b['token']]
    gather_pass(gathers[0][0], prev)

    mem_x = P['mem'][0]
    mem_n = _rms_fwd(mem_x, P['mem_norm'][None, :], 'rms_mem')
    xs = P['x'][0]
    saved = []
    for l in range(L):
        W = gather_wait(gathers[l][0], [xs])
        s = {'x0': xs, 'W': W}
        wt = W['w_in']
        s['h1'] = _rms_fwd(xs, row(P['mix_norm'], l), 'rms_mix')
        s['z1'] = z1 = _mm(s['h1'], wt, mode='nt', b_rows=(0, f_off), name='mm_in_lo')
        s['z2'] = z2 = _mm(s['h1'], wt, mode='nt', b_rows=(hi_off, r_in - hi_off), name='mm_in_hi')
        s['zf'] = _mm(s['h1'], wt, mode='nt', b_rows=(f_off, FOX_HEADS), out_dtype=F32, name='mm_in_f')
        s['ya_pre'] = _mixa_fwd(z1, W['conv_a'])
        s['c'] = _fox_c(s['zf'], row(P['b_forget'], l))
        s['o'], s['lse'] = _fox_fwd(z1, s['c'])
        for g in gathers[l][1:]:
            gather_pass(g, [s['o']])
        s['u3'], s['u1'] = _mixc_fwd(z2, W['conv_c'], row(P['conv_c_bias'], l), row(P['ln_c_gain'], l),
                                     row(P['ln_c_bias'], l))
        for g in gathers[l][1:]:
            W.update(gather_wait(g, [s['u3']]))
        s['ya'] = _mm(s['ya_pre'], W['w_out_a'], mode='nn', name='mm_out_a')
        s['yb'] = _mm(s['o'], W['w_out_b'], mode='nn', name='mm_out_b')
        s['yc'] = _mm(s['u3'], W['w_out_c'], mode='nn', name='mm_out_c')
        s['merged'] = _gate_fwd(z2, row(P['b_gate'], l), s['ya'], s['yb'], s['yc'])
        xs = _mm(s['merged'], W['w_o'], mode='nn', add=xs, out_dtype=F32, name='mm_o')
        s['x1'] = xs
        deps = []
        if l + 1 < L:
            gather_pass(gathers[l + 1][0], [xs])
            deps = [gathers[l + 1][0]['token']]
        s['h2'] = _rms_fwd(xs, row(P['xattn_norm'], l), 'rms_xattn', deps=deps)
        s['qx'] = _mm(s['h2'], W['w_xq'], mode='nn', name='mm_xq')
        s['kv'] = _mm(mem_n, W['w_xkv'], mode='nn', name='mm_xkv')
        s['ox'] = _xattn_fwd(s['qx'], s['kv'])
        xs = _mm(s['ox'], W['w_xo'], mode='nt', add=xs, out_dtype=F32, name='mm_xo')
        s['x2'] = xs
        s['h3'] = _rms_fwd(xs, row(P['ffn_norm'], l), 'rms_ffn')
        s['gu'] = _mm(s['h3'], W['w_gate_up'], mode='nt', name='mm_gate_up')
        s['act'] = _swiglu_fwd(s['gu'])
        xs = _mm(s['act'], W['w_down'], mode='nn', add=xs, out_dtype=F32, name='mm_down')
        saved.append(s)

    dx, d_final_norm, loss_part = _final(xs, P['final_norm'][None, :], P['loss_target'][0])

    GS = {w: [None] * L for w in SMALL}
    G_conv = {'conv_a': [None] * L, 'conv_c': [None] * L}
    d_mem_n = None
    results = {w: None for w in _BIG}

    def exchange_start(G, names, l, tag, after):
        srcs = [G[w] for w in names]
        plan, lands = [], []
        for i, w in enumerate(names):
            step, n = (stride, win) if w == 'w_in' else (n_own[w], n_own[w])
            lands.append(_sds((N_DEV, n, srcs[i].shape[1]), BF16))
            plan.append((i, i, 'chunk', (step, n)))
        lands = _place_own(srcs, plan, lands, 'exchange_own', [])
        sems, srcs, lands, token = _scatter_start(srcs, lands, plan, after, f'exchange_start_{tag}{l}')
        return (names, l, tag, plan, sems, srcs, lands, token)

    def exchange_finish(e, after):
        names, l, tag, plan, sems, srcs, lands, _ = e
        lands = _scatter_wait(sems, srcs, lands, plan, after, f'exchange_wait_{tag}{l}')
        for i, w in enumerate(names):
            results[w] = _adamw_layer(lands[i], V[w], V['m_' + w], V['v_' + w], l, results[w],
                                      shift if w == 'w_in' else no_shift, 'adamw_' + w)

    pending = []
    for l in reversed(range(L)):
        s = saved[l]
        W, z1, z2 = s['W'], s['z1'], s['z2']
        wt = W['w_in']
        G = {}
        d_act = _mm(dx, W['w_down'], mode='nt', name='mmb_down', deps=[e[-1] for e in pending])
        G['w_down'] = _mm(s['act'], dx, mode='tn', name='mmg_w_down')
        dgu = _swiglu_bwd(s['gu'], d_act)
        dh3 = _mm(dgu, W['w_gate_up'], mode='nn', out_dtype=F32, name='mmb_gate_up')
        G['w_gate_up'] = _mm(dgu, s['h3'], mode='tn', name='mmg_w_gate_up')
        dx, GS['ffn_norm'][l] = _rms_bwd(s['x2'], row(P['ffn_norm'], l), dh3, dx, 'rms_ffn_bwd')
        d_ox = _mm(dx, W['w_xo'], mode='nn', out_dtype=F32, name='mmb_xo')
        G['w_xo'] = _mm(dx, s['ox'], mode='tn', name='mmg_w_xo')
        dqx, dkv = _xattn_bwd(s['qx'], s['kv'], d_ox)
        dh2 = _mm(dqx, W['w_xq'], mode='nt', out_dtype=F32, name='mmb_xq')
        G['w_xq'] = _mm(s['h2'], dqx, mode='tn', name='mmg_w_xq')
        G['w_xkv'] = _mm(mem_n, dkv, mode='tn', name='mmg_w_xkv')
        d_mem_n = _mm(dkv, W['w_xkv'], mode='nt', add=d_mem_n, out_dtype=F32, name='mmb_xkv')
        dx, GS['xattn_norm'][l] = _rms_bwd(s['x1'], row(P['xattn_norm'], l), dh2, dx, 'rms_xattn_bwd')
        dm = _mm(dx, W['w_o'], mode='nt', out_dtype=F32, name='mmb_o')
        G['w_o'] = _mm(s['merged'], dx, mode='tn', name='mmg_w_o')
        early = exchange_start(G, _EARLY, l, 'a', [])
        dya, dyb, dyc, dz2, GS['b_gate'][l] = _gate_bwd(z2, row(P['b_gate'], l), s['ya'], s['yb'], s['yc'], dm)
        du3 = _mm(dyc, W['w_out_c'], mode='nt', out_dtype=F32, name='mmb_out_c', deps=[early[-1]])
        G['w_out_c'] = _mm(s['u3'], dyc, mode='tn', name='mmg_w_out_c')
        dz2, G_conv['conv_c'][l], GS['conv_c_bias'][l], GS['ln_c_gain'][l], GS['ln_c_bias'][l] = _mixc_bwd(
            z2, W['conv_c'], row(P['ln_c_gain'], l), row(P['ln_c_bias'], l), s['u1'], du3, dz2)
        dya_pre = _mm(dya, W['w_out_a'], mode='nt', out_dtype=F32, name='mmb_out_a')
        G['w_out_a'] = _mm(s['ya_pre'], dya, mode='tn', name='mmg_w_out_a')
        dz1, G_conv['conv_a'][l] = _mixa_bwd(z1, W['conv_a'], dya_pre)
        do = _mm(dyb, W['w_out_b'], mode='nt', name='mmb_out_b')
        G['w_out_b'] = _mm(s['o'], dyb, mode='tn', name='mmg_w_out_b')
        dz1, dc_pairs = _fox_bwd(z1, s['c'], s['o'], s['lse'], do, dz1)
        hp = FOX_HEADS // dc_pairs.shape[0]
        dc = jnp.transpose(dc_pairs[:, :, :hp], (1, 0, 2)).reshape(T, FOX_HEADS)
        dzf, GS['b_forget'][l] = _fox_c_bwd(dc, s['zf'], row(P['b_forget'], l))
        segs = [(dzf, f_off), (dz1, 0), (dz2, hi_off)]
        g_in = r_in
        for dz, off in segs:
            g_in = _mm(dz, s['h1'], mode='tn', out=g_in, o_off=off, name='mmg_w_in')
        G['w_in'] = g_in
        late = exchange_start(G, _LATE, l, 'b', [])
        dh1 = None
        for n, (dz, off) in enumerate(segs):
            dh1 = _mm(dz, wt, mode='nn', b_rows=(off, dz.shape[1]), add=dh1, out_dtype=F32, name='mmb_in',
                      deps=[late[-1]] if n == 0 else ())
        dx, GS['mix_norm'][l] = _rms_bwd(s['x0'], row(P['mix_norm'], l), dh1, dx, 'rms_mix_bwd')
        for e in pending:
            exchange_finish(e, [dx])
        pending = [early, late]

    _, d_mem_norm = _rms_bwd(mem_x, P['mem_norm'][None, :], d_mem_n, None, 'rms_mem_bwd')

    small_parts = {w: jnp.concatenate(GS[w], axis=0) for w in SMALL if w not in ('mem_norm', 'final_norm')}
    small_parts['mem_norm'] = d_mem_norm
    small_parts['final_norm'] = d_final_norm
    conv_parts = [jnp.stack(G_conv[w])[:, :taps] for w, taps, _ in _CONVS]
    sizes = [P[w].size for w in SMALL]
    conv_sizes = [c.size for c in conv_parts]
    n_small = sum(sizes) + sum(conv_sizes) + LANES
    n_rows = -(-n_small // (8 * LANES)) * 8

    def pack(parts):
        flat = jnp.concatenate([p.reshape(-1) for p in parts])
        return jnp.pad(flat, (0, n_rows * LANES - flat.size)).reshape(n_rows, LANES)

    g_small = pack([small_parts[w] for w in SMALL] + conv_parts + [loss_part])
    small_plan = [(0, 0, 'slot', None)]
    small_land = _place_own([g_small], small_plan, [_sds((N_DEV,) + g_small.shape, F32)], 'small_own', [])
    small_sems, small_src, small_land, _ = _scatter_start([g_small], small_land, small_plan, [], 'small_start')
    early0, late0 = pending
    exchange_finish(early0, [small_land[0]])
    exchange_finish(late0, [results[w][0] for w in _EARLY])
    (all_small,) = _scatter_wait(small_sems, small_src, small_land, small_plan, [results['w_in'][0]], 'small_wait')
    sm = _adamw(all_small, pack([P[w] for w in SMALL]), pack([P['m_' + w] for w in SMALL]),
                pack([P['v_' + w] for w in SMALL]), 'adamw_small')
    g_all = sm[0].reshape(-1)
    loss = g_all[sum(sizes) + sum(conv_sizes)]
    final = {}
    off = 0
    for w, n in zip(SMALL, sizes):
        final[w] = [o.reshape(-1)[off:off + n].reshape(P[w].shape) for o in sm]
        off += n
    for (w, taps, _), c, n in zip(_CONVS, conv_parts, conv_sizes):
        n_col = P[w].shape[2]
        g_own = lax.dynamic_slice(g_all[off:off + n].reshape(c.shape), (0, 0, n_col * me), (L, taps, n_col))
        flat = lambda a: a.reshape(L * taps, n_col)
        outs = _adamw(flat(g_own)[None], flat(P[w]), flat(P['m_' + w]), flat(P['v_' + w]), 'adamw_' + w)
        final[w] = [o.reshape(P[w].shape) for o in outs]
        off += n

    for w in _BIG:
        final[w] = [_view(w, o) for o in results[w]]

    out = [loss, dx[None]]
    for k in range(4):
        out += [final[w][k] for w in WEIGHTS]
    return tuple(out)
```

```python
import functools

import jax
import jax.numpy as jnp
from jax import lax
from jax.experimental import pallas as pl
from jax.experimental.pallas import tpu as pltpu

F32 = jnp.float32
BF16 = jnp.bfloat16
MESH = pl.DeviceIdType.MESH

N_DEV = 8
EPS = 1e-6
FOX_HEADS = 16
X_HEADS = 4
LANES = 128
ROW_TILE = 16
HALO_A = 16
HALO_C = 32
SUBLANES = 8
CONV_ROWS = 128
ROW_CHUNK = 32
CONV_A_W = 3
CONV_C_W = 31
VMEM_LIMIT = 56 << 20

ADAM_LR = 0.001
ADAM_B1 = 0.9
ADAM_B2 = 0.999
ADAM_EPS = 1e-08
ADAM_WD = 0.01
ADAM_STEP = 10

WEIGHTS = ['mix_norm', 'w_in', 'b_gate', 'b_forget', 'conv_a', 'w_out_a', 'w_out_b', 'conv_c', 'conv_c_bias',
           'ln_c_gain', 'ln_c_bias', 'w_out_c', 'w_o', 'xattn_norm', 'mem_norm', 'w_xq', 'w_xkv', 'w_xo',
           'ffn_norm', 'w_gate_up', 'w_down', 'final_norm']
SHARDED = ['w_in', 'conv_a', 'w_out_a', 'w_out_b', 'conv_c', 'w_out_c', 'w_o', 'w_xq', 'w_xkv', 'w_xo',
           'w_gate_up', 'w_down']
ROW_SHARDED = ['w_out_a', 'w_out_b', 'w_out_c', 'w_o', 'w_xq', 'w_xkv', 'w_down']
SMALL = [w for w in WEIGHTS if w not in SHARDED]
ARG_NAMES = (['x', 'mem'] + WEIGHTS + ['loss_target'] + ['m_' + w for w in WEIGHTS] + ['v_' + w for w in WEIGHTS])


def _pick(n, cands=(1024, 1408, 512, 256, 128)):
    for c in cands:
        if n % c == 0:
            return c
    return n


def _call(body, *, name, out_shape, grid=(), in_specs=None, out_specs=None, scratch=(), sem=None, aliases=None,
          deps=()):
    params = dict(vmem_limit_bytes=VMEM_LIMIT)
    if sem is not None:
        params['dimension_semantics'] = sem
    n_in, n_dep = len(in_specs), len(deps)

    def body_without_deps(*refs):
        body(*refs[:n_in], *refs[n_in + n_dep:])

    call = pl.pallas_call(body_without_deps, name=name, out_shape=out_shape, grid=grid, scratch_shapes=list(scratch),
                          in_specs=list(in_specs) + [pl.BlockSpec(memory_space=pl.ANY)] * n_dep, out_specs=out_specs,
                          input_output_aliases=aliases or {}, compiler_params=pltpu.CompilerParams(**params),
                          interpret=False)
    return lambda *args: call(*args, *deps)


def _sds(shape, dtype):
    return jax.ShapeDtypeStruct(tuple(shape), dtype)


def _sigmoid(x):
    return 1.0 / (1.0 + jnp.exp(-x))


_DN = {'nn': (((1,), (0,)), ((), ())), 'nt': (((1,), (1,)), ((), ())), 'tn': (((0,), (0,)), ((), ()))}


def _mm(a, b, *, mode, name, out_dtype=BF16, add=None, b_rows=None, out=None, o_off=0, deps=()):
    if mode == 'tn':
        K, M = a.shape
    else:
        M, K = a.shape
    b_off, b_n = (0, b.shape[0]) if b_rows is None else b_rows
    N = b_n if mode == 'nt' else b.shape[1]
    assert (b.shape[1] if mode == 'nt' else b_n) == K
    tm, tn = _pick(M), _pick(N)
    tk = K if K <= 2048 else _pick(K)
    ni, nj, nk = M // tm, N // tn, K // tk
    a_bytes, b_bytes = M * K * a.dtype.itemsize, K * N * b.dtype.itemsize
    n_outer = (b_bytes + nj * a_bytes) < (a_bytes + ni * b_bytes)
    if n_outer:
        grid = (nj, ni, nk)
        ij = lambda g0, g1: (g1, g0)
    else:
        grid = (ni, nj, nk)
        ij = lambda g0, g1: (g0, g1)

    def a_map(g0, g1, k):
        i, _ = ij(g0, g1)
        return (k, i) if mode == 'tn' else (i, k)

    def elems(blk):
        return tuple(pl.Element(s) for s in blk)

    def at(rows, cols):
        return pl.multiple_of(rows, ROW_TILE), pl.multiple_of(cols, LANES)

    a_blk = (tk, tm) if mode == 'tn' else (tm, tk)
    b_blk = (tn, tk) if mode == 'nt' else (tk, tn)
    if b_rows is None:
        def b_map(g0, g1, k):
            _, j = ij(g0, g1)
            return (j, k) if mode == 'nt' else (k, j)
        b_spec = pl.BlockSpec(b_blk, b_map)
    else:
        def b_map(g0, g1, k):
            _, j = ij(g0, g1)
            return at(b_off + j * tn, k * tk) if mode == 'nt' else at(b_off + k * tk, j * tn)
        b_spec = pl.BlockSpec(elems(b_blk), b_map)
    in_specs = [pl.BlockSpec(a_blk, a_map), b_spec]
    args = [a, b]
    has_add = add is not None
    if has_add:
        in_specs.append(pl.BlockSpec((tm, tn), lambda g0, g1, k: ij(g0, g1)))
        args.append(add)
    aliases = {}
    if out is None:
        out_shape = _sds((M, N), out_dtype)
        out_spec = pl.BlockSpec((tm, tn), lambda g0, g1, k: ij(g0, g1))
    else:
        if isinstance(out, int):
            out_shape = _sds((out, N), out_dtype)
        else:
            out_shape = _sds(out.shape, out.dtype)
            in_specs.append(pl.BlockSpec(memory_space=pl.ANY))
            aliases = {len(args): 0}
            args.append(out)

        def o_map(g0, g1, k):
            i, j = ij(g0, g1)
            return at(o_off + i * tm, j * tn)
        out_spec = pl.BlockSpec(elems((tm, tn)), o_map)
    dn = _DN[mode]
    n_in = len(args)

    def body(*refs):
        a_ref, b_ref = refs[0], refs[1]
        add_ref = refs[2] if has_add else None
        o_ref = refs[n_in]
        part = lax.dot_general(a_ref[...].astype(BF16), b_ref[...].astype(BF16), dn, preferred_element_type=F32)

        def finish(r):
            if has_add:
                r = r + add_ref[...]
            o_ref[...] = r.astype(o_ref.dtype)

        if nk == 1:
            finish(part)
        else:
            acc = refs[n_in + 1]
            k = pl.program_id(2)

            @pl.when(k == 0)
            def _():
                acc[...] = part

            @pl.when(k > 0)
            def _():
                acc[...] += part

            @pl.when(k == nk - 1)
            def _():
                finish(acc[...])

    scratch = [pltpu.VMEM((tm, tn), F32)] if nk > 1 else []
    return _call(body, name=name, out_shape=out_shape, grid=grid, in_specs=in_specs, out_specs=out_spec,
                 scratch=scratch, sem=('parallel', 'parallel', 'arbitrary'), aliases=aliases, deps=deps)(*args)


def _rms_fwd(x, g, name, deps=()):
    R, D = x.shape
    tb = _pick(R, (512, 256, 128))

    def body(x_ref, g_ref, o_ref):
        xv = x_ref[...]
        r = lax.rsqrt(jnp.mean(xv * xv, axis=-1, keepdims=True) + EPS)
        o_ref[...] = (xv * r * g_ref[...]).astype(o_ref.dtype)

    row = pl.BlockSpec((tb, D), lambda i: (i, 0))
    vec = pl.BlockSpec((1, D), lambda i: (0, 0))
    return _call(body, name=name, out_shape=_sds((R, D), BF16), grid=(R // tb,), in_specs=[row, vec],
                 out_specs=row, sem=('parallel',), deps=deps)(x, g)


def _rms_bwd(x, g, dh, dres, name, deps=()):
    R, D = x.shape
    tb = _pick(R, (512, 256, 128))
    has_res = dres is not None

    def body(*refs):
        x_ref, g_ref, dh_ref = refs[:3]
        dx_ref, dg_ref = refs[-2:]
        xv = x_ref[...]
        dhv = dh_ref[...].astype(F32)
        r = lax.rsqrt(jnp.mean(xv * xv, axis=-1, keepdims=True) + EPS)
        u = dhv * g_ref[...]
        m = jnp.mean(u * xv, axis=-1, keepdims=True)
        dx = r * u - xv * (r * r * r * m)
        if has_res:
            dx = dx + refs[3][...]
        dx_ref[...] = dx

        @pl.when(pl.program_id(0) == 0)
        def _():
            dg_ref[...] = jnp.zeros_like(dg_ref)

        dg_ref[...] += jnp.sum(dhv * xv * r, axis=0, keepdims=True)

    row = pl.BlockSpec((tb, D), lambda i: (i, 0))
    vec = pl.BlockSpec((1, D), lambda i: (0, 0))
    args = [x, g, dh] + ([dres] if has_res else [])
    return _call(body, name=name, out_shape=(_sds((R, D), F32), _sds((1, D), F32)), grid=(R // tb,),
                 in_specs=[row, vec, row] + ([row] if has_res else []), out_specs=(row, vec),
                 sem=('arbitrary',), deps=deps)(*args)


def _final(x, g, tgt):
    R, D = x.shape
    tb = _pick(R, (512, 256, 128))

    def body(x_ref, g_ref, t_ref, dx_ref, dg_ref, loss_ref):
        xv = x_ref[...]
        gv = g_ref[...]
        r = lax.rsqrt(jnp.mean(xv * xv, axis=-1, keepdims=True) + EPS)
        e = xv * r * gv - t_ref[...]
        row_loss = jnp.mean(e * e, axis=-1, keepdims=True)
        blk_loss = 0.5 * jnp.sum(row_loss, axis=0, keepdims=True)
        dy = e * (1.0 / D)
        u = dy * gv
        m = jnp.mean(u * xv, axis=-1, keepdims=True)
        dx_ref[...] = r * u - xv * (r * r * r * m)

        @pl.when(pl.program_id(0) == 0)
        def _():
            dg_ref[...] = jnp.zeros_like(dg_ref)
            loss_ref[...] = jnp.zeros_like(loss_ref)

        dg_ref[...] += jnp.sum(dy * xv * r, axis=0, keepdims=True)
        loss_ref[...] += jnp.broadcast_to(blk_loss, loss_ref.shape)

    row = pl.BlockSpec((tb, D), lambda i: (i, 0))
    vec = pl.BlockSpec((1, D), lambda i: (0, 0))
    one = pl.BlockSpec((1, LANES), lambda i: (0, 0))
    return _call(body, name='final_loss', out_shape=(_sds((R, D), F32), _sds((1, D), F32), _sds((1, LANES), F32)),
                 grid=(R // tb,), in_specs=[row, vec, row], out_specs=(row, vec, one), sem=('arbitrary',))(x, g, tgt)


def _col(tb, w, cb):
    return pl.BlockSpec((tb, w), lambda i: (i, cb))


def _prev(tb, h, w, cb):
    return pl.BlockSpec((h, w), lambda i: (jnp.maximum(i * (tb // h) - 1, 0), cb))


def _next(tb, h, w, cb, rows):
    return pl.BlockSpec((h, w), lambda i: (jnp.minimum((i + 1) * (tb // h), rows // h - 1), cb))


def _full(shape):
    return pl.BlockSpec(shape, lambda i: (0,) * len(shape))


def _mixa_fwd(z, w):
    T = z.shape[0]
    D = w.shape[1]
    tb = _pick(T, (256, 128))
    H = HALO_A

    def body(ab, ac, au, acp, aup, w_ref, o_ref, ext):
        first = pl.program_id(0) == 0
        ext[0:H, :] = jnp.where(first, 0.0, acp[...].astype(F32) * aup[...].astype(F32))
        ext[H:H + tb, :] = ac[...].astype(F32) * au[...].astype(F32)
        cp = jnp.zeros((tb, D), F32)
        for k in range(CONV_A_W):
            off = H - (CONV_A_W - 1) + k
            cp = cp + ext[off:off + tb, :] * w_ref[k:k + 1, :]
        o_ref[...] = (ab[...].astype(F32) * cp).astype(o_ref.dtype)

    return _call(body, name='mixa_fwd', out_shape=_sds((T, D), BF16), grid=(T // tb,),
                 in_specs=[_col(tb, D, 0), _col(tb, D, 1), _col(tb, D, 2), _prev(tb, H, D, 1), _prev(tb, H, D, 2),
                           _full((8, D))],
                 out_specs=_col(tb, D, 0), scratch=[pltpu.VMEM((H + tb, D), F32)], sem=('parallel',))(z, z, z, z, z, w)


def _mixa_bwd(z, w, dy):
    T = z.shape[0]
    D = w.shape[1]
    tb = _pick(T, (256, 128))
    H = HALO_A
    nb = T // tb

    def body(ab, ac, au, acp, aup, abn, dy_ref, dyn, w_ref, dz_ref, dw_ref, pext, dext):
        i = pl.program_id(0)
        a_b, a_c, a_u = ab[...].astype(F32), ac[...].astype(F32), au[...].astype(F32)
        pext[0:H, :] = jnp.where(i == 0, 0.0, acp[...].astype(F32) * aup[...].astype(F32))
        pext[H:H + tb, :] = a_c * a_u
        dyv = dy_ref[...].astype(F32)
        dcp = dyv * a_b
        dext[0:tb, :] = dcp
        dext[tb:tb + H, :] = jnp.where(i == nb - 1, 0.0, dyn[...].astype(F32) * abn[...].astype(F32))

        @pl.when(i == 0)
        def _():
            dw_ref[...] = jnp.zeros_like(dw_ref)

        cp = jnp.zeros((tb, D), F32)
        dp = jnp.zeros((tb, D), F32)
        for k in range(CONV_A_W):
            off = H - (CONV_A_W - 1) + k
            wk = w_ref[k:k + 1, :]
            pk = pext[off:off + tb, :]
            cp = cp + pk * wk
            dp = dp + dext[CONV_A_W - 1 - k:CONV_A_W - 1 - k + tb, :] * wk
            dw_ref[k:k + 1, :] += jnp.sum(dcp * pk, axis=0, keepdims=True)
        dz_ref[:, 0:D] = (dyv * cp).astype(dz_ref.dtype)
        dz_ref[:, D:2 * D] = (dp * a_u).astype(dz_ref.dtype)
        dz_ref[:, 2 * D:3 * D] = (dp * a_c).astype(dz_ref.dtype)

    return _call(body, name='mixa_bwd', out_shape=(_sds((T, 6 * D), BF16), _sds((8, D), F32)), grid=(nb,),
                 in_specs=[_col(tb, D, 0), _col(tb, D, 1), _col(tb, D, 2), _prev(tb, H, D, 1), _prev(tb, H, D, 2),
                           _next(tb, H, D, 0, T), _col(tb, D, 0), _next(tb, H, D, 0, T), _full((8, D))],
                 out_specs=(_col(tb, 3 * D, 0), _full((8, D))),
                 scratch=[pltpu.VMEM((H + tb, D), F32), pltpu.VMEM((tb + H, D), F32)],
                 sem=('arbitrary',))(z, z, z, z, z, z, dy, dy, w)


def _split3(v):
    hi = v.astype(BF16)
    r1 = v - hi.astype(F32)
    mid = r1.astype(BF16)
    lo = (r1 - mid.astype(F32)).astype(BF16)
    return hi, mid, lo


def _tri_dot(tri, v):
    hi, mid, lo = _split3(v)
    d = lambda p: jnp.dot(tri, p, preferred_element_type=F32)
    return d(hi) + d(mid) + d(lo)


def _fox_c(zf, bf):
    T, H = zf.shape
    tb = LANES

    def body(zf_ref, b_ref, c_ref, carry):
        @pl.when(pl.program_id(0) == 0)
        def _():
            carry[...] = jnp.zeros_like(carry)

        xv = zf_ref[...] + b_ref[...]
        lf = jnp.minimum(xv, 0.0) - jnp.log(1.0 + jnp.exp(-jnp.abs(xv)))
        r = lax.broadcasted_iota(jnp.int32, (tb, tb), 0)
        c = lax.broadcasted_iota(jnp.int32, (tb, tb), 1)
        tri = (r >= c).astype(BF16)
        cs = _tri_dot(tri, lf) + carry[...]
        c_ref[...] = cs
        carry[...] = cs[tb - 1:tb, :]

    blk = pl.BlockSpec((tb, H), lambda i: (i, 0))
    return _call(body, name='fox_cumsum', out_shape=_sds((T, H), F32), grid=(T // tb,),
                 in_specs=[blk, _full((1, H))], out_specs=blk, scratch=[pltpu.VMEM((1, H), F32)],
                 sem=('arbitrary',))(zf, bf)


def _fox_c_bwd(dc, zf, bf):
    T, H = zf.shape
    tb = LANES
    nb = T // tb

    def body(dc_ref, zf_ref, b_ref, dz_ref, db_ref, carry):
        @pl.when(pl.program_id(0) == 0)
        def _():
            carry[...] = jnp.zeros_like(carry)
            db_ref[...] = jnp.zeros_like(db_ref)

        r = lax.broadcasted_iota(jnp.int32, (tb, tb), 0)
        c = lax.broadcasted_iota(jnp.int32, (tb, tb), 1)
        tri = (c >= r).astype(BF16)
        dlf = _tri_dot(tri, dc_ref[...]) + carry[...]
        carry[...] = dlf[0:1, :]
        xv = zf_ref[...] + b_ref[...]
        dz = dlf * _sigmoid(-xv)
        dz_ref[...] = dz
        db_ref[...] += jnp.sum(dz, axis=0, keepdims=True)

    blk = pl.BlockSpec((tb, H), lambda i: (nb - 1 - i, 0))
    return _call(body, name='fox_cumsum_bwd', out_shape=(_sds((T, H), F32), _sds((1, H), F32)), grid=(nb,),
                 in_specs=[blk, blk, _full((1, H))], out_specs=(blk, _full((1, H))),
                 scratch=[pltpu.VMEM((1, H), F32)], sem=('arbitrary',))(dc, zf, bf)


N_PIECES = 3


def _head_col(c_ref, h):
    lane = lax.broadcasted_iota(jnp.int32, c_ref.shape, 1)
    return jnp.sum(jnp.where(lane == h, c_ref[...], 0.0), axis=1, keepdims=True)


def _pieces(v, sign=1.0):
    return [sign * p.astype(F32) for p in _split3(v)]


def _augment(block, hh, dh, a_cols, b_cols):
    lane = lax.broadcasted_iota(jnp.int32, block.shape, 1)
    base = (1 - hh) * dh
    out = jnp.where((lane >= hh * dh) & (lane < (hh + 1) * dh), block.astype(F32), 0.0)
    for first, cols in ((base, list(a_cols)), (base + N_PIECES, list(b_cols))):
        if cols and all(isinstance(c, float) and c == cols[0] for c in cols):
            out = jnp.where((lane >= first) & (lane < first + len(cols)), cols[0], out)
        else:
            for n, col in enumerate(cols):
                out = jnp.where(lane == first + n, col, out)
    return out.astype(BF16)


def _fox_probs(q_aug, k_aug, tri, q0, ke, shift=None):
    s = lax.dot_general(q_aug[q0:ke, :], k_aug[0:ke, :], _DN['nt'], preferred_element_type=F32)
    s_dg = jnp.where(tri, s[:, q0:ke], -jnp.inf)
    if shift is None:
        shift = jnp.max(s_dg, axis=-1, keepdims=True)
        if q0:
            shift = jnp.maximum(shift, jnp.max(s[:, 0:q0], axis=-1, keepdims=True))
    parts = ([jnp.exp(s[:, 0:q0] - shift)] if q0 else []) + [jnp.exp(s_dg - shift)]
    return parts, shift


def _cat(parts):
    parts = [p.astype(BF16) for p in parts]
    return parts[0] if len(parts) == 1 else jnp.concatenate(parts, axis=1)


def _fox_fwd(z, c):
    T = z.shape[0]
    D = z.shape[1] // 6
    dh = D // FOX_HEADS
    hp = LANES // dh
    ncb = D // LANES
    tq = _pick(T, (256, 128))
    scale = dh ** -0.5
    ones = [1.0] * N_PIECES

    def body(q_ref, k_ref, v_ref, c_ref, o_ref, lse_ref, o_all):
        j = pl.program_id(0)
        lse_ref[...] = jnp.zeros_like(lse_ref)
        r = lax.broadcasted_iota(jnp.int32, (tq, tq), 0)
        tri = r >= lax.broadcasted_iota(jnp.int32, (tq, tq), 1)
        for hh in range(hp):
            base = (1 - hh) * dh
            ch = _head_col(c_ref, j * hp + hh)
            q_aug = _augment(q_ref[...].astype(F32) * scale, hh, dh, _pieces(ch), ones)
            k_aug = _augment(k_ref[...], hh, dh, ones, _pieces(ch, -1.0))
            v_aug = _augment(v_ref[...], hh, dh, ones, [])
            for q0 in range(0, T, tq):
                ke = q0 + tq
                parts, m = _fox_probs(q_aug, k_aug, tri, q0, ke)
                o_aug = jnp.dot(_cat(parts), v_aug[0:ke, :], preferred_element_type=F32)
                l = o_aug[:, base:base + 1]
                o_all[hh, q0:ke, :] = o_aug * (1.0 / l)
                lse_ref[q0:ke, hh:hh + 1] = m + jnp.log(l)
        lane = lax.broadcasted_iota(jnp.int32, (T, LANES), 1)
        out = o_all[0]
        for hh in range(1, hp):
            out = jnp.where(lane >= hh * dh, o_all[hh], out)
        o_ref[...] = out.astype(o_ref.dtype)

    blk = lambda cb0: pl.BlockSpec((T, LANES), lambda j: (0, cb0 + j))
    return _call(body, name='fox_fwd', out_shape=(_sds((T, D), BF16), _sds((ncb, T, LANES), F32)), grid=(ncb,),
                 in_specs=[blk(3 * ncb), blk(4 * ncb), blk(5 * ncb), _full((T, FOX_HEADS))],
                 out_specs=(blk(0), pl.BlockSpec((None, T, LANES), lambda j: (j, 0, 0))),
                 scratch=[pltpu.VMEM((hp, T, LANES), F32)], sem=('parallel',))(z, z, z, c)


def _fox_bwd(z, c, o, lse, do, dz1):
    T = z.shape[0]
    D = o.shape[1]
    dh = D // FOX_HEADS
    hp = LANES // dh
    ncb = D // LANES
    tq = _pick(T, (256, 128))
    scale = dh ** -0.5
    ones = [1.0] * N_PIECES

    def body(q_ref, k_ref, v_ref, c_ref, o_ref, lse_ref, do_ref, _, dz_ref, dc_ref, dq_all, dk_acc, dv_acc, stage,
             sems):
        j = pl.program_id(0)
        dk_acc[...] = jnp.zeros_like(dk_acc)
        dv_acc[...] = jnp.zeros_like(dv_acc)
        dc_ref[...] = jnp.zeros_like(dc_ref)
        r = lax.broadcasted_iota(jnp.int32, (tq, tq), 0)
        tri = r >= lax.broadcasted_iota(jnp.int32, (tq, tq), 1)
        lane = lax.broadcasted_iota(jnp.int32, (T, LANES), 1)
        for hh in range(hp):
            base = (1 - hh) * dh
            own = (lane >= hh * dh) & (lane < (hh + 1) * dh)
            ch = _head_col(c_ref, j * hp + hh)
            dov = do_ref[...].astype(F32)
            delta = jnp.sum(jnp.where(own, dov * o_ref[...].astype(F32), 0.0), axis=1, keepdims=True)
            q_aug = _augment(q_ref[...].astype(F32) * scale, hh, dh, _pieces(ch - lse_ref[:, hh:hh + 1]), ones)
            k_aug = _augment(k_ref[...], hh, dh, ones, _pieces(ch, -1.0))
            v_aug = _augment(v_ref[...], hh, dh, ones, [])
            do_aug = _augment(dov, hh, dh, _pieces(delta, -1.0), [])
            for q0 in range(0, T, tq):
                ke = q0 + tq
                p, _ = _fox_probs(q_aug, k_aug, tri, q0, ke, shift=0.0)
                dp = lax.dot_general(do_aug[q0:ke, :], v_aug[0:ke, :], _DN['nt'], preferred_element_type=F32)
                ds = [p[0] * dp[:, 0:q0], p[1] * dp[:, q0:ke]] if q0 else [p[0] * dp]
                dsb, pb = _cat(ds), _cat(p)
                dq_all[hh, q0:ke, :] = jnp.dot(dsb, k_aug[0:ke, :], preferred_element_type=F32)
                dk_acc[hh, 0:ke, :] += lax.dot_general(dsb, q_aug[q0:ke, :], _DN['tn'], preferred_element_type=F32)
                dv_acc[hh, 0:ke, :] += lax.dot_general(pb, do_aug[q0:ke, :], _DN['tn'], preferred_element_type=F32)
            dc_ref[:, hh:hh + 1] = dq_all[hh, :, base:base + 1] - dk_acc[hh, :, base + N_PIECES:base + N_PIECES + 1]
        dq, dk, dv = dq_all[0], dk_acc[0], dv_acc[0]
        for hh in range(1, hp):
            dq = jnp.where(lane >= hh * dh, dq_all[hh], dq)
            dk = jnp.where(lane >= hh * dh, dk_acc[hh], dk)
            dv = jnp.where(lane >= hh * dh, dv_acc[hh], dv)
        copies = []
        for n, g in enumerate((dq * scale, dk, dv)):
            stage[n] = g.astype(stage.dtype)
            col = pl.multiple_of(((3 + n) * ncb + j) * LANES, LANES)
            copies.append(pltpu.make_async_copy(stage.at[n], dz_ref.at[:, pl.ds(col, LANES)], sems.at[n]))
            copies[n].start()
        for cp in copies:
            cp.wait()

    blk = lambda cb0: pl.BlockSpec((T, LANES), lambda j: (0, cb0 + j))
    pair = pl.BlockSpec((None, T, LANES), lambda j: (j, 0, 0))
    acc = pltpu.VMEM((hp, T, LANES), F32)
    anywhere = pl.BlockSpec(memory_space=pl.ANY)
    return _call(body, name='fox_bwd', out_shape=(_sds(dz1.shape, dz1.dtype), _sds((ncb, T, LANES), F32)), grid=(ncb,),
                 in_specs=[blk(3 * ncb), blk(4 * ncb), blk(5 * ncb), _full((T, FOX_HEADS)), blk(0), pair, blk(0),
                           anywhere],
                 out_specs=(anywhere, pair),
                 scratch=[acc, acc, acc, pltpu.VMEM((3, T, LANES), BF16), pltpu.SemaphoreType.DMA((3,))],
                 sem=('arbitrary',), aliases={7: 0})(z, z, z, c, o, lse, do, dz1)


def _ln_parts(u1, gain, bias):
    mu = jnp.mean(u1, axis=-1, keepdims=True)
    xc = u1 - mu
    rstd = lax.rsqrt(jnp.mean(xc * xc, axis=-1, keepdims=True) + EPS)
    xhat = xc * rstd
    return xhat, rstd, xhat * gain + bias


def _tap_groups(offsets):
    groups = {}
    for k, off in enumerate(offsets):
        groups.setdefault(off % SUBLANES, []).append((off - off % SUBLANES, k))
    return groups


def _for_taps(src_ref, r0, cols, groups, aligned, visit):
    for r, taps in groups.items():
        if r == 0:
            for base, k in taps:
                visit(k, src_ref[r0 + base:r0 + base + CONV_ROWS, cols])
            continue
        need = max(base for base, _ in taps) + CONV_ROWS
        aligned[r, 0:need, :] = src_ref[r0 + r:r0 + r + need, cols]
        for base, k in taps:
            visit(k, aligned[r, base:base + CONV_ROWS, :])


def _row_chunk(j, first=0):
    return slice(first + j * ROW_CHUNK, first + (j + 1) * ROW_CHUNK)


def _glu_rows(cv, cg, ext, halo, tb):
    for j in range(tb // ROW_CHUNK):
        rs = _row_chunk(j)
        ext[_row_chunk(j, halo), :] = cv[rs, :].astype(F32) * _sigmoid(cg[rs, :].astype(F32))


def _mixc_fwd(z, w, cb, gain, bias):
    T = z.shape[0]
    D = w.shape[1]
    tb = _pick(T, (256, 128))
    H = HALO_C

    groups = _tap_groups([H - (CONV_C_W - 1) + k for k in range(CONV_C_W)])

    def body(cv, cg, cvp, cgp, w_ref, cb_ref, g_ref, b_ref, o_ref, u1_ref, ext, aligned):
        first = pl.program_id(0) == 0
        ext[0:H, :] = jnp.where(first, 0.0, cvp[...].astype(F32) * _sigmoid(cgp[...].astype(F32)))
        _glu_rows(cv, cg, ext, H, tb)
        for r0 in range(0, tb, CONV_ROWS):
            for c0 in range(0, D, LANES):
                cols = slice(c0, c0 + LANES)
                acc = [jnp.zeros((CONV_ROWS, LANES), F32)]

                def tap(k, rows):
                    acc[0] = acc[0] + rows * w_ref[k:k + 1, cols]

                _for_taps(ext, r0, cols, groups, aligned, tap)
                u1_ref[r0:r0 + CONV_ROWS, cols] = acc[0] + cb_ref[:, cols]

        for j in range(tb // ROW_CHUNK):
            rs = _row_chunk(j)
            _, _, u2 = _ln_parts(u1_ref[rs, :], g_ref[...], b_ref[...])
            o_ref[rs, :] = (u2 * _sigmoid(u2)).astype(o_ref.dtype)

    vec = _full((1, D))
    row = _col(tb, D, 0)
    return _call(body, name='mixc_fwd', out_shape=(_sds((T, D), BF16), _sds((T, D), F32)), grid=(T // tb,),
                 in_specs=[_col(tb, D, 0), _col(tb, D, 1), _prev(tb, H, D, 0), _prev(tb, H, D, 1), _full((32, D)), vec,
                           vec, vec],
                 out_specs=(row, row), scratch=[pltpu.VMEM((H + tb, D), F32), pltpu.VMEM((SUBLANES, CONV_ROWS + H, LANES), F32)],
                 sem=('parallel',))(z, z, z, z, w, cb, gain, bias)


def _mixc_bwd(z, w, gain, bias, u1, du3, dz2):
    T = z.shape[0]
    D = w.shape[1]
    tb = _pick(T, (256, 128))
    H = HALO_C
    nb = T // tb

    def du1_of(u1v, du3v, gv, bv):
        xhat, rstd, u2 = _ln_parts(u1v, gv, bv)
        sg = _sigmoid(u2)
        du2 = du3v * (sg * (1.0 + u2 * (1.0 - sg)))
        dxh = du2 * gv
        m1 = jnp.mean(dxh, axis=-1, keepdims=True)
        m2 = jnp.mean(dxh * xhat, axis=-1, keepdims=True)
        return rstd * (dxh - m1 - xhat * m2), du2, xhat

    fwd_groups = _tap_groups([H - (CONV_C_W - 1) + k for k in range(CONV_C_W)])
    bwd_groups = _tap_groups([CONV_C_W - 1 - k for k in range(CONV_C_W)])

    def body(cv, cg, cvp, cgp, w_ref, g_ref, b_ref, u1_ref, u1n, du3_ref, du3n, _, dz_ref, dw_ref, dcb_ref, dg_ref,
             db_ref, uext, dext, dw_acc, aligned, sums):
        i = pl.program_id(0)
        uext[0:H, :] = jnp.where(i == 0, 0.0, cvp[...].astype(F32) * _sigmoid(cgp[...].astype(F32)))
        _glu_rows(cv, cg, uext, H, tb)
        du1n, _, _ = du1_of(u1n[...], du3n[...].astype(F32), g_ref[...], b_ref[...])
        dext[tb:tb + H, :] = jnp.where(i == nb - 1, 0.0, du1n)

        @pl.when(i == 0)
        def _():
            dw_acc[...] = jnp.zeros_like(dw_acc)
            sums[...] = jnp.zeros_like(sums)

        def halves(v):
            part = v[0:SUBLANES, :]
            for q in range(SUBLANES, ROW_CHUNK, SUBLANES):
                part = part + v[q:q + SUBLANES, :]
            return part

        for j in range(tb // ROW_CHUNK):
            rs = _row_chunk(j)
            du1, du2, xhat = du1_of(u1_ref[rs, :], du3_ref[rs, :].astype(F32), g_ref[...], b_ref[...])
            dext[rs, :] = du1
            sums[0:SUBLANES, :] += halves(du1)
            sums[SUBLANES:2 * SUBLANES, :] += halves(du2 * xhat)
            sums[2 * SUBLANES:3 * SUBLANES, :] += halves(du2)
        for r0 in range(0, tb, CONV_ROWS):
            rows = slice(r0, r0 + CONV_ROWS)
            for c0 in range(0, D, LANES):
                cols = slice(c0, c0 + LANES)
                acc = [jnp.zeros((CONV_ROWS, LANES), F32)]

                def tap(k, src):
                    acc[0] = acc[0] + src * w_ref[k:k + 1, cols]

                _for_taps(dext, r0, cols, bwd_groups, aligned, tap)
                c_val = cv[rows, cols].astype(F32)
                sg = _sigmoid(cg[rows, cols].astype(F32))
                dz_ref[rows, cols] = (acc[0] * sg).astype(dz_ref.dtype)
                dz_ref[rows, D + c0:D + c0 + LANES] = (acc[0] * c_val * sg * (1.0 - sg)).astype(dz_ref.dtype)
                du1_blk = dext[rows, cols]

                def tap_w(k, src):
                    prod = du1_blk * src
                    part = prod[0:SUBLANES, :]
                    for j in range(SUBLANES, CONV_ROWS, SUBLANES):
                        part = part + prod[j:j + SUBLANES, :]
                    dw_acc[k * SUBLANES:(k + 1) * SUBLANES, cols] += part

                _for_taps(uext, r0, cols, fwd_groups, aligned, tap_w)

        @pl.when(i == nb - 1)
        def _():
            dw_ref[...] = jnp.zeros_like(dw_ref)
            for k in range(CONV_C_W):
                dw_ref[k:k + 1, :] = jnp.sum(dw_acc[k * SUBLANES:(k + 1) * SUBLANES, :], axis=0, keepdims=True)
            for n, ref in enumerate((dcb_ref, dg_ref, db_ref)):
                ref[...] = jnp.sum(sums[n * SUBLANES:(n + 1) * SUBLANES, :], axis=0, keepdims=True)

    vec = _full((1, D))
    row = _col(tb, D, 0)
    nxt = _next(tb, H, D, 0, T)
    return _call(body, name='mixc_bwd',
                 out_shape=(_sds(dz2.shape, dz2.dtype), _sds((32, D), F32), _sds((1, D), F32), _sds((1, D), F32),
                            _sds((1, D), F32)), grid=(nb,),
                 in_specs=[_col(tb, D, 0), _col(tb, D, 1), _prev(tb, H, D, 0), _prev(tb, H, D, 1), _full((32, D)), vec,
                           vec, row, nxt, row, nxt, pl.BlockSpec(memory_space=pl.ANY)],
                 out_specs=(_col(tb, 2 * D, 0), _full((32, D)), vec, vec, vec),
                 scratch=[pltpu.VMEM((H + tb, D), F32), pltpu.VMEM((tb + H, D), F32),
                          pltpu.VMEM((32 * SUBLANES, D), F32), pltpu.VMEM((SUBLANES, CONV_ROWS + H, LANES), F32),
                          pltpu.VMEM((3 * SUBLANES, D), F32)],
                 sem=('arbitrary',), aliases={11: 0})(z, z, z, z, w, gain, bias, u1, u1, du3, du3, dz2)


def _gate_fwd(z, bg, ya, yb, yc):
    T = z.shape[0]
    D = ya.shape[1]
    tb = _pick(T, (256, 128))

    def body(ga, gb, gc, bg_ref, ya_ref, yb_ref, yc_ref, o_ref):
        acc = jnp.zeros((tb, D), F32)
        for n, (g, y) in enumerate(((ga, ya_ref), (gb, yb_ref), (gc, yc_ref))):
            acc = acc + _sigmoid(g[...].astype(F32) + bg_ref[:, n * D:(n + 1) * D]) * y[...].astype(F32)
        o_ref[...] = acc.astype(o_ref.dtype)

    row = _col(tb, D, 0)
    return _call(body, name='gate_fwd', out_shape=_sds((T, D), BF16), grid=(T // tb,),
                 in_specs=[_col(tb, D, 2), _col(tb, D, 3), _col(tb, D, 4), _full((1, 3 * D)), row, row, row],
                 out_specs=row, sem=('parallel',))(z, z, z, bg, ya, yb, yc)


def _gate_bwd(z, bg, ya, yb, yc, dm):
    T = z.shape[0]
    D = ya.shape[1]
    tb = _pick(T, (256, 128))

    def body(ga, gb, gc, bg_ref, ya_ref, yb_ref, yc_ref, dm_ref, dya, dyb, dyc, dg_ref, dbg_ref):
        @pl.when(pl.program_id(0) == 0)
        def _():
            dbg_ref[...] = jnp.zeros_like(dbg_ref)

        dmv = dm_ref[...]
        for n, (g, y, dy) in enumerate(((ga, ya_ref, dya), (gb, yb_ref, dyb), (gc, yc_ref, dyc))):
            cols = slice(n * D, (n + 1) * D)
            sg = _sigmoid(g[...].astype(F32) + bg_ref[:, cols])
            dy[...] = (dmv * sg).astype(dy.dtype)
            dg = dmv * y[...].astype(F32) * sg * (1.0 - sg)
            dg_ref[:, 2 * D + n * D:2 * D + (n + 1) * D] = dg.astype(dg_ref.dtype)
            dbg_ref[:, cols] += jnp.sum(dg, axis=0, keepdims=True)

    row = _col(tb, D, 0)
    act = _sds((T, D), BF16)
    return _call(body, name='gate_bwd', out_shape=(act, act, act, _sds((T, 5 * D), BF16), _sds((1, 3 * D), F32)),
                 grid=(T // tb,),
                 in_specs=[_col(tb, D, 2), _col(tb, D, 3), _col(tb, D, 4), _full((1, 3 * D)), row, row, row, row],
                 out_specs=(row, row, row, _col(tb, 5 * D, 0), _full((1, 3 * D))),
                 sem=('arbitrary',))(z, z, z, bg, ya, yb, yc, dm)


def _xattn_probs(qs, ks, scale):
    s = lax.dot_general(qs, ks, _DN['nt'], preferred_element_type=F32) * scale
    p = jnp.exp(s - jnp.max(s, axis=-1, keepdims=True))
    return p * (1.0 / jnp.sum(p, axis=-1, keepdims=True))


def _xattn_fwd(q, kv):
    T, DX = q.shape
    M = kv.shape[0]
    dh = DX // X_HEADS
    tb = _pick(T, (512, 256, 128))
    scale = dh ** -0.5

    def body(q_ref, kv_ref, o_ref):
        for h in range(X_HEADS):
            hs = slice(h * dh, (h + 1) * dh)
            p = _xattn_probs(q_ref[:, hs], kv_ref[:, hs], scale)
            o_ref[:, hs] = jnp.dot(p.astype(BF16), kv_ref[:, DX + h * dh:DX + (h + 1) * dh],
                                   preferred_element_type=F32).astype(o_ref.dtype)

    row = _col(tb, DX, 0)
    return _call(body, name='xattn_fwd', out_shape=_sds((T, DX), BF16), grid=(T // tb,),
                 in_specs=[row, _full((M, 2 * DX))], out_specs=row, sem=('parallel',))(q, kv)


def _xattn_bwd(q, kv, do):
    T, DX = q.shape
    M = kv.shape[0]
    dh = DX // X_HEADS
    tb = _pick(T, (512, 256, 128))
    scale = dh ** -0.5

    def body(q_ref, kv_ref, do_ref, dq_ref, dkv_ref):
        @pl.when(pl.program_id(0) == 0)
        def _():
            dkv_ref[...] = jnp.zeros_like(dkv_ref)

        for h in range(X_HEADS):
            hs = slice(h * dh, (h + 1) * dh)
            vs_cols = slice(DX + h * dh, DX + (h + 1) * dh)
            qs, ks, vs = q_ref[:, hs], kv_ref[:, hs], kv_ref[:, vs_cols]
            dos = do_ref[:, hs].astype(BF16)
            p = _xattn_probs(qs, ks, scale)
            dp = lax.dot_general(dos, vs, _DN['nt'], preferred_element_type=F32)
            ds = p * (dp - jnp.sum(p * dp, axis=-1, keepdims=True))
            dsb = ds.astype(BF16)
            dq_ref[:, hs] = (jnp.dot(dsb, ks, preferred_element_type=F32) * scale).astype(dq_ref.dtype)
            dkv_ref[:, hs] += lax.dot_general(dsb, qs, _DN['tn'], preferred_element_type=F32) * scale
            dkv_ref[:, vs_cols] += lax.dot_general(p.astype(BF16), dos, _DN['tn'], preferred_element_type=F32)

    row = _col(tb, DX, 0)
    return _call(body, name='xattn_bwd', out_shape=(_sds((T, DX), BF16), _sds((M, 2 * DX), F32)), grid=(T // tb,),
                 in_specs=[row, _full((M, 2 * DX)), row], out_specs=(row, _full((M, 2 * DX))),
                 sem=('arbitrary',))(q, kv, do)


def _swiglu_fwd(gu):
    T, F2 = gu.shape
    F = F2 // 2
    tb = _pick(T, (256, 128))

    def body(g_ref, u_ref, o_ref):
        g = g_ref[...].astype(F32)
        o_ref[...] = (g * _sigmoid(g) * u_ref[...].astype(F32)).astype(o_ref.dtype)

    return _call(body, name='swiglu_fwd', out_shape=_sds((T, F), BF16), grid=(T // tb,),
                 in_specs=[_col(tb, F, 0), _col(tb, F, 1)], out_specs=_col(tb, F, 0), sem=('parallel',))(gu, gu)


def _swiglu_bwd(gu, da):
    T, F2 = gu.shape
    F = F2 // 2
    tb = _pick(T, (256, 128))

    def body(g_ref, u_ref, da_ref, o_ref):
        g = g_ref[...].astype(F32)
        u = u_ref[...].astype(F32)
        dav = da_ref[...].astype(F32)
        sg = _sigmoid(g)
        o_ref[:, 0:F] = (dav * u * (sg * (1.0 + g * (1.0 - sg)))).astype(o_ref.dtype)
        o_ref[:, F:F2] = (dav * g * sg).astype(o_ref.dtype)

    return _call(body, name='swiglu_bwd', out_shape=_sds((T, F2), BF16), grid=(T // tb,),
                 in_specs=[_col(tb, F, 0), _col(tb, F, 1), _col(tb, F, 0)], out_specs=_col(tb, F2, 0),
                 sem=('parallel',))(gu, gu, da)


def _place():
    x, y, c = lax.axis_index('x'), lax.axis_index('y'), lax.axis_index('c')
    return x, y, c


_PEER_ORDER = (1, 4, 2, 5, 3, 6, 7)
_HBM = pl.BlockSpec(memory_space=pltpu.HBM)
_SEM = pl.BlockSpec(memory_space=pltpu.SEMAPHORE)


def _my_index():
    x, y, c = _place()
    return 4 * x + 2 * y + c


def _remote(src, dst, send_sems, recv_sems, k, dev):
    return pltpu.make_async_remote_copy(src_ref=src, dst_ref=dst, send_sem=send_sems.at[k], recv_sem=recv_sems.at[k],
                                        device_id=dev, device_id_type=MESH)


def _own_part(src, kind, args):
    return src if args is None else src.at[pl.ds(args[0], args[1])]


def _block(land, kind, args, owner):
    if kind == 'rows':
        _, n, stride, d0 = args
        return land.at[pl.ds(pl.multiple_of(stride * owner + d0, ROW_TILE), n)]
    return land.at[owner]


def _chunk(src, args, dest):
    stride, n = args
    return src.at[pl.ds(pl.multiple_of(stride * dest, ROW_TILE), n)]


def _scatter_copies(plan, srcs, lands, send_sems, recv_sems):
    x, y, c = _place()
    me = 4 * x + 2 * y + c
    copies = []
    for r in _PEER_ORDER:
        px, py, pc = x ^ (r >> 2), y ^ ((r >> 1) & 1), c ^ (r & 1)
        for si, li, kind, args in plan:
            src = srcs[si] if kind == 'slot' else _chunk(srcs[si], args, 4 * px + 2 * py + pc)
            copies.append(_remote(src, lands[li].at[me], send_sems, recv_sems, r - 1, (px, py, pc)))
    return copies


def _gather_copies(plan, srcs, lands, send_sems, recv_sems, passing):
    x, y, c = _place()
    me = 4 * x + 2 * y + c
    sibling = (x, y, 1 - c)
    chips = [(1 - x, y), (x, 1 - y), (1 - x, 1 - y)]
    copies = []
    if passing:
        for k, chip in enumerate(chips):
            for _, li, kind, args in plan:
                blk = _block(lands[li], kind, args, 4 * chip[0] + 2 * chip[1] + c)
                copies.append((k, _remote(blk, blk, send_sems, recv_sems, k, sibling)))
    else:
        for k, dev in enumerate([sibling] + [(*chip, c) for chip in chips]):
            for si, li, kind, args in plan:
                dst = _block(lands[li], kind, args, me)
                src = dst if srcs is None else _own_part(srcs[si], kind, args)
                copies.append((k, _remote(src, dst, send_sems, recv_sems, k, dev)))
    return copies


def _sem_call(name, bufs, sems_in, new_sems, after, action):
    nb, ni, nn = len(bufs), len(sems_in), len(new_sems)

    def body(*refs):
        action(refs[:nb], refs[nb:nb + ni], refs[-1 - nb - nn:-1 - nb])
        refs[-1][...] = jnp.zeros_like(refs[-1])

    bufs = [pltpu.with_memory_space_constraint(v, pltpu.HBM) for v in bufs]
    res = pl.pallas_call(
        body, name=name,
        out_shape=(*[pltpu.SemaphoreType.DMA((n,)) for n in new_sems], *[pltpu.HBM(v.shape, v.dtype) for v in bufs],
                   _sds((8, LANES), F32)),
        in_specs=[_HBM] * nb + [_SEM] * ni + [pl.BlockSpec(memory_space=pl.ANY)] * len(after),
        out_specs=(*[_SEM] * nn, *[_HBM] * nb, pl.BlockSpec(memory_space=pltpu.VMEM)),
        input_output_aliases={i: nn + i for i in range(nb)},
        compiler_params=pltpu.CompilerParams(has_side_effects=pltpu.SideEffectType.DATAFLOW_SIDE_EFFECTING),
        interpret=False)(*bufs, *sems_in, *after)
    return list(res[:nn]), list(res[nn:nn + nb]), res[-1]


def _place_own(srcs, plan, lands_like, name, deps):
    ns, nl, ne = len(srcs), len(lands_like), len(plan)

    def part_shape(entry):
        si, _, kind, args = entry
        rows = args[1] if (kind == 'chunk' or args is not None) else None
        return srcs[si].shape if rows is None else (rows,) + srcs[si].shape[1:]

    def body(*refs):
        src_r, land_r = refs[:ns], refs[ns:ns + nl]
        stage, sems_in, sems_out = refs[ns + nl:ns + nl + ne], refs[-2], refs[-1]
        me = _my_index()
        loads, stores = [], []
        for i, (si, li, kind, args) in enumerate(plan):
            if kind == 'chunk':
                src, dst = _chunk(src_r[si], args, me), land_r[li].at[me]
            else:
                src, dst = _own_part(src_r[si], kind, args), _block(land_r[li], kind, args, me)
            loads.append(pltpu.make_async_copy(src, stage[i], sems_in.at[i]))
            stores.append(pltpu.make_async_copy(stage[i], dst, sems_out.at[i]))
        for cp in loads:
            cp.start()
        for load, store in zip(loads, stores):
            load.wait()
            store.start()
        for cp in stores:
            cp.wait()

    anywhere = pl.BlockSpec(memory_space=pl.ANY)
    scratch = [pltpu.VMEM(part_shape(e), srcs[e[0]].dtype) for e in plan]
    scratch += [pltpu.SemaphoreType.DMA((ne,)), pltpu.SemaphoreType.DMA((ne,))]
    return list(_call(body, name=name, out_shape=tuple(lands_like), in_specs=[anywhere] * ns,
                      out_specs=tuple([anywhere] * nl), scratch=scratch, deps=deps)(*srcs))


def _scatter_start(srcs, lands, plan, after, name):
    ns = len(srcs)

    def action(bufs, _, new):
        for cp in _scatter_copies(plan, bufs[:ns], bufs[ns:], new[0], new[1]):
            cp.start()

    sems, bufs, token = _sem_call(name, [*srcs, *lands], [], [N_DEV - 1, N_DEV - 1], after, action)
    return sems, bufs[:ns], bufs[ns:], token


def _scatter_wait(sems, srcs, lands, plan, after, name):
    ns = len(srcs)

    def action(bufs, old, _):
        copies = _scatter_copies(plan, bufs[:ns], bufs[ns:], old[0], old[1])
        for cp in copies:
            cp.wait_send()
        for cp in copies:
            cp.wait_recv()

    return _sem_call(name, [*srcs, *lands], sems, [], after, action)[1][ns:]


def _gather_start(srcs, lands, plan, name):
    ns = len(srcs)

    def action(bufs, _, new):
        for _, cp in _gather_copies(plan, bufs[:ns], bufs[ns:], new[0], new[1], False):
            cp.start()

    sems, bufs, token = _sem_call(name, [*srcs, *lands], [], [4, 4], [], action)
    return sems, bufs[:ns], bufs[ns:], token


def _gather_pass(recv_sems, lands, plan, after, name):
    def action(bufs, old, new):
        for _, cp in _gather_copies(plan, None, bufs, old[0], old[0], False):
            cp.wait_recv()
        for _, cp in _gather_copies(plan, None, bufs, new[0], new[1], True):
            cp.start()

    return _sem_call(name, lands, [recv_sems], [3, 3], after, action)


def _gather_wait(send_sems, pass_sems, srcs, lands, plan, after, name):
    ns = len(srcs)

    def action(bufs, old, _):
        for _, cp in _gather_copies(plan, bufs[:ns], bufs[ns:], old[0], old[0], False):
            cp.wait_send()
        passed = _gather_copies(plan, None, bufs[ns:], old[1], old[2], True)
        for _, cp in passed:
            cp.wait_send()
        for _, cp in passed:
            cp.wait_recv()

    return _sem_call(name, [*srcs, *lands], [send_sems, *pass_sems], [], after, action)[1][ns:]


def _merge_edges(wt, edges, tiles_per_dev):
    D = wt.shape[1]

    def body(w_ref, e_ref, o_ref):
        s = pl.program_id(0)
        o_ref[...] = jnp.where(s == 0, e_ref[...], w_ref[...] + e_ref[...])

    tile = pl.BlockSpec((ROW_TILE, D), lambda s: (s * tiles_per_dev, 0))
    return _call(body, name='merge_edges', out_shape=_sds(wt.shape, wt.dtype), grid=(N_DEV,),
                 in_specs=[tile, pl.BlockSpec((None, ROW_TILE, D), lambda s: (s, 0, 0))], out_specs=tile,
                 sem=('arbitrary',), aliases={0: 0})(wt, edges)


def _adam_update(g, w, m, v):
    c1 = 1.0 - ADAM_B1 ** ADAM_STEP
    c2 = 1.0 - ADAM_B2 ** ADAM_STEP
    mn = ADAM_B1 * m + (1.0 - ADAM_B1) * g
    vn = ADAM_B2 * v + (1.0 - ADAM_B2) * (g * g)
    m_hat = mn / c1
    v_hat = vn / c2
    return -ADAM_LR * (m_hat / (jnp.sqrt(v_hat) + ADAM_EPS) + ADAM_WD * w), mn, vn


def _adamw(parts, w, m, v, name):
    n_parts = parts.shape[0]
    R, C = w.shape
    tb = R if (n_parts + 7) * R * C * 4 <= (12 << 20) else _pick(R, (128, 64, 32, 16, 8))

    def body(p_ref, w_ref, m_ref, v_ref, g_out, d_out, m_out, v_out):
        g = p_ref[0].astype(F32)
        for s in range(1, n_parts):
            g = g + p_ref[s].astype(F32)
        g_out[...] = g
        d_out[...], m_out[...], v_out[...] = _adam_update(g, w_ref[...], m_ref[...], v_ref[...])

    row = pl.BlockSpec((tb, C), lambda i: (i, 0))
    o = _sds((R, C), F32)
    return _call(body, name=name, out_shape=(o, o, o, o), grid=(R // tb,),
                 in_specs=[pl.BlockSpec((n_parts, tb, C), lambda i: (0, i, 0)), row, row, row],
                 out_specs=(row, row, row, row), sem=('parallel',))(parts, w, m, v)


def _adamw_layer(parts, w, m, v, layer, prev, shift, name):
    L, n, C = w.shape
    n_p = parts.shape[1]
    cb = C if N_DEV * n_p * C * parts.dtype.itemsize <= (6 << 20) else 2 * LANES
    n_prev = 0 if prev is None else 4

    def body(sh_ref, p_ref, w_ref, m_ref, v_ref, *rest):
        g_out, d_out, m_out, v_out = rest[n_prev:n_prev + 4]
        g = p_ref[0].astype(F32)
        for s in range(1, N_DEV):
            g = g + p_ref[s].astype(F32)
        if n_p != n:
            rolled = rest[-1]
            rolled[...] = pltpu.roll(g, n_p - sh_ref[0], 0)
            g = rolled[0:n, :]
        g_out[...] = g
        d_out[...], m_out[...], v_out[...] = _adam_update(g, w_ref[...], m_ref[...], v_ref[...])

    lay = pl.BlockSpec((None, n, cb), lambda j, sh: (layer, 0, j))
    stack = _sds((L, n, C), F32)
    grid_spec = pltpu.PrefetchScalarGridSpec(
        num_scalar_prefetch=1, grid=(C // cb,),
        in_specs=[pl.BlockSpec((N_DEV, n_p, cb), lambda j, sh: (0, 0, j)), lay, lay, lay]
        + [pl.BlockSpec(memory_space=pl.ANY)] * n_prev,
        out_specs=(lay, lay, lay, lay), scratch_shapes=[pltpu.VMEM((n_p, cb), F32)] if n_p != n else [])
    return pl.pallas_call(body, name=name, grid_spec=grid_spec, out_shape=(stack, stack, stack, stack),
                          input_output_aliases={5 + k: k for k in range(n_prev)},
                          compiler_params=pltpu.CompilerParams(dimension_semantics=('parallel',),
                                                               vmem_limit_bytes=VMEM_LIMIT),
                          interpret=False)(shift, parts, w, m, v, *(prev or ()))


def kernel(x, mem, mix_norm, w_in, b_gate, b_forget, conv_a, w_out_a, w_out_b, conv_c, conv_c_bias, ln_c_gain, ln_c_bias, w_out_c, w_o, xattn_norm, mem_norm, w_xq, w_xkv, w_xo, ffn_norm, w_gate_up, w_down, final_norm, loss_target, m_mix_norm, m_w_in, m_b_gate, m_b_forget, m_conv_a, m_w_out_a, m_w_out_b, m_conv_c, m_conv_c_bias, m_ln_c_gain, m_ln_c_bias, m_w_out_c, m_w_o, m_xattn_norm, m_mem_norm, m_w_xq, m_w_xkv, m_w_xo, m_ffn_norm, m_w_gate_up, m_w_down, m_final_norm, v_mix_norm, v_w_in, v_b_gate, v_b_forget, v_conv_a, v_w_out_a, v_w_out_b, v_conv_c, v_conv_c_bias, v_ln_c_gain, v_ln_c_bias, v_w_out_c, v_w_o, v_xattn_norm, v_mem_norm, v_w_xq, v_w_xkv, v_w_xo, v_ffn_norm, v_w_gate_up, v_w_down, v_final_norm):
    P = dict(zip(ARG_NAMES, (x, mem, mix_norm, w_in, b_gate, b_forget, conv_a, w_out_a, w_out_b, conv_c, conv_c_bias, ln_c_gain, ln_c_bias, w_out_c, w_o, xattn_norm, mem_norm, w_xq, w_xkv, w_xo, ffn_norm, w_gate_up, w_down, final_norm, loss_target, m_mix_norm, m_w_in, m_b_gate, m_b_forget, m_conv_a, m_w_out_a, m_w_out_b, m_conv_c, m_conv_c_bias, m_ln_c_gain, m_ln_c_bias, m_w_out_c, m_w_o, m_xattn_norm, m_mem_norm, m_w_xq, m_w_xkv, m_w_xo, m_ffn_norm, m_w_gate_up, m_w_down, m_final_norm, v_mix_norm, v_w_in, v_b_gate, v_b_forget, v_conv_a, v_w_out_a, v_w_out_b, v_conv_c, v_conv_c_bias, v_ln_c_gain, v_ln_c_bias, v_w_out_c, v_w_o, v_xattn_norm, v_mem_norm, v_w_xq, v_w_xkv, v_w_xo, v_ffn_norm, v_w_gate_up, v_w_down, v_final_norm)))
    return _step(P)


_T_VIEW = ('w_in', 'w_gate_up', 'w_xo')
_BIG = [w for w in SHARDED if w not in ('conv_a', 'conv_c')]
_OTHER = [w for w in _BIG if w != 'w_in']
_CONVS = (('conv_a', CONV_A_W, 8), ('conv_c', CONV_C_W, 32))
_EARLY = ['w_down', 'w_gate_up', 'w_xo', 'w_xq', 'w_xkv', 'w_o']
_LATE = ['w_in', 'w_out_c', 'w_out_b', 'w_out_a']


def _view(name, a):
    return jnp.transpose(a, (0, 2, 1)) if name in _T_VIEW else a


def _step(P):
    L, D = P['mix_norm'].shape
    T = P['x'].shape[1]
    n_in = P['w_in'].shape[2]
    stride = n_in // ROW_TILE * ROW_TILE
    rem = n_in - stride
    win = stride + ROW_TILE
    assert rem * N_DEV == ROW_TILE
    r_in = n_in * N_DEV
    f_off = 6 * D
    hi_off = f_off + FOX_HEADS
    x_i, y_i, c_i = _place()
    me = 4 * x_i + 2 * y_i + c_i
    V = {}
    for w in _BIG:
        for k in (w, 'm_' + w, 'v_' + w):
            V[k] = _view(w, P[k])
    n_own = {w: V[w].shape[1] for w in _OTHER}
    shift = jnp.reshape(rem * me, (1,)).astype(jnp.int32)
    no_shift = jnp.zeros((1,), jnp.int32)
    row = lambda a, l: a[l][None, :]
    dus = lax.dynamic_update_slice

    conv_taps = {w: (taps, padded) for w, taps, padded in _CONVS}

    def gather_start(l, names, tag, deps):
        srcs, lands, plan = [], [], []
        for w in names:
            si, li = len(srcs), len(lands)
            if w == 'w_in':
                first = rem * me + sum(t[0, 0] for t in deps).astype(jnp.int32) if deps else rem * me
                srcs.append(dus(jnp.zeros((win, D), BF16), V[w][l].astype(BF16), (first, 0)))
                lands += [_sds((r_in, D), BF16), _sds((N_DEV, ROW_TILE, D), BF16)]
                plan += [(si, li, 'rows', (ROW_TILE, stride, stride, ROW_TILE)), (si, li + 1, 'slot', (0, ROW_TILE))]
            elif w in conv_taps:
                srcs.append(P[w][l])
                lands.append(_sds((N_DEV,) + srcs[si].shape, F32))
                plan.append((si, li, 'slot', None))
            else:
                srcs.append(V[w][l].astype(BF16))
                lands.append(_sds((N_DEV * n_own[w], srcs[si].shape[1]), BF16))
                plan.append((si, li, 'rows', (0, n_own[w], n_own[w], 0)))
        lands = _place_own(srcs, plan, lands, 'gather_own', deps)
        sems, srcs, lands, token = _gather_start(srcs, lands, plan, f'gather_start_{tag}{l}')
        return dict(names=names, name=f'{tag}{l}', plan=plan, send=sems[0], recv=sems[1], srcs=srcs, lands=lands,
                    token=token)

    def gather_pass(g, after):
        g['pass'], g['lands'], g['token'] = _gather_pass(g['recv'], g['lands'], g['plan'], after,
                                                        'gather_pass_' + g['name'])

    def gather_wait(g, after):
        lands = iter(_gather_wait(g['send'], g['pass'], g['srcs'], g['lands'], g['plan'], after,
                                  'gather_wait_' + g['name']))
        W = {}
        for w in g['names']:
            if w == 'w_in':
                W[w] = _merge_edges(next(lands), next(lands), stride // ROW_TILE)
            elif w in conv_taps:
                taps, padded = conv_taps[w]
                full = jnp.transpose(next(lands), (1, 0, 2)).reshape(taps, D)
                W[w] = jnp.pad(full, ((0, padded - taps), (0, 0)))
            else:
                W[w] = next(lands)
        return W

    first_part = ['w_in'] + list(conv_taps)
    gathers, prev = [], []
    for l in range(L):
        part_a = gather_start(l, first_part, 'a', prev)
        part_b = gather_start(l, _OTHER, 'b', [part_a['token']])
        gathers.append([part_a, part_b])
        prev = [part_b['token']]
    gather_pass(gathers[0][0], prev)

    mem_x = P['mem'][0]
    mem_n = _rms_fwd(mem_x, P['mem_norm'][None, :], 'rms_mem')
    xs = P['x'][0]
    saved = []
    for l in range(L):
        W = gather_wait(gathers[l][0], [xs])
        s = {'x0': xs, 'W': W}
        wt = W['w_in']
        s['h1'] = _rms_fwd(xs, row(P['mix_norm'], l), 'rms_mix')
        s['z1'] = z1 = _mm(s['h1'], wt, mode='nt', b_rows=(0, f_off), name='mm_in_lo')
        s['z2'] = z2 = _mm(s['h1'], wt, mode='nt', b_rows=(hi_off, r_in - hi_off), name='mm_in_hi')
        s['zf'] = _mm(s['h1'], wt, mode='nt', b_rows=(f_off, FOX_HEADS), out_dtype=F32, name='mm_in_f')
        s['ya_pre'] = _mixa_fwd(z1, W['conv_a'])
        s['c'] = _fox_c(s['zf'], row(P['b_forget'], l))
        s['o'], s['lse'] = _fox_fwd(z1, s['c'])
        for g in gathers[l][1:]:
            gather_pass(g, [s['o']])
        s['u3'], s['u1'] = _mixc_fwd(z2, W['conv_c'], row(P['conv_c_bias'], l), row(P['ln_c_gain'], l),
                                     row(P['ln_c_bias'], l))
        for g in gathers[l][1:]:
            W.update(gather_wait(g, [s['u3']]))
        s['ya'] = _mm(s['ya_pre'], W['w_out_a'], mode='nn', name='mm_out_a')
        s['yb'] = _mm(s['o'], W['w_out_b'], mode='nn', name='mm_out_b')
        s['yc'] = _mm(s['u3'], W['w_out_c'], mode='nn', name='mm_out_c')
        s['merged'] = _gate_fwd(z2, row(P['b_gate'], l), s['ya'], s['yb'], s['yc'])
        xs = _mm(s['merged'], W['w_o'], mode='nn', add=xs, out_dtype=F32, name='mm_o')
        s['x1'] = xs
        deps = []
        if l + 1 < L:
            gather_pass(gathers[l + 1][0], [xs])
            deps = [gathers[l + 1][0]['token']]
        s['h2'] = _rms_fwd(xs, row(P['xattn_norm'], l), 'rms_xattn', deps=deps)
        s['qx'] = _mm(s['h2'], W['w_xq'], mode='nn', name='mm_xq')
        s['kv'] = _mm(mem_n, W['w_xkv'], mode='nn', name='mm_xkv')
        s['ox'] = _xattn_fwd(s['qx'], s['kv'])
        xs = _mm(s['ox'], W['w_xo'], mode='nt', add=xs, out_dtype=F32, name='mm_xo')
        s['x2'] = xs
        s['h3'] = _rms_fwd(xs, row(P['ffn_norm'], l), 'rms_ffn')
        s['gu'] = _mm(s['h3'], W['w_gate_up'], mode='nt', name='mm_gate_up')
        s['act'] = _swiglu_fwd(s['gu'])
        xs = _mm(s['act'], W['w_down'], mode='nn', add=xs, out_dtype=F32, name='mm_down')
        saved.append(s)

    dx, d_final_norm, loss_part = _final(xs, P['final_norm'][None, :], P['loss_target'][0])

    GS = {w: [None] * L for w in SMALL}
    G_conv = {'conv_a': [None] * L, 'conv_c': [None] * L}
    d_mem_n = None
    results = {w: None for w in _BIG}

    def exchange_start(G, names, l, tag, after):
        srcs = [G[w] for w in names]
        plan, lands = [], []
        for i, w in enumerate(names):
            step, n = (stride, win) if w == 'w_in' else (n_own[w], n_own[w])
            lands.append(_sds((N_DEV, n, srcs[i].shape[1]), BF16))
            plan.append((i, i, 'chunk', (step, n)))
        lands = _place_own(srcs, plan, lands, 'exchange_own', [])
        sems, srcs, lands, token = _scatter_start(srcs, lands, plan, after, f'exchange_start_{tag}{l}')
        return (names, l, tag, plan, sems, srcs, lands, token)

    def exchange_finish(e, after):
        names, l, tag, plan, sems, srcs, lands, _ = e
        lands = _scatter_wait(sems, srcs, lands, plan, after, f'exchange_wait_{tag}{l}')
        for i, w in enumerate(names):
            results[w] = _adamw_layer(lands[i], V[w], V['m_' + w], V['v_' + w], l, results[w],
                                      shift if w == 'w_in' else no_shift, 'adamw_' + w)

    pending = []
    for l in reversed(range(L)):
        s = saved[l]
        W, z1, z2 = s['W'], s['z1'], s['z2']
        wt = W['w_in']
        G = {}
        d_act = _mm(dx, W['w_down'], mode='nt', name='mmb_down', deps=[e[-1] for e in pending])
        G['w_down'] = _mm(s['act'], dx, mode='tn', name='mmg_w_down')
        dgu = _swiglu_bwd(s['gu'], d_act)
        dh3 = _mm(dgu, W['w_gate_up'], mode='nn', out_dtype=F32, name='mmb_gate_up')
        G['w_gate_up'] = _mm(dgu, s['h3'], mode='tn', name='mmg_w_gate_up')
        dx, GS['ffn_norm'][l] = _rms_bwd(s['x2'], row(P['ffn_norm'], l), dh3, dx, 'rms_ffn_bwd')
        d_ox = _mm(dx, W['w_xo'], mode='nn', out_dtype=F32, name='mmb_xo')
        G['w_xo'] = _mm(dx, s['ox'], mode='tn', name='mmg_w_xo')
        dqx, dkv = _xattn_bwd(s['qx'], s['kv'], d_ox)
        dh2 = _mm(dqx, W['w_xq'], mode='nt', out_dtype=F32, name='mmb_xq')
        G['w_xq'] = _mm(s['h2'], dqx, mode='tn', name='mmg_w_xq')
        G['w_xkv'] = _mm(mem_n, dkv, mode='tn', name='mmg_w_xkv')
        d_mem_n = _mm(dkv, W['w_xkv'], mode='nt', add=d_mem_n, out_dtype=F32, name='mmb_xkv')
        dx, GS['xattn_norm'][l] = _rms_bwd(s['x1'], row(P['xattn_norm'], l), dh2, dx, 'rms_xattn_bwd')
        dm = _mm(dx, W['w_o'], mode='nt', out_dtype=F32, name='mmb_o')
        G['w_o'] = _mm(s['merged'], dx, mode='tn', name='mmg_w_o')
        early = exchange_start(G, _EARLY, l, 'a', [])
        dya, dyb, dyc, dz2, GS['b_gate'][l] = _gate_bwd(z2, row(P['b_gate'], l), s['ya'], s['yb'], s['yc'], dm)
        du3 = _mm(dyc, W['w_out_c'], mode='nt', out_dtype=F32, name='mmb_out_c', deps=[early[-1]])
        G['w_out_c'] = _mm(s['u3'], dyc, mode='tn', name='mmg_w_out_c')
        dz2, G_conv['conv_c'][l], GS['conv_c_bias'][l], GS['ln_c_gain'][l], GS['ln_c_bias'][l] = _mixc_bwd(
            z2, W['conv_c'], row(P['ln_c_gain'], l), row(P['ln_c_bias'], l), s['u1'], du3, dz2)
        dya_pre = _mm(dya, W['w_out_a'], mode='nt', out_dtype=F32, name='mmb_out_a')
        G['w_out_a'] = _mm(s['ya_pre'], dya, mode='tn', name='mmg_w_out_a')
        dz1, G_conv['conv_a'][l] = _mixa_bwd(z1, W['conv_a'], dya_pre)
        do = _mm(dyb, W['w_out_b'], mode='nt', name='mmb_out_b')
        G['w_out_b'] = _mm(s['o'], dyb, mode='tn', name='mmg_w_out_b')
        dz1, dc_pairs = _fox_bwd(z1, s['c'], s['o'], s['lse'], do, dz1)
        hp = FOX_HEADS // dc_pairs.shape[0]
        dc = jnp.transpose(dc_pairs[:, :, :hp], (1, 0, 2)).reshape(T, FOX_HEADS)
        dzf, GS['b_forget'][l] = _fox_c_bwd(dc, s['zf'], row(P['b_forget'], l))
        segs = [(dzf, f_off), (dz1, 0), (dz2, hi_off)]
        g_in = r_in
        for dz, off in segs:
            g_in = _mm(dz, s['h1'], mode='tn', out=g_in, o_off=off, name='mmg_w_in')
        G['w_in'] = g_in
        late = exchange_start(G, _LATE, l, 'b', [])
        dh1 = None
        for n, (dz, off) in enumerate(segs):
            dh1 = _mm(dz, wt, mode='nn', b_rows=(off, dz.shape[1]), add=dh1, out_dtype=F32, name='mmb_in',
                      deps=[late[-1]] if n == 0 else ())
        dx, GS['mix_norm'][l] = _rms_bwd(s['x0'], row(P['mix_norm'], l), dh1, dx, 'rms_mix_bwd')
        for e in pending:
            exchange_finish(e, [dx])
        pending = [early, late]

    _, d_mem_norm = _rms_bwd(mem_x, P['mem_norm'][None, :], d_mem_n, None, 'rms_mem_bwd')

    small_parts = {w: jnp.concatenate(GS[w], axis=0) for w in SMALL if w not in ('mem_norm', 'final_norm')}
    small_parts['mem_norm'] = d_mem_norm
    small_parts['final_norm'] = d_final_norm
    conv_parts = [jnp.stack(G_conv[w])[:, :taps] for w, taps, _ in _CONVS]
    sizes = [P[w].size for w in SMALL]
    conv_sizes = [c.size for c in conv_parts]
    n_small = sum(sizes) + sum(conv_sizes) + LANES
    n_rows = -(-n_small // (8 * LANES)) * 8

    def pack(parts):
        flat = jnp.concatenate([p.reshape(-1) for p in parts])
        return jnp.pad(flat, (0, n_rows * LANES - flat.size)).reshape(n_rows, LANES)

    g_small = pack([small_parts[w] for w in SMALL] + conv_parts + [loss_part])
    small_plan = [(0, 0, 'slot', None)]
    small_land = _place_own([g_small], small_plan, [_sds((N_DEV,) + g_small.shape, F32)], 'small_own', [])
    small_sems, small_src, small_land, _ = _scatter_start([g_small], small_land, small_plan, [], 'small_start')
    early0, late0 = pending
    exchange_finish(early0, [small_land[0]])
    exchange_finish(late0, [results[w][0] for w in _EARLY])
    (all_small,) = _scatter_wait(small_sems, small_src, small_land, small_plan, [results['w_in'][0]], 'small_wait')
    sm = _adamw(all_small, pack([P[w] for w in SMALL]), pack([P['m_' + w] for w in SMALL]),
                pack([P['v_' + w] for w in SMALL]), 'adamw_small')
    g_all = sm[0].reshape(-1)
    loss = g_all[sum(sizes) + sum(conv_sizes)]
    final = {}
    off = 0
    for w, n in zip(SMALL, sizes):
        final[w] = [o.reshape(-1)[off:off + n].reshape(P[w].shape) for o in sm]
        off += n
    for (w, taps, _), c, n in zip(_CONVS, conv_parts, conv_sizes):
        n_col = P[w].shape[2]
        g_own = lax.dynamic_slice(g_all[off:off + n].reshape(c.shape), (0, 0, n_col * me), (L, taps, n_col))
        flat = lambda a: a.reshape(L * taps, n_col)
        outs = _adamw(flat(g_own)[None], flat(P[w]), flat(P['m_' + w]), flat(P['v_' + w]), 'adamw_' + w)
        final[w] = [o.reshape(P[w].shape) for o in outs]
        off += n

    for w in _BIG:
        final[w] = [_view(w, o) for o in results[w]]

    out = [loss, dx[None]]
    for k in range(4):
        out += [final[w][k] for w in WEIGHTS]
    return tuple(out)
```
